```python
import math
import jax, jax.numpy as jnp
from jax import lax
import numpy as np

D_MODEL = 1024
BATCH = 16
SEQ = 4096
DEPTH = 1

DA_HEADS = 8
DA_HEAD_DIM = 64
DA_V_DIM = 2 * DA_HEAD_DIM
ROPE_THETA = 10000.0
QUERY_BLOCK = 128
POOL_WINDOWS = (2, 4, 8, 16)
POOL_GROUPS = len(POOL_WINDOWS)
POOL_GROUP_DIM = 128
POOL_DIM = POOL_GROUPS * POOL_GROUP_DIM
MEM_LEN = 256
MEM_HEADS = 4
MEM_HEAD_DIM = 128
MEM_DIM = MEM_HEADS * MEM_HEAD_DIM
Q_DIM = DA_HEADS * 2 * DA_HEAD_DIM
K_DIM = DA_HEADS * 2 * DA_HEAD_DIM
V_DIM = DA_HEADS * DA_V_DIM
N_BRANCHES = 3
GATE_DIM = N_BRANCHES * D_MODEL
IN_DIM = Q_DIM + K_DIM + V_DIM + POOL_DIM + MEM_DIM + GATE_DIM
SPLITS = (Q_DIM, Q_DIM + K_DIM, Q_DIM + K_DIM + V_DIM,
          Q_DIM + K_DIM + V_DIM + POOL_DIM,
          Q_DIM + K_DIM + V_DIM + POOL_DIM + MEM_DIM)
N_EXPERTS = 32
TOP_K = 4
D_EXPERT = D_MODEL
SWIGLU_LIMIT = 7.0
SWIGLU_ALPHA = 1.702
EXPERT_BLOCK = 256
LAYER_INDEX = 1
LAMBDA_INIT = 0.8 - 0.6 * math.exp(-0.3 * (LAYER_INDEX - 1))
EPS = 1e-6
NEG_INF = -1e30

kernel_name = "hybrid_gated_diffattn_pool_memxattn_moe"


def rmsnorm(t, g):
    t32 = t.astype(jnp.float32)
    t32 = t32 * lax.rsqrt(jnp.mean(t32 * t32, axis=-1, keepdims=True) + EPS)
    return (t32 * g.astype(jnp.float32)).astype(t.dtype)


def rotary(t, cos, sin):
    half = t.shape[-1] // 2
    t32 = t.astype(jnp.float32)
    c = cos[None, :, None, None, :]
    s = sin[None, :, None, None, :]
    t1, t2 = t32[..., :half], t32[..., half:]
    return jnp.concatenate([t1 * c - t2 * s, t2 * c + t1 * s], axis=-1).astype(t.dtype)


def diff_attention(q, k, v, lam):
    B, S = q.shape[0], q.shape[1]
    nb = S // QUERY_BLOCK
    scale = 1.0 / math.sqrt(DA_HEAD_DIM)
    qb = q.reshape(B, nb, QUERY_BLOCK, DA_HEADS, 2, DA_HEAD_DIM).transpose(1, 0, 2, 3, 4, 5)
    key_pos = jnp.arange(S)

    def block(args):
        q_blk, i = args
        s = jnp.einsum('bqhcd,bkhcd->bhcqk', q_blk, k).astype(jnp.float32) * scale
        q_pos = i * QUERY_BLOCK + jnp.arange(QUERY_BLOCK)
        mask = key_pos[None, :] <= q_pos[:, None]
        s = jnp.where(mask, s, NEG_INF)
        p = jax.nn.softmax(s, axis=-1)
        a = p[:, :, 0] - lam * p[:, :, 1]
        return jnp.einsum('bhqk,bkhe->bqhe', a.astype(v.dtype), v)

    o = lax.map(block, (qb, jnp.arange(nb)))
    return o.transpose(1, 0, 2, 3, 4).reshape(B, S, DA_HEADS, DA_V_DIM)


def multiscale_pool(u, pool_w, pool_scale):
    B, S, _ = u.shape
    u32 = u.astype(jnp.float32)
    cs = jnp.concatenate([jnp.zeros((B, 1, POOL_DIM), jnp.float32),
                          jnp.cumsum(u32, axis=1)], axis=1)
    t1 = jnp.arange(S) + 1
    outs = []
    for g, w in enumerate(POOL_WINDOWS):
        sl = slice(g * POOL_GROUP_DIM, (g + 1) * POOL_GROUP_DIM)
        csg = cs[..., sl]
        prev = jnp.concatenate([jnp.zeros((B, w - 1, POOL_GROUP_DIM), jnp.float32),
                                csg[:, :S + 1 - w]], axis=1)
        count = jnp.minimum(t1, w).astype(jnp.float32)[None, :, None]
        outs.append((csg[:, 1:] - prev) / count - u32[..., sl])
    z = jnp.stack(outs, axis=2)
    z = jnp.einsum('bsgc,gcd->bsgd', z, pool_w.astype(jnp.float32)).reshape(B, S, POOL_DIM)
    return (z * pool_scale.astype(jnp.float32)).astype(u.dtype)


def memory_cross_attention(q_mem, mem, g_mem, w_mem_kv, mq_norm, mk_norm):
    B, S = q_mem.shape[0], q_mem.shape[1]
    M = mem.shape[1]
    qm = rmsnorm(q_mem.reshape(B, S, MEM_HEADS, MEM_HEAD_DIM), mq_norm)
    kv = rmsnorm(mem, g_mem) @ w_mem_kv
    km, vm = jnp.split(kv, 2, axis=-1)
    km = rmsnorm(km.reshape(B, M, MEM_HEADS, MEM_HEAD_DIM), mk_norm)
    vm = vm.reshape(B, M, MEM_HEADS, MEM_HEAD_DIM)
    s = jnp.einsum('bshd,bmhd->bhsm', qm, km).astype(jnp.float32) / math.sqrt(MEM_HEAD_DIM)
    p = jax.nn.softmax(s, axis=-1)
    o = jnp.einsum('bhsm,bmhd->bshd', p.astype(vm.dtype), vm)
    return o.reshape(B, S, MEM_DIM)


def moe(hn, w_router, b_router, w_gate, b_gate, w_up, b_up, w_down, b_down):
    B, S, D = hn.shape
    T = B * S
    xt = hn.reshape(T, D)
    logits = (xt @ w_router + b_router).astype(jnp.float32)
    top_val, top_idx = lax.top_k(logits, TOP_K)
    gates = jax.nn.softmax(top_val, axis=-1)
    A = T * TOP_K
    e_flat = top_idx.reshape(A).astype(jnp.int32)
    tok_flat = jnp.arange(A, dtype=jnp.int32) // TOP_K
    w_flat = gates.reshape(A)
    order = jnp.argsort(e_flat)
    e_sorted, tok_sorted, w_sorted = e_flat[order], tok_flat[order], w_flat[order]
    counts = jnp.zeros((N_EXPERTS,), jnp.int32).at[e_flat].add(1)
    padded = ((counts + EXPERT_BLOCK - 1) // EXPERT_BLOCK) * EXPERT_BLOCK
    off = jnp.cumsum(counts) - counts
    pend = jnp.cumsum(padded)
    poff = pend - padded
    dest = poff[e_sorted] + (jnp.arange(A, dtype=jnp.int32) - off[e_sorted])
    n_blocks = -(-A // EXPERT_BLOCK) + N_EXPERTS
    P = n_blocks * EXPERT_BLOCK
    tok_buf = jnp.zeros((P,), jnp.int32).at[dest].set(tok_sorted)
    w_buf = jnp.zeros((P,), jnp.float32).at[dest].set(w_sorted)
    block_e = jnp.minimum(
        jnp.searchsorted(pend, jnp.arange(n_blocks, dtype=jnp.int32) * EXPERT_BLOCK, side='right'),
        N_EXPERTS - 1)

    def body(i, y):
        rows = lax.dynamic_slice(tok_buf, (i * EXPERT_BLOCK,), (EXPERT_BLOCK,))
        wts = lax.dynamic_slice(w_buf, (i * EXPERT_BLOCK,), (EXPERT_BLOCK,))
        e = block_e[i]
        xb = xt[rows]
        g = jnp.minimum(xb @ w_gate[e] + b_gate[e], SWIGLU_LIMIT)
        u = jnp.clip(xb @ w_up[e] + b_up[e], -SWIGLU_LIMIT, SWIGLU_LIMIT)
        act = g * jax.nn.sigmoid(SWIGLU_ALPHA * g) * (u + 1.0)
        out = act @ w_down[e] + b_down[e]
        return y.at[rows].add(out * wts[:, None].astype(out.dtype))

    y = lax.fori_loop(0, n_blocks, body, jnp.zeros_like(xt))
    return y.reshape(B, S, D)


def setup_inputs(seed: int = 0) -> dict:
    key = jax.random.key(seed)
    ks = jax.random.split(key, 32)
    f32 = jnp.float32
    nrm = lambda k, shape, fan_in: jax.random.normal(k, shape, f32) * (fan_in ** -0.5)
    gain = lambda k, n: 1.0 + 0.05 * jax.random.normal(k, (n,), f32)
    return {
        "x": jax.random.normal(ks[0], (BATCH, SEQ, D_MODEL), f32),
        "mem": jax.random.normal(ks[1], (BATCH, MEM_LEN, D_MODEL), f32),
        "g_mix": gain(ks[2], D_MODEL),
        "w_in": nrm(ks[3], (D_MODEL, IN_DIM), D_MODEL),
        "b_gates": 0.02 * jax.random.normal(ks[4], (GATE_DIM,), f32),
        "q_norm": gain(ks[5], DA_HEAD_DIM),
        "k_norm": gain(ks[6], DA_HEAD_DIM),
        "lambda_q1": 0.1 * jax.random.normal(ks[7], (DA_HEAD_DIM,), f32),
        "lambda_k1": 0.1 * jax.random.normal(ks[8], (DA_HEAD_DIM,), f32),
        "lambda_q2": 0.1 * jax.random.normal(ks[9], (DA_HEAD_DIM,), f32),
        "lambda_k2": 0.1 * jax.random.normal(ks[10], (DA_HEAD_DIM,), f32),
        "g_subln": gain(ks[11], DA_V_DIM),
        "pool_w": nrm(ks[12], (POOL_GROUPS, POOL_GROUP_DIM, POOL_GROUP_DIM), POOL_GROUP_DIM),
        "pool_scale": 1.0 + 0.1 * jax.random.normal(ks[13], (POOL_DIM,), f32),
        "g_mem": gain(ks[14], D_MODEL),
        "w_mem_kv": nrm(ks[15], (D_MODEL, 2 * MEM_DIM), D_MODEL),
        "mq_norm": gain(ks[16], MEM_HEAD_DIM),
        "mk_norm": gain(ks[17], MEM_HEAD_DIM),
        "w_b_attn": nrm(ks[18], (V_DIM, D_MODEL), V_DIM),
        "w_b_pool": nrm(ks[19], (POOL_DIM, D_MODEL), POOL_DIM),
        "w_b_mem": nrm(ks[20], (MEM_DIM, D_MODEL), MEM_DIM),
        "w_out": nrm(ks[21], (D_MODEL, D_MODEL), D_MODEL),
        "g_ffn": gain(ks[22], D_MODEL),
        "w_router": nrm(ks[23], (D_MODEL, N_EXPERTS), D_MODEL),
        "b_router": 0.01 * jax.random.normal(ks[24], (N_EXPERTS,), f32),
        "w_gate": nrm(ks[25], (N_EXPERTS, D_MODEL, D_EXPERT), D_MODEL),
        "b_gate": 0.02 * jax.random.normal(ks[26], (N_EXPERTS, D_EXPERT), f32),
        "w_up": nrm(ks[27], (N_EXPERTS, D_MODEL, D_EXPERT), D_MODEL),
        "b_up": 0.02 * jax.random.normal(ks[28], (N_EXPERTS, D_EXPERT), f32),
        "w_down": nrm(ks[29], (N_EXPERTS, D_EXPERT, D_MODEL), D_EXPERT),
        "b_down": 0.02 * jax.random.normal(ks[30], (N_EXPERTS, D_MODEL), f32),
    }


def reference(x, mem, g_mix, w_in, b_gates, q_norm, k_norm, lambda_q1, lambda_k1,
              lambda_q2, lambda_k2, g_subln, pool_w, pool_scale, g_mem, w_mem_kv,
              mq_norm, mk_norm, w_b_attn, w_b_pool, w_b_mem, w_out, g_ffn,
              w_router, b_router, w_gate, b_gate, w_up, b_up, w_down, b_down):
    B, S, D = x.shape
    inv_freq = ROPE_THETA ** (-jnp.arange(0, DA_HEAD_DIM, 2, dtype=jnp.float32) / DA_HEAD_DIM)
    ang = jnp.arange(S, dtype=jnp.float32)[:, None] * inv_freq[None, :]
    cos, sin = jnp.cos(ang), jnp.sin(ang)
    lam = (jnp.exp(jnp.sum(lambda_q1.astype(jnp.float32) * lambda_k1.astype(jnp.float32)))
           - jnp.exp(jnp.sum(lambda_q2.astype(jnp.float32) * lambda_k2.astype(jnp.float32)))
           + LAMBDA_INIT)

    h = x
    for _ in range(DEPTH):
        xn = rmsnorm(h, g_mix)
        proj = xn @ w_in
        q, k, v, u_pool, q_mem, gate_logits = jnp.split(proj, SPLITS, axis=-1)

        q = rotary(rmsnorm(q.reshape(B, S, DA_HEADS, 2, DA_HEAD_DIM), q_norm), cos, sin)
        k = rotary(rmsnorm(k.reshape(B, S, DA_HEADS, 2, DA_HEAD_DIM), k_norm), cos, sin)
        v = v.reshape(B, S, DA_HEADS, DA_V_DIM)
        o = diff_attention(q, k, v, lam)
        o = rmsnorm(o, g_subln) * (1.0 - LAMBDA_INIT)
        y_attn = o.reshape(B, S, V_DIM) @ w_b_attn

        y_pool = multiscale_pool(u_pool, pool_w, pool_scale) @ w_b_pool

        y_mem = memory_cross_attention(q_mem, mem, g_mem, w_mem_kv, mq_norm, mk_norm) @ w_b_mem

        g = jax.nn.sigmoid((gate_logits + b_gates).astype(jnp.float32)).reshape(B, S, N_BRANCHES, D)
        merged = (g[:, :, 0] * y_attn.astype(jnp.float32)
                  + g[:, :, 1] * y_pool.astype(jnp.float32)
                  + g[:, :, 2] * y_mem.astype(jnp.float32)).astype(h.dtype)
        h = h + merged @ w_out

        h = h + moe(rmsnorm(h, g_ffn), w_router, b_router, w_gate, b_gate,
                    w_up, b_up, w_down, b_down)
    return h
```

```python
import functools
import math

import jax
import jax.numpy as jnp
from jax import lax
from jax.experimental import pallas as pl
from jax.experimental.pallas import tpu as pltpu

DA_HEADS = 8
DA_HEAD_DIM = 64
DA_V_DIM = 2 * DA_HEAD_DIM
ROPE_THETA = 10000.0
POOL_WINDOWS = (2, 4, 8, 16)
POOL_GROUP_DIM = 128
POOL_HALO = 16
MEM_HEADS = 4
MEM_HEAD_DIM = 128
N_EXPERTS = 32
TOP_K = 4
SWIGLU_LIMIT = 7.0
SWIGLU_ALPHA = 1.702
LAMBDA_INIT = 0.8 - 0.6 * math.exp(-0.3 * 0.0)
EPS = 1e-6
NEG_INF = -1e30

LANES = 128
MXU_DIM = 256
VMEM_LIMIT_BYTES = 56 * 1024 * 1024

BF16 = jnp.bfloat16
F32 = jnp.float32


def _cparams(*sem):
    return pltpu.CompilerParams(dimension_semantics=sem, vmem_limit_bytes=VMEM_LIMIT_BYTES)


def _const_spec(shape):
    nd = len(shape)
    return pl.BlockSpec(shape, lambda *_: (0,) * nd)


def _dot(a, b):
    return jnp.dot(a, b, preferred_element_type=F32)


def _dot_nt(a, b):
    return lax.dot_general(a, b, (((1,), (1,)), ((), ())), preferred_element_type=F32)


def _rms_rows(x, gain):
    ms = jnp.mean(x * x, axis=-1, keepdims=True)
    return x * lax.rsqrt(ms + EPS) * gain


def _memkv_kernel(mem_ref, gmem_ref, w_ref, mkn_ref, km_ref, vm_ref):
    mem_dim = MEM_HEADS * MEM_HEAD_DIM
    mn = _rms_rows(mem_ref[0], gmem_ref[...]).astype(BF16)
    kv = _dot(mn, w_ref[...])
    for h in range(MEM_HEADS):
        sl = slice(h * MEM_HEAD_DIM, (h + 1) * MEM_HEAD_DIM)
        km_ref[0, :, sl] = _rms_rows(kv[:, sl], mkn_ref[...]).astype(BF16)
    vm_ref[0] = kv[:, mem_dim:].astype(BF16)


def _memkv(mem, g_mem, w_mem_kv, mk_norm):
    B, M, D = mem.shape
    mem_dim = MEM_HEADS * MEM_HEAD_DIM
    out = jax.ShapeDtypeStruct((B, M, mem_dim), BF16)
    return pl.pallas_call(
        _memkv_kernel,
        grid=(B,),
        in_specs=[pl.BlockSpec((1, M, D), lambda b: (b, 0, 0)),
                  _const_spec((1, D)),
                  _const_spec((D, 2 * mem_dim)),
                  _const_spec((1, MEM_HEAD_DIM))],
        out_specs=[pl.BlockSpec((1, M, mem_dim), lambda b: (b, 0, 0))] * 2,
        out_shape=[out, out],
        compiler_params=_cparams("arbitrary"),
        name="memkv",
    )(mem, g_mem.reshape(1, D), w_mem_kv.astype(BF16), mk_norm.reshape(1, MEM_HEAD_DIM))


def _swap32(x):
    lane = lax.broadcasted_iota(jnp.int32, x.shape, 1)
    low = (lane & 32) == 0
    return jnp.where(low, pltpu.roll(x, LANES - 32, 1), pltpu.roll(x, 32, 1))


def _inproj_kernel(x_ref, gmix_ref, w_ref, gq_ref, gk_ref, cos_ref, sin_ref, ones_ref,
                   poolw_ref, pscale_ref, km_ref, vm_ref, mqn_ref,
                   q_ref, k_ref, v_ref, pool_ref, memo_ref,
                   xn_ref, uext_ref, *, tm, tiles_per_seq, d_qk, d_v, d_pool, d_mem):
    i = pl.program_id(0)
    pos0 = (i % tiles_per_seq) * tm
    xn_ref[...] = _rms_rows(x_ref[...], gmix_ref[...]).astype(BF16)

    cos = cos_ref[...]
    sin = sin_ref[...]
    ones = ones_ref[...]

    def qk_segment(col0, gain_ref, out_ref, scale):
        gain = gain_ref[...]
        for j in range(d_qk // MXU_DIM):
            c = j * MXU_DIM
            p = _dot(xn_ref[...], w_ref[:, col0 + c:col0 + c + MXU_DIM])
            ss = _dot((p * p).astype(BF16), ones)
            n = p * (lax.rsqrt(ss * (1.0 / DA_HEAD_DIM) + EPS) * scale) * gain
            for half in range(MXU_DIM // LANES):
                nh = n[:, half * LANES:(half + 1) * LANES]
                r = nh * cos + _swap32(nh) * sin
                out_ref[:, c + half * LANES:c + (half + 1) * LANES] = r.astype(BF16)

    qk_segment(0, gq_ref, q_ref, 1.0 / math.sqrt(DA_HEAD_DIM))
    qk_segment(d_qk, gk_ref, k_ref, 1.0)

    col_v = 2 * d_qk
    for j in range(d_v // MXU_DIM):
        c = j * MXU_DIM
        v_ref[:, c:c + MXU_DIM] = _dot(xn_ref[...], w_ref[:, col_v + c:col_v + c + MXU_DIM]).astype(BF16)

    col_p = col_v + d_v

    @pl.when(pos0 == 0)
    def _():
        uext_ref[0:POOL_HALO, :] = jnp.zeros((POOL_HALO, d_pool), F32)

    for j in range(d_pool // MXU_DIM):
        c = j * MXU_DIM
        uext_ref[POOL_HALO:POOL_HALO + tm, c:c + MXU_DIM] = _dot(
            xn_ref[...], w_ref[:, col_p + c:col_p + c + MXU_DIM])
    t1 = pos0 + 1 + lax.broadcasted_iota(jnp.int32, (tm, POOL_GROUP_DIM), 0)
    for g, w in enumerate(POOL_WINDOWS):
        sl = slice(g * POOL_GROUP_DIM, (g + 1) * POOL_GROUP_DIM)
        u = uext_ref[POOL_HALO:POOL_HALO + tm, sl]
        acc = u
        for s in range(1, w):
            acc = acc + uext_ref[POOL_HALO - s:POOL_HALO - s + tm, sl]
        cnt = jnp.minimum(t1, w).astype(F32)
        z = acc / cnt - u
        zp = _dot(z.astype(BF16), poolw_ref[g])
        pool_ref[:, sl] = (zp * pscale_ref[:, sl]).astype(BF16)
    uext_ref[0:POOL_HALO, :] = uext_ref[tm:tm + POOL_HALO, :]

    col_m = col_p + d_pool
    inv_sqrt = 1.0 / math.sqrt(MEM_HEAD_DIM)
    for j in range(d_mem // MXU_DIM):
        c = j * MXU_DIM
        qm2 = _dot(xn_ref[...], w_ref[:, col_m + c:col_m + c + MXU_DIM])
        for half in range(MXU_DIM // LANES):
            sl = slice(c + half * LANES, c + (half + 1) * LANES)
            qn = (_rms_rows(qm2[:, half * LANES:(half + 1) * LANES], mqn_ref[...]) * inv_sqrt).astype(BF16)
            s = _dot_nt(qn, km_ref[0, :, sl])
            m = jnp.max(s, axis=-1, keepdims=True)
            p = jnp.exp(s - m)
            l = jnp.sum(p, axis=-1, keepdims=True)
            o = _dot(p.astype(BF16), vm_ref[0, :, sl])
            memo_ref[:, sl] = (o / l).astype(BF16)


def _inproj(x2, g_mix, w_in, q_norm, k_norm, cos_tab, sin_tab, pool_w, pool_scale,
            km, vm, mq_norm, *, seq, tm):
    T, D = x2.shape
    d_qk = DA_HEADS * 2 * DA_HEAD_DIM
    d_v = DA_HEADS * DA_V_DIM
    d_pool = len(POOL_WINDOWS) * POOL_GROUP_DIM
    d_mem = MEM_HEADS * MEM_HEAD_DIM
    d_all = 2 * d_qk + d_v + d_pool + d_mem
    M = km.shape[1]
    tiles_per_seq = seq // tm
    reps = MXU_DIM // DA_HEAD_DIM
    gq = jnp.tile(q_norm, reps).reshape(1, MXU_DIM)
    gk = jnp.tile(k_norm, reps).reshape(1, MXU_DIM)
    grp = jnp.arange(MXU_DIM) // DA_HEAD_DIM
    ones = (grp[:, None] == grp[None, :]).astype(BF16)
    kern = functools.partial(_inproj_kernel, tm=tm, tiles_per_seq=tiles_per_seq,
                             d_qk=d_qk, d_v=d_v, d_pool=d_pool, d_mem=d_mem)
    row = lambda width: pl.BlockSpec((tm, width), lambda i: (i, 0))
    return pl.pallas_call(
        kern,
        grid=(T // tm,),
        in_specs=[row(D),
                  _const_spec((1, D)),
                  _const_spec((D, d_all)),
                  _const_spec((1, MXU_DIM)),
                  _const_spec((1, MXU_DIM)),
                  pl.BlockSpec((tm, LANES), lambda i: (i % tiles_per_seq, 0)),
                  pl.BlockSpec((tm, LANES), lambda i: (i % tiles_per_seq, 0)),
                  _const_spec((MXU_DIM, MXU_DIM)),
                  _const_spec((len(POOL_WINDOWS), POOL_GROUP_DIM, POOL_GROUP_DIM)),
                  _const_spec((1, d_pool)),
                  pl.BlockSpec((1, M, d_mem), lambda i: (i // tiles_per_seq, 0, 0)),
                  pl.BlockSpec((1, M, d_mem), lambda i: (i // tiles_per_seq, 0, 0)),
                  _const_spec((1, MEM_HEAD_DIM))],
        out_specs=[row(d_qk), row(d_qk), row(d_v), row(d_pool), row(d_mem)],
        out_shape=[jax.ShapeDtypeStruct((T, d_qk), BF16),
                   jax.ShapeDtypeStruct((T, d_qk), BF16),
                   jax.ShapeDtypeStruct((T, d_v), BF16),
                   jax.ShapeDtypeStruct((T, d_pool), BF16),
                   jax.ShapeDtypeStruct((T, d_mem), BF16)],
        scratch_shapes=[pltpu.VMEM((tm, D), BF16),
                        pltpu.VMEM((tm + POOL_HALO, d_pool), F32)],
        compiler_params=_cparams("arbitrary"),
        name="inproj",
    )(x2, g_mix.reshape(1, D), w_in[:, :d_all].astype(BF16), gq, gk, cos_tab, sin_tab, ones,
      pool_w.astype(BF16), pool_scale.reshape(1, d_pool), km, vm, mq_norm.reshape(1, MEM_HEAD_DIM))


def _attn_kernel(lamp_ref, q_ref, k_ref, v_ref, gsub_ref, o_ref, m_ref, l_ref, acc_ref, *, seq, tq):
    lp = lamp_ref[...]
    lam = (jnp.exp(jnp.sum(lp[0:1] * lp[1:2], axis=-1, keepdims=True))
           - jnp.exp(jnp.sum(lp[2:3] * lp[3:4], axis=-1, keepdims=True)) + LAMBDA_INIT)
    lane = lax.broadcasted_iota(jnp.int32, (tq, DA_V_DIM), 1)
    comp_mask = (lane < DA_HEAD_DIM, lane >= DA_HEAD_DIM)
    row = lax.broadcasted_iota(jnp.int32, (tq, tq), 0)
    col = lax.broadcasted_iota(jnp.int32, (tq, tq), 1)
    causal = col <= row

    def q_block(qi, _):
        q0 = pl.multiple_of(qi * tq, tq)
        q = q_ref[pl.ds(q0, tq), :]
        qc = [jnp.where(comp_mask[c], q, jnp.zeros_like(q)) for c in range(2)]
        m_ref[...] = jnp.full(m_ref.shape, NEG_INF, F32)
        l_ref[...] = jnp.zeros(l_ref.shape, F32)
        acc_ref[...] = jnp.zeros(acc_ref.shape, F32)

        def kv_block(j, masked):
            k0 = pl.multiple_of(j * tq, tq)
            k = k_ref[pl.ds(k0, tq), :]
            v = v_ref[pl.ds(k0, tq), :]
            for c in range(2):
                s = _dot_nt(qc[c], k)
                if masked:
                    s = jnp.where(causal, s, NEG_INF)
                m_prev = m_ref[c]
                m_new = jnp.maximum(m_prev, jnp.max(s, axis=-1, keepdims=True))
                alpha = jnp.exp(m_prev - m_new)
                p = jnp.exp(s - m_new)
                l_ref[c] = alpha * l_ref[c] + jnp.sum(p, axis=-1, keepdims=True)
                acc_ref[c] = alpha * acc_ref[c] + _dot(p.astype(BF16), v)
                m_ref[c] = m_new

        def full_block(j, carry):
            kv_block(j, False)
            return carry

        lax.fori_loop(0, qi, full_block, 0)
        kv_block(qi, True)

        o = acc_ref[0] / l_ref[0] - lam * (acc_ref[1] / l_ref[1])
        o = _rms_rows(o, gsub_ref[...]) * (1.0 - LAMBDA_INIT)
        o_ref[pl.ds(q0, tq), :] = o.astype(BF16)
        return 0

    lax.fori_loop(0, seq // tq, q_block, 0)


def _attn(q, k, v, lam_params, g_subln, *, batch, seq, tq):
    T, d_v = v.shape
    blk = lambda: pl.BlockSpec((seq, DA_V_DIM), lambda b, h: (b, h))
    return pl.pallas_call(
        functools.partial(_attn_kernel, seq=seq, tq=tq),
        grid=(batch, DA_HEADS),
        in_specs=[_const_spec((4, DA_HEAD_DIM)), blk(), blk(), blk(), _const_spec((1, DA_V_DIM))],
        out_specs=blk(),
        out_shape=jax.ShapeDtypeStruct((T, d_v), BF16),
        scratch_shapes=[pltpu.VMEM((2, tq, 1), F32),
                        pltpu.VMEM((2, tq, 1), F32),
                        pltpu.VMEM((2, tq, DA_V_DIM), F32)],
        compiler_params=_cparams("arbitrary", "arbitrary"),
        name="attn",
    )(lam_params, q, k, v, g_subln.reshape(1, DA_V_DIM))


def _pack_bf16_pairs(a, b):
    hi = pltpu.bitcast(a.astype(BF16).astype(F32), jnp.uint32)
    lo = pltpu.bitcast(b.astype(BF16).astype(F32), jnp.uint32)
    return hi | (lo >> 16)


def _unpack_bf16_pairs(p):
    a = pltpu.bitcast(p & jnp.uint32(0xFFFF0000), F32)
    b = pltpu.bitcast(p << 16, F32)
    return a, b


def _merge_kernel(x_ref, oa_ref, pool_ref, memo_ref, gmix_ref, wg_ref, bg_ref, wba_ref, wbp_ref,
                  wbm_ref, wout_ref, gffn_ref, wrh_ref, wrl_ref, br_ref,
                  h_ref, hn_ref, idx_ref, rank_ref, gate_ref, cnt_ref,
                  xn_ref, mrg_ref, carry_ref, *, tm, d_model):
    i = pl.program_id(0)
    D = d_model

    @pl.when(i == 0)
    def _():
        carry_ref[...] = jnp.zeros(carry_ref.shape, F32)

    x = x_ref[...]
    xn_ref[...] = _rms_rows(x, gmix_ref[...]).astype(BF16)
    for j in range(D // MXU_DIM):
        sl = slice(j * MXU_DIM, (j + 1) * MXU_DIM)
        merged = None
        for br, (y_ref, w_ref) in enumerate(((oa_ref, wba_ref), (pool_ref, wbp_ref), (memo_ref, wbm_ref))):
            gsl = slice(br * D + j * MXU_DIM, br * D + (j + 1) * MXU_DIM)
            logit = _dot(xn_ref[...], wg_ref[:, gsl]) + bg_ref[:, gsl]
            gate = 1.0 / (1.0 + jnp.exp(-logit))
            term = gate * _dot(y_ref[...], w_ref[:, sl])
            merged = term if merged is None else merged + term
        mrg_ref[:, sl] = merged.astype(BF16)
    h = x + _dot(mrg_ref[...], wout_ref[...])
    h_ref[...] = h
    hn = _rms_rows(h, gffn_ref[...])
    hn_ref[...] = _pack_bf16_pairs(hn[:, :D // 2], hn[:, D // 2:])

    hn_hi = hn.astype(BF16)
    hn_lo = (hn - hn_hi.astype(F32)).astype(BF16)
    logits = (_dot(hn_hi, wrh_ref[...]) + _dot(hn_lo, wrh_ref[...]) + _dot(hn_hi, wrl_ref[...])
              + br_ref[...])

    lane = lax.broadcasted_iota(jnp.int32, (tm, LANES), 1)
    work = logits
    vals, sels = [], []
    idx_out = jnp.zeros((tm, LANES), jnp.int32)
    for kk in range(TOP_K):
        mx = jnp.max(work, axis=-1, keepdims=True)
        idx = jnp.min(jnp.where(work == mx, lane, LANES), axis=-1, keepdims=True)
        sel = lane == idx
        vals.append(mx)
        sels.append(sel)
        idx_out = jnp.where(lane == kk, idx, idx_out)
        work = jnp.where(sel, -jnp.inf, work)
    exps = [jnp.exp(v - vals[0]) for v in vals]
    denom = exps[0] + exps[1] + exps[2] + exps[3]
    gate_out = jnp.zeros((tm, LANES), F32)
    onehot = jnp.zeros((tm, LANES), F32)
    for kk in range(TOP_K):
        gate_out = jnp.where(lane == kk, exps[kk] / denom, gate_out)
        onehot = jnp.where(sels[kk], 1.0, onehot)

    r_i = lax.broadcasted_iota(jnp.int32, (tm, tm), 0)
    c_i = lax.broadcasted_iota(jnp.int32, (tm, tm), 1)
    lower = (c_i < r_i).astype(BF16)
    prefix = _dot(lower, onehot.astype(BF16)) + carry_ref[0:1, :]
    rank_out = jnp.zeros((tm, LANES), jnp.int32)
    for kk in range(TOP_K):
        rk = jnp.sum(jnp.where(sels[kk], prefix, 0.0), axis=-1, keepdims=True)
        rank_out = jnp.where(lane == kk, rk.astype(jnp.int32), rank_out)
    new_carry = prefix[tm - 1:tm, :] + onehot[tm - 1:tm, :]
    carry_ref[...] = jnp.broadcast_to(new_carry, carry_ref.shape)
    idx_ref[...] = idx_out
    rank_ref[...] = rank_out
    gate_ref[...] = gate_out
    cnt_ref[...] = jnp.broadcast_to(new_carry, cnt_ref.shape).astype(jnp.int32)


def _merge(x2, oa, pool, memo, g_mix, w_gates, b_gates, w_b_attn, w_b_pool, w_b_mem, w_out, g_ffn,
           w_router, b_router, *, tm):
    T, D = x2.shape
    d_pool = pool.shape[1]
    d_mem = memo.shape[1]
    E = w_router.shape[1]
    wr = jnp.zeros((D, LANES), F32).at[:, :E].set(w_router)
    wr_hi = wr.astype(BF16)
    wr_lo = (wr - wr_hi.astype(F32)).astype(BF16)
    br = jnp.full((1, LANES), NEG_INF, F32).at[0, :E].set(b_router)
    row = lambda width: pl.BlockSpec((tm, width), lambda i: (i, 0))
    sublanes = 8
    return pl.pallas_call(
        functools.partial(_merge_kernel, tm=tm, d_model=D),
        grid=(T // tm,),
        in_specs=[row(D), row(oa.shape[1]), row(d_pool), row(d_mem),
                  _const_spec((1, D)),
                  _const_spec((D, 3 * D)), _const_spec((1, 3 * D)),
                  _const_spec((oa.shape[1], D)), _const_spec((d_pool, D)), _const_spec((d_mem, D)),
                  _const_spec((D, D)), _const_spec((1, D)),
                  _const_spec((D, LANES)), _const_spec((D, LANES)), _const_spec((1, LANES))],
        out_specs=[row(D), row(D // 2), row(LANES), row(LANES), row(LANES),
                   _const_spec((sublanes, LANES))],
        out_shape=[jax.ShapeDtypeStruct((T, D), F32),
                   jax.ShapeDtypeStruct((T, D // 2), jnp.uint32),
                   jax.ShapeDtypeStruct((T, LANES), jnp.int32),
                   jax.ShapeDtypeStruct((T, LANES), jnp.int32),
                   jax.ShapeDtypeStruct((T, LANES), F32),
                   jax.ShapeDtypeStruct((sublanes, LANES), jnp.int32)],
        scratch_shapes=[pltpu.VMEM((tm, D), BF16),
                        pltpu.VMEM((tm, D), BF16),
                        pltpu.VMEM((sublanes, LANES), F32)],
        compiler_params=_cparams("arbitrary"),
        name="merge",
    )(x2, oa, pool, memo, g_mix.reshape(1, D), w_gates.astype(BF16), b_gates.reshape(1, 3 * D),
      w_b_attn.astype(BF16), w_b_pool.astype(BF16), w_b_mem.astype(BF16), w_out.astype(BF16),
      g_ffn.reshape(1, D), wr_hi, wr_lo, br)


def _dispatch_kernel(poff_ref, nv_ref, idx_ref, rank_ref, hn_ref, xs_ref, zero_ref, sem, zsem,
                     *, tm, bm, n_blocks):
    def row_copy(r, dest):
        return pltpu.make_async_copy(hn_ref.at[pl.ds(r, 1)], xs_ref.at[pl.ds(dest, 1)], sem)

    def zero_copy(blk):
        return pltpu.make_async_copy(zero_ref, xs_ref.at[pl.ds(pl.multiple_of(blk * bm, bm), bm)], zsem)

    @pl.when(pl.program_id(0) == 0)
    def _():
        zero_ref[...] = jnp.zeros(zero_ref.shape, zero_ref.dtype)

        def start(blk, n):
            @pl.when(nv_ref[blk] < bm)
            def _():
                zero_copy(blk).start()
            return n + (nv_ref[blk] < bm).astype(jnp.int32)

        n_zeroed = lax.fori_loop(0, n_blocks, start, 0)

        def finish(_, carry):
            zero_copy(0).wait()
            return carry

        lax.fori_loop(0, n_zeroed, finish, 0)

    def issue(r, carry):
        for kk in range(TOP_K):
            a = r * TOP_K + kk
            row_copy(r, poff_ref[idx_ref[a]] + rank_ref[a]).start()
        return carry

    lax.fori_loop(0, tm, issue, 0)
    for _ in range(TOP_K):
        pltpu.make_async_copy(hn_ref, xs_ref.at[pl.ds(0, tm)], sem).wait()


def _dispatch(poff, blk_valid, idx_flat, rank_flat, hn, *, bm, tm):
    T, half = hn.shape
    n_blocks = blk_valid.shape[0]
    flat = pl.BlockSpec((tm * TOP_K,), lambda i, poff, nv: (i,), memory_space=pltpu.SMEM)
    return pl.pallas_call(
        functools.partial(_dispatch_kernel, tm=tm, bm=bm, n_blocks=n_blocks),
        grid_spec=pltpu.PrefetchScalarGridSpec(
            num_scalar_prefetch=2,
            grid=(T // tm,),
            in_specs=[flat, flat, pl.BlockSpec((tm, half), lambda i, poff, nv: (i, 0))],
            out_specs=pl.BlockSpec(memory_space=pl.ANY),
            scratch_shapes=[pltpu.VMEM((bm, half), jnp.uint32),
                            pltpu.SemaphoreType.DMA, pltpu.SemaphoreType.DMA]),
        out_shape=jax.ShapeDtypeStruct((n_blocks * bm, half), jnp.uint32),
        compiler_params=_cparams("arbitrary"),
        name="dispatch",
    )(poff, blk_valid, idx_flat, rank_flat, hn)


def _experts_kernel(be_ref, nv_ref, src_ref, x_ref, wg_ref, bg_ref, wu_ref, bu_ref, wd_ref, bd_ref,
                    y_ref, act_ref, *, bm, d_model, d_expert):
    i = pl.program_id(0)
    half = d_model // 2

    @pl.when(nv_ref[i] == 0)
    def _():
        y_ref[...] = jnp.zeros(y_ref.shape, y_ref.dtype)

    @pl.when(nv_ref[i] > 0)
    def _():
        a, b = _unpack_bf16_pairs(x_ref[...])
        a = a.astype(BF16)
        b = b.astype(BF16)
        for j in range(d_expert // MXU_DIM):
            sl = slice(j * MXU_DIM, (j + 1) * MXU_DIM)
            g = _dot(a, wg_ref[0, :half, sl]) + _dot(b, wg_ref[0, half:, sl]) + bg_ref[0, :, sl]
            u = _dot(a, wu_ref[0, :half, sl]) + _dot(b, wu_ref[0, half:, sl]) + bu_ref[0, :, sl]
            g = jnp.minimum(g, SWIGLU_LIMIT)
            u = jnp.clip(u, -SWIGLU_LIMIT, SWIGLU_LIMIT)
            act = g * (1.0 / (1.0 + jnp.exp(-SWIGLU_ALPHA * g))) * (u + 1.0)
            act_ref[:, sl] = act.astype(BF16)
        out = _dot(act_ref[...], wd_ref[0]) + bd_ref[0]
        y_ref[...] = _pack_bf16_pairs(out[:, :half], out[:, half:])


def _experts(blk_e, blk_valid, blk_src, xs, w_gate, b_gate, w_up, b_up, w_down, b_down, *, bm):
    rows_sorted, half = xs.shape
    E, D, De = w_gate.shape
    n_blocks = rows_sorted // bm
    wspec = lambda r, c: pl.BlockSpec((1, r, c), lambda i, be, nv, src: (be[i], 0, 0))
    xspec = pl.BlockSpec((bm, half), lambda i, be, nv, src: (src[i], 0))
    return pl.pallas_call(
        functools.partial(_experts_kernel, bm=bm, d_model=D, d_expert=De),
        grid_spec=pltpu.PrefetchScalarGridSpec(
            num_scalar_prefetch=3,
            grid=(n_blocks,),
            in_specs=[xspec, wspec(D, De), wspec(1, De), wspec(D, De), wspec(1, De),
                      wspec(De, D), wspec(1, D)],
            out_specs=pl.BlockSpec((bm, half), lambda i, be, nv, src: (i, 0)),
            scratch_shapes=[pltpu.VMEM((bm, De), BF16)]),
        out_shape=jax.ShapeDtypeStruct((rows_sorted, half), jnp.uint32),
        compiler_params=_cparams("arbitrary"),
        name="experts",
    )(blk_e, blk_valid, blk_src, xs, w_gate.astype(BF16), b_gate.reshape(E, 1, De),
      w_up.astype(BF16), b_up.reshape(E, 1, De), w_down.astype(BF16), b_down.reshape(E, 1, D))


def _combine_kernel(poff_ref, idx_ref, rank_ref, gate_ref, h_ref, ys_ref, o_ref, buf_ref, sem, *, tm, d_model):
    half = d_model // 2

    def row_copy(src, kk, r):
        return pltpu.make_async_copy(ys_ref.at[pl.ds(src, 1)], buf_ref.at[kk, pl.ds(r, 1)], sem)

    def issue(r, carry):
        for kk in range(TOP_K):
            a = r * TOP_K + kk
            row_copy(poff_ref[idx_ref[a]] + rank_ref[a], kk, r).start()
        return carry

    lax.fori_loop(0, tm, issue, 0)
    for kk in range(TOP_K):
        pltpu.make_async_copy(ys_ref.at[pl.ds(0, tm)], buf_ref.at[kk], sem).wait()
    h = h_ref[...]
    lo, hi = h[:, :half], h[:, half:]
    gates = gate_ref[...]
    for kk in range(TOP_K):
        a, b = _unpack_bf16_pairs(buf_ref[kk])
        w = gates[:, kk:kk + 1]
        lo = lo + w * a
        hi = hi + w * b
    o_ref[:, :half] = lo
    o_ref[:, half:] = hi


def _combine(poff, idx_flat, rank_flat, gate, h, ys, *, tm):
    T, D = h.shape
    flat = pl.BlockSpec((tm * TOP_K,), lambda i, poff: (i,), memory_space=pltpu.SMEM)
    row = lambda width: pl.BlockSpec((tm, width), lambda i, poff: (i, 0))
    return pl.pallas_call(
        functools.partial(_combine_kernel, tm=tm, d_model=D),
        grid_spec=pltpu.PrefetchScalarGridSpec(
            num_scalar_prefetch=1,
            grid=(T // tm,),
            in_specs=[flat, flat, row(LANES), row(D), pl.BlockSpec(memory_space=pl.ANY)],
            out_specs=row(D),
            scratch_shapes=[pltpu.VMEM((TOP_K, tm, D // 2), jnp.uint32), pltpu.SemaphoreType.DMA]),
        out_shape=jax.ShapeDtypeStruct((T, D), F32),
        compiler_params=_cparams("arbitrary"),
        name="combine",
    )(poff, idx_flat, rank_flat, gate, h, ys)


def _block_tables(counts, *, bm, n_blocks):
    padded = ((counts + bm - 1) // bm) * bm
    pend = jnp.cumsum(padded)
    poff = pend - padded
    n_used = pend[-1] // bm
    src = jnp.minimum(jnp.arange(n_blocks, dtype=jnp.int32), n_used - 1)
    blk_e = jnp.minimum(jnp.searchsorted(pend, src * bm, side="right"), N_EXPERTS - 1).astype(jnp.int32)
    blk_valid = jnp.clip(poff[blk_e] + counts[blk_e] - src * bm, 0, bm)
    blk_valid = jnp.where(jnp.arange(n_blocks) < n_used, blk_valid, 0).astype(jnp.int32)
    return poff.astype(jnp.int32), blk_e, blk_valid, src.astype(jnp.int32)


def _rope_tables(seq):
    inv_freq = ROPE_THETA ** (-jnp.arange(0, DA_HEAD_DIM, 2, dtype=F32) / DA_HEAD_DIM)
    ang = jnp.arange(seq, dtype=F32)[:, None] * inv_freq[None, :]
    reps = LANES // (DA_HEAD_DIM // 2)
    cos = jnp.tile(jnp.cos(ang), (1, reps))
    sin = jnp.tile(jnp.sin(ang), (1, reps))
    first_half = (jnp.arange(LANES) % DA_HEAD_DIM) < DA_HEAD_DIM // 2
    return cos, jnp.where(first_half[None, :], -sin, sin)


ROW_TILE = 512
ATTN_TILE = 256
EXPERT_BLOCK = 512
MOVE_TILE = 256


def _forward(x, mem, g_mix, w_in, b_gates, q_norm, k_norm, lambda_q1, lambda_k1, lambda_q2, lambda_k2, g_subln, pool_w, pool_scale, g_mem, w_mem_kv, mq_norm, mk_norm, w_b_attn, w_b_pool, w_b_mem, w_out, g_ffn, w_router, b_router, w_gate, b_gate, w_up, b_up, w_down, b_down):
    B, S, D = x.shape
    T = B * S
    x2 = x.reshape(T, D)
    cos_tab, sin_tab = _rope_tables(S)
    km, vm = _memkv(mem, g_mem, w_mem_kv, mk_norm)
    q, k, v, pool, memo = _inproj(x2, g_mix, w_in, q_norm, k_norm, cos_tab, sin_tab, pool_w,
                                  pool_scale, km, vm, mq_norm, seq=S, tm=ROW_TILE)
    lam_params = jnp.stack([lambda_q1, lambda_k1, lambda_q2, lambda_k2])
    o = _attn(q, k, v, lam_params, g_subln, batch=B, seq=S, tq=ATTN_TILE)
    d_in = q.shape[1] + k.shape[1] + v.shape[1] + pool.shape[1] + memo.shape[1]
    h1, hn, idx, rank, gate, cnt = _merge(x2, o, pool, memo, g_mix, w_in[:, d_in:], b_gates, w_b_attn,
                                          w_b_pool, w_b_mem, w_out, g_ffn, w_router, b_router,
                                          tm=ROW_TILE)
    n_blocks = (T * TOP_K) // EXPERT_BLOCK + N_EXPERTS
    poff, blk_e, blk_valid, blk_src = _block_tables(cnt[0, :N_EXPERTS], bm=EXPERT_BLOCK, n_blocks=n_blocks)
    idx_flat = idx[:, :TOP_K].reshape(T * TOP_K)
    rank_flat = rank[:, :TOP_K].reshape(T * TOP_K)
    xs = _dispatch(poff, blk_valid, idx_flat, rank_flat, hn, bm=EXPERT_BLOCK, tm=MOVE_TILE)
    ys = _experts(blk_e, blk_valid, blk_src, xs, w_gate, b_gate, w_up, b_up, w_down, b_down,
                  bm=EXPERT_BLOCK)
    out = _combine(poff, idx_flat, rank_flat, gate, h1, ys, tm=MOVE_TILE)
    return dict(q=q, k=k, v=v, pool=pool, memo=memo, o=o, h1=h1, hn=hn, idx=idx, rank=rank,
                gate=gate, cnt=cnt, out=out.reshape(B, S, D))


def kernel(x, mem, g_mix, w_in, b_gates, q_norm, k_norm, lambda_q1, lambda_k1, lambda_q2, lambda_k2, g_subln, pool_w, pool_scale, g_mem, w_mem_kv, mq_norm, mk_norm, w_b_attn, w_b_pool, w_b_mem, w_out, g_ffn, w_router, b_router, w_gate, b_gate, w_up, b_up, w_down, b_down):
    return _forward(x, mem, g_mix, w_in, b_gates, q_norm, k_norm, lambda_q1, lambda_k1, lambda_q2,
                    lambda_k2, g_subln, pool_w, pool_scale, g_mem, w_mem_kv, mq_norm, mk_norm,
                    w_b_attn, w_b_pool, w_b_mem, w_out, g_ffn, w_router, b_router, w_gate, b_gate,
                    w_up, b_up, w_down, b_down)["out"]
```

```python
import functools
import math

import jax
import jax.numpy as jnp
from jax import lax
from jax.experimental import pallas as pl
from jax.experimental.pallas import tpu as pltpu

DA_HEADS = 8
DA_HEAD_DIM = 64
DA_V_DIM = 2 * DA_HEAD_DIM
ROPE_THETA = 10000.0
POOL_WINDOWS = (2, 4, 8, 16)
POOL_GROUP_DIM = 128
POOL_HALO = 16
MEM_HEADS = 4
MEM_HEAD_DIM = 128
N_EXPERTS = 32
TOP_K = 4
SWIGLU_LIMIT = 7.0
SWIGLU_ALPHA = 1.702
LAMBDA_INIT = 0.8 - 0.6 * math.exp(-0.3 * 0.0)
EPS = 1e-6
NEG_INF = -1e30

LANES = 128
MXU_DIM = 256
VMEM_LIMIT_BYTES = 56 * 1024 * 1024

BF16 = jnp.bfloat16
F32 = jnp.float32


def _cparams(*sem):
    return pltpu.CompilerParams(dimension_semantics=sem, vmem_limit_bytes=VMEM_LIMIT_BYTES)


def _const_spec(shape):
    nd = len(shape)
    return pl.BlockSpec(shape, lambda *_: (0,) * nd)


def _dot(a, b):
    return jnp.dot(a, b, preferred_element_type=F32)


def _dot_nt(a, b):
    return lax.dot_general(a, b, (((1,), (1,)), ((), ())), preferred_element_type=F32)


def _rms_rows(x, gain):
    ms = jnp.mean(x * x, axis=-1, keepdims=True)
    return x * lax.rsqrt(ms + EPS) * gain


def _memkv_kernel(mem_ref, gmem_ref, w_ref, mkn_ref, km_ref, vm_ref):
    mem_dim = MEM_HEADS * MEM_HEAD_DIM
    mn = _rms_rows(mem_ref[0], gmem_ref[...]).astype(BF16)
    kv = _dot(mn, w_ref[...])
    for h in range(MEM_HEADS):
        sl = slice(h * MEM_HEAD_DIM, (h + 1) * MEM_HEAD_DIM)
        km_ref[0, :, sl] = _rms_rows(kv[:, sl], mkn_ref[...]).astype(BF16)
    vm_ref[0] = kv[:, mem_dim:].astype(BF16)


def _memkv(mem, g_mem, w_mem_kv, mk_norm):
    B, M, D = mem.shape
    mem_dim = MEM_HEADS * MEM_HEAD_DIM
    out = jax.ShapeDtypeStruct((B, M, mem_dim), BF16)
    return pl.pallas_call(
        _memkv_kernel,
        grid=(B,),
        in_specs=[pl.BlockSpec((1, M, D), lambda b: (b, 0, 0)),
                  _const_spec((1, D)),
                  _const_spec((D, 2 * mem_dim)),
                  _const_spec((1, MEM_HEAD_DIM))],
        out_specs=[pl.BlockSpec((1, M, mem_dim), lambda b: (b, 0, 0))] * 2,
        out_shape=[out, out],
        compiler_params=_cparams("arbitrary"),
        name="memkv",
    )(mem, g_mem.reshape(1, D), w_mem_kv.astype(BF16), mk_norm.reshape(1, MEM_HEAD_DIM))


def _swap32(x):
    lane = lax.broadcasted_iota(jnp.int32, x.shape, 1)
    low = (lane & 32) == 0
    return jnp.where(low, pltpu.roll(x, LANES - 32, 1), pltpu.roll(x, 32, 1))


def _inproj_kernel(x_ref, gmix_ref, w_ref, gq_ref, gk_ref, cos_ref, sin_ref, ones_ref,
                   poolw_ref, pscale_ref, km_ref, vm_ref, mqn_ref,
                   q_ref, k_ref, v_ref, pool_ref, memo_ref,
                   xn_ref, uext_ref, *, tm, tiles_per_seq, d_qk, d_v, d_pool, d_mem):
    i = pl.program_id(0)
    pos0 = (i % tiles_per_seq) * tm
    xn_ref[...] = _rms_rows(x_ref[...], gmix_ref[...]).astype(BF16)

    cos = cos_ref[...]
    sin = sin_ref[...]
    ones = ones_ref[...]

    def qk_segment(col0, gain_ref, out_ref, scale):
        gain = gain_ref[...]
        for j in range(d_qk // MXU_DIM):
            c = j * MXU_DIM
            p = _dot(xn_ref[...], w_ref[:, col0 + c:col0 + c + MXU_DIM])
            ss = _dot((p * p).astype(BF16), ones)
            n = p * (lax.rsqrt(ss * (1.0 / DA_HEAD_DIM) + EPS) * scale) * gain
            for half in range(MXU_DIM // LANES):
                nh = n[:, half * LANES:(half + 1) * LANES]
                r = nh * cos + _swap32(nh) * sin
                out_ref[:, c + half * LANES:c + (half + 1) * LANES] = r.astype(BF16)

    qk_segment(0, gq_ref, q_ref, 1.0 / math.sqrt(DA_HEAD_DIM))
    qk_segment(d_qk, gk_ref, k_ref, 1.0)

    col_v = 2 * d_qk
    for j in range(d_v // MXU_DIM):
        c = j * MXU_DIM
        v_ref[:, c:c + MXU_DIM] = _dot(xn_ref[...], w_ref[:, col_v + c:col_v + c + MXU_DIM]).astype(BF16)

    col_p = col_v + d_v

    @pl.when(pos0 == 0)
    def _():
        uext_ref[0:POOL_HALO, :] = jnp.zeros((POOL_HALO, d_pool), F32)

    for j in range(d_pool // MXU_DIM):
        c = j * MXU_DIM
        uext_ref[POOL_HALO:POOL_HALO + tm, c:c + MXU_DIM] = _dot(
            xn_ref[...], w_ref[:, col_p + c:col_p + c + MXU_DIM])
    t1 = pos0 + 1 + lax.broadcasted_iota(jnp.int32, (tm, POOL_GROUP_DIM), 0)
    for g, w in enumerate(POOL_WINDOWS):
        sl = slice(g * POOL_GROUP_DIM, (g + 1) * POOL_GROUP_DIM)
        u = uext_ref[POOL_HALO:POOL_HALO + tm, sl]
        acc = u
        for s in range(1, w):
            acc = acc + uext_ref[POOL_HALO - s:POOL_HALO - s + tm, sl]
        cnt = jnp.minimum(t1, w).astype(F32)
        z = acc / cnt - u
        zp = _dot(z.astype(BF16), poolw_ref[g])
        pool_ref[:, sl] = (zp * pscale_ref[:, sl]).astype(BF16)
    uext_ref[0:POOL_HALO, :] = uext_ref[tm:tm + POOL_HALO, :]

    col_m = col_p + d_pool
    inv_sqrt = 1.0 / math.sqrt(MEM_HEAD_DIM)
    for j in range(d_mem // MXU_DIM):
        c = j * MXU_DIM
        qm2 = _dot(xn_ref[...], w_ref[:, col_m + c:col_m + c + MXU_DIM])
        for half in range(MXU_DIM // LANES):
            sl = slice(c + half * LANES, c + (half + 1) * LANES)
            qn = (_rms_rows(qm2[:, half * LANES:(half + 1) * LANES], mqn_ref[...]) * inv_sqrt).astype(BF16)
            s = _dot_nt(qn, km_ref[0, :, sl])
            m = jnp.max(s, axis=-1, keepdims=True)
            p = jnp.exp(s - m)
            l = jnp.sum(p, axis=-1, keepdims=True)
            o = _dot(p.astype(BF16), vm_ref[0, :, sl])
            memo_ref[:, sl] = (o / l).astype(BF16)


def _inproj(x2, g_mix, w_in, q_norm, k_norm, cos_tab, sin_tab, pool_w, pool_scale,
            km, vm, mq_norm, *, seq, tm):
    T, D = x2.shape
    d_qk = DA_HEADS * 2 * DA_HEAD_DIM
    d_v = DA_HEADS * DA_V_DIM
    d_pool = len(POOL_WINDOWS) * POOL_GROUP_DIM
    d_mem = MEM_HEADS * MEM_HEAD_DIM
    d_all = 2 * d_qk + d_v + d_pool + d_mem
    M = km.shape[1]
    tiles_per_seq = seq // tm
    reps = MXU_DIM // DA_HEAD_DIM
    gq = jnp.tile(q_norm, reps).reshape(1, MXU_DIM)
    gk = jnp.tile(k_norm, reps).reshape(1, MXU_DIM)
    grp = jnp.arange(MXU_DIM) // DA_HEAD_DIM
    ones = (grp[:, None] == grp[None, :]).astype(BF16)
    kern = functools.partial(_inproj_kernel, tm=tm, tiles_per_seq=tiles_per_seq,
                             d_qk=d_qk, d_v=d_v, d_pool=d_pool, d_mem=d_mem)
    row = lambda width: pl.BlockSpec((tm, width), lambda i: (i, 0))
    return pl.pallas_call(
        kern,
        grid=(T // tm,),
        in_specs=[row(D),
                  _const_spec((1, D)),
                  _const_spec((D, d_all)),
                  _const_spec((1, MXU_DIM)),
                  _const_spec((1, MXU_DIM)),
                  pl.BlockSpec((tm, LANES), lambda i: (i % tiles_per_seq, 0)),
                  pl.BlockSpec((tm, LANES), lambda i: (i % tiles_per_seq, 0)),
                  _const_spec((MXU_DIM, MXU_DIM)),
                  _const_spec((len(POOL_WINDOWS), POOL_GROUP_DIM, POOL_GROUP_DIM)),
                  _const_spec((1, d_pool)),
                  pl.BlockSpec((1, M, d_mem), lambda i: (i // tiles_per_seq, 0, 0)),
                  pl.BlockSpec((1, M, d_mem), lambda i: (i // tiles_per_seq, 0, 0)),
                  _const_spec((1, MEM_HEAD_DIM))],
        out_specs=[row(d_qk), row(d_qk), row(d_v), row(d_pool), row(d_mem)],
        out_shape=[jax.ShapeDtypeStruct((T, d_qk), BF16),
                   jax.ShapeDtypeStruct((T, d_qk), BF16),
                   jax.ShapeDtypeStruct((T, d_v), BF16),
                   jax.ShapeDtypeStruct((T, d_pool), BF16),
                   jax.ShapeDtypeStruct((T, d_mem), BF16)],
        scratch_shapes=[pltpu.VMEM((tm, D), BF16),
                        pltpu.VMEM((tm + POOL_HALO, d_pool), F32)],
        compiler_params=_cparams("arbitrary"),
        name="inproj",
    )(x2, g_mix.reshape(1, D), w_in[:, :d_all].astype(BF16), gq, gk, cos_tab, sin_tab, ones,
      pool_w.astype(BF16), pool_scale.reshape(1, d_pool), km, vm, mq_norm.reshape(1, MEM_HEAD_DIM))


def _attn_kernel(lamp_ref, q_ref, k_ref, v_ref, gsub_ref, o_ref, vt_ref, acc_ref,
                 sa_ref, sb_ref, pa_ref, pb_ref, lfin_ref, *, seq, tq):
    lp = lamp_ref[...]
    lam = (jnp.exp(jnp.sum(lp[0:1] * lp[1:2], axis=-1, keepdims=True))
           - jnp.exp(jnp.sum(lp[2:3] * lp[3:4], axis=-1, keepdims=True)) + LAMBDA_INIT)
    sub = lax.broadcasted_iota(jnp.int32, (DA_V_DIM, tq), 0)
    key_i = lax.broadcasted_iota(jnp.int32, (tq, 2 * tq), 0)
    qry_i = lax.broadcasted_iota(jnp.int32, (tq, 2 * tq), 1) % tq
    causal = key_i <= qry_i
    n_blk = seq // tq

    def transpose_v(j, carry):
        r0 = pl.multiple_of(j * tq, tq)
        vt_ref[j] = v_ref[pl.ds(r0, tq), :].astype(F32).T.astype(BF16)
        return carry

    lax.fori_loop(0, n_blk, transpose_v, 0)

    def q_block(qi, _):
        q0 = pl.multiple_of(qi * tq, tq)
        qt = q_ref[pl.ds(q0, tq), :].astype(F32).T
        qcat = jnp.concatenate([jnp.where(sub < DA_HEAD_DIM, qt, 0.0),
                                jnp.where(sub >= DA_HEAD_DIM, qt, 0.0)], axis=1).astype(BF16)
        acc_ref[...] = jnp.zeros(acc_ref.shape, F32)
        pb_ref[...] = jnp.zeros(pb_ref.shape, BF16)

        def scores(j, dst_ref):
            k0 = pl.multiple_of(j * tq, tq)
            dst_ref[...] = _dot(k_ref[pl.ds(k0, tq), :], qcat)

        def values(j, p_ref, alpha):
            acc_ref[...] = alpha * acc_ref[...] + _dot(vt_ref[j], p_ref[...])

        def step(j, state, s_cur, s_next, p_prev, p_cur, *, masked, prefetch):
            m_prev, l_prev, alpha_prev = state
            if prefetch:
                scores(j + 1, s_next)
            s = s_cur[...]
            if masked:
                s = jnp.where(causal, s, NEG_INF)
            m_new = jnp.maximum(m_prev, jnp.max(s, axis=0, keepdims=True))
            alpha = jnp.exp(m_prev - m_new)
            p = jnp.exp(s - m_new)
            l_new = alpha * l_prev + jnp.sum(p, axis=0, keepdims=True)
            p_cur[...] = p.astype(BF16)
            values(jnp.maximum(j - 1, 0), p_prev, alpha_prev)
            return m_new, l_new, alpha

        def pair(t, state):
            state = step(2 * t, state, sa_ref, sb_ref, pb_ref, pa_ref, masked=False, prefetch=True)
            return step(2 * t + 1, state, sb_ref, sa_ref, pa_ref, pb_ref, masked=False, prefetch=True)

        scores(0, sa_ref)
        init = (jnp.full((1, 2 * tq), NEG_INF, F32), jnp.zeros((1, 2 * tq), F32),
                jnp.ones((1, 2 * tq), F32))
        state = lax.fori_loop(0, qi // 2, pair, init)

        @pl.when(qi % 2 == 0)
        def _():
            _, l_fin, alpha = step(qi, state, sa_ref, sb_ref, pb_ref, pa_ref, masked=True, prefetch=False)
            values(qi, pa_ref, alpha)
            lfin_ref[...] = l_fin

        @pl.when(qi % 2 == 1)
        def _():
            st = step(qi - 1, state, sa_ref, sb_ref, pb_ref, pa_ref, masked=False, prefetch=True)
            _, l_fin, alpha = step(qi, st, sb_ref, sa_ref, pa_ref, pb_ref, masked=True, prefetch=False)
            values(qi, pb_ref, alpha)
            lfin_ref[...] = l_fin

        on = acc_ref[...] / lfin_ref[...]
        ot = on[:, :tq] - lam * on[:, tq:]
        ot = ot * lax.rsqrt(jnp.mean(ot * ot, axis=0, keepdims=True) + EPS)
        o = ot.T * (gsub_ref[...] * (1.0 - LAMBDA_INIT))
        o_ref[pl.ds(q0, tq), :] = o.astype(BF16)
        return 0

    lax.fori_loop(0, n_blk, q_block, 0)


def _attn(q, k, v, lam_params, g_subln, *, batch, seq, tq):
    T, d_v = v.shape
    blk = lambda: pl.BlockSpec((seq, DA_V_DIM), lambda b, h: (b, h))
    return pl.pallas_call(
        functools.partial(_attn_kernel, seq=seq, tq=tq),
        grid=(batch, DA_HEADS),
        in_specs=[_const_spec((4, DA_HEAD_DIM)), blk(), blk(), blk(), _const_spec((1, DA_V_DIM))],
        out_specs=blk(),
        out_shape=jax.ShapeDtypeStruct((T, d_v), BF16),
        scratch_shapes=[pltpu.VMEM((seq // tq, DA_V_DIM, tq), BF16),
                        pltpu.VMEM((DA_V_DIM, 2 * tq), F32),
                        pltpu.VMEM((tq, 2 * tq), F32), pltpu.VMEM((tq, 2 * tq), F32),
                        pltpu.VMEM((tq, 2 * tq), BF16), pltpu.VMEM((tq, 2 * tq), BF16),
                        pltpu.VMEM((1, 2 * tq), F32)],
        compiler_params=_cparams("arbitrary", "arbitrary"),
        name="attn",
    )(lam_params, q, k, v, g_subln.reshape(1, DA_V_DIM))


def _pack_bf16_pairs(a, b):
    hi = pltpu.bitcast(a.astype(BF16).astype(F32), jnp.uint32)
    lo = pltpu.bitcast(b.astype(BF16).astype(F32), jnp.uint32)
    return hi | (lo >> 16)


def _unpack_bf16_pairs(p):
    a = pltpu.bitcast(p & jnp.uint32(0xFFFF0000), F32)
    b = pltpu.bitcast(p << 16, F32)
    return a, b


def _merge_kernel(x_ref, oa_ref, pool_ref, memo_ref, gmix_ref, wg_ref, bg_ref, wba_ref, wbp_ref,
                  wbm_ref, wout_ref, gffn_ref, wrh_ref, wrl_ref, br_ref,
                  h_ref, hn_ref, idx_ref, rank_ref, gate_ref, cnt_ref,
                  xn_ref, mrg_ref, carry_ref, *, tm, d_model):
    i = pl.program_id(0)
    D = d_model

    @pl.when(i == 0)
    def _():
        carry_ref[...] = jnp.zeros(carry_ref.shape, F32)

    x = x_ref[...]
    xn_ref[...] = _rms_rows(x, gmix_ref[...]).astype(BF16)
    for j in range(D // MXU_DIM):
        sl = slice(j * MXU_DIM, (j + 1) * MXU_DIM)
        merged = None
        for br, (y_ref, w_ref) in enumerate(((oa_ref, wba_ref), (pool_ref, wbp_ref), (memo_ref, wbm_ref))):
            gsl = slice(br * D + j * MXU_DIM, br * D + (j + 1) * MXU_DIM)
            logit = _dot(xn_ref[...], wg_ref[:, gsl]) + bg_ref[:, gsl]
            gate = 1.0 / (1.0 + jnp.exp(-logit))
            term = gate * _dot(y_ref[...], w_ref[:, sl])
            merged = term if merged is None else merged + term
        mrg_ref[:, sl] = merged.astype(BF16)
    h = x + _dot(mrg_ref[...], wout_ref[...])
    h_ref[...] = h
    hn = _rms_rows(h, gffn_ref[...])
    hn_ref[...] = _pack_bf16_pairs(hn[:, :D // 2], hn[:, D // 2:])

    hn_hi = hn.astype(BF16)
    hn_lo = (hn - hn_hi.astype(F32)).astype(BF16)
    logits = (_dot(hn_hi, wrh_ref[...]) + _dot(hn_lo, wrh_ref[...]) + _dot(hn_hi, wrl_ref[...])
              + br_ref[...])

    lane = lax.broadcasted_iota(jnp.int32, (tm, LANES), 1)
    work = logits
    vals, sels = [], []
    idx_out = jnp.zeros((tm, LANES), jnp.int32)
    for kk in range(TOP_K):
        mx = jnp.max(work, axis=-1, keepdims=True)
        idx = jnp.min(jnp.where(work == mx, lane, LANES), axis=-1, keepdims=True)
        sel = lane == idx
        vals.append(mx)
        sels.append(sel)
        idx_out = jnp.where(lane == kk, idx, idx_out)
        work = jnp.where(sel, -jnp.inf, work)
    exps = [jnp.exp(v - vals[0]) for v in vals]
    denom = exps[0] + exps[1] + exps[2] + exps[3]
    gate_out = jnp.zeros((tm, LANES), F32)
    onehot = jnp.zeros((tm, LANES), F32)
    for kk in range(TOP_K):
        gate_out = jnp.where(lane == kk, exps[kk] / denom, gate_out)
        onehot = jnp.where(sels[kk], 1.0, onehot)

    r_i = lax.broadcasted_iota(jnp.int32, (tm, tm), 0)
    c_i = lax.broadcasted_iota(jnp.int32, (tm, tm), 1)
    lower = (c_i < r_i).astype(BF16)
    prefix = _dot(lower, onehot.astype(BF16)) + carry_ref[0:1, :]
    rank_out = jnp.zeros((tm, LANES), jnp.int32)
    for kk in range(TOP_K):
        rk = jnp.sum(jnp.where(sels[kk], prefix, 0.0), axis=-1, keepdims=True)
        rank_out = jnp.where(lane == kk, rk.astype(jnp.int32), rank_out)
    new_carry = prefix[tm - 1:tm, :] + onehot[tm - 1:tm, :]
    carry_ref[...] = jnp.broadcast_to(new_carry, carry_ref.shape)
    idx_ref[...] = idx_out
    rank_ref[...] = rank_out
    gate_ref[...] = gate_out
    cnt_ref[...] = jnp.broadcast_to(new_carry, cnt_ref.shape).astype(jnp.int32)


def _merge(x2, oa, pool, memo, g_mix, w_gates, b_gates, w_b_attn, w_b_pool, w_b_mem, w_out, g_ffn,
           w_router, b_router, *, tm):
    T, D = x2.shape
    d_pool = pool.shape[1]
    d_mem = memo.shape[1]
    E = w_router.shape[1]
    wr = jnp.zeros((D, LANES), F32).at[:, :E].set(w_router)
    wr_hi = wr.astype(BF16)
    wr_lo = (wr - wr_hi.astype(F32)).astype(BF16)
    br = jnp.full((1, LANES), NEG_INF, F32).at[0, :E].set(b_router)
    row = lambda width: pl.BlockSpec((tm, width), lambda i: (i, 0))
    sublanes = 8
    return pl.pallas_call(
        functools.partial(_merge_kernel, tm=tm, d_model=D),
        grid=(T // tm,),
        in_specs=[row(D), row(oa.shape[1]), row(d_pool), row(d_mem),
                  _const_spec((1, D)),
                  _const_spec((D, 3 * D)), _const_spec((1, 3 * D)),
                  _const_spec((oa.shape[1], D)), _const_spec((d_pool, D)), _const_spec((d_mem, D)),
                  _const_spec((D, D)), _const_spec((1, D)),
                  _const_spec((D, LANES)), _const_spec((D, LANES)), _const_spec((1, LANES))],
        out_specs=[row(D), row(D // 2), row(LANES), row(LANES), row(LANES),
                   _const_spec((sublanes, LANES))],
        out_shape=[jax.ShapeDtypeStruct((T, D), F32),
                   jax.ShapeDtypeStruct((T, D // 2), jnp.uint32),
                   jax.ShapeDtypeStruct((T, LANES), jnp.int32),
                   jax.ShapeDtypeStruct((T, LANES), jnp.int32),
                   jax.ShapeDtypeStruct((T, LANES), F32),
                   jax.ShapeDtypeStruct((sublanes, LANES), jnp.int32)],
        scratch_shapes=[pltpu.VMEM((tm, D), BF16),
                        pltpu.VMEM((tm, D), BF16),
                        pltpu.VMEM((sublanes, LANES), F32)],
        compiler_params=_cparams("arbitrary"),
        name="merge",
    )(x2, oa, pool, memo, g_mix.reshape(1, D), w_gates.astype(BF16), b_gates.reshape(1, 3 * D),
      w_b_attn.astype(BF16), w_b_pool.astype(BF16), w_b_mem.astype(BF16), w_out.astype(BF16),
      g_ffn.reshape(1, D), wr_hi, wr_lo, br)


def _dispatch_kernel(poff_ref, nv_ref, idx_ref, rank_ref, hn_ref, xs_ref, zero_ref, sem, zsem,
                     *, tm, bm, n_blocks):
    def row_copy(r, dest):
        return pltpu.make_async_copy(hn_ref.at[pl.ds(r, 1)], xs_ref.at[pl.ds(dest, 1)], sem)

    def zero_copy(blk):
        return pltpu.make_async_copy(zero_ref, xs_ref.at[pl.ds(pl.multiple_of(blk * bm, bm), bm)], zsem)

    @pl.when(pl.program_id(0) == 0)
    def _():
        zero_ref[...] = jnp.zeros(zero_ref.shape, zero_ref.dtype)

        def start(blk, n):
            @pl.when(nv_ref[blk] < bm)
            def _():
                zero_copy(blk).start()
            return n + (nv_ref[blk] < bm).astype(jnp.int32)

        n_zeroed = lax.fori_loop(0, n_blocks, start, 0)

        def finish(_, carry):
            zero_copy(0).wait()
            return carry

        lax.fori_loop(0, n_zeroed, finish, 0)

    def issue(r, carry):
        for kk in range(TOP_K):
            a = r * TOP_K + kk
            row_copy(r, poff_ref[idx_ref[a]] + rank_ref[a]).start()
        return carry

    lax.fori_loop(0, tm, issue, 0)
    for _ in range(TOP_K):
        pltpu.make_async_copy(hn_ref, xs_ref.at[pl.ds(0, tm)], sem).wait()


def _dispatch(poff, blk_valid, idx_flat, rank_flat, hn, *, bm, tm):
    T, half = hn.shape
    n_blocks = blk_valid.shape[0]
    flat = pl.BlockSpec((tm * TOP_K,), lambda i, poff, nv: (i,), memory_space=pltpu.SMEM)
    return pl.pallas_call(
        functools.partial(_dispatch_kernel, tm=tm, bm=bm, n_blocks=n_blocks),
        grid_spec=pltpu.PrefetchScalarGridSpec(
            num_scalar_prefetch=2,
            grid=(T // tm,),
            in_specs=[flat, flat, pl.BlockSpec((tm, half), lambda i, poff, nv: (i, 0))],
            out_specs=pl.BlockSpec(memory_space=pl.ANY),
            scratch_shapes=[pltpu.VMEM((bm, half), jnp.uint32),
                            pltpu.SemaphoreType.DMA, pltpu.SemaphoreType.DMA]),
        out_shape=jax.ShapeDtypeStruct((n_blocks * bm, half), jnp.uint32),
        compiler_params=_cparams("arbitrary"),
        name="dispatch",
    )(poff, blk_valid, idx_flat, rank_flat, hn)


def _experts_kernel(be_ref, nv_ref, src_ref, x_ref, wg_ref, bg_ref, wu_ref, bu_ref, wd_ref, bd_ref,
                    y_ref, act_ref, *, bm, d_model, d_expert):
    i = pl.program_id(0)
    half = d_model // 2

    @pl.when(nv_ref[i] == 0)
    def _():
        y_ref[...] = jnp.zeros(y_ref.shape, y_ref.dtype)

    @pl.when(nv_ref[i] > 0)
    def _():
        a, b = _unpack_bf16_pairs(x_ref[...])
        a = a.astype(BF16)
        b = b.astype(BF16)
        for j in range(d_expert // MXU_DIM):
            sl = slice(j * MXU_DIM, (j + 1) * MXU_DIM)
            g = _dot(a, wg_ref[0, :half, sl]) + _dot(b, wg_ref[0, half:, sl]) + bg_ref[0, :, sl]
            u = _dot(a, wu_ref[0, :half, sl]) + _dot(b, wu_ref[0, half:, sl]) + bu_ref[0, :, sl]
            g = jnp.minimum(g, SWIGLU_LIMIT)
            u = jnp.clip(u, -SWIGLU_LIMIT, SWIGLU_LIMIT)
            act = g * (1.0 / (1.0 + jnp.exp(-SWIGLU_ALPHA * g))) * (u + 1.0)
            act_ref[:, sl] = act.astype(BF16)
        out = _dot(act_ref[...], wd_ref[0]) + bd_ref[0]
        y_ref[...] = _pack_bf16_pairs(out[:, :half], out[:, half:])


def _experts(blk_e, blk_valid, blk_src, xs, w_gate, b_gate, w_up, b_up, w_down, b_down, *, bm):
    rows_sorted, half = xs.shape
    E, D, De = w_gate.shape
    n_blocks = rows_sorted // bm
    wspec = lambda r, c: pl.BlockSpec((1, r, c), lambda i, be, nv, src: (be[i], 0, 0))
    xspec = pl.BlockSpec((bm, half), lambda i, be, nv, src: (src[i], 0))
    return pl.pallas_call(
        functools.partial(_experts_kernel, bm=bm, d_model=D, d_expert=De),
        grid_spec=pltpu.PrefetchScalarGridSpec(
            num_scalar_prefetch=3,
            grid=(n_blocks,),
            in_specs=[xspec, wspec(D, De), wspec(1, De), wspec(D, De), wspec(1, De),
                      wspec(De, D), wspec(1, D)],
            out_specs=pl.BlockSpec((bm, half), lambda i, be, nv, src: (i, 0)),
            scratch_shapes=[pltpu.VMEM((bm, De), BF16)]),
        out_shape=jax.ShapeDtypeStruct((rows_sorted, half), jnp.uint32),
        compiler_params=_cparams("arbitrary"),
        name="experts",
    )(blk_e, blk_valid, blk_src, xs, w_gate.astype(BF16), b_gate.reshape(E, 1, De),
      w_up.astype(BF16), b_up.reshape(E, 1, De), w_down.astype(BF16), b_down.reshape(E, 1, D))


def _combine_kernel(poff_ref, idx_ref, rank_ref, gate_ref, h_ref, ys_ref, o_ref, buf_ref, sem, *, tm, d_model):
    half = d_model // 2

    def row_copy(src, kk, r):
        return pltpu.make_async_copy(ys_ref.at[pl.ds(src, 1)], buf_ref.at[kk, pl.ds(r, 1)], sem)

    def issue(r, carry):
        for kk in range(TOP_K):
            a = r * TOP_K + kk
            row_copy(poff_ref[idx_ref[a]] + rank_ref[a], kk, r).start()
        return carry

    lax.fori_loop(0, tm, issue, 0)
    for kk in range(TOP_K):
        pltpu.make_async_copy(ys_ref.at[pl.ds(0, tm)], buf_ref.at[kk], sem).wait()
    h = h_ref[...]
    lo, hi = h[:, :half], h[:, half:]
    gates = gate_ref[...]
    for kk in range(TOP_K):
        a, b = _unpack_bf16_pairs(buf_ref[kk])
        w = gates[:, kk:kk + 1]
        lo = lo + w * a
        hi = hi + w * b
    o_ref[:, :half] = lo
    o_ref[:, half:] = hi


def _combine(poff, idx_flat, rank_flat, gate, h, ys, *, tm):
    T, D = h.shape
    flat = pl.BlockSpec((tm * TOP_K,), lambda i, poff: (i,), memory_space=pltpu.SMEM)
    row = lambda width: pl.BlockSpec((tm, width), lambda i, poff: (i, 0))
    return pl.pallas_call(
        functools.partial(_combine_kernel, tm=tm, d_model=D),
        grid_spec=pltpu.PrefetchScalarGridSpec(
            num_scalar_prefetch=1,
            grid=(T // tm,),
            in_specs=[flat, flat, row(LANES), row(D), pl.BlockSpec(memory_space=pl.ANY)],
            out_specs=row(D),
            scratch_shapes=[pltpu.VMEM((TOP_K, tm, D // 2), jnp.uint32), pltpu.SemaphoreType.DMA]),
        out_shape=jax.ShapeDtypeStruct((T, D), F32),
        compiler_params=_cparams("arbitrary"),
        name="combine",
    )(poff, idx_flat, rank_flat, gate, h, ys)


def _block_tables(counts, *, bm, n_blocks):
    padded = ((counts + bm - 1) // bm) * bm
    pend = jnp.cumsum(padded)
    poff = pend - padded
    n_used = pend[-1] // bm
    src = jnp.minimum(jnp.arange(n_blocks, dtype=jnp.int32), n_used - 1)
    blk_e = jnp.sum(pend[None, :] <= (src * bm)[:, None], axis=1)
    blk_e = jnp.minimum(blk_e, N_EXPERTS - 1).astype(jnp.int32)
    blk_valid = jnp.clip(poff[blk_e] + counts[blk_e] - src * bm, 0, bm)
    blk_valid = jnp.where(jnp.arange(n_blocks) < n_used, blk_valid, 0).astype(jnp.int32)
    return poff.astype(jnp.int32), blk_e, blk_valid, src.astype(jnp.int32)


def _rope_tables(seq):
    inv_freq = ROPE_THETA ** (-jnp.arange(0, DA_HEAD_DIM, 2, dtype=F32) / DA_HEAD_DIM)
    ang = jnp.arange(seq, dtype=F32)[:, None] * inv_freq[None, :]
    reps = LANES // (DA_HEAD_DIM // 2)
    cos = jnp.tile(jnp.cos(ang), (1, reps))
    sin = jnp.tile(jnp.sin(ang), (1, reps))
    first_half = (jnp.arange(LANES) % DA_HEAD_DIM) < DA_HEAD_DIM // 2
    return cos, jnp.where(first_half[None, :], -sin, sin)


ROW_TILE = 512
ATTN_TILE = 256
EXPERT_BLOCK = 512
MOVE_TILE = 256


def _forward(x, mem, g_mix, w_in, b_gates, q_norm, k_norm, lambda_q1, lambda_k1, lambda_q2, lambda_k2, g_subln, pool_w, pool_scale, g_mem, w_mem_kv, mq_norm, mk_norm, w_b_attn, w_b_pool, w_b_mem, w_out, g_ffn, w_router, b_router, w_gate, b_gate, w_up, b_up, w_down, b_down):
    B, S, D = x.shape
    T = B * S
    x2 = x.reshape(T, D)
    cos_tab, sin_tab = _rope_tables(S)
    km, vm = _memkv(mem, g_mem, w_mem_kv, mk_norm)
    q, k, v, pool, memo = _inproj(x2, g_mix, w_in, q_norm, k_norm, cos_tab, sin_tab, pool_w,
                                  pool_scale, km, vm, mq_norm, seq=S, tm=ROW_TILE)
    lam_params = jnp.stack([lambda_q1, lambda_k1, lambda_q2, lambda_k2])
    o = _attn(q, k, v, lam_params, g_subln, batch=B, seq=S, tq=ATTN_TILE)
    d_in = q.shape[1] + k.shape[1] + v.shape[1] + pool.shape[1] + memo.shape[1]
    h1, hn, idx, rank, gate, cnt = _merge(x2, o, pool, memo, g_mix, w_in[:, d_in:], b_gates, w_b_attn,
                                          w_b_pool, w_b_mem, w_out, g_ffn, w_router, b_router,
                                          tm=ROW_TILE)
    n_blocks = (T * TOP_K) // EXPERT_BLOCK + N_EXPERTS
    poff, blk_e, blk_valid, blk_src = _block_tables(cnt[0, :N_EXPERTS], bm=EXPERT_BLOCK, n_blocks=n_blocks)
    idx_flat = idx[:, :TOP_K].reshape(T * TOP_K)
    rank_flat = rank[:, :TOP_K].reshape(T * TOP_K)
    xs = _dispatch(poff, blk_valid, idx_flat, rank_flat, hn, bm=EXPERT_BLOCK, tm=MOVE_TILE)
    ys = _experts(blk_e, blk_valid, blk_src, xs, w_gate, b_gate, w_up, b_up, w_down, b_down,
                  bm=EXPERT_BLOCK)
    out = _combine(poff, idx_flat, rank_flat, gate, h1, ys, tm=MOVE_TILE)
    return dict(q=q, k=k, v=v, pool=pool, memo=memo, o=o, h1=h1, hn=hn, idx=idx, rank=rank,
                gate=gate, cnt=cnt, out=out.reshape(B, S, D))


def kernel(x, mem, g_mix, w_in, b_gates, q_norm, k_norm, lambda_q1, lambda_k1, lambda_q2, lambda_k2, g_subln, pool_w, pool_scale, g_mem, w_mem_kv, mq_norm, mk_norm, w_b_attn, w_b_pool, w_b_mem, w_out, g_ffn, w_router, b_router, w_gate, b_gate, w_up, b_up, w_down, b_down):
    return _forward(x, mem, g_mix, w_in, b_gates, q_norm, k_norm, lambda_q1, lambda_k1, lambda_q2,
                    lambda_k2, g_subln, pool_w, pool_scale, g_mem, w_mem_kv, mq_norm, mk_norm,
                    w_b_attn, w_b_pool, w_b_mem, w_out, g_ffn, w_router, b_router, w_gate, b_gate,
                    w_up, b_up, w_down, b_down)["out"]
```

```python
import functools
import math

import jax
import jax.numpy as jnp
from jax import lax
from jax.experimental import pallas as pl
from jax.experimental.pallas import tpu as pltpu

DA_HEADS = 8
DA_HEAD_DIM = 64
DA_V_DIM = 2 * DA_HEAD_DIM
ROPE_THETA = 10000.0
POOL_WINDOWS = (2, 4, 8, 16)
POOL_GROUP_DIM = 128
POOL_HALO = 16
MEM_HEADS = 4
MEM_HEAD_DIM = 128
N_EXPERTS = 32
TOP_K = 4
SWIGLU_LIMIT = 7.0
SWIGLU_ALPHA = 1.702
LAMBDA_INIT = 0.8 - 0.6 * math.exp(-0.3 * 0.0)
EPS = 1e-6
NEG_INF = -1e30
QUERY_SCALE = math.log2(math.e) / math.sqrt(DA_HEAD_DIM)

LANES = 128
MXU_DIM = 256
VMEM_LIMIT_BYTES = 56 * 1024 * 1024

BF16 = jnp.bfloat16
F32 = jnp.float32


def _cparams(*sem):
    return pltpu.CompilerParams(dimension_semantics=sem, vmem_limit_bytes=VMEM_LIMIT_BYTES)


def _const_spec(shape):
    nd = len(shape)
    return pl.BlockSpec(shape, lambda *_: (0,) * nd)


def _dot(a, b):
    return jnp.dot(a, b, preferred_element_type=F32)


def _dot_nt(a, b):
    return lax.dot_general(a, b, (((1,), (1,)), ((), ())), preferred_element_type=F32)


def _rms_rows(x, gain):
    ms = jnp.mean(x * x, axis=-1, keepdims=True)
    return x * lax.rsqrt(ms + EPS) * gain


def _memkv_kernel(mem_ref, gmem_ref, w_ref, mkn_ref, km_ref, vm_ref):
    mem_dim = MEM_HEADS * MEM_HEAD_DIM
    mn = _rms_rows(mem_ref[0], gmem_ref[...]).astype(BF16)
    kv = _dot(mn, w_ref[...])
    for h in range(MEM_HEADS):
        sl = slice(h * MEM_HEAD_DIM, (h + 1) * MEM_HEAD_DIM)
        km_ref[0, :, sl] = _rms_rows(kv[:, sl], mkn_ref[...]).astype(BF16)
    vm_ref[0] = kv[:, mem_dim:].astype(BF16)


def _memkv(mem, g_mem, w_mem_kv, mk_norm):
    B, M, D = mem.shape
    mem_dim = MEM_HEADS * MEM_HEAD_DIM
    out = jax.ShapeDtypeStruct((B, M, mem_dim), BF16)
    return pl.pallas_call(
        _memkv_kernel,
        grid=(B,),
        in_specs=[pl.BlockSpec((1, M, D), lambda b: (b, 0, 0)),
                  _const_spec((1, D)),
                  _const_spec((D, 2 * mem_dim)),
                  _const_spec((1, MEM_HEAD_DIM))],
        out_specs=[pl.BlockSpec((1, M, mem_dim), lambda b: (b, 0, 0))] * 2,
        out_shape=[out, out],
        compiler_params=_cparams("arbitrary"),
        name="memkv",
    )(mem, g_mem.reshape(1, D), w_mem_kv.astype(BF16), mk_norm.reshape(1, MEM_HEAD_DIM))


def _swap32(x):
    lane = lax.broadcasted_iota(jnp.int32, x.shape, 1)
    low = (lane & 32) == 0
    return jnp.where(low, pltpu.roll(x, LANES - 32, 1), pltpu.roll(x, 32, 1))


def _inproj_kernel(x_ref, gmix_ref, w_ref, gq_ref, gk_ref, cos_ref, sin_ref, ones_ref,
                   poolw_ref, pscale_ref, km_ref, vm_ref, mqn_ref,
                   q_ref, k_ref, v_ref, pool_ref, memo_ref,
                   xn_ref, uext_ref, *, tm, tiles_per_seq, d_qk, d_v, d_pool, d_mem):
    i = pl.program_id(0)
    pos0 = (i % tiles_per_seq) * tm
    xn_ref[...] = _rms_rows(x_ref[...], gmix_ref[...]).astype(BF16)

    cos = cos_ref[...]
    sin = sin_ref[...]
    ones = ones_ref[...]

    def qk_slab(p, c, gain_ref, out_ref, scale):
        ss = _dot((p * p).astype(BF16), ones)
        n = p * (lax.rsqrt(ss * (1.0 / DA_HEAD_DIM) + EPS) * scale) * gain_ref[...]
        for half in range(MXU_DIM // LANES):
            nh = n[:, half * LANES:(half + 1) * LANES]
            r = nh * cos + _swap32(nh) * sin
            out_ref[:, c + half * LANES:c + (half + 1) * LANES] = r.astype(BF16)

    def v_slab(p, c):
        v_ref[:, c:c + MXU_DIM] = p.astype(BF16)

    def pool_slab(p, c):
        @pl.when(pos0 == 0)
        def _():
            uext_ref[0:POOL_HALO, c:c + MXU_DIM] = jnp.zeros((POOL_HALO, MXU_DIM), F32)

        uext_ref[POOL_HALO:POOL_HALO + tm, c:c + MXU_DIM] = p
        t1 = pos0 + 1 + lax.broadcasted_iota(jnp.int32, (tm, POOL_GROUP_DIM), 0)
        for g in range(c // POOL_GROUP_DIM, (c + MXU_DIM) // POOL_GROUP_DIM):
            w = POOL_WINDOWS[g]
            sl = slice(g * POOL_GROUP_DIM, (g + 1) * POOL_GROUP_DIM)
            u = uext_ref[POOL_HALO:POOL_HALO + tm, sl]
            acc = u
            for s in range(1, w):
                acc = acc + uext_ref[POOL_HALO - s:POOL_HALO - s + tm, sl]
            cnt = jnp.minimum(t1, w).astype(F32)
            z = acc / cnt - u
            zp = _dot(z.astype(BF16), poolw_ref[g])
            pool_ref[:, sl] = (zp * pscale_ref[:, sl]).astype(BF16)
        uext_ref[0:POOL_HALO, c:c + MXU_DIM] = uext_ref[tm:tm + POOL_HALO, c:c + MXU_DIM]

    def mem_slab(p, c):
        inv_sqrt = 1.0 / math.sqrt(MEM_HEAD_DIM)
        for half in range(MXU_DIM // LANES):
            sl = slice(c + half * LANES, c + (half + 1) * LANES)
            qn = (_rms_rows(p[:, half * LANES:(half + 1) * LANES], mqn_ref[...]) * inv_sqrt).astype(BF16)
            s = _dot_nt(qn, km_ref[0, :, sl])
            m = jnp.max(s, axis=-1, keepdims=True)
            e = jnp.exp(s - m)
            l = jnp.sum(e, axis=-1, keepdims=True)
            o = _dot(e.astype(BF16), vm_ref[0, :, sl])
            memo_ref[:, sl] = (o / l).astype(BF16)

    slabs = []
    col = 0
    for c in range(0, d_qk, MXU_DIM):
        slabs.append((col + c, functools.partial(qk_slab, c=c, gain_ref=gq_ref, out_ref=q_ref,
                                                 scale=QUERY_SCALE)))
    col += d_qk
    for c in range(0, d_qk, MXU_DIM):
        slabs.append((col + c, functools.partial(qk_slab, c=c, gain_ref=gk_ref, out_ref=k_ref, scale=1.0)))
    col += d_qk
    for c in range(0, d_v, MXU_DIM):
        slabs.append((col + c, functools.partial(v_slab, c=c)))
    col += d_v
    for c in range(0, d_pool, MXU_DIM):
        slabs.append((col + c, functools.partial(pool_slab, c=c)))
    col += d_pool
    for c in range(0, d_mem, MXU_DIM):
        slabs.append((col + c, functools.partial(mem_slab, c=c)))

    def project(col0):
        return _dot(xn_ref[...], w_ref[:, col0:col0 + MXU_DIM])

    cur = project(slabs[0][0])
    for n, (_, epilogue) in enumerate(slabs):
        nxt = project(slabs[n + 1][0]) if n + 1 < len(slabs) else None
        epilogue(cur)
        cur = nxt


def _inproj(x2, g_mix, w_in, q_norm, k_norm, cos_tab, sin_tab, pool_w, pool_scale,
            km, vm, mq_norm, *, seq, tm):
    T, D = x2.shape
    d_qk = DA_HEADS * 2 * DA_HEAD_DIM
    d_v = DA_HEADS * DA_V_DIM
    d_pool = len(POOL_WINDOWS) * POOL_GROUP_DIM
    d_mem = MEM_HEADS * MEM_HEAD_DIM
    d_all = 2 * d_qk + d_v + d_pool + d_mem
    M = km.shape[1]
    tiles_per_seq = seq // tm
    reps = MXU_DIM // DA_HEAD_DIM
    gq = jnp.tile(q_norm, reps).reshape(1, MXU_DIM)
    gk = jnp.tile(k_norm, reps).reshape(1, MXU_DIM)
    grp = jnp.arange(MXU_DIM) // DA_HEAD_DIM
    ones = (grp[:, None] == grp[None, :]).astype(BF16)
    kern = functools.partial(_inproj_kernel, tm=tm, tiles_per_seq=tiles_per_seq,
                             d_qk=d_qk, d_v=d_v, d_pool=d_pool, d_mem=d_mem)
    row = lambda width: pl.BlockSpec((tm, width), lambda i: (i, 0))
    return pl.pallas_call(
        kern,
        grid=(T // tm,),
        in_specs=[row(D),
                  _const_spec((1, D)),
                  _const_spec((D, d_all)),
                  _const_spec((1, MXU_DIM)),
                  _const_spec((1, MXU_DIM)),
                  pl.BlockSpec((tm, LANES), lambda i: (i % tiles_per_seq, 0)),
                  pl.BlockSpec((tm, LANES), lambda i: (i % tiles_per_seq, 0)),
                  _const_spec((MXU_DIM, MXU_DIM)),
                  _const_spec((len(POOL_WINDOWS), POOL_GROUP_DIM, POOL_GROUP_DIM)),
                  _const_spec((1, d_pool)),
                  pl.BlockSpec((1, M, d_mem), lambda i: (i // tiles_per_seq, 0, 0)),
                  pl.BlockSpec((1, M, d_mem), lambda i: (i // tiles_per_seq, 0, 0)),
                  _const_spec((1, MEM_HEAD_DIM))],
        out_specs=[row(d_qk), row(d_qk), row(d_v), row(d_pool), row(d_mem)],
        out_shape=[jax.ShapeDtypeStruct((T, d_qk), BF16),
                   jax.ShapeDtypeStruct((T, d_qk), BF16),
                   jax.ShapeDtypeStruct((T, d_v), BF16),
                   jax.ShapeDtypeStruct((T, d_pool), BF16),
                   jax.ShapeDtypeStruct((T, d_mem), BF16)],
        scratch_shapes=[pltpu.VMEM((tm, D), BF16),
                        pltpu.VMEM((tm + POOL_HALO, d_pool), F32)],
        compiler_params=_cparams("arbitrary"),
        name="inproj",
    )(x2, g_mix.reshape(1, D), w_in[:, :d_all].astype(BF16), gq, gk, cos_tab, sin_tab, ones,
      pool_w.astype(BF16), pool_scale.reshape(1, d_pool), km, vm, mq_norm.reshape(1, MEM_HEAD_DIM))


def _attn_kernel(lamp_ref, q_ref, k_ref, v_ref, gsub_ref, o_ref, vt_ref, acc_ref,
                 sa_ref, sb_ref, pa_ref, pb_ref, lfin_ref, *, seq, tq):
    lp = lamp_ref[...]
    lam = (jnp.exp(jnp.sum(lp[0:1] * lp[1:2], axis=-1, keepdims=True))
           - jnp.exp(jnp.sum(lp[2:3] * lp[3:4], axis=-1, keepdims=True)) + LAMBDA_INIT)
    sub = lax.broadcasted_iota(jnp.int32, (DA_V_DIM, tq), 0)
    key_i = lax.broadcasted_iota(jnp.int32, (tq, 2 * tq), 0)
    qry_i = lax.broadcasted_iota(jnp.int32, (tq, 2 * tq), 1) % tq
    causal = key_i <= qry_i
    n_blk = seq // tq

    def transpose_v(j, carry):
        r0 = pl.multiple_of(j * tq, tq)
        vt_ref[j] = v_ref[pl.ds(r0, tq), :].astype(F32).T.astype(BF16)
        return carry

    lax.fori_loop(0, n_blk, transpose_v, 0)

    def q_block(qi, _):
        q0 = pl.multiple_of(qi * tq, tq)
        qt = q_ref[pl.ds(q0, tq), :].astype(F32).T
        qcat = jnp.concatenate([jnp.where(sub < DA_HEAD_DIM, qt, 0.0),
                                jnp.where(sub >= DA_HEAD_DIM, qt, 0.0)], axis=1).astype(BF16)
        acc_ref[...] = jnp.zeros(acc_ref.shape, F32)
        pb_ref[...] = jnp.zeros(pb_ref.shape, BF16)

        def scores(j, dst_ref):
            k0 = pl.multiple_of(j * tq, tq)
            dst_ref[...] = _dot(k_ref[pl.ds(k0, tq), :], qcat)

        def values(j, p_ref, alpha):
            acc_ref[...] = alpha * acc_ref[...] + _dot(vt_ref[j], p_ref[...])

        def step(j, state, s_cur, s_next, p_prev, p_cur, *, masked, prefetch):
            m_prev, l_prev, alpha_prev = state
            if prefetch:
                scores(j + 1, s_next)
            s = s_cur[...]
            if masked:
                s = jnp.where(causal, s, NEG_INF)
            m_new = jnp.maximum(m_prev, jnp.max(s, axis=0, keepdims=True))
            alpha = jnp.exp2(m_prev - m_new)
            p = jnp.exp2(s - m_new)
            l_new = alpha * l_prev + jnp.sum(p, axis=0, keepdims=True)
            p_cur[...] = p.astype(BF16)
            values(jnp.maximum(j - 1, 0), p_prev, alpha_prev)
            return m_new, l_new, alpha

        def pair(t, state):
            state = step(2 * t, state, sa_ref, sb_ref, pb_ref, pa_ref, masked=False, prefetch=True)
            return step(2 * t + 1, state, sb_ref, sa_ref, pa_ref, pb_ref, masked=False, prefetch=True)

        scores(0, sa_ref)
        init = (jnp.full((1, 2 * tq), NEG_INF, F32), jnp.zeros((1, 2 * tq), F32),
                jnp.ones((1, 2 * tq), F32))
        state = lax.fori_loop(0, qi // 2, pair, init)

        @pl.when(qi % 2 == 0)
        def _():
            _, l_fin, alpha = step(qi, state, sa_ref, sb_ref, pb_ref, pa_ref, masked=True, prefetch=False)
            values(qi, pa_ref, alpha)
            lfin_ref[...] = l_fin

        @pl.when(qi % 2 == 1)
        def _():
            st = step(qi - 1, state, sa_ref, sb_ref, pb_ref, pa_ref, masked=False, prefetch=True)
            _, l_fin, alpha = step(qi, st, sb_ref, sa_ref, pa_ref, pb_ref, masked=True, prefetch=False)
            values(qi, pb_ref, alpha)
            lfin_ref[...] = l_fin

        on = acc_ref[...] / lfin_ref[...]
        ot = on[:, :tq] - lam * on[:, tq:]
        ot = ot * lax.rsqrt(jnp.mean(ot * ot, axis=0, keepdims=True) + EPS)
        o = ot.T * (gsub_ref[...] * (1.0 - LAMBDA_INIT))
        o_ref[pl.ds(q0, tq), :] = o.astype(BF16)
        return 0

    lax.fori_loop(0, n_blk, q_block, 0)


def _attn(q, k, v, lam_params, g_subln, *, batch, seq, tq):
    T, d_v = v.shape
    blk = lambda: pl.BlockSpec((seq, DA_V_DIM), lambda b, h: (b, h))
    return pl.pallas_call(
        functools.partial(_attn_kernel, seq=seq, tq=tq),
        grid=(batch, DA_HEADS),
        in_specs=[_const_spec((4, DA_HEAD_DIM)), blk(), blk(), blk(), _const_spec((1, DA_V_DIM))],
        out_specs=blk(),
        out_shape=jax.ShapeDtypeStruct((T, d_v), BF16),
        scratch_shapes=[pltpu.VMEM((seq // tq, DA_V_DIM, tq), BF16),
                        pltpu.VMEM((DA_V_DIM, 2 * tq), F32),
                        pltpu.VMEM((tq, 2 * tq), F32), pltpu.VMEM((tq, 2 * tq), F32),
                        pltpu.VMEM((tq, 2 * tq), BF16), pltpu.VMEM((tq, 2 * tq), BF16),
                        pltpu.VMEM((1, 2 * tq), F32)],
        compiler_params=_cparams("arbitrary", "arbitrary"),
        name="attn",
    )(lam_params, q, k, v, g_subln.reshape(1, DA_V_DIM))


def _pack_bf16_pairs(a, b):
    hi = pltpu.bitcast(a.astype(BF16).astype(F32), jnp.uint32)
    lo = pltpu.bitcast(b.astype(BF16).astype(F32), jnp.uint32)
    return hi | (lo >> 16)


def _unpack_bf16_pairs(p):
    a = pltpu.bitcast(p & jnp.uint32(0xFFFF0000), F32)
    b = pltpu.bitcast(p << 16, F32)
    return a, b


def _merge_kernel(x_ref, oa_ref, pool_ref, memo_ref, gmix_ref, wg_ref, bg_ref, wba_ref, wbp_ref,
                  wbm_ref, wout_ref, gffn_ref, wrh_ref, wrl_ref, br_ref,
                  h_ref, hn_ref, idx_ref, rank_ref, gate_ref, cnt_ref,
                  xn_ref, mrg_ref, carry_ref, *, tm, d_model):
    i = pl.program_id(0)
    D = d_model

    @pl.when(i == 0)
    def _():
        carry_ref[...] = jnp.zeros(carry_ref.shape, F32)

    x = x_ref[...]
    xn_ref[...] = _rms_rows(x, gmix_ref[...]).astype(BF16)
    for j in range(D // MXU_DIM):
        sl = slice(j * MXU_DIM, (j + 1) * MXU_DIM)
        merged = None
        for br, (y_ref, w_ref) in enumerate(((oa_ref, wba_ref), (pool_ref, wbp_ref), (memo_ref, wbm_ref))):
            gsl = slice(br * D + j * MXU_DIM, br * D + (j + 1) * MXU_DIM)
            logit = _dot(xn_ref[...], wg_ref[:, gsl]) + bg_ref[:, gsl]
            gate = 1.0 / (1.0 + jnp.exp(-logit))
            term = gate * _dot(y_ref[...], w_ref[:, sl])
            merged = term if merged is None else merged + term
        mrg_ref[:, sl] = merged.astype(BF16)
    h = x + _dot(mrg_ref[...], wout_ref[...])
    h_ref[...] = h
    hn = _rms_rows(h, gffn_ref[...])
    hn_ref[...] = _pack_bf16_pairs(hn[:, :D // 2], hn[:, D // 2:])

    hn_hi = hn.astype(BF16)
    hn_lo = (hn - hn_hi.astype(F32)).astype(BF16)
    logits = (_dot(hn_hi, wrh_ref[...]) + _dot(hn_lo, wrh_ref[...]) + _dot(hn_hi, wrl_ref[...])
              + br_ref[...])

    lane = lax.broadcasted_iota(jnp.int32, (tm, LANES), 1)
    work = logits
    vals, sels = [], []
    idx_out = jnp.zeros((tm, LANES), jnp.int32)
    for kk in range(TOP_K):
        mx = jnp.max(work, axis=-1, keepdims=True)
        idx = jnp.min(jnp.where(work == mx, lane, LANES), axis=-1, keepdims=True)
        sel = lane == idx
        vals.append(mx)
        sels.append(sel)
        idx_out = jnp.where(lane == kk, idx, idx_out)
        work = jnp.where(sel, -jnp.inf, work)
    exps = [jnp.exp(v - vals[0]) for v in vals]
    denom = exps[0] + exps[1] + exps[2] + exps[3]
    gate_out = jnp.zeros((tm, LANES), F32)
    onehot = jnp.zeros((tm, LANES), F32)
    for kk in range(TOP_K):
        gate_out = jnp.where(lane == kk, exps[kk] / denom, gate_out)
        onehot = jnp.where(sels[kk], 1.0, onehot)

    r_i = lax.broadcasted_iota(jnp.int32, (tm, tm), 0)
    c_i = lax.broadcasted_iota(jnp.int32, (tm, tm), 1)
    lower = (c_i < r_i).astype(BF16)
    prefix = _dot(lower, onehot.astype(BF16)) + carry_ref[0:1, :]
    rank_out = jnp.zeros((tm, LANES), jnp.int32)
    for kk in range(TOP_K):
        rk = jnp.sum(jnp.where(sels[kk], prefix, 0.0), axis=-1, keepdims=True)
        rank_out = jnp.where(lane == kk, rk.astype(jnp.int32), rank_out)
    new_carry = prefix[tm - 1:tm, :] + onehot[tm - 1:tm, :]
    carry_ref[...] = jnp.broadcast_to(new_carry, carry_ref.shape)
    idx_ref[...] = idx_out
    rank_ref[...] = rank_out
    gate_ref[...] = gate_out
    cnt_ref[...] = jnp.broadcast_to(new_carry, cnt_ref.shape).astype(jnp.int32)


def _merge(x2, oa, pool, memo, g_mix, w_gates, b_gates, w_b_attn, w_b_pool, w_b_mem, w_out, g_ffn,
           w_router, b_router, *, tm):
    T, D = x2.shape
    d_pool = pool.shape[1]
    d_mem = memo.shape[1]
    E = w_router.shape[1]
    wr = jnp.zeros((D, LANES), F32).at[:, :E].set(w_router)
    wr_hi = wr.astype(BF16)
    wr_lo = (wr - wr_hi.astype(F32)).astype(BF16)
    br = jnp.full((1, LANES), NEG_INF, F32).at[0, :E].set(b_router)
    row = lambda width: pl.BlockSpec((tm, width), lambda i: (i, 0))
    sublanes = 8
    return pl.pallas_call(
        functools.partial(_merge_kernel, tm=tm, d_model=D),
        grid=(T // tm,),
        in_specs=[row(D), row(oa.shape[1]), row(d_pool), row(d_mem),
                  _const_spec((1, D)),
                  _const_spec((D, 3 * D)), _const_spec((1, 3 * D)),
                  _const_spec((oa.shape[1], D)), _const_spec((d_pool, D)), _const_spec((d_mem, D)),
                  _const_spec((D, D)), _const_spec((1, D)),
                  _const_spec((D, LANES)), _const_spec((D, LANES)), _const_spec((1, LANES))],
        out_specs=[row(D), row(D // 2), row(LANES), row(LANES), row(LANES),
                   _const_spec((sublanes, LANES))],
        out_shape=[jax.ShapeDtypeStruct((T, D), F32),
                   jax.ShapeDtypeStruct((T, D // 2), jnp.uint32),
                   jax.ShapeDtypeStruct((T, LANES), jnp.int32),
                   jax.ShapeDtypeStruct((T, LANES), jnp.int32),
                   jax.ShapeDtypeStruct((T, LANES), F32),
                   jax.ShapeDtypeStruct((sublanes, LANES), jnp.int32)],
        scratch_shapes=[pltpu.VMEM((tm, D), BF16),
                        pltpu.VMEM((tm, D), BF16),
                        pltpu.VMEM((sublanes, LANES), F32)],
        compiler_params=_cparams("arbitrary"),
        name="merge",
    )(x2, oa, pool, memo, g_mix.reshape(1, D), w_gates.astype(BF16), b_gates.reshape(1, 3 * D),
      w_b_attn.astype(BF16), w_b_pool.astype(BF16), w_b_mem.astype(BF16), w_out.astype(BF16),
      g_ffn.reshape(1, D), wr_hi, wr_lo, br)


def _dispatch_kernel(poff_ref, nv_ref, idx_ref, rank_ref, hn_ref, xs_ref, zero_ref, sem, zsem,
                     *, tm, bm, n_blocks):
    def row_copy(r, dest):
        return pltpu.make_async_copy(hn_ref.at[pl.ds(r, 1)], xs_ref.at[pl.ds(dest, 1)], sem)

    def zero_copy(blk):
        return pltpu.make_async_copy(zero_ref, xs_ref.at[pl.ds(pl.multiple_of(blk * bm, bm), bm)], zsem)

    @pl.when(pl.program_id(0) == 0)
    def _():
        zero_ref[...] = jnp.zeros(zero_ref.shape, zero_ref.dtype)

        def start(blk, n):
            @pl.when(nv_ref[blk] < bm)
            def _():
                zero_copy(blk).start()
            return n + (nv_ref[blk] < bm).astype(jnp.int32)

        n_zeroed = lax.fori_loop(0, n_blocks, start, 0)

        def finish(_, carry):
            zero_copy(0).wait()
            return carry

        lax.fori_loop(0, n_zeroed, finish, 0)

    def issue(r, carry):
        for kk in range(TOP_K):
            a = r * TOP_K + kk
            row_copy(r, poff_ref[idx_ref[a]] + rank_ref[a]).start(priority=kk % 2)
        return carry

    lax.fori_loop(0, tm, issue, 0, unroll=ROW_DMA_UNROLL)
    for _ in range(TOP_K):
        pltpu.make_async_copy(hn_ref, xs_ref.at[pl.ds(0, tm)], sem).wait()


def _dispatch(poff, blk_valid, idx_flat, rank_flat, hn, *, bm, tm):
    T, half = hn.shape
    n_blocks = blk_valid.shape[0]
    flat = pl.BlockSpec((tm * TOP_K,), lambda i, poff, nv: (i,), memory_space=pltpu.SMEM)
    return pl.pallas_call(
        functools.partial(_dispatch_kernel, tm=tm, bm=bm, n_blocks=n_blocks),
        grid_spec=pltpu.PrefetchScalarGridSpec(
            num_scalar_prefetch=2,
            grid=(T // tm,),
            in_specs=[flat, flat, pl.BlockSpec((tm, half), lambda i, poff, nv: (i, 0))],
            out_specs=pl.BlockSpec(memory_space=pl.ANY),
            scratch_shapes=[pltpu.VMEM((bm, half), jnp.uint32),
                            pltpu.SemaphoreType.DMA, pltpu.SemaphoreType.DMA]),
        out_shape=jax.ShapeDtypeStruct((n_blocks * bm, half), jnp.uint32),
        compiler_params=_cparams("arbitrary"),
        name="dispatch",
    )(poff, blk_valid, idx_flat, rank_flat, hn)


def _experts_kernel(be_ref, nv_ref, src_ref, x_ref, wg_ref, bg_ref, wu_ref, bu_ref, wd_ref, bd_ref,
                    y_ref, act_ref, *, bm, d_model, d_expert):
    i = pl.program_id(0)
    half = d_model // 2

    @pl.when(nv_ref[i] == 0)
    def _():
        y_ref[...] = jnp.zeros(y_ref.shape, y_ref.dtype)

    @pl.when(nv_ref[i] > 0)
    def _():
        a, b = _unpack_bf16_pairs(x_ref[...])
        a = a.astype(BF16)
        b = b.astype(BF16)
        for j in range(d_expert // MXU_DIM):
            sl = slice(j * MXU_DIM, (j + 1) * MXU_DIM)
            g = _dot(a, wg_ref[0, :half, sl]) + _dot(b, wg_ref[0, half:, sl]) + bg_ref[0, :, sl]
            u = _dot(a, wu_ref[0, :half, sl]) + _dot(b, wu_ref[0, half:, sl]) + bu_ref[0, :, sl]
            g = jnp.minimum(g, SWIGLU_LIMIT)
            u = jnp.clip(u, -SWIGLU_LIMIT, SWIGLU_LIMIT)
            act = g * (1.0 / (1.0 + jnp.exp(-SWIGLU_ALPHA * g))) * (u + 1.0)
            act_ref[:, sl] = act.astype(BF16)
        out = _dot(act_ref[...], wd_ref[0]) + bd_ref[0]
        y_ref[...] = _pack_bf16_pairs(out[:, :half], out[:, half:])


def _experts(blk_e, blk_valid, blk_src, xs, w_gate, b_gate, w_up, b_up, w_down, b_down, *, bm):
    rows_sorted, half = xs.shape
    E, D, De = w_gate.shape
    n_blocks = rows_sorted // bm
    wspec = lambda r, c: pl.BlockSpec((1, r, c), lambda i, be, nv, src: (be[i], 0, 0))
    xspec = pl.BlockSpec((bm, half), lambda i, be, nv, src: (src[i], 0))
    return pl.pallas_call(
        functools.partial(_experts_kernel, bm=bm, d_model=D, d_expert=De),
        grid_spec=pltpu.PrefetchScalarGridSpec(
            num_scalar_prefetch=3,
            grid=(n_blocks,),
            in_specs=[xspec, wspec(D, De), wspec(1, De), wspec(D, De), wspec(1, De),
                      wspec(De, D), wspec(1, D)],
            out_specs=pl.BlockSpec((bm, half), lambda i, be, nv, src: (i, 0)),
            scratch_shapes=[pltpu.VMEM((bm, De), BF16)]),
        out_shape=jax.ShapeDtypeStruct((rows_sorted, half), jnp.uint32),
        compiler_params=_cparams("arbitrary"),
        name="experts",
    )(blk_e, blk_valid, blk_src, xs, w_gate.astype(BF16), b_gate.reshape(E, 1, De),
      w_up.astype(BF16), b_up.reshape(E, 1, De), w_down.astype(BF16), b_down.reshape(E, 1, D))


def _combine_kernel(poff_ref, idx_ref, rank_ref, gate_ref, h_ref, ys_ref, o_ref, buf_ref, sem, *, tm, d_model):
    half = d_model // 2

    def row_copy(src, kk, r):
        return pltpu.make_async_copy(ys_ref.at[pl.ds(src, 1)], buf_ref.at[kk, pl.ds(r, 1)], sem)

    def issue(r, carry):
        for kk in range(TOP_K):
            a = r * TOP_K + kk
            row_copy(poff_ref[idx_ref[a]] + rank_ref[a], kk, r).start(priority=kk % 2)
        return carry

    lax.fori_loop(0, tm, issue, 0, unroll=ROW_DMA_UNROLL)
    for kk in range(TOP_K):
        pltpu.make_async_copy(ys_ref.at[pl.ds(0, tm)], buf_ref.at[kk], sem).wait()
    h = h_ref[...]
    lo, hi = h[:, :half], h[:, half:]
    gates = gate_ref[...]
    for kk in range(TOP_K):
        a, b = _unpack_bf16_pairs(buf_ref[kk])
        w = gates[:, kk:kk + 1]
        lo = lo + w * a
        hi = hi + w * b
    o_ref[:, :half] = lo
    o_ref[:, half:] = hi


def _combine(poff, idx_flat, rank_flat, gate, h, ys, *, tm):
    T, D = h.shape
    flat = pl.BlockSpec((tm * TOP_K,), lambda i, poff: (i,), memory_space=pltpu.SMEM)
    row = lambda width: pl.BlockSpec((tm, width), lambda i, poff: (i, 0))
    return pl.pallas_call(
        functools.partial(_combine_kernel, tm=tm, d_model=D),
        grid_spec=pltpu.PrefetchScalarGridSpec(
            num_scalar_prefetch=1,
            grid=(T // tm,),
            in_specs=[flat, flat, row(LANES), row(D), pl.BlockSpec(memory_space=pl.ANY)],
            out_specs=row(D),
            scratch_shapes=[pltpu.VMEM((TOP_K, tm, D // 2), jnp.uint32), pltpu.SemaphoreType.DMA]),
        out_shape=jax.ShapeDtypeStruct((T, D), F32),
        compiler_params=_cparams("arbitrary"),
        name="combine",
    )(poff, idx_flat, rank_flat, gate, h, ys)


def _block_tables(counts, *, bm, n_blocks):
    padded = ((counts + bm - 1) // bm) * bm
    pend = jnp.cumsum(padded)
    poff = pend - padded
    n_used = pend[-1] // bm
    src = jnp.minimum(jnp.arange(n_blocks, dtype=jnp.int32), n_used - 1)
    blk_e = jnp.sum(pend[None, :] <= (src * bm)[:, None], axis=1)
    blk_e = jnp.minimum(blk_e, N_EXPERTS - 1).astype(jnp.int32)
    blk_valid = jnp.clip(poff[blk_e] + counts[blk_e] - src * bm, 0, bm)
    blk_valid = jnp.where(jnp.arange(n_blocks) < n_used, blk_valid, 0).astype(jnp.int32)
    return poff.astype(jnp.int32), blk_e, blk_valid, src.astype(jnp.int32)


def _rope_tables(seq):
    inv_freq = ROPE_THETA ** (-jnp.arange(0, DA_HEAD_DIM, 2, dtype=F32) / DA_HEAD_DIM)
    ang = jnp.arange(seq, dtype=F32)[:, None] * inv_freq[None, :]
    reps = LANES // (DA_HEAD_DIM // 2)
    cos = jnp.tile(jnp.cos(ang), (1, reps))
    sin = jnp.tile(jnp.sin(ang), (1, reps))
    first_half = (jnp.arange(LANES) % DA_HEAD_DIM) < DA_HEAD_DIM // 2
    return cos, jnp.where(first_half[None, :], -sin, sin)


ROW_TILE = 512
ATTN_TILE = 256
EXPERT_BLOCK = 512
MOVE_TILE = 256
ROW_DMA_UNROLL = 4


def _forward(x, mem, g_mix, w_in, b_gates, q_norm, k_norm, lambda_q1, lambda_k1, lambda_q2, lambda_k2, g_subln, pool_w, pool_scale, g_mem, w_mem_kv, mq_norm, mk_norm, w_b_attn, w_b_pool, w_b_mem, w_out, g_ffn, w_router, b_router, w_gate, b_gate, w_up, b_up, w_down, b_down):
    B, S, D = x.shape
    T = B * S
    x2 = x.reshape(T, D)
    cos_tab, sin_tab = _rope_tables(S)
    km, vm = _memkv(mem, g_mem, w_mem_kv, mk_norm)
    q, k, v, pool, memo = _inproj(x2, g_mix, w_in, q_norm, k_norm, cos_tab, sin_tab, pool_w,
                                  pool_scale, km, vm, mq_norm, seq=S, tm=ROW_TILE)
    lam_params = jnp.stack([lambda_q1, lambda_k1, lambda_q2, lambda_k2])
    o = _attn(q, k, v, lam_params, g_subln, batch=B, seq=S, tq=ATTN_TILE)
    d_in = q.shape[1] + k.shape[1] + v.shape[1] + pool.shape[1] + memo.shape[1]
    h1, hn, idx, rank, gate, cnt = _merge(x2, o, pool, memo, g_mix, w_in[:, d_in:], b_gates, w_b_attn,
                                          w_b_pool, w_b_mem, w_out, g_ffn, w_router, b_router,
                                          tm=ROW_TILE)
    n_blocks = (T * TOP_K) // EXPERT_BLOCK + N_EXPERTS
    poff, blk_e, blk_valid, blk_src = _block_tables(cnt[0, :N_EXPERTS], bm=EXPERT_BLOCK, n_blocks=n_blocks)
    idx_flat = idx[:, :TOP_K].reshape(T * TOP_K)
    rank_flat = rank[:, :TOP_K].reshape(T * TOP_K)
    xs = _dispatch(poff, blk_valid, idx_flat, rank_flat, hn, bm=EXPERT_BLOCK, tm=MOVE_TILE)
    ys = _experts(blk_e, blk_valid, blk_src, xs, w_gate, b_gate, w_up, b_up, w_down, b_down,
                  bm=EXPERT_BLOCK)
    out = _combine(poff, idx_flat, rank_flat, gate, h1, ys, tm=MOVE_TILE)
    return dict(q=q, k=k, v=v, pool=pool, memo=memo, o=o, h1=h1, hn=hn, idx=idx, rank=rank,
                gate=gate, cnt=cnt, out=out.reshape(B, S, D))


def kernel(x, mem, g_mix, w_in, b_gates, q_norm, k_norm, lambda_q1, lambda_k1, lambda_q2, lambda_k2, g_subln, pool_w, pool_scale, g_mem, w_mem_kv, mq_norm, mk_norm, w_b_attn, w_b_pool, w_b_mem, w_out, g_ffn, w_router, b_router, w_gate, b_gate, w_up, b_up, w_down, b_down):
    return _forward(x, mem, g_mix, w_in, b_gates, q_norm, k_norm, lambda_q1, lambda_k1, lambda_q2,
                    lambda_k2, g_subln, pool_w, pool_scale, g_mem, w_mem_kv, mq_norm, mk_norm,
                    w_b_attn, w_b_pool, w_b_mem, w_out, g_ffn, w_router, b_router, w_gate, b_gate,
                    w_up, b_up, w_down, b_down)["out"]
```

```python
import functools
import math

import jax
import jax.numpy as jnp
from jax import lax
from jax.experimental import pallas as pl
from jax.experimental.pallas import tpu as pltpu
from jax.experimental.pallas import tpu_sc as plsc

DA_HEADS = 8
DA_HEAD_DIM = 64
DA_V_DIM = 2 * DA_HEAD_DIM
ROPE_THETA = 10000.0
POOL_WINDOWS = (2, 4, 8, 16)
POOL_GROUP_DIM = 128
POOL_HALO = 16
MEM_HEADS = 4
MEM_HEAD_DIM = 128
N_EXPERTS = 32
TOP_K = 4
SWIGLU_LIMIT = 7.0
SWIGLU_ALPHA = 1.702
LAMBDA_INIT = 0.8 - 0.6 * math.exp(-0.3 * 0.0)
EPS = 1e-6
NEG_INF = -1e30
QUERY_SCALE = math.log2(math.e) / math.sqrt(DA_HEAD_DIM)

LANES = 128
MXU_DIM = 256
VMEM_LIMIT_BYTES = 56 * 1024 * 1024

BF16 = jnp.bfloat16
F32 = jnp.float32


def _cparams(*sem):
    return pltpu.CompilerParams(dimension_semantics=sem, vmem_limit_bytes=VMEM_LIMIT_BYTES)


def _const_spec(shape):
    nd = len(shape)
    return pl.BlockSpec(shape, lambda *_: (0,) * nd)


def _dot(a, b):
    return jnp.dot(a, b, preferred_element_type=F32)


def _dot_nt(a, b):
    return lax.dot_general(a, b, (((1,), (1,)), ((), ())), preferred_element_type=F32)


def _rms_rows(x, gain):
    ms = jnp.mean(x * x, axis=-1, keepdims=True)
    return x * lax.rsqrt(ms + EPS) * gain


def _memkv_kernel(mem_ref, gmem_ref, w_ref, mkn_ref, km_ref, vm_ref):
    mem_dim = MEM_HEADS * MEM_HEAD_DIM
    mn = _rms_rows(mem_ref[0], gmem_ref[...]).astype(BF16)
    kv = _dot(mn, w_ref[...])
    for h in range(MEM_HEADS):
        sl = slice(h * MEM_HEAD_DIM, (h + 1) * MEM_HEAD_DIM)
        km_ref[0, :, sl] = _rms_rows(kv[:, sl], mkn_ref[...]).astype(BF16)
    vm_ref[0] = kv[:, mem_dim:].astype(BF16)


def _memkv(mem, g_mem, w_mem_kv, mk_norm):
    B, M, D = mem.shape
    mem_dim = MEM_HEADS * MEM_HEAD_DIM
    out = jax.ShapeDtypeStruct((B, M, mem_dim), BF16)
    return pl.pallas_call(
        _memkv_kernel,
        grid=(B,),
        in_specs=[pl.BlockSpec((1, M, D), lambda b: (b, 0, 0)),
                  _const_spec((1, D)),
                  _const_spec((D, 2 * mem_dim)),
                  _const_spec((1, MEM_HEAD_DIM))],
        out_specs=[pl.BlockSpec((1, M, mem_dim), lambda b: (b, 0, 0))] * 2,
        out_shape=[out, out],
        compiler_params=_cparams("arbitrary"),
        name="memkv",
    )(mem, g_mem.reshape(1, D), w_mem_kv.astype(BF16), mk_norm.reshape(1, MEM_HEAD_DIM))


def _swap32(x):
    lane = lax.broadcasted_iota(jnp.int32, x.shape, 1)
    low = (lane & 32) == 0
    return jnp.where(low, pltpu.roll(x, LANES - 32, 1), pltpu.roll(x, 32, 1))


def _inproj_kernel(x_ref, gmix_ref, w_ref, gq_ref, gk_ref, cos_ref, sin_ref, ones_ref,
                   poolw_ref, pscale_ref, km_ref, vm_ref, mqn_ref,
                   q_ref, k_ref, v_ref, pool_ref, memo_ref,
                   xn_ref, uext_ref, *, tm, tiles_per_seq, d_qk, d_v, d_pool, d_mem):
    i = pl.program_id(0)
    pos0 = (i % tiles_per_seq) * tm
    xn_ref[...] = _rms_rows(x_ref[...], gmix_ref[...]).astype(BF16)

    cos = cos_ref[...]
    sin = sin_ref[...]
    ones = ones_ref[...]

    def qk_slab(p, c, gain_ref, out_ref, scale):
        ss = _dot((p * p).astype(BF16), ones)
        n = p * (lax.rsqrt(ss * (1.0 / DA_HEAD_DIM) + EPS) * scale) * gain_ref[...]
        for half in range(MXU_DIM // LANES):
            nh = n[:, half * LANES:(half + 1) * LANES]
            r = nh * cos + _swap32(nh) * sin
            out_ref[:, c + half * LANES:c + (half + 1) * LANES] = r.astype(BF16)

    def v_slab(p, c):
        v_ref[:, c:c + MXU_DIM] = p.astype(BF16)

    def pool_slab(p, c):
        @pl.when(pos0 == 0)
        def _():
            uext_ref[0:POOL_HALO, c:c + MXU_DIM] = jnp.zeros((POOL_HALO, MXU_DIM), F32)

        uext_ref[POOL_HALO:POOL_HALO + tm, c:c + MXU_DIM] = p
        t1 = pos0 + 1 + lax.broadcasted_iota(jnp.int32, (tm, POOL_GROUP_DIM), 0)
        for g in range(c // POOL_GROUP_DIM, (c + MXU_DIM) // POOL_GROUP_DIM):
            w = POOL_WINDOWS[g]
            sl = slice(g * POOL_GROUP_DIM, (g + 1) * POOL_GROUP_DIM)
            u = uext_ref[POOL_HALO:POOL_HALO + tm, sl]
            acc = u
            for s in range(1, w):
                acc = acc + uext_ref[POOL_HALO - s:POOL_HALO - s + tm, sl]
            cnt = jnp.minimum(t1, w).astype(F32)
            z = acc / cnt - u
            zp = _dot(z.astype(BF16), poolw_ref[g])
            pool_ref[:, sl] = (zp * pscale_ref[:, sl]).astype(BF16)
        uext_ref[0:POOL_HALO, c:c + MXU_DIM] = uext_ref[tm:tm + POOL_HALO, c:c + MXU_DIM]

    def mem_slab(p, c):
        inv_sqrt = 1.0 / math.sqrt(MEM_HEAD_DIM)
        for half in range(MXU_DIM // LANES):
            sl = slice(c + half * LANES, c + (half + 1) * LANES)
            qn = (_rms_rows(p[:, half * LANES:(half + 1) * LANES], mqn_ref[...]) * inv_sqrt).astype(BF16)
            s = _dot_nt(qn, km_ref[0, :, sl])
            m = jnp.max(s, axis=-1, keepdims=True)
            e = jnp.exp(s - m)
            l = jnp.sum(e, axis=-1, keepdims=True)
            o = _dot(e.astype(BF16), vm_ref[0, :, sl])
            memo_ref[:, sl] = (o / l).astype(BF16)

    slabs = []
    col = 0
    for c in range(0, d_qk, MXU_DIM):
        slabs.append((col + c, functools.partial(qk_slab, c=c, gain_ref=gq_ref, out_ref=q_ref,
                                                 scale=QUERY_SCALE)))
    col += d_qk
    for c in range(0, d_qk, MXU_DIM):
        slabs.append((col + c, functools.partial(qk_slab, c=c, gain_ref=gk_ref, out_ref=k_ref, scale=1.0)))
    col += d_qk
    for c in range(0, d_v, MXU_DIM):
        slabs.append((col + c, functools.partial(v_slab, c=c)))
    col += d_v
    for c in range(0, d_pool, MXU_DIM):
        slabs.append((col + c, functools.partial(pool_slab, c=c)))
    col += d_pool
    for c in range(0, d_mem, MXU_DIM):
        slabs.append((col + c, functools.partial(mem_slab, c=c)))

    def project(col0):
        return _dot(xn_ref[...], w_ref[:, col0:col0 + MXU_DIM])

    cur = project(slabs[0][0])
    for n, (_, epilogue) in enumerate(slabs):
        nxt = project(slabs[n + 1][0]) if n + 1 < len(slabs) else None
        epilogue(cur)
        cur = nxt


def _inproj(x2, g_mix, w_in, q_norm, k_norm, cos_tab, sin_tab, pool_w, pool_scale,
            km, vm, mq_norm, *, seq, tm):
    T, D = x2.shape
    d_qk = DA_HEADS * 2 * DA_HEAD_DIM
    d_v = DA_HEADS * DA_V_DIM
    d_pool = len(POOL_WINDOWS) * POOL_GROUP_DIM
    d_mem = MEM_HEADS * MEM_HEAD_DIM
    d_all = 2 * d_qk + d_v + d_pool + d_mem
    M = km.shape[1]
    tiles_per_seq = seq // tm
    reps = MXU_DIM // DA_HEAD_DIM
    gq = jnp.tile(q_norm, reps).reshape(1, MXU_DIM)
    gk = jnp.tile(k_norm, reps).reshape(1, MXU_DIM)
    grp = jnp.arange(MXU_DIM) // DA_HEAD_DIM
    ones = (grp[:, None] == grp[None, :]).astype(BF16)
    kern = functools.partial(_inproj_kernel, tm=tm, tiles_per_seq=tiles_per_seq,
                             d_qk=d_qk, d_v=d_v, d_pool=d_pool, d_mem=d_mem)
    row = lambda width: pl.BlockSpec((tm, width), lambda i: (i, 0))
    return pl.pallas_call(
        kern,
        grid=(T // tm,),
        in_specs=[row(D),
                  _const_spec((1, D)),
                  _const_spec((D, d_all)),
                  _const_spec((1, MXU_DIM)),
                  _const_spec((1, MXU_DIM)),
                  pl.BlockSpec((tm, LANES), lambda i: (i % tiles_per_seq, 0)),
                  pl.BlockSpec((tm, LANES), lambda i: (i % tiles_per_seq, 0)),
                  _const_spec((MXU_DIM, MXU_DIM)),
                  _const_spec((len(POOL_WINDOWS), POOL_GROUP_DIM, POOL_GROUP_DIM)),
                  _const_spec((1, d_pool)),
                  pl.BlockSpec((1, M, d_mem), lambda i: (i // tiles_per_seq, 0, 0)),
                  pl.BlockSpec((1, M, d_mem), lambda i: (i // tiles_per_seq, 0, 0)),
                  _const_spec((1, MEM_HEAD_DIM))],
        out_specs=[row(d_qk), row(d_qk), row(d_v), row(d_pool), row(d_mem)],
        out_shape=[jax.ShapeDtypeStruct((T, d_qk), BF16),
                   jax.ShapeDtypeStruct((T, d_qk), BF16),
                   jax.ShapeDtypeStruct((T, d_v), BF16),
                   jax.ShapeDtypeStruct((T, d_pool), BF16),
                   jax.ShapeDtypeStruct((T, d_mem), BF16)],
        scratch_shapes=[pltpu.VMEM((tm, D), BF16),
                        pltpu.VMEM((tm + POOL_HALO, d_pool), F32)],
        compiler_params=_cparams("arbitrary"),
        name="inproj",
    )(x2, g_mix.reshape(1, D), w_in[:, :d_all].astype(BF16), gq, gk, cos_tab, sin_tab, ones,
      pool_w.astype(BF16), pool_scale.reshape(1, d_pool), km, vm, mq_norm.reshape(1, MEM_HEAD_DIM))


def _attn_kernel(lamp_ref, q_ref, k_ref, v_ref, gsub_ref, o_ref, vt_ref, acc_ref,
                 sa_ref, sb_ref, pa_ref, pb_ref, lfin_ref, *, seq, tq):
    lp = lamp_ref[...]
    lam = (jnp.exp(jnp.sum(lp[0:1] * lp[1:2], axis=-1, keepdims=True))
           - jnp.exp(jnp.sum(lp[2:3] * lp[3:4], axis=-1, keepdims=True)) + LAMBDA_INIT)
    sub = lax.broadcasted_iota(jnp.int32, (DA_V_DIM, tq), 0)
    key_i = lax.broadcasted_iota(jnp.int32, (tq, 2 * tq), 0)
    qry_i = lax.broadcasted_iota(jnp.int32, (tq, 2 * tq), 1) % tq
    causal = key_i <= qry_i
    n_blk = seq // tq

    def transpose_v(j, carry):
        r0 = pl.multiple_of(j * tq, tq)
        vt_ref[j] = v_ref[pl.ds(r0, tq), :].astype(F32).T.astype(BF16)
        return carry

    lax.fori_loop(0, n_blk, transpose_v, 0)

    def q_block(qi, _):
        q0 = pl.multiple_of(qi * tq, tq)
        qt = q_ref[pl.ds(q0, tq), :].astype(F32).T
        qcat = jnp.concatenate([jnp.where(sub < DA_HEAD_DIM, qt, 0.0),
                                jnp.where(sub >= DA_HEAD_DIM, qt, 0.0)], axis=1).astype(BF16)
        acc_ref[...] = jnp.zeros(acc_ref.shape, F32)
        pb_ref[...] = jnp.zeros(pb_ref.shape, BF16)

        def scores(j, dst_ref):
            k0 = pl.multiple_of(j * tq, tq)
            dst_ref[...] = _dot(k_ref[pl.ds(k0, tq), :], qcat)

        def values(j, p_ref, alpha):
            acc_ref[...] = alpha * acc_ref[...] + _dot(vt_ref[j], p_ref[...])

        def step(j, state, s_cur, s_next, p_prev, p_cur, *, masked, prefetch):
            m_prev, l_prev, alpha_prev = state
            if prefetch:
                scores(j + 1, s_next)
            s = s_cur[...]
            if masked:
                s = jnp.where(causal, s, NEG_INF)
            m_new = jnp.maximum(m_prev, jnp.max(s, axis=0, keepdims=True))
            alpha = jnp.exp2(m_prev - m_new)
            p = jnp.exp2(s - m_new)
            l_new = alpha * l_prev + jnp.sum(p, axis=0, keepdims=True)
            p_cur[...] = p.astype(BF16)
            values(jnp.maximum(j - 1, 0), p_prev, alpha_prev)
            return m_new, l_new, alpha

        def pair(t, state):
            state = step(2 * t, state, sa_ref, sb_ref, pb_ref, pa_ref, masked=False, prefetch=True)
            return step(2 * t + 1, state, sb_ref, sa_ref, pa_ref, pb_ref, masked=False, prefetch=True)

        scores(0, sa_ref)
        init = (jnp.full((1, 2 * tq), NEG_INF, F32), jnp.zeros((1, 2 * tq), F32),
                jnp.ones((1, 2 * tq), F32))
        state = lax.fori_loop(0, qi // 2, pair, init)

        @pl.when(qi % 2 == 0)
        def _():
            _, l_fin, alpha = step(qi, state, sa_ref, sb_ref, pb_ref, pa_ref, masked=True, prefetch=False)
            values(qi, pa_ref, alpha)
            lfin_ref[...] = l_fin

        @pl.when(qi % 2 == 1)
        def _():
            st = step(qi - 1, state, sa_ref, sb_ref, pb_ref, pa_ref, masked=False, prefetch=True)
            _, l_fin, alpha = step(qi, st, sb_ref, sa_ref, pa_ref, pb_ref, masked=True, prefetch=False)
            values(qi, pb_ref, alpha)
            lfin_ref[...] = l_fin

        on = acc_ref[...] / lfin_ref[...]
        ot = on[:, :tq] - lam * on[:, tq:]
        ot = ot * lax.rsqrt(jnp.mean(ot * ot, axis=0, keepdims=True) + EPS)
        o = ot.T * (gsub_ref[...] * (1.0 - LAMBDA_INIT))
        o_ref[pl.ds(q0, tq), :] = o.astype(BF16)
        return 0

    lax.fori_loop(0, n_blk, q_block, 0)


def _attn(q, k, v, lam_params, g_subln, *, batch, seq, tq):
    T, d_v = v.shape
    blk = lambda: pl.BlockSpec((seq, DA_V_DIM), lambda b, h: (b, h))
    return pl.pallas_call(
        functools.partial(_attn_kernel, seq=seq, tq=tq),
        grid=(batch, DA_HEADS),
        in_specs=[_const_spec((4, DA_HEAD_DIM)), blk(), blk(), blk(), _const_spec((1, DA_V_DIM))],
        out_specs=blk(),
        out_shape=jax.ShapeDtypeStruct((T, d_v), BF16),
        scratch_shapes=[pltpu.VMEM((seq // tq, DA_V_DIM, tq), BF16),
                        pltpu.VMEM((DA_V_DIM, 2 * tq), F32),
                        pltpu.VMEM((tq, 2 * tq), F32), pltpu.VMEM((tq, 2 * tq), F32),
                        pltpu.VMEM((tq, 2 * tq), BF16), pltpu.VMEM((tq, 2 * tq), BF16),
                        pltpu.VMEM((1, 2 * tq), F32)],
        compiler_params=_cparams("arbitrary", "arbitrary"),
        name="attn",
    )(lam_params, q, k, v, g_subln.reshape(1, DA_V_DIM))


def _pack_bf16_pairs(a, b):
    hi = pltpu.bitcast(a.astype(BF16).astype(F32), jnp.uint32)
    lo = pltpu.bitcast(b.astype(BF16).astype(F32), jnp.uint32)
    return hi | (lo >> 16)


def _unpack_bf16_pairs(p):
    a = pltpu.bitcast(p & jnp.uint32(0xFFFF0000), F32)
    b = pltpu.bitcast(p << 16, F32)
    return a, b


def _merge_kernel(x_ref, oa_ref, pool_ref, memo_ref, gmix_ref, wg_ref, bg_ref, wba_ref, wbp_ref,
                  wbm_ref, wout_ref, gffn_ref, wrh_ref, wrl_ref, br_ref,
                  h_ref, hn_ref, idx_ref, rank_ref, gate_ref, cnt_ref,
                  xn_ref, mrg_ref, carry_ref, *, tm, d_model):
    i = pl.program_id(0)
    D = d_model

    @pl.when(i == 0)
    def _():
        carry_ref[...] = jnp.zeros(carry_ref.shape, F32)

    x = x_ref[...]
    xn_ref[...] = _rms_rows(x, gmix_ref[...]).astype(BF16)
    for j in range(D // MXU_DIM):
        sl = slice(j * MXU_DIM, (j + 1) * MXU_DIM)
        merged = None
        for br, (y_ref, w_ref) in enumerate(((oa_ref, wba_ref), (pool_ref, wbp_ref), (memo_ref, wbm_ref))):
            gsl = slice(br * D + j * MXU_DIM, br * D + (j + 1) * MXU_DIM)
            logit = _dot(xn_ref[...], wg_ref[:, gsl]) + bg_ref[:, gsl]
            gate = 1.0 / (1.0 + jnp.exp(-logit))
            term = gate * _dot(y_ref[...], w_ref[:, sl])
            merged = term if merged is None else merged + term
        mrg_ref[:, sl] = merged.astype(BF16)
    h = x + _dot(mrg_ref[...], wout_ref[...])
    h_ref[...] = h
    hn = _rms_rows(h, gffn_ref[...])
    hn_ref[...] = _pack_bf16_pairs(hn[:, :D // 2], hn[:, D // 2:])

    hn_hi = hn.astype(BF16)
    hn_lo = (hn - hn_hi.astype(F32)).astype(BF16)
    logits = (_dot(hn_hi, wrh_ref[...]) + _dot(hn_lo, wrh_ref[...]) + _dot(hn_hi, wrl_ref[...])
              + br_ref[...])

    lane = lax.broadcasted_iota(jnp.int32, (tm, LANES), 1)
    work = logits
    vals, sels = [], []
    idx_out = jnp.zeros((tm, LANES), jnp.int32)
    for kk in range(TOP_K):
        mx = jnp.max(work, axis=-1, keepdims=True)
        idx = jnp.min(jnp.where(work == mx, lane, LANES), axis=-1, keepdims=True)
        sel = lane == idx
        vals.append(mx)
        sels.append(sel)
        idx_out = jnp.where(lane == kk, idx, idx_out)
        work = jnp.where(sel, -jnp.inf, work)
    exps = [jnp.exp(v - vals[0]) for v in vals]
    denom = exps[0] + exps[1] + exps[2] + exps[3]
    gate_out = jnp.zeros((tm, LANES), F32)
    onehot = jnp.zeros((tm, LANES), F32)
    for kk in range(TOP_K):
        gate_out = jnp.where(lane == kk, exps[kk] / denom, gate_out)
        onehot = jnp.where(sels[kk], 1.0, onehot)

    r_i = lax.broadcasted_iota(jnp.int32, (tm, tm), 0)
    c_i = lax.broadcasted_iota(jnp.int32, (tm, tm), 1)
    lower = (c_i < r_i).astype(BF16)
    prefix = _dot(lower, onehot.astype(BF16)) + carry_ref[0:1, :]
    rank_out = jnp.zeros((tm, LANES), jnp.int32)
    for kk in range(TOP_K):
        rk = jnp.sum(jnp.where(sels[kk], prefix, 0.0), axis=-1, keepdims=True)
        rank_out = jnp.where(lane == kk, rk.astype(jnp.int32), rank_out)
    new_carry = prefix[tm - 1:tm, :] + onehot[tm - 1:tm, :]
    carry_ref[...] = jnp.broadcast_to(new_carry, carry_ref.shape)
    idx_ref[...] = idx_out
    rank_ref[...] = rank_out
    gate_ref[...] = gate_out
    cnt_ref[...] = jnp.broadcast_to(new_carry, cnt_ref.shape).astype(jnp.int32)


def _merge(x2, oa, pool, memo, g_mix, w_gates, b_gates, w_b_attn, w_b_pool, w_b_mem, w_out, g_ffn,
           w_router, b_router, *, tm):
    T, D = x2.shape
    d_pool = pool.shape[1]
    d_mem = memo.shape[1]
    E = w_router.shape[1]
    wr = jnp.zeros((D, LANES), F32).at[:, :E].set(w_router)
    wr_hi = wr.astype(BF16)
    wr_lo = (wr - wr_hi.astype(F32)).astype(BF16)
    br = jnp.full((1, LANES), NEG_INF, F32).at[0, :E].set(b_router)
    row = lambda width: pl.BlockSpec((tm, width), lambda i: (i, 0))
    sublanes = 8
    return pl.pallas_call(
        functools.partial(_merge_kernel, tm=tm, d_model=D),
        grid=(T // tm,),
        in_specs=[row(D), row(oa.shape[1]), row(d_pool), row(d_mem),
                  _const_spec((1, D)),
                  _const_spec((D, 3 * D)), _const_spec((1, 3 * D)),
                  _const_spec((oa.shape[1], D)), _const_spec((d_pool, D)), _const_spec((d_mem, D)),
                  _const_spec((D, D)), _const_spec((1, D)),
                  _const_spec((D, LANES)), _const_spec((D, LANES)), _const_spec((1, LANES))],
        out_specs=[row(D), row(D // 2), row(LANES), row(LANES), row(LANES),
                   _const_spec((sublanes, LANES))],
        out_shape=[jax.ShapeDtypeStruct((T, D), F32),
                   jax.ShapeDtypeStruct((T, D // 2), jnp.uint32),
                   jax.ShapeDtypeStruct((T, LANES), jnp.int32),
                   jax.ShapeDtypeStruct((T, LANES), jnp.int32),
                   jax.ShapeDtypeStruct((T, LANES), F32),
                   jax.ShapeDtypeStruct((sublanes, LANES), jnp.int32)],
        scratch_shapes=[pltpu.VMEM((tm, D), BF16),
                        pltpu.VMEM((tm, D), BF16),
                        pltpu.VMEM((sublanes, LANES), F32)],
        compiler_params=_cparams("arbitrary"),
        name="merge",
    )(x2, oa, pool, memo, g_mix.reshape(1, D), w_gates.astype(BF16), b_gates.reshape(1, 3 * D),
      w_b_attn.astype(BF16), w_b_pool.astype(BF16), w_b_mem.astype(BF16), w_out.astype(BF16),
      g_ffn.reshape(1, D), wr_hi, wr_lo, br)


def _dispatch_kernel(poff_ref, nv_ref, idx_ref, rank_ref, hn_ref, xs_ref, zero_ref, sem, zsem,
                     *, tm, bm, n_blocks):
    def row_copy(r, dest):
        return pltpu.make_async_copy(hn_ref.at[pl.ds(r, 1)], xs_ref.at[pl.ds(dest, 1)], sem)

    def zero_copy(blk):
        return pltpu.make_async_copy(zero_ref, xs_ref.at[pl.ds(pl.multiple_of(blk * bm, bm), bm)], zsem)

    @pl.when(pl.program_id(0) == 0)
    def _():
        zero_ref[...] = jnp.zeros(zero_ref.shape, zero_ref.dtype)

        def start(blk, n):
            @pl.when(nv_ref[blk] < bm)
            def _():
                zero_copy(blk).start()
            return n + (nv_ref[blk] < bm).astype(jnp.int32)

        n_zeroed = lax.fori_loop(0, n_blocks, start, 0)

        def finish(_, carry):
            zero_copy(0).wait()
            return carry

        lax.fori_loop(0, n_zeroed, finish, 0)

    def issue(r, carry):
        for kk in range(TOP_K):
            a = r * TOP_K + kk
            row_copy(r, poff_ref[idx_ref[a]] + rank_ref[a]).start(priority=kk % 2)
        return carry

    lax.fori_loop(0, tm, issue, 0, unroll=ROW_DMA_UNROLL)
    for _ in range(TOP_K):
        pltpu.make_async_copy(hn_ref, xs_ref.at[pl.ds(0, tm)], sem).wait()


def _dispatch(poff, blk_valid, idx_flat, rank_flat, hn, *, bm, tm):
    T, half = hn.shape
    n_blocks = blk_valid.shape[0]
    flat = pl.BlockSpec((tm * TOP_K,), lambda i, poff, nv: (i,), memory_space=pltpu.SMEM)
    return pl.pallas_call(
        functools.partial(_dispatch_kernel, tm=tm, bm=bm, n_blocks=n_blocks),
        grid_spec=pltpu.PrefetchScalarGridSpec(
            num_scalar_prefetch=2,
            grid=(T // tm,),
            in_specs=[flat, flat, pl.BlockSpec((tm, half), lambda i, poff, nv: (i, 0))],
            out_specs=pl.BlockSpec(memory_space=pl.ANY),
            scratch_shapes=[pltpu.VMEM((bm, half), jnp.uint32),
                            pltpu.SemaphoreType.DMA, pltpu.SemaphoreType.DMA]),
        out_shape=jax.ShapeDtypeStruct((n_blocks * bm, half), jnp.uint32),
        compiler_params=_cparams("arbitrary"),
        name="dispatch",
    )(poff, blk_valid, idx_flat, rank_flat, hn)


def _experts_kernel(be_ref, nv_ref, src_ref, x_ref, wg_ref, bg_ref, wu_ref, bu_ref, wd_ref, bd_ref,
                    y_ref, act_ref, *, bm, d_model, d_expert):
    i = pl.program_id(0)
    half = d_model // 2

    @pl.when(nv_ref[i] == 0)
    def _():
        y_ref[...] = jnp.zeros(y_ref.shape, y_ref.dtype)

    @pl.when(nv_ref[i] > 0)
    def _():
        a, b = _unpack_bf16_pairs(x_ref[...])
        valid = lax.broadcasted_iota(jnp.int32, (bm, half), 0) < nv_ref[i]
        a = jnp.where(valid, a, 0.0).astype(BF16)
        b = jnp.where(valid, b, 0.0).astype(BF16)
        for j in range(d_expert // MXU_DIM):
            sl = slice(j * MXU_DIM, (j + 1) * MXU_DIM)
            g = _dot(a, wg_ref[0, :half, sl]) + _dot(b, wg_ref[0, half:, sl]) + bg_ref[0, :, sl]
            u = _dot(a, wu_ref[0, :half, sl]) + _dot(b, wu_ref[0, half:, sl]) + bu_ref[0, :, sl]
            g = jnp.minimum(g, SWIGLU_LIMIT)
            u = jnp.clip(u, -SWIGLU_LIMIT, SWIGLU_LIMIT)
            act = g * (1.0 / (1.0 + jnp.exp(-SWIGLU_ALPHA * g))) * (u + 1.0)
            act_ref[:, sl] = act.astype(BF16)
        out = _dot(act_ref[...], wd_ref[0]) + bd_ref[0]
        y_ref[...] = _pack_bf16_pairs(out[:, :half], out[:, half:])


def _experts(blk_e, blk_valid, blk_src, xs, w_gate, b_gate, w_up, b_up, w_down, b_down, *, bm):
    rows_sorted, half = xs.shape
    E, D, De = w_gate.shape
    n_blocks = rows_sorted // bm
    wspec = lambda r, c: pl.BlockSpec((1, r, c), lambda i, be, nv, src: (be[i], 0, 0))
    xspec = pl.BlockSpec((bm, half), lambda i, be, nv, src: (src[i], 0))
    return pl.pallas_call(
        functools.partial(_experts_kernel, bm=bm, d_model=D, d_expert=De),
        grid_spec=pltpu.PrefetchScalarGridSpec(
            num_scalar_prefetch=3,
            grid=(n_blocks,),
            in_specs=[xspec, wspec(D, De), wspec(1, De), wspec(D, De), wspec(1, De),
                      wspec(De, D), wspec(1, D)],
            out_specs=pl.BlockSpec((bm, half), lambda i, be, nv, src: (i, 0)),
            scratch_shapes=[pltpu.VMEM((bm, De), BF16)]),
        out_shape=jax.ShapeDtypeStruct((rows_sorted, half), jnp.uint32),
        compiler_params=_cparams("arbitrary"),
        name="experts",
    )(blk_e, blk_valid, blk_src, xs, w_gate.astype(BF16), b_gate.reshape(E, 1, De),
      w_up.astype(BF16), b_up.reshape(E, 1, De), w_down.astype(BF16), b_down.reshape(E, 1, D))


def _combine_kernel(poff_ref, idx_ref, rank_ref, gate_ref, h_ref, ys_ref, o_ref, buf_ref, sem, *, tm, d_model):
    half = d_model // 2

    def row_copy(src, kk, r):
        return pltpu.make_async_copy(ys_ref.at[pl.ds(src, 1)], buf_ref.at[kk, pl.ds(r, 1)], sem)

    def issue(r, carry):
        for kk in range(TOP_K):
            a = r * TOP_K + kk
            row_copy(poff_ref[idx_ref[a]] + rank_ref[a], kk, r).start(priority=kk % 2)
        return carry

    lax.fori_loop(0, tm, issue, 0, unroll=ROW_DMA_UNROLL)
    for kk in range(TOP_K):
        pltpu.make_async_copy(ys_ref.at[pl.ds(0, tm)], buf_ref.at[kk], sem).wait()
    h = h_ref[...]
    lo, hi = h[:, :half], h[:, half:]
    gates = gate_ref[...]
    for kk in range(TOP_K):
        a, b = _unpack_bf16_pairs(buf_ref[kk])
        w = gates[:, kk:kk + 1]
        lo = lo + w * a
        hi = hi + w * b
    o_ref[:, :half] = lo
    o_ref[:, half:] = hi


def _combine(poff, idx_flat, rank_flat, gate, h, ys, *, tm):
    T, D = h.shape
    flat = pl.BlockSpec((tm * TOP_K,), lambda i, poff: (i,), memory_space=pltpu.SMEM)
    row = lambda width: pl.BlockSpec((tm, width), lambda i, poff: (i, 0))
    return pl.pallas_call(
        functools.partial(_combine_kernel, tm=tm, d_model=D),
        grid_spec=pltpu.PrefetchScalarGridSpec(
            num_scalar_prefetch=1,
            grid=(T // tm,),
            in_specs=[flat, flat, row(LANES), row(D), pl.BlockSpec(memory_space=pl.ANY)],
            out_specs=row(D),
            scratch_shapes=[pltpu.VMEM((TOP_K, tm, D // 2), jnp.uint32), pltpu.SemaphoreType.DMA]),
        out_shape=jax.ShapeDtypeStruct((T, D), F32),
        compiler_params=_cparams("arbitrary"),
        name="combine",
    )(poff, idx_flat, rank_flat, gate, h, ys)


SC_CORES = 2
SC_SUBCORES = 16
SC_CHUNK = 128


def _sc_mesh():
    return plsc.VectorSubcoreMesh(core_axis_name="c", subcore_axis_name="s",
                                  num_cores=SC_CORES, num_subcores=SC_SUBCORES)


def _sc_scatter_rows(rows, dest, *, rows_out):
    T, width = rows.shape
    n_chunks = T // SC_CHUNK
    per_worker = n_chunks // (SC_CORES * SC_SUBCORES)

    @functools.partial(
        pl.kernel, mesh=_sc_mesh(),
        out_type=jax.ShapeDtypeStruct((rows_out, width), rows.dtype),
        scratch_types=[pltpu.VMEM((TOP_K, SC_CHUNK), jnp.int32),
                       pltpu.VMEM((SC_CHUNK, width), rows.dtype)],
        name="sc_dispatch")
    def scatter(rows_hbm, dest_hbm, out_hbm, idx_v, rows_v):
        worker = lax.axis_index("s") * SC_CORES + lax.axis_index("c")

        @pl.loop(0, per_worker)
        def _(n):
            c = worker * per_worker + n
            pltpu.sync_copy(rows_hbm.at[pl.ds(c * SC_CHUNK, SC_CHUNK)], rows_v)
            pltpu.sync_copy(dest_hbm.at[c], idx_v)
            for kk in range(TOP_K):
                pltpu.sync_copy(rows_v, out_hbm.at[idx_v.at[kk]])

    return scatter(rows, dest)


def _sc_gather_rows(table, src, *, n_tokens):
    _, width = table.shape
    n_chunks = n_tokens // SC_CHUNK
    per_worker = n_chunks // (SC_CORES * SC_SUBCORES)

    @functools.partial(
        pl.kernel, mesh=_sc_mesh(),
        out_type=jax.ShapeDtypeStruct((TOP_K, n_tokens, width), table.dtype),
        scratch_types=[pltpu.VMEM((TOP_K, SC_CHUNK), jnp.int32),
                       pltpu.VMEM((SC_CHUNK, width), table.dtype)],
        name="sc_combine")
    def gather(table_hbm, src_hbm, out_hbm, idx_v, rows_v):
        worker = lax.axis_index("s") * SC_CORES + lax.axis_index("c")

        @pl.loop(0, per_worker)
        def _(n):
            c = worker * per_worker + n
            pltpu.sync_copy(src_hbm.at[c], idx_v)
            for kk in range(TOP_K):
                pltpu.sync_copy(table_hbm.at[idx_v.at[kk]], rows_v)
                pltpu.sync_copy(rows_v, out_hbm.at[kk, pl.ds(c * SC_CHUNK, SC_CHUNK)])

    return gather(table, src)


def _weighted_sum_kernel(gate_ref, h_ref, y_ref, o_ref, *, d_model):
    half = d_model // 2
    h = h_ref[...]
    lo, hi = h[:, :half], h[:, half:]
    gates = gate_ref[...]
    for kk in range(TOP_K):
        a, b = _unpack_bf16_pairs(y_ref[kk])
        w = gates[:, kk:kk + 1]
        lo = lo + w * a
        hi = hi + w * b
    o_ref[:, :half] = lo
    o_ref[:, half:] = hi


def _weighted_sum(gate, h, y4, *, tm):
    T, D = h.shape
    row = lambda width: pl.BlockSpec((tm, width), lambda i: (i, 0))
    return pl.pallas_call(
        functools.partial(_weighted_sum_kernel, d_model=D),
        grid=(T // tm,),
        in_specs=[row(LANES), row(D), pl.BlockSpec((TOP_K, tm, D // 2), lambda i: (0, i, 0))],
        out_specs=row(D),
        out_shape=jax.ShapeDtypeStruct((T, D), F32),
        compiler_params=_cparams("arbitrary"),
        name="weighted_sum",
    )(gate, h, y4)


def _block_tables(counts, *, bm, n_blocks):
    padded = ((counts + bm - 1) // bm) * bm
    pend = jnp.cumsum(padded)
    poff = pend - padded
    n_used = pend[-1] // bm
    src = jnp.minimum(jnp.arange(n_blocks, dtype=jnp.int32), n_used - 1)
    blk_e = jnp.sum(pend[None, :] <= (src * bm)[:, None], axis=1)
    blk_e = jnp.minimum(blk_e, N_EXPERTS - 1).astype(jnp.int32)
    blk_valid = jnp.clip(poff[blk_e] + counts[blk_e] - src * bm, 0, bm)
    blk_valid = jnp.where(jnp.arange(n_blocks) < n_used, blk_valid, 0).astype(jnp.int32)
    return poff.astype(jnp.int32), blk_e, blk_valid, src.astype(jnp.int32)


def _rope_tables(seq):
    inv_freq = ROPE_THETA ** (-jnp.arange(0, DA_HEAD_DIM, 2, dtype=F32) / DA_HEAD_DIM)
    ang = jnp.arange(seq, dtype=F32)[:, None] * inv_freq[None, :]
    reps = LANES // (DA_HEAD_DIM // 2)
    cos = jnp.tile(jnp.cos(ang), (1, reps))
    sin = jnp.tile(jnp.sin(ang), (1, reps))
    first_half = (jnp.arange(LANES) % DA_HEAD_DIM) < DA_HEAD_DIM // 2
    return cos, jnp.where(first_half[None, :], -sin, sin)


ROW_TILE = 512
ATTN_TILE = 256
EXPERT_BLOCK = 512
MOVE_TILE = 256
ROW_DMA_UNROLL = 4


def _forward(x, mem, g_mix, w_in, b_gates, q_norm, k_norm, lambda_q1, lambda_k1, lambda_q2, lambda_k2, g_subln, pool_w, pool_scale, g_mem, w_mem_kv, mq_norm, mk_norm, w_b_attn, w_b_pool, w_b_mem, w_out, g_ffn, w_router, b_router, w_gate, b_gate, w_up, b_up, w_down, b_down):
    B, S, D = x.shape
    T = B * S
    x2 = x.reshape(T, D)
    cos_tab, sin_tab = _rope_tables(S)
    km, vm = _memkv(mem, g_mem, w_mem_kv, mk_norm)
    q, k, v, pool, memo = _inproj(x2, g_mix, w_in, q_norm, k_norm, cos_tab, sin_tab, pool_w,
                                  pool_scale, km, vm, mq_norm, seq=S, tm=ROW_TILE)
    lam_params = jnp.stack([lambda_q1, lambda_k1, lambda_q2, lambda_k2])
    o = _attn(q, k, v, lam_params, g_subln, batch=B, seq=S, tq=ATTN_TILE)
    d_in = q.shape[1] + k.shape[1] + v.shape[1] + pool.shape[1] + memo.shape[1]
    h1, hn, idx, rank, gate, cnt = _merge(x2, o, pool, memo, g_mix, w_in[:, d_in:], b_gates, w_b_attn,
                                          w_b_pool, w_b_mem, w_out, g_ffn, w_router, b_router,
                                          tm=ROW_TILE)
    n_blocks = (T * TOP_K) // EXPERT_BLOCK + N_EXPERTS
    poff, blk_e, blk_valid, blk_src = _block_tables(cnt[0, :N_EXPERTS], bm=EXPERT_BLOCK, n_blocks=n_blocks)
    idx_flat = idx[:, :TOP_K].reshape(T * TOP_K)
    rank_flat = rank[:, :TOP_K].reshape(T * TOP_K)
    dest = (poff[idx[:, :TOP_K]] + rank[:, :TOP_K]).astype(jnp.int32)
    dest = dest.reshape(T // SC_CHUNK, SC_CHUNK, TOP_K).transpose(0, 2, 1)
    xs = _sc_scatter_rows(hn, dest, rows_out=n_blocks * EXPERT_BLOCK)
    ys = _experts(blk_e, blk_valid, blk_src, xs, w_gate, b_gate, w_up, b_up, w_down, b_down,
                  bm=EXPERT_BLOCK)
    y4 = _sc_gather_rows(ys, dest, n_tokens=T)
    out = _weighted_sum(gate, h1, y4, tm=ROW_TILE)
    return dict(q=q, k=k, v=v, pool=pool, memo=memo, o=o, h1=h1, hn=hn, idx=idx, rank=rank,
                gate=gate, cnt=cnt, out=out.reshape(B, S, D))


def kernel(x, mem, g_mix, w_in, b_gates, q_norm, k_norm, lambda_q1, lambda_k1, lambda_q2, lambda_k2, g_subln, pool_w, pool_scale, g_mem, w_mem_kv, mq_norm, mk_norm, w_b_attn, w_b_pool, w_b_mem, w_out, g_ffn, w_router, b_router, w_gate, b_gate, w_up, b_up, w_down, b_down):
    return _forward(x, mem, g_mix, w_in, b_gates, q_norm, k_norm, lambda_q1, lambda_k1, lambda_q2,
                    lambda_k2, g_subln, pool_w, pool_scale, g_mem, w_mem_kv, mq_norm, mk_norm,
                    w_b_attn, w_b_pool, w_b_mem, w_out, g_ffn, w_router, b_router, w_gate, b_gate,
                    w_up, b_up, w_down, b_down)["out"]
```

```python
import functools
import math

import jax
import jax.numpy as jnp
import numpy as np
from jax import lax
from jax.experimental import pallas as pl
from jax.experimental.pallas import tpu as pltpu
from jax.experimental.pallas import tpu_sc as plsc

DA_HEADS = 8
DA_HEAD_DIM = 64
DA_V_DIM = 2 * DA_HEAD_DIM
ROPE_THETA = 10000.0
POOL_WINDOWS = (2, 4, 8, 16)
POOL_GROUP_DIM = 128
POOL_HALO = 16
MEM_HEADS = 4
MEM_HEAD_DIM = 128
N_EXPERTS = 32
TOP_K = 4
SWIGLU_LIMIT = 7.0
SWIGLU_ALPHA = 1.702
LAMBDA_INIT = 0.8 - 0.6 * math.exp(-0.3 * 0.0)
EPS = 1e-6
NEG_INF = -1e30
QUERY_SCALE = math.log2(math.e) / math.sqrt(DA_HEAD_DIM)

LANES = 128
MXU_DIM = 256
VMEM_LIMIT_BYTES = 56 * 1024 * 1024

BF16 = jnp.bfloat16
F32 = jnp.float32


def _cparams(*sem):
    return pltpu.CompilerParams(dimension_semantics=sem, vmem_limit_bytes=VMEM_LIMIT_BYTES)


def _const_spec(shape):
    nd = len(shape)
    return pl.BlockSpec(shape, lambda *_: (0,) * nd)


def _dot(a, b):
    return jnp.dot(a, b, preferred_element_type=F32)


def _dot_nt(a, b):
    return lax.dot_general(a, b, (((1,), (1,)), ((), ())), preferred_element_type=F32)


def _rms_rows(x, gain):
    ms = jnp.mean(x * x, axis=-1, keepdims=True)
    return x * lax.rsqrt(ms + EPS) * gain


def _memkv_kernel(mem_ref, gmem_ref, w_ref, mkn_ref, km_ref, vm_ref):
    mem_dim = MEM_HEADS * MEM_HEAD_DIM
    mn = _rms_rows(mem_ref[0], gmem_ref[...]).astype(BF16)
    kv = _dot(mn, w_ref[...])
    for h in range(MEM_HEADS):
        sl = slice(h * MEM_HEAD_DIM, (h + 1) * MEM_HEAD_DIM)
        km_ref[0, :, sl] = _rms_rows(kv[:, sl], mkn_ref[...]).astype(BF16)
    vm_ref[0] = kv[:, mem_dim:].astype(BF16)


def _memkv(mem, g_mem, w_mem_kv, mk_norm):
    B, M, D = mem.shape
    mem_dim = MEM_HEADS * MEM_HEAD_DIM
    out = jax.ShapeDtypeStruct((B, M, mem_dim), BF16)
    return pl.pallas_call(
        _memkv_kernel,
        grid=(B,),
        in_specs=[pl.BlockSpec((1, M, D), lambda b: (b, 0, 0)),
                  _const_spec((1, D)),
                  _const_spec((D, 2 * mem_dim)),
                  _const_spec((1, MEM_HEAD_DIM))],
        out_specs=[pl.BlockSpec((1, M, mem_dim), lambda b: (b, 0, 0))] * 2,
        out_shape=[out, out],
        compiler_params=_cparams("arbitrary"),
        name="memkv",
    )(mem, g_mem.reshape(1, D), w_mem_kv.astype(BF16), mk_norm.reshape(1, MEM_HEAD_DIM))


def _swap32(x):
    lane = lax.broadcasted_iota(jnp.int32, x.shape, 1)
    low = (lane & 32) == 0
    return jnp.where(low, pltpu.roll(x, LANES - 32, 1), pltpu.roll(x, 32, 1))


def _inproj_kernel(x_ref, gmix_ref, w_ref, gq_ref, gk_ref, cos_ref, sin_ref, ones_ref,
                   poolw_ref, pscale_ref, km_ref, vm_ref, mqn_ref,
                   q_ref, k_ref, v_ref, pool_ref, memo_ref,
                   xn_ref, uext_ref, *, tm, tiles_per_seq, d_qk, d_v, d_pool, d_mem):
    i = pl.program_id(0)
    pos0 = (i % tiles_per_seq) * tm
    xn_ref[...] = _rms_rows(x_ref[...], gmix_ref[...]).astype(BF16)

    cos = cos_ref[...]
    sin = sin_ref[...]
    ones = ones_ref[...]

    def qk_slab(p, c, gain_ref, out_ref, scale):
        ss = _dot((p * p).astype(BF16), ones)
        n = p * (lax.rsqrt(ss * (1.0 / DA_HEAD_DIM) + EPS) * scale) * gain_ref[...]
        for half in range(MXU_DIM // LANES):
            nh = n[:, half * LANES:(half + 1) * LANES]
            r = nh * cos + _swap32(nh) * sin
            out_ref[:, c + half * LANES:c + (half + 1) * LANES] = r.astype(BF16)

    def v_slab(p, c):
        v_ref[:, c:c + MXU_DIM] = p.astype(BF16)

    def pool_slab(p, c):
        @pl.when(pos0 == 0)
        def _():
            uext_ref[0:POOL_HALO, c:c + MXU_DIM] = jnp.zeros((POOL_HALO, MXU_DIM), F32)

        uext_ref[POOL_HALO:POOL_HALO + tm, c:c + MXU_DIM] = p
        t1 = pos0 + 1 + lax.broadcasted_iota(jnp.int32, (tm, POOL_GROUP_DIM), 0)
        for g in range(c // POOL_GROUP_DIM, (c + MXU_DIM) // POOL_GROUP_DIM):
            w = POOL_WINDOWS[g]
            sl = slice(g * POOL_GROUP_DIM, (g + 1) * POOL_GROUP_DIM)
            u = uext_ref[POOL_HALO:POOL_HALO + tm, sl]
            acc = u
            for s in range(1, w):
                acc = acc + uext_ref[POOL_HALO - s:POOL_HALO - s + tm, sl]
            cnt = jnp.minimum(t1, w).astype(F32)
            z = acc / cnt - u
            zp = _dot(z.astype(BF16), poolw_ref[g])
            pool_ref[:, sl] = (zp * pscale_ref[:, sl]).astype(BF16)
        uext_ref[0:POOL_HALO, c:c + MXU_DIM] = uext_ref[tm:tm + POOL_HALO, c:c + MXU_DIM]

    def mem_slab(p, c):
        inv_sqrt = 1.0 / math.sqrt(MEM_HEAD_DIM)
        for half in range(MXU_DIM // LANES):
            sl = slice(c + half * LANES, c + (half + 1) * LANES)
            qn = (_rms_rows(p[:, half * LANES:(half + 1) * LANES], mqn_ref[...]) * inv_sqrt).astype(BF16)
            s = _dot_nt(qn, km_ref[0, :, sl])
            m = jnp.max(s, axis=-1, keepdims=True)
            e = jnp.exp(s - m)
            l = jnp.sum(e, axis=-1, keepdims=True)
            o = _dot(e.astype(BF16), vm_ref[0, :, sl])
            memo_ref[:, sl] = (o / l).astype(BF16)

    slabs = []
    col = 0
    for c in range(0, d_qk, MXU_DIM):
        slabs.append((col + c, functools.partial(qk_slab, c=c, gain_ref=gq_ref, out_ref=q_ref,
                                                 scale=QUERY_SCALE)))
    col += d_qk
    for c in range(0, d_qk, MXU_DIM):
        slabs.append((col + c, functools.partial(qk_slab, c=c, gain_ref=gk_ref, out_ref=k_ref, scale=1.0)))
    col += d_qk
    for c in range(0, d_v, MXU_DIM):
        slabs.append((col + c, functools.partial(v_slab, c=c)))
    col += d_v
    for c in range(0, d_pool, MXU_DIM):
        slabs.append((col + c, functools.partial(pool_slab, c=c)))
    col += d_pool
    for c in range(0, d_mem, MXU_DIM):
        slabs.append((col + c, functools.partial(mem_slab, c=c)))

    def project(col0):
        return _dot(xn_ref[...], w_ref[:, col0:col0 + MXU_DIM])

    cur = project(slabs[0][0])
    for n, (_, epilogue) in enumerate(slabs):
        nxt = project(slabs[n + 1][0]) if n + 1 < len(slabs) else None
        epilogue(cur)
        cur = nxt


def _inproj(x2, g_mix, w_in, q_norm, k_norm, cos_tab, sin_tab, pool_w, pool_scale,
            km, vm, mq_norm, *, seq, tm):
    T, D = x2.shape
    d_qk = DA_HEADS * 2 * DA_HEAD_DIM
    d_v = DA_HEADS * DA_V_DIM
    d_pool = len(POOL_WINDOWS) * POOL_GROUP_DIM
    d_mem = MEM_HEADS * MEM_HEAD_DIM
    d_all = 2 * d_qk + d_v + d_pool + d_mem
    M = km.shape[1]
    tiles_per_seq = seq // tm
    reps = MXU_DIM // DA_HEAD_DIM
    gq = jnp.tile(q_norm, reps).reshape(1, MXU_DIM)
    gk = jnp.tile(k_norm, reps).reshape(1, MXU_DIM)
    grp = jnp.arange(MXU_DIM) // DA_HEAD_DIM
    ones = (grp[:, None] == grp[None, :]).astype(BF16)
    kern = functools.partial(_inproj_kernel, tm=tm, tiles_per_seq=tiles_per_seq,
                             d_qk=d_qk, d_v=d_v, d_pool=d_pool, d_mem=d_mem)
    row = lambda width: pl.BlockSpec((tm, width), lambda i: (i, 0))
    return pl.pallas_call(
        kern,
        grid=(T // tm,),
        in_specs=[row(D),
                  _const_spec((1, D)),
                  _const_spec((D, d_all)),
                  _const_spec((1, MXU_DIM)),
                  _const_spec((1, MXU_DIM)),
                  pl.BlockSpec((tm, LANES), lambda i: (i % tiles_per_seq, 0)),
                  pl.BlockSpec((tm, LANES), lambda i: (i % tiles_per_seq, 0)),
                  _const_spec((MXU_DIM, MXU_DIM)),
                  _const_spec((len(POOL_WINDOWS), POOL_GROUP_DIM, POOL_GROUP_DIM)),
                  _const_spec((1, d_pool)),
                  pl.BlockSpec((1, M, d_mem), lambda i: (i // tiles_per_seq, 0, 0)),
                  pl.BlockSpec((1, M, d_mem), lambda i: (i // tiles_per_seq, 0, 0)),
                  _const_spec((1, MEM_HEAD_DIM))],
        out_specs=[row(d_qk), row(d_qk), row(d_v), row(d_pool), row(d_mem)],
        out_shape=[jax.ShapeDtypeStruct((T, d_qk), BF16),
                   jax.ShapeDtypeStruct((T, d_qk), BF16),
                   jax.ShapeDtypeStruct((T, d_v), BF16),
                   jax.ShapeDtypeStruct((T, d_pool), BF16),
                   jax.ShapeDtypeStruct((T, d_mem), BF16)],
        scratch_shapes=[pltpu.VMEM((tm, D), BF16),
                        pltpu.VMEM((tm + POOL_HALO, d_pool), F32)],
        compiler_params=_cparams("arbitrary"),
        name="inproj",
    )(x2, g_mix.reshape(1, D), w_in[:, :d_all].astype(BF16), gq, gk, cos_tab, sin_tab, ones,
      pool_w.astype(BF16), pool_scale.reshape(1, d_pool), km, vm, mq_norm.reshape(1, MEM_HEAD_DIM))


def _attn_kernel_tiled(lamp_ref, q_ref, k_ref, v_ref, gsub_ref, o_ref, vt_ref, acc_ref,
                       sa_ref, sb_ref, pa_ref, pb_ref, lfin_ref, *, seq, tq):
    lp = lamp_ref[...]
    lam = (jnp.exp(jnp.sum(lp[0:1] * lp[1:2], axis=-1, keepdims=True))
           - jnp.exp(jnp.sum(lp[2:3] * lp[3:4], axis=-1, keepdims=True)) + LAMBDA_INIT)
    sub = lax.broadcasted_iota(jnp.int32, (DA_V_DIM, tq), 0)
    key_i = lax.broadcasted_iota(jnp.int32, (tq, 2 * tq), 0)
    qry_i = lax.broadcasted_iota(jnp.int32, (tq, 2 * tq), 1) % tq
    causal = key_i <= qry_i
    n_blk = seq // tq

    def transpose_v(j, carry):
        r0 = pl.multiple_of(j * tq, tq)
        vt_ref[j] = v_ref[pl.ds(r0, tq), :].astype(F32).T.astype(BF16)
        return carry

    lax.fori_loop(0, n_blk, transpose_v, 0)

    def q_block(qi, _):
        q0 = pl.multiple_of(qi * tq, tq)
        qt = q_ref[pl.ds(q0, tq), :].astype(F32).T
        qcat = jnp.concatenate([jnp.where(sub < DA_HEAD_DIM, qt, 0.0),
                                jnp.where(sub >= DA_HEAD_DIM, qt, 0.0)], axis=1).astype(BF16)
        acc_ref[...] = jnp.zeros(acc_ref.shape, F32)
        pb_ref[...] = jnp.zeros(pb_ref.shape, BF16)

        def scores(j, dst_ref):
            k0 = pl.multiple_of(j * tq, tq)
            dst_ref[...] = _dot(k_ref[pl.ds(k0, tq), :], qcat)

        def values(j, p_ref, alpha):
            acc_ref[...] = alpha * acc_ref[...] + _dot(vt_ref[j], p_ref[...])

        def step(j, state, s_cur, s_next, p_prev, p_cur, *, masked, prefetch):
            m_prev, l_prev, alpha_prev = state
            if prefetch:
                scores(j + 1, s_next)
            s = s_cur[...]
            if masked:
                s = jnp.where(causal, s, NEG_INF)
            m_new = jnp.maximum(m_prev, jnp.max(s, axis=0, keepdims=True))
            alpha = jnp.exp2(m_prev - m_new)
            p = jnp.exp2(s - m_new)
            l_new = alpha * l_prev + jnp.sum(p, axis=0, keepdims=True)
            p_cur[...] = p.astype(BF16)
            values(jnp.maximum(j - 1, 0), p_prev, alpha_prev)
            return m_new, l_new, alpha

        def pair(t, state):
            state = step(2 * t, state, sa_ref, sb_ref, pb_ref, pa_ref, masked=False, prefetch=True)
            return step(2 * t + 1, state, sb_ref, sa_ref, pa_ref, pb_ref, masked=False, prefetch=True)

        scores(0, sa_ref)
        init = (jnp.full((1, 2 * tq), NEG_INF, F32), jnp.zeros((1, 2 * tq), F32),
                jnp.ones((1, 2 * tq), F32))
        state = lax.fori_loop(0, qi // 2, pair, init)

        @pl.when(qi % 2 == 0)
        def _():
            _, l_fin, alpha = step(qi, state, sa_ref, sb_ref, pb_ref, pa_ref, masked=True, prefetch=False)
            values(qi, pa_ref, alpha)
            lfin_ref[...] = l_fin

        @pl.when(qi % 2 == 1)
        def _():
            st = step(qi - 1, state, sa_ref, sb_ref, pb_ref, pa_ref, masked=False, prefetch=True)
            _, l_fin, alpha = step(qi, st, sb_ref, sa_ref, pa_ref, pb_ref, masked=True, prefetch=False)
            values(qi, pb_ref, alpha)
            lfin_ref[...] = l_fin

        on = acc_ref[...] / lfin_ref[...]
        ot = on[:, :tq] - lam * on[:, tq:]
        ot = ot * lax.rsqrt(jnp.mean(ot * ot, axis=0, keepdims=True) + EPS)
        o = ot.T * (gsub_ref[...] * (1.0 - LAMBDA_INIT))
        o_ref[pl.ds(q0, tq), :] = o.astype(BF16)
        return 0

    lax.fori_loop(0, n_blk, q_block, 0)


def _attn_tiled(q, k, v, lam_params, g_subln, *, batch, seq, tq):
    T, d_v = v.shape
    blk = lambda: pl.BlockSpec((seq, DA_V_DIM), lambda b, h: (b, h))
    return pl.pallas_call(
        functools.partial(_attn_kernel_tiled, seq=seq, tq=tq),
        grid=(batch, DA_HEADS),
        in_specs=[_const_spec((4, DA_HEAD_DIM)), blk(), blk(), blk(), _const_spec((1, DA_V_DIM))],
        out_specs=blk(),
        out_shape=jax.ShapeDtypeStruct((T, d_v), BF16),
        scratch_shapes=[pltpu.VMEM((seq // tq, DA_V_DIM, tq), BF16),
                        pltpu.VMEM((DA_V_DIM, 2 * tq), F32),
                        pltpu.VMEM((tq, 2 * tq), F32), pltpu.VMEM((tq, 2 * tq), F32),
                        pltpu.VMEM((tq, 2 * tq), BF16), pltpu.VMEM((tq, 2 * tq), BF16),
                        pltpu.VMEM((1, 2 * tq), F32)],
        compiler_params=_cparams("arbitrary", "arbitrary"),
        name="attn",
    )(lam_params, q, k, v, g_subln.reshape(1, DA_V_DIM))


ATTN_UNROLL = 4


def _attn_tables(n_blk):
    qi = [q for q in range(1, n_blk) for _ in range(q)]
    kj = [j for q in range(1, n_blk) for j in range(q)]
    return (np.asarray(qi + qi[-1:] * 2, np.int32), np.asarray(kj + kj[-1:] * 2, np.int32))


def _attn_kernel(tqi_ref, tkj_ref, lamp_ref, q_ref, k_ref, v_ref, gsub_ref, o_ref,
                 vt_ref, qc_ref, acc_ref, m_ref, l_ref, s0_ref, s1_ref, s2_ref, s3_ref, pa_ref, pb_ref,
                 *, seq, tq, n_full):
    lp = lamp_ref[...]
    lam = (jnp.exp(jnp.sum(lp[0:1] * lp[1:2], axis=-1, keepdims=True))
           - jnp.exp(jnp.sum(lp[2:3] * lp[3:4], axis=-1, keepdims=True)) + LAMBDA_INIT)
    n_blk = seq // tq
    sub = lax.broadcasted_iota(jnp.int32, (DA_V_DIM, tq), 0)
    key_i = lax.broadcasted_iota(jnp.int32, (tq, 2 * tq), 0)
    qry_i = lax.broadcasted_iota(jnp.int32, (tq, 2 * tq), 1) % tq
    causal = key_i <= qry_i
    s_bufs = (s0_ref, s1_ref, s2_ref, s3_ref)
    p_bufs = (pa_ref, pb_ref)

    def setup(t, carry):
        r0 = pl.multiple_of(t * tq, tq)
        vt_ref[t] = v_ref[pl.ds(r0, tq), :].astype(F32).T.astype(BF16)
        qt = q_ref[pl.ds(r0, tq), :].astype(F32).T
        qc_ref[t] = jnp.concatenate([jnp.where(sub < DA_HEAD_DIM, qt, 0.0),
                                     jnp.where(sub >= DA_HEAD_DIM, qt, 0.0)], axis=1).astype(BF16)
        m_ref[t] = jnp.full((1, 2 * tq), NEG_INF, F32)
        l_ref[t] = jnp.zeros((1, 2 * tq), F32)
        acc_ref[t] = jnp.zeros((DA_V_DIM, 2 * tq), F32)
        return carry

    lax.fori_loop(0, n_blk, setup, 0)

    def scores(qi, kj, dst_ref, masked):
        k0 = pl.multiple_of(kj * tq, tq)
        s = _dot(k_ref[pl.ds(k0, tq), :], qc_ref[qi])
        dst_ref[...] = jnp.where(causal, s, NEG_INF) if masked else s

    def softmax(qi, s_ref, p_ref):
        m_prev = m_ref[qi]
        s = s_ref[...]
        m_new = jnp.maximum(m_prev, jnp.max(s, axis=0, keepdims=True))
        alpha = jnp.exp2(m_prev - m_new)
        p = jnp.exp2(s - m_new)
        m_ref[qi] = m_new
        l_ref[qi] = alpha * l_ref[qi] + jnp.sum(p, axis=0, keepdims=True)
        p_ref[...] = p.astype(BF16)
        return alpha

    def values(qi, kj, p_ref, alpha):
        acc_ref[qi] = alpha * acc_ref[qi] + _dot(vt_ref[kj], p_ref[...])

    def finalize(qi):
        on = acc_ref[qi] / l_ref[qi]
        ot = on[:, :tq] - lam * on[:, tq:]
        ot = ot * lax.rsqrt(jnp.mean(ot * ot, axis=0, keepdims=True) + EPS)
        o = ot.T * (gsub_ref[...] * (1.0 - LAMBDA_INIT))
        o_ref[pl.ds(pl.multiple_of(qi * tq, tq), tq), :] = o.astype(BF16)

    ones_row = jnp.ones((1, 2 * tq), F32)

    def full_step(n, u, alpha_prev):
        scores(tqi_ref[n + 2], tkj_ref[n + 2], s_bufs[(u + 2) % 4], False)
        alpha = softmax(tqi_ref[n], s_bufs[u], p_bufs[u % 2])
        prev = jnp.maximum(n - 1, 0)
        values(tqi_ref[prev], tkj_ref[prev], p_bufs[(u + 1) % 2], alpha_prev)
        return alpha

    def full_steps(t, alpha_prev):
        for u in range(ATTN_UNROLL):
            alpha_prev = full_step(ATTN_UNROLL * t + u, u, alpha_prev)
        return alpha_prev

    pb_ref[...] = jnp.zeros(pb_ref.shape, BF16)
    scores(tqi_ref[0], tkj_ref[0], s0_ref, False)
    scores(tqi_ref[1], tkj_ref[1], s1_ref, False)
    alpha_last = lax.fori_loop(0, n_full // ATTN_UNROLL, full_steps, ones_row)
    values(tqi_ref[n_full - 1], tkj_ref[n_full - 1], pb_ref, alpha_last)

    def diag_step(qi, u, alpha_prev):
        nxt = jnp.minimum(qi + 1, n_blk - 1)
        scores(nxt, nxt, s_bufs[1 - u], True)
        alpha = softmax(qi, s_bufs[u], p_bufs[u])
        prev = jnp.maximum(qi - 1, 0)
        values(prev, prev, p_bufs[1 - u], alpha_prev)
        finalize(prev)
        return alpha

    def diag_steps(t, alpha_prev):
        return diag_step(2 * t + 1, 1, diag_step(2 * t, 0, alpha_prev))

    pb_ref[...] = jnp.zeros(pb_ref.shape, BF16)
    scores(0, 0, s0_ref, True)
    alpha_last = lax.fori_loop(0, n_blk // 2, diag_steps, ones_row)
    values(n_blk - 1, n_blk - 1, pb_ref, alpha_last)
    finalize(n_blk - 1)


def _attn(q, k, v, lam_params, g_subln, *, batch, seq, tq):
    T, d_v = v.shape
    n_blk = seq // tq
    tqi, tkj = _attn_tables(n_blk)
    n_full = tqi.shape[0] - 2
    assert n_full % ATTN_UNROLL == 0 and n_blk % 2 == 0, (seq, tq)
    blk = lambda: pl.BlockSpec((seq, DA_V_DIM), lambda b, h, tqi, tkj: (b, h))
    const = lambda shape: pl.BlockSpec(shape, lambda b, h, tqi, tkj: (0,) * len(shape))
    return pl.pallas_call(
        functools.partial(_attn_kernel, seq=seq, tq=tq, n_full=n_full),
        grid_spec=pltpu.PrefetchScalarGridSpec(
            num_scalar_prefetch=2,
            grid=(batch, DA_HEADS),
            in_specs=[const((4, DA_HEAD_DIM)), blk(), blk(), blk(), const((1, DA_V_DIM))],
            out_specs=blk(),
            scratch_shapes=[pltpu.VMEM((n_blk, DA_V_DIM, tq), BF16),
                            pltpu.VMEM((n_blk, DA_V_DIM, 2 * tq), BF16),
                            pltpu.VMEM((n_blk, DA_V_DIM, 2 * tq), F32),
                            pltpu.VMEM((n_blk, 1, 2 * tq), F32),
                            pltpu.VMEM((n_blk, 1, 2 * tq), F32)]
                           + [pltpu.VMEM((tq, 2 * tq), F32)] * 4
                           + [pltpu.VMEM((tq, 2 * tq), BF16)] * 2),
        out_shape=jax.ShapeDtypeStruct((T, d_v), BF16),
        compiler_params=_cparams("arbitrary", "arbitrary"),
        name="attn",
    )(jnp.asarray(tqi), jnp.asarray(tkj), lam_params, q, k, v, g_subln.reshape(1, DA_V_DIM))


def _pack_bf16_pairs(a, b):
    hi = pltpu.bitcast(a.astype(BF16).astype(F32), jnp.uint32)
    lo = pltpu.bitcast(b.astype(BF16).astype(F32), jnp.uint32)
    return hi | (lo >> 16)


def _unpack_bf16_pairs(p):
    a = pltpu.bitcast(p & jnp.uint32(0xFFFF0000), F32)
    b = pltpu.bitcast(p << 16, F32)
    return a, b


def _merge_kernel(x_ref, oa_ref, pool_ref, memo_ref, gmix_ref, wg_ref, bg_ref, wba_ref, wbp_ref,
                  wbm_ref, wout_ref, gffn_ref, wrh_ref, wrl_ref, br_ref,
                  h_ref, hn_ref, idx_ref, rank_ref, gate_ref, cnt_ref,
                  xn_ref, mrg_ref, carry_ref, *, tm, d_model):
    i = pl.program_id(0)
    D = d_model

    @pl.when(i == 0)
    def _():
        carry_ref[...] = jnp.zeros(carry_ref.shape, F32)

    x = x_ref[...]
    xn_ref[...] = _rms_rows(x, gmix_ref[...]).astype(BF16)
    for j in range(D // MXU_DIM):
        sl = slice(j * MXU_DIM, (j + 1) * MXU_DIM)
        merged = None
        for br, (y_ref, w_ref) in enumerate(((oa_ref, wba_ref), (pool_ref, wbp_ref), (memo_ref, wbm_ref))):
            gsl = slice(br * D + j * MXU_DIM, br * D + (j + 1) * MXU_DIM)
            logit = _dot(xn_ref[...], wg_ref[:, gsl]) + bg_ref[:, gsl]
            gate = 1.0 / (1.0 + jnp.exp(-logit))
            term = gate * _dot(y_ref[...], w_ref[:, sl])
            merged = term if merged is None else merged + term
        mrg_ref[:, sl] = merged.astype(BF16)
    h = x + _dot(mrg_ref[...], wout_ref[...])
    h_ref[...] = h
    hn = _rms_rows(h, gffn_ref[...])
    hn_ref[...] = _pack_bf16_pairs(hn[:, :D // 2], hn[:, D // 2:])

    hn_hi = hn.astype(BF16)
    hn_lo = (hn - hn_hi.astype(F32)).astype(BF16)
    logits = (_dot(hn_hi, wrh_ref[...]) + _dot(hn_lo, wrh_ref[...]) + _dot(hn_hi, wrl_ref[...])
              + br_ref[...])

    lane = lax.broadcasted_iota(jnp.int32, (tm, LANES), 1)
    work = logits
    vals, sels = [], []
    idx_out = jnp.zeros((tm, LANES), jnp.int32)
    for kk in range(TOP_K):
        mx = jnp.max(work, axis=-1, keepdims=True)
        idx = jnp.min(jnp.where(work == mx, lane, LANES), axis=-1, keepdims=True)
        sel = lane == idx
        vals.append(mx)
        sels.append(sel)
        idx_out = jnp.where(lane == kk, idx, idx_out)
        work = jnp.where(sel, -jnp.inf, work)
    exps = [jnp.exp(v - vals[0]) for v in vals]
    denom = exps[0] + exps[1] + exps[2] + exps[3]
    gate_out = jnp.zeros((tm, LANES), F32)
    onehot = jnp.zeros((tm, LANES), F32)
    for kk in range(TOP_K):
        gate_out = jnp.where(lane == kk, exps[kk] / denom, gate_out)
        onehot = jnp.where(sels[kk], 1.0, onehot)

    r_i = lax.broadcasted_iota(jnp.int32, (tm, tm), 0)
    c_i = lax.broadcasted_iota(jnp.int32, (tm, tm), 1)
    lower = (c_i < r_i).astype(BF16)
    prefix = _dot(lower, onehot.astype(BF16)) + carry_ref[0:1, :]
    rank_out = jnp.zeros((tm, LANES), jnp.int32)
    for kk in range(TOP_K):
        rk = jnp.sum(jnp.where(sels[kk], prefix, 0.0), axis=-1, keepdims=True)
        rank_out = jnp.where(lane == kk, rk.astype(jnp.int32), rank_out)
    new_carry = prefix[tm - 1:tm, :] + onehot[tm - 1:tm, :]
    carry_ref[...] = jnp.broadcast_to(new_carry, carry_ref.shape)
    idx_ref[...] = idx_out
    rank_ref[...] = rank_out
    gate_ref[...] = gate_out
    cnt_ref[...] = jnp.broadcast_to(new_carry, cnt_ref.shape).astype(jnp.int32)


def _merge(x2, oa, pool, memo, g_mix, w_gates, b_gates, w_b_attn, w_b_pool, w_b_mem, w_out, g_ffn,
           w_router, b_router, *, tm):
    T, D = x2.shape
    d_pool = pool.shape[1]
    d_mem = memo.shape[1]
    E = w_router.shape[1]
    wr = jnp.zeros((D, LANES), F32).at[:, :E].set(w_router)
    wr_hi = wr.astype(BF16)
    wr_lo = (wr - wr_hi.astype(F32)).astype(BF16)
    br = jnp.full((1, LANES), NEG_INF, F32).at[0, :E].set(b_router)
    row = lambda width: pl.BlockSpec((tm, width), lambda i: (i, 0))
    sublanes = 8
    return pl.pallas_call(
        functools.partial(_merge_kernel, tm=tm, d_model=D),
        grid=(T // tm,),
        in_specs=[row(D), row(oa.shape[1]), row(d_pool), row(d_mem),
                  _const_spec((1, D)),
                  _const_spec((D, 3 * D)), _const_spec((1, 3 * D)),
                  _const_spec((oa.shape[1], D)), _const_spec((d_pool, D)), _const_spec((d_mem, D)),
                  _const_spec((D, D)), _const_spec((1, D)),
                  _const_spec((D, LANES)), _const_spec((D, LANES)), _const_spec((1, LANES))],
        out_specs=[row(D), row(D // 2), row(LANES), row(LANES), row(LANES),
                   _const_spec((sublanes, LANES))],
        out_shape=[jax.ShapeDtypeStruct((T, D), F32),
                   jax.ShapeDtypeStruct((T, D // 2), jnp.uint32),
                   jax.ShapeDtypeStruct((T, LANES), jnp.int32),
                   jax.ShapeDtypeStruct((T, LANES), jnp.int32),
                   jax.ShapeDtypeStruct((T, LANES), F32),
                   jax.ShapeDtypeStruct((sublanes, LANES), jnp.int32)],
        scratch_shapes=[pltpu.VMEM((tm, D), BF16),
                        pltpu.VMEM((tm, D), BF16),
                        pltpu.VMEM((sublanes, LANES), F32)],
        compiler_params=_cparams("arbitrary"),
        name="merge",
    )(x2, oa, pool, memo, g_mix.reshape(1, D), w_gates.astype(BF16), b_gates.reshape(1, 3 * D),
      w_b_attn.astype(BF16), w_b_pool.astype(BF16), w_b_mem.astype(BF16), w_out.astype(BF16),
      g_ffn.reshape(1, D), wr_hi, wr_lo, br)


def _dispatch_kernel(poff_ref, nv_ref, idx_ref, rank_ref, hn_ref, xs_ref, zero_ref, sem, zsem,
                     *, tm, bm, n_blocks):
    def row_copy(r, dest):
        return pltpu.make_async_copy(hn_ref.at[pl.ds(r, 1)], xs_ref.at[pl.ds(dest, 1)], sem)

    def zero_copy(blk):
        return pltpu.make_async_copy(zero_ref, xs_ref.at[pl.ds(pl.multiple_of(blk * bm, bm), bm)], zsem)

    @pl.when(pl.program_id(0) == 0)
    def _():
        zero_ref[...] = jnp.zeros(zero_ref.shape, zero_ref.dtype)

        def start(blk, n):
            @pl.when(nv_ref[blk] < bm)
            def _():
                zero_copy(blk).start()
            return n + (nv_ref[blk] < bm).astype(jnp.int32)

        n_zeroed = lax.fori_loop(0, n_blocks, start, 0)

        def finish(_, carry):
            zero_copy(0).wait()
            return carry

        lax.fori_loop(0, n_zeroed, finish, 0)

    def issue(r, carry):
        for kk in range(TOP_K):
            a = r * TOP_K + kk
            row_copy(r, poff_ref[idx_ref[a]] + rank_ref[a]).start(priority=kk % 2)
        return carry

    lax.fori_loop(0, tm, issue, 0, unroll=ROW_DMA_UNROLL)
    for _ in range(TOP_K):
        pltpu.make_async_copy(hn_ref, xs_ref.at[pl.ds(0, tm)], sem).wait()


def _dispatch(poff, blk_valid, idx_flat, rank_flat, hn, *, bm, tm):
    T, half = hn.shape
    n_blocks = blk_valid.shape[0]
    flat = pl.BlockSpec((tm * TOP_K,), lambda i, poff, nv: (i,), memory_space=pltpu.SMEM)
    return pl.pallas_call(
        functools.partial(_dispatch_kernel, tm=tm, bm=bm, n_blocks=n_blocks),
        grid_spec=pltpu.PrefetchScalarGridSpec(
            num_scalar_prefetch=2,
            grid=(T // tm,),
            in_specs=[flat, flat, pl.BlockSpec((tm, half), lambda i, poff, nv: (i, 0))],
            out_specs=pl.BlockSpec(memory_space=pl.ANY),
            scratch_shapes=[pltpu.VMEM((bm, half), jnp.uint32),
                            pltpu.SemaphoreType.DMA, pltpu.SemaphoreType.DMA]),
        out_shape=jax.ShapeDtypeStruct((n_blocks * bm, half), jnp.uint32),
        compiler_params=_cparams("arbitrary"),
        name="dispatch",
    )(poff, blk_valid, idx_flat, rank_flat, hn)


def _experts_kernel(be_ref, nv_ref, src_ref, x_ref, wg_ref, bg_ref, wu_ref, bu_ref, wd_ref, bd_ref,
                    y_ref, act_ref, *, bm, d_model, d_expert):
    i = pl.program_id(0)
    half = d_model // 2

    @pl.when(nv_ref[i] == 0)
    def _():
        y_ref[...] = jnp.zeros(y_ref.shape, y_ref.dtype)

    @pl.when(nv_ref[i] > 0)
    def _():
        a, b = _unpack_bf16_pairs(x_ref[...])
        valid = lax.broadcasted_iota(jnp.int32, (bm, half), 0) < nv_ref[i]
        a = jnp.where(valid, a, 0.0).astype(BF16)
        b = jnp.where(valid, b, 0.0).astype(BF16)
        for j in range(d_expert // MXU_DIM):
            sl = slice(j * MXU_DIM, (j + 1) * MXU_DIM)
            g = _dot(a, wg_ref[0, :half, sl]) + _dot(b, wg_ref[0, half:, sl]) + bg_ref[0, :, sl]
            u = _dot(a, wu_ref[0, :half, sl]) + _dot(b, wu_ref[0, half:, sl]) + bu_ref[0, :, sl]
            g = jnp.minimum(g, SWIGLU_LIMIT)
            u = jnp.clip(u, -SWIGLU_LIMIT, SWIGLU_LIMIT)
            act = g * (1.0 / (1.0 + jnp.exp(-SWIGLU_ALPHA * g))) * (u + 1.0)
            act_ref[:, sl] = act.astype(BF16)
        out = _dot(act_ref[...], wd_ref[0]) + bd_ref[0]
        y_ref[...] = _pack_bf16_pairs(out[:, :half], out[:, half:])


def _experts(blk_e, blk_valid, blk_src, xs, w_gate, b_gate, w_up, b_up, w_down, b_down, *, bm):
    rows_sorted, half = xs.shape
    E, D, De = w_gate.shape
    n_blocks = rows_sorted // bm
    wspec = lambda r, c: pl.BlockSpec((1, r, c), lambda i, be, nv, src: (be[i], 0, 0))
    xspec = pl.BlockSpec((bm, half), lambda i, be, nv, src: (src[i], 0))
    return pl.pallas_call(
        functools.partial(_experts_kernel, bm=bm, d_model=D, d_expert=De),
        grid_spec=pltpu.PrefetchScalarGridSpec(
            num_scalar_prefetch=3,
            grid=(n_blocks,),
            in_specs=[xspec, wspec(D, De), wspec(1, De), wspec(D, De), wspec(1, De),
                      wspec(De, D), wspec(1, D)],
            out_specs=pl.BlockSpec((bm, half), lambda i, be, nv, src: (i, 0)),
            scratch_shapes=[pltpu.VMEM((bm, De), BF16)]),
        out_shape=jax.ShapeDtypeStruct((rows_sorted, half), jnp.uint32),
        compiler_params=_cparams("arbitrary"),
        name="experts",
    )(blk_e, blk_valid, blk_src, xs, w_gate.astype(BF16), b_gate.reshape(E, 1, De),
      w_up.astype(BF16), b_up.reshape(E, 1, De), w_down.astype(BF16), b_down.reshape(E, 1, D))


def _combine_kernel(poff_ref, idx_ref, rank_ref, gate_ref, h_ref, ys_ref, o_ref, buf_ref, sem, *, tm, d_model):
    half = d_model // 2

    def row_copy(src, kk, r):
        return pltpu.make_async_copy(ys_ref.at[pl.ds(src, 1)], buf_ref.at[kk, pl.ds(r, 1)], sem)

    def issue(r, carry):
        for kk in range(TOP_K):
            a = r * TOP_K + kk
            row_copy(poff_ref[idx_ref[a]] + rank_ref[a], kk, r).start(priority=kk % 2)
        return carry

    lax.fori_loop(0, tm, issue, 0, unroll=ROW_DMA_UNROLL)
    for kk in range(TOP_K):
        pltpu.make_async_copy(ys_ref.at[pl.ds(0, tm)], buf_ref.at[kk], sem).wait()
    h = h_ref[...]
    lo, hi = h[:, :half], h[:, half:]
    gates = gate_ref[...]
    for kk in range(TOP_K):
        a, b = _unpack_bf16_pairs(buf_ref[kk])
        w = gates[:, kk:kk + 1]
        lo = lo + w * a
        hi = hi + w * b
    o_ref[:, :half] = lo
    o_ref[:, half:] = hi


def _combine(poff, idx_flat, rank_flat, gate, h, ys, *, tm):
    T, D = h.shape
    flat = pl.BlockSpec((tm * TOP_K,), lambda i, poff: (i,), memory_space=pltpu.SMEM)
    row = lambda width: pl.BlockSpec((tm, width), lambda i, poff: (i, 0))
    return pl.pallas_call(
        functools.partial(_combine_kernel, tm=tm, d_model=D),
        grid_spec=pltpu.PrefetchScalarGridSpec(
            num_scalar_prefetch=1,
            grid=(T // tm,),
            in_specs=[flat, flat, row(LANES), row(D), pl.BlockSpec(memory_space=pl.ANY)],
            out_specs=row(D),
            scratch_shapes=[pltpu.VMEM((TOP_K, tm, D // 2), jnp.uint32), pltpu.SemaphoreType.DMA]),
        out_shape=jax.ShapeDtypeStruct((T, D), F32),
        compiler_params=_cparams("arbitrary"),
        name="combine",
    )(poff, idx_flat, rank_flat, gate, h, ys)


SC_CORES = 2
SC_SUBCORES = 16
SC_CHUNK = 128


def _sc_mesh():
    return plsc.VectorSubcoreMesh(core_axis_name="c", subcore_axis_name="s",
                                  num_cores=SC_CORES, num_subcores=SC_SUBCORES)


def _sc_scatter_rows(rows, dest, *, rows_out):
    T, width = rows.shape
    n_chunks = T // SC_CHUNK
    per_worker = n_chunks // (SC_CORES * SC_SUBCORES)

    @functools.partial(
        pl.kernel, mesh=_sc_mesh(),
        out_type=jax.ShapeDtypeStruct((rows_out, width), rows.dtype),
        scratch_types=[pltpu.VMEM((TOP_K, SC_CHUNK), jnp.int32),
                       pltpu.VMEM((SC_CHUNK, width), rows.dtype)],
        name="sc_dispatch")
    def scatter(rows_hbm, dest_hbm, out_hbm, idx_v, rows_v):
        worker = lax.axis_index("s") * SC_CORES + lax.axis_index("c")

        @pl.loop(0, per_worker)
        def _(n):
            c = worker * per_worker + n
            pltpu.sync_copy(rows_hbm.at[pl.ds(c * SC_CHUNK, SC_CHUNK)], rows_v)
            pltpu.sync_copy(dest_hbm.at[c], idx_v)
            for kk in range(TOP_K):
                pltpu.sync_copy(rows_v, out_hbm.at[idx_v.at[kk]])

    return scatter(rows, dest)


def _sc_gather_rows(table, src, *, n_tokens):
    _, width = table.shape
    n_chunks = n_tokens // SC_CHUNK
    per_worker = n_chunks // (SC_CORES * SC_SUBCORES)

    @functools.partial(
        pl.kernel, mesh=_sc_mesh(),
        out_type=jax.ShapeDtypeStruct((TOP_K, n_tokens, width), table.dtype),
        scratch_types=[pltpu.VMEM((TOP_K, SC_CHUNK), jnp.int32),
                       pltpu.VMEM((SC_CHUNK, width), table.dtype)],
        name="sc_combine")
    def gather(table_hbm, src_hbm, out_hbm, idx_v, rows_v):
        worker = lax.axis_index("s") * SC_CORES + lax.axis_index("c")

        @pl.loop(0, per_worker)
        def _(n):
            c = worker * per_worker + n
            pltpu.sync_copy(src_hbm.at[c], idx_v)
            for kk in range(TOP_K):
                pltpu.sync_copy(table_hbm.at[idx_v.at[kk]], rows_v)
                pltpu.sync_copy(rows_v, out_hbm.at[kk, pl.ds(c * SC_CHUNK, SC_CHUNK)])

    return gather(table, src)


def _weighted_sum_kernel(gate_ref, h_ref, y_ref, o_ref, *, d_model):
    half = d_model // 2
    h = h_ref[...]
    lo, hi = h[:, :half], h[:, half:]
    gates = gate_ref[...]
    for kk in range(TOP_K):
        a, b = _unpack_bf16_pairs(y_ref[kk])
        w = gates[:, kk:kk + 1]
        lo = lo + w * a
        hi = hi + w * b
    o_ref[:, :half] = lo
    o_ref[:, half:] = hi


def _weighted_sum(gate, h, y4, *, tm):
    T, D = h.shape
    row = lambda width: pl.BlockSpec((tm, width), lambda i: (i, 0))
    return pl.pallas_call(
        functools.partial(_weighted_sum_kernel, d_model=D),
        grid=(T // tm,),
        in_specs=[row(LANES), row(D), pl.BlockSpec((TOP_K, tm, D // 2), lambda i: (0, i, 0))],
        out_specs=row(D),
        out_shape=jax.ShapeDtypeStruct((T, D), F32),
        compiler_params=_cparams("arbitrary"),
        name="weighted_sum",
    )(gate, h, y4)


def _block_tables(counts, *, bm, n_blocks):
    padded = ((counts + bm - 1) // bm) * bm
    pend = jnp.cumsum(padded)
    poff = pend - padded
    n_used = pend[-1] // bm
    src = jnp.minimum(jnp.arange(n_blocks, dtype=jnp.int32), n_used - 1)
    blk_e = jnp.sum(pend[None, :] <= (src * bm)[:, None], axis=1)
    blk_e = jnp.minimum(blk_e, N_EXPERTS - 1).astype(jnp.int32)
    blk_valid = jnp.clip(poff[blk_e] + counts[blk_e] - src * bm, 0, bm)
    blk_valid = jnp.where(jnp.arange(n_blocks) < n_used, blk_valid, 0).astype(jnp.int32)
    return poff.astype(jnp.int32), blk_e, blk_valid, src.astype(jnp.int32)


def _rope_tables(seq):
    inv_freq = ROPE_THETA ** (-jnp.arange(0, DA_HEAD_DIM, 2, dtype=F32) / DA_HEAD_DIM)
    ang = jnp.arange(seq, dtype=F32)[:, None] * inv_freq[None, :]
    reps = LANES // (DA_HEAD_DIM // 2)
    cos = jnp.tile(jnp.cos(ang), (1, reps))
    sin = jnp.tile(jnp.sin(ang), (1, reps))
    first_half = (jnp.arange(LANES) % DA_HEAD_DIM) < DA_HEAD_DIM // 2
    return cos, jnp.where(first_half[None, :], -sin, sin)


ROW_TILE = 512
ATTN_TILE = 256
EXPERT_BLOCK = 512
MOVE_TILE = 256
ROW_DMA_UNROLL = 4


def _forward(x, mem, g_mix, w_in, b_gates, q_norm, k_norm, lambda_q1, lambda_k1, lambda_q2, lambda_k2, g_subln, pool_w, pool_scale, g_mem, w_mem_kv, mq_norm, mk_norm, w_b_attn, w_b_pool, w_b_mem, w_out, g_ffn, w_router, b_router, w_gate, b_gate, w_up, b_up, w_down, b_down):
    B, S, D = x.shape
    T = B * S
    x2 = x.reshape(T, D)
    cos_tab, sin_tab = _rope_tables(S)
    km, vm = _memkv(mem, g_mem, w_mem_kv, mk_norm)
    q, k, v, pool, memo = _inproj(x2, g_mix, w_in, q_norm, k_norm, cos_tab, sin_tab, pool_w,
                                  pool_scale, km, vm, mq_norm, seq=S, tm=ROW_TILE)
    lam_params = jnp.stack([lambda_q1, lambda_k1, lambda_q2, lambda_k2])
    o = _attn(q, k, v, lam_params, g_subln, batch=B, seq=S, tq=ATTN_TILE)
    d_in = q.shape[1] + k.shape[1] + v.shape[1] + pool.shape[1] + memo.shape[1]
    h1, hn, idx, rank, gate, cnt = _merge(x2, o, pool, memo, g_mix, w_in[:, d_in:], b_gates, w_b_attn,
                                          w_b_pool, w_b_mem, w_out, g_ffn, w_router, b_router,
                                          tm=ROW_TILE)
    n_blocks = (T * TOP_K) // EXPERT_BLOCK + N_EXPERTS
    poff, blk_e, blk_valid, blk_src = _block_tables(cnt[0, :N_EXPERTS], bm=EXPERT_BLOCK, n_blocks=n_blocks)
    idx_flat = idx[:, :TOP_K].reshape(T * TOP_K)
    rank_flat = rank[:, :TOP_K].reshape(T * TOP_K)
    dest = (poff[idx[:, :TOP_K]] + rank[:, :TOP_K]).astype(jnp.int32)
    dest = dest.reshape(T // SC_CHUNK, SC_CHUNK, TOP_K).transpose(0, 2, 1)
    xs = _sc_scatter_rows(hn, dest, rows_out=n_blocks * EXPERT_BLOCK)
    ys = _experts(blk_e, blk_valid, blk_src, xs, w_gate, b_gate, w_up, b_up, w_down, b_down,
                  bm=EXPERT_BLOCK)
    y4 = _sc_gather_rows(ys, dest, n_tokens=T)
    out = _weighted_sum(gate, h1, y4, tm=ROW_TILE)
    return dict(q=q, k=k, v=v, pool=pool, memo=memo, o=o, h1=h1, hn=hn, idx=idx, rank=rank,
                gate=gate, cnt=cnt, out=out.reshape(B, S, D))


def kernel(x, mem, g_mix, w_in, b_gates, q_norm, k_norm, lambda_q1, lambda_k1, lambda_q2, lambda_k2, g_subln, pool_w, pool_scale, g_mem, w_mem_kv, mq_norm, mk_norm, w_b_attn, w_b_pool, w_b_mem, w_out, g_ffn, w_router, b_router, w_gate, b_gate, w_up, b_up, w_down, b_down):
    return _forward(x, mem, g_mix, w_in, b_gates, q_norm, k_norm, lambda_q1, lambda_k1, lambda_q2,
                    lambda_k2, g_subln, pool_w, pool_scale, g_mem, w_mem_kv, mq_norm, mk_norm,
                    w_b_attn, w_b_pool, w_b_mem, w_out, g_ffn, w_router, b_router, w_gate, b_gate,
                    w_up, b_up, w_down, b_down)["out"]
```

```python
import functools
import math

import jax
import jax.numpy as jnp
import numpy as np
from jax import lax
from jax.experimental import pallas as pl
from jax.experimental.pallas import tpu as pltpu
from jax.experimental.pallas import tpu_sc as plsc

DA_HEADS = 8
DA_HEAD_DIM = 64
DA_V_DIM = 2 * DA_HEAD_DIM
ROPE_THETA = 10000.0
POOL_WINDOWS = (2, 4, 8, 16)
POOL_GROUP_DIM = 128
POOL_HALO = 16
PROJECT_AHEAD = 1
MEM_HEADS = 4
MEM_HEAD_DIM = 128
N_EXPERTS = 32
TOP_K = 4
SWIGLU_LIMIT = 7.0
SWIGLU_ALPHA = 1.702
LAMBDA_INIT = 0.8 - 0.6 * math.exp(-0.3 * 0.0)
EPS = 1e-6
NEG_INF = -1e30
QUERY_SCALE = math.log2(math.e) / math.sqrt(DA_HEAD_DIM)

LANES = 128
MXU_DIM = 256
VMEM_LIMIT_BYTES = 56 * 1024 * 1024

BF16 = jnp.bfloat16
F32 = jnp.float32


def _cparams(*sem):
    return pltpu.CompilerParams(dimension_semantics=sem, vmem_limit_bytes=VMEM_LIMIT_BYTES)


def _const_spec(shape):
    nd = len(shape)
    return pl.BlockSpec(shape, lambda *_: (0,) * nd)


def _dot(a, b):
    return jnp.dot(a, b, preferred_element_type=F32)


def _dot_nt(a, b):
    return lax.dot_general(a, b, (((1,), (1,)), ((), ())), preferred_element_type=F32)


def _rms_rows(x, gain):
    ms = jnp.mean(x * x, axis=-1, keepdims=True)
    return x * lax.rsqrt(ms + EPS) * gain


def _memkv_kernel(mem_ref, gmem_ref, w_ref, mkn_ref, km_ref, vm_ref):
    mem_dim = MEM_HEADS * MEM_HEAD_DIM
    mn = _rms_rows(mem_ref[0], gmem_ref[...]).astype(BF16)
    kv = _dot(mn, w_ref[...])
    for h in range(MEM_HEADS):
        sl = slice(h * MEM_HEAD_DIM, (h + 1) * MEM_HEAD_DIM)
        km_ref[0, :, sl] = _rms_rows(kv[:, sl], mkn_ref[...]).astype(BF16)
    vm_ref[0] = kv[:, mem_dim:].astype(BF16)


def _memkv(mem, g_mem, w_mem_kv, mk_norm):
    B, M, D = mem.shape
    mem_dim = MEM_HEADS * MEM_HEAD_DIM
    out = jax.ShapeDtypeStruct((B, M, mem_dim), BF16)
    return pl.pallas_call(
        _memkv_kernel,
        grid=(B,),
        in_specs=[pl.BlockSpec((1, M, D), lambda b: (b, 0, 0)),
                  _const_spec((1, D)),
                  _const_spec((D, 2 * mem_dim)),
                  _const_spec((1, MEM_HEAD_DIM))],
        out_specs=[pl.BlockSpec((1, M, mem_dim), lambda b: (b, 0, 0))] * 2,
        out_shape=[out, out],
        compiler_params=_cparams("arbitrary"),
        name="memkv",
    )(mem, g_mem.reshape(1, D), w_mem_kv.astype(BF16), mk_norm.reshape(1, MEM_HEAD_DIM))


def _swap32(x):
    lane = lax.broadcasted_iota(jnp.int32, x.shape, 1)
    low = (lane & 32) == 0
    return jnp.where(low, pltpu.roll(x, LANES - 32, 1), pltpu.roll(x, 32, 1))


def _inproj_kernel(x_ref, gmix_ref, w_ref, gq_ref, gk_ref, cos_ref, sin_ref, ones_ref,
                   poolw_ref, pscale_ref, km_ref, vm_ref, mqn_ref,
                   q_ref, k_ref, v_ref, pool_ref, memo_ref,
                   xn_ref, uext_ref, *, tm, tiles_per_seq, d_qk, d_v, d_pool, d_mem):
    i = pl.program_id(0)
    pos0 = (i % tiles_per_seq) * tm
    xn_ref[...] = _rms_rows(x_ref[...], gmix_ref[...]).astype(BF16)

    cos = cos_ref[...]
    sin = sin_ref[...]
    ones = ones_ref[...]

    def qk_slab(p, c, gain_ref, out_ref, scale):
        ss = _dot((p * p).astype(BF16), ones)
        n = p * (lax.rsqrt(ss * (1.0 / DA_HEAD_DIM) + EPS) * scale) * gain_ref[...]
        for half in range(MXU_DIM // LANES):
            nh = n[:, half * LANES:(half + 1) * LANES]
            r = nh * cos + _swap32(nh) * sin
            out_ref[:, c + half * LANES:c + (half + 1) * LANES] = r.astype(BF16)

    def v_slab(p, c):
        v_ref[:, c:c + MXU_DIM] = p.astype(BF16)

    def pool_slab(p, c):
        @pl.when(pos0 == 0)
        def _():
            uext_ref[0:POOL_HALO, c:c + MXU_DIM] = jnp.zeros((POOL_HALO, MXU_DIM), F32)

        uext_ref[POOL_HALO:POOL_HALO + tm, c:c + MXU_DIM] = p
        t1 = pos0 + 1 + lax.broadcasted_iota(jnp.int32, (tm, POOL_GROUP_DIM), 0)
        for g in range(c // POOL_GROUP_DIM, (c + MXU_DIM) // POOL_GROUP_DIM):
            w = POOL_WINDOWS[g]
            sl = slice(g * POOL_GROUP_DIM, (g + 1) * POOL_GROUP_DIM)
            u = uext_ref[POOL_HALO:POOL_HALO + tm, sl]
            acc = u
            for s in range(1, w):
                acc = acc + uext_ref[POOL_HALO - s:POOL_HALO - s + tm, sl]
            cnt = jnp.minimum(t1, w).astype(F32)
            z = acc / cnt - u
            zp = _dot(z.astype(BF16), poolw_ref[g])
            pool_ref[:, sl] = (zp * pscale_ref[:, sl]).astype(BF16)
        uext_ref[0:POOL_HALO, c:c + MXU_DIM] = uext_ref[tm:tm + POOL_HALO, c:c + MXU_DIM]

    def mem_slab(p, c):
        inv_sqrt = 1.0 / math.sqrt(MEM_HEAD_DIM)
        for half in range(MXU_DIM // LANES):
            sl = slice(c + half * LANES, c + (half + 1) * LANES)
            qn = (_rms_rows(p[:, half * LANES:(half + 1) * LANES], mqn_ref[...]) * inv_sqrt).astype(BF16)
            s = _dot_nt(qn, km_ref[0, :, sl])
            m = jnp.max(s, axis=-1, keepdims=True)
            e = jnp.exp(s - m)
            l = jnp.sum(e, axis=-1, keepdims=True)
            o = _dot(e.astype(BF16), vm_ref[0, :, sl])
            memo_ref[:, sl] = (o / l).astype(BF16)

    col = 0
    q_slabs = [(col + c, functools.partial(qk_slab, c=c, gain_ref=gq_ref, out_ref=q_ref, scale=QUERY_SCALE))
               for c in range(0, d_qk, MXU_DIM)]
    col += d_qk
    k_slabs = [(col + c, functools.partial(qk_slab, c=c, gain_ref=gk_ref, out_ref=k_ref, scale=1.0))
               for c in range(0, d_qk, MXU_DIM)]
    col += d_qk
    v_slabs = [(col + c, functools.partial(v_slab, c=c)) for c in range(0, d_v, MXU_DIM)]
    col += d_v
    pool_slabs = [(col + c, functools.partial(pool_slab, c=c)) for c in range(0, d_pool, MXU_DIM)]
    col += d_pool
    mem_slabs = [(col + c, functools.partial(mem_slab, c=c)) for c in range(0, d_mem, MXU_DIM)]
    slabs = q_slabs + k_slabs + v_slabs + pool_slabs + mem_slabs

    def project(col0):
        return _dot(xn_ref[...], w_ref[:, col0:col0 + MXU_DIM])

    queue = [project(c0) for c0, _ in slabs[:PROJECT_AHEAD]]
    for n, (_, epilogue) in enumerate(slabs):
        if n + PROJECT_AHEAD < len(slabs):
            queue.append(project(slabs[n + PROJECT_AHEAD][0]))
        epilogue(queue.pop(0))


def _inproj(x2, g_mix, w_in, q_norm, k_norm, cos_tab, sin_tab, pool_w, pool_scale,
            km, vm, mq_norm, *, seq, tm):
    T, D = x2.shape
    d_qk = DA_HEADS * 2 * DA_HEAD_DIM
    d_v = DA_HEADS * DA_V_DIM
    d_pool = len(POOL_WINDOWS) * POOL_GROUP_DIM
    d_mem = MEM_HEADS * MEM_HEAD_DIM
    d_all = 2 * d_qk + d_v + d_pool + d_mem
    M = km.shape[1]
    tiles_per_seq = seq // tm
    reps = MXU_DIM // DA_HEAD_DIM
    gq = jnp.tile(q_norm, reps).reshape(1, MXU_DIM)
    gk = jnp.tile(k_norm, reps).reshape(1, MXU_DIM)
    grp = jnp.arange(MXU_DIM) // DA_HEAD_DIM
    ones = (grp[:, None] == grp[None, :]).astype(BF16)
    kern = functools.partial(_inproj_kernel, tm=tm, tiles_per_seq=tiles_per_seq,
                             d_qk=d_qk, d_v=d_v, d_pool=d_pool, d_mem=d_mem)
    row = lambda width: pl.BlockSpec((tm, width), lambda i: (i, 0))
    return pl.pallas_call(
        kern,
        grid=(T // tm,),
        in_specs=[row(D),
                  _const_spec((1, D)),
                  _const_spec((D, d_all)),
                  _const_spec((1, MXU_DIM)),
                  _const_spec((1, MXU_DIM)),
                  pl.BlockSpec((tm, LANES), lambda i: (i % tiles_per_seq, 0)),
                  pl.BlockSpec((tm, LANES), lambda i: (i % tiles_per_seq, 0)),
                  _const_spec((MXU_DIM, MXU_DIM)),
                  _const_spec((len(POOL_WINDOWS), POOL_GROUP_DIM, POOL_GROUP_DIM)),
                  _const_spec((1, d_pool)),
                  pl.BlockSpec((1, M, d_mem), lambda i: (i // tiles_per_seq, 0, 0)),
                  pl.BlockSpec((1, M, d_mem), lambda i: (i // tiles_per_seq, 0, 0)),
                  _const_spec((1, MEM_HEAD_DIM))],
        out_specs=[row(d_qk), row(d_qk), row(d_v), row(d_pool), row(d_mem)],
        out_shape=[jax.ShapeDtypeStruct((T, d_qk), BF16),
                   jax.ShapeDtypeStruct((T, d_qk), BF16),
                   jax.ShapeDtypeStruct((T, d_v), BF16),
                   jax.ShapeDtypeStruct((T, d_pool), BF16),
                   jax.ShapeDtypeStruct((T, d_mem), BF16)],
        scratch_shapes=[pltpu.VMEM((tm, D), BF16),
                        pltpu.VMEM((tm + POOL_HALO, d_pool), F32)],
        compiler_params=_cparams("arbitrary"),
        name="inproj",
    )(x2, g_mix.reshape(1, D), w_in[:, :d_all].astype(BF16), gq, gk, cos_tab, sin_tab, ones,
      pool_w.astype(BF16), pool_scale.reshape(1, d_pool), km, vm, mq_norm.reshape(1, MEM_HEAD_DIM))


def _attn_kernel_tiled(lamp_ref, q_ref, k_ref, v_ref, gsub_ref, o_ref, vt_ref, acc_ref,
                       sa_ref, sb_ref, pa_ref, pb_ref, lfin_ref, *, seq, tq):
    lp = lamp_ref[...]
    lam = (jnp.exp(jnp.sum(lp[0:1] * lp[1:2], axis=-1, keepdims=True))
           - jnp.exp(jnp.sum(lp[2:3] * lp[3:4], axis=-1, keepdims=True)) + LAMBDA_INIT)
    sub = lax.broadcasted_iota(jnp.int32, (DA_V_DIM, tq), 0)
    key_i = lax.broadcasted_iota(jnp.int32, (tq, 2 * tq), 0)
    qry_i = lax.broadcasted_iota(jnp.int32, (tq, 2 * tq), 1) % tq
    causal = key_i <= qry_i
    n_blk = seq // tq

    def transpose_v(j, carry):
        r0 = pl.multiple_of(j * tq, tq)
        vt_ref[j] = v_ref[pl.ds(r0, tq), :].astype(F32).T.astype(BF16)
        return carry

    lax.fori_loop(0, n_blk, transpose_v, 0)

    def q_block(qi, _):
        q0 = pl.multiple_of(qi * tq, tq)
        qt = q_ref[pl.ds(q0, tq), :].astype(F32).T
        qcat = jnp.concatenate([jnp.where(sub < DA_HEAD_DIM, qt, 0.0),
                                jnp.where(sub >= DA_HEAD_DIM, qt, 0.0)], axis=1).astype(BF16)
        acc_ref[...] = jnp.zeros(acc_ref.shape, F32)
        pb_ref[...] = jnp.zeros(pb_ref.shape, BF16)

        def scores(j, dst_ref):
            k0 = pl.multiple_of(j * tq, tq)
            dst_ref[...] = _dot(k_ref[pl.ds(k0, tq), :], qcat)

        def values(j, p_ref, alpha):
            acc_ref[...] = alpha * acc_ref[...] + _dot(vt_ref[j], p_ref[...])

        def step(j, state, s_cur, s_next, p_prev, p_cur, *, masked, prefetch):
            m_prev, l_prev, alpha_prev = state
            if prefetch:
                scores(j + 1, s_next)
            s = s_cur[...]
            if masked:
                s = jnp.where(causal, s, NEG_INF)
            m_new = jnp.maximum(m_prev, jnp.max(s, axis=0, keepdims=True))
            alpha = jnp.exp2(m_prev - m_new)
            p = jnp.exp2(s - m_new)
            l_new = alpha * l_prev + jnp.sum(p, axis=0, keepdims=True)
            p_cur[...] = p.astype(BF16)
            values(jnp.maximum(j - 1, 0), p_prev, alpha_prev)
            return m_new, l_new, alpha

        def pair(t, state):
            state = step(2 * t, state, sa_ref, sb_ref, pb_ref, pa_ref, masked=False, prefetch=True)
            return step(2 * t + 1, state, sb_ref, sa_ref, pa_ref, pb_ref, masked=False, prefetch=True)

        scores(0, sa_ref)
        init = (jnp.full((1, 2 * tq), NEG_INF, F32), jnp.zeros((1, 2 * tq), F32),
                jnp.ones((1, 2 * tq), F32))
        state = lax.fori_loop(0, qi // 2, pair, init)

        @pl.when(qi % 2 == 0)
        def _():
            _, l_fin, alpha = step(qi, state, sa_ref, sb_ref, pb_ref, pa_ref, masked=True, prefetch=False)
            values(qi, pa_ref, alpha)
            lfin_ref[...] = l_fin

        @pl.when(qi % 2 == 1)
        def _():
            st = step(qi - 1, state, sa_ref, sb_ref, pb_ref, pa_ref, masked=False, prefetch=True)
            _, l_fin, alpha = step(qi, st, sb_ref, sa_ref, pa_ref, pb_ref, masked=True, prefetch=False)
            values(qi, pb_ref, alpha)
            lfin_ref[...] = l_fin

        on = acc_ref[...] / lfin_ref[...]
        ot = on[:, :tq] - lam * on[:, tq:]
        ot = ot * lax.rsqrt(jnp.mean(ot * ot, axis=0, keepdims=True) + EPS)
        o = ot.T * (gsub_ref[...] * (1.0 - LAMBDA_INIT))
        o_ref[pl.ds(q0, tq), :] = o.astype(BF16)
        return 0

    lax.fori_loop(0, n_blk, q_block, 0)


def _attn_tiled(q, k, v, lam_params, g_subln, *, batch, seq, tq):
    T, d_v = v.shape
    blk = lambda: pl.BlockSpec((seq, DA_V_DIM), lambda b, h: (b, h))
    return pl.pallas_call(
        functools.partial(_attn_kernel_tiled, seq=seq, tq=tq),
        grid=(batch, DA_HEADS),
        in_specs=[_const_spec((4, DA_HEAD_DIM)), blk(), blk(), blk(), _const_spec((1, DA_V_DIM))],
        out_specs=blk(),
        out_shape=jax.ShapeDtypeStruct((T, d_v), BF16),
        scratch_shapes=[pltpu.VMEM((seq // tq, DA_V_DIM, tq), BF16),
                        pltpu.VMEM((DA_V_DIM, 2 * tq), F32),
                        pltpu.VMEM((tq, 2 * tq), F32), pltpu.VMEM((tq, 2 * tq), F32),
                        pltpu.VMEM((tq, 2 * tq), BF16), pltpu.VMEM((tq, 2 * tq), BF16),
                        pltpu.VMEM((1, 2 * tq), F32)],
        compiler_params=_cparams("arbitrary", "arbitrary"),
        name="attn",
    )(lam_params, q, k, v, g_subln.reshape(1, DA_V_DIM))


ATTN_UNROLL = 4


def _attn_tables(n_blk):
    qi = [q for q in range(1, n_blk) for _ in range(q)]
    kj = [j for q in range(1, n_blk) for j in range(q)]
    return (np.asarray(qi + qi[-1:] * 2, np.int32), np.asarray(kj + kj[-1:] * 2, np.int32))


def _attn_kernel(tqi_ref, tkj_ref, lamp_ref, q_ref, k_ref, v_ref, gsub_ref, o_ref,
                 vt_ref, qc_ref, acc_ref, m_ref, l_ref, s0_ref, s1_ref, s2_ref, s3_ref, pa_ref, pb_ref,
                 *, seq, tq, n_full):
    lp = lamp_ref[...]
    lam = (jnp.exp(jnp.sum(lp[0:1] * lp[1:2], axis=-1, keepdims=True))
           - jnp.exp(jnp.sum(lp[2:3] * lp[3:4], axis=-1, keepdims=True)) + LAMBDA_INIT)
    n_blk = seq // tq
    sub = lax.broadcasted_iota(jnp.int32, (DA_V_DIM, tq), 0)
    key_i = lax.broadcasted_iota(jnp.int32, (tq, 2 * tq), 0)
    qry_i = lax.broadcasted_iota(jnp.int32, (tq, 2 * tq), 1) % tq
    causal = key_i <= qry_i
    s_bufs = (s0_ref, s1_ref, s2_ref, s3_ref)
    p_bufs = (pa_ref, pb_ref)

    def setup(t, carry):
        r0 = pl.multiple_of(t * tq, tq)
        vt_ref[t] = v_ref[pl.ds(r0, tq), :].astype(F32).T.astype(BF16)
        qt = q_ref[pl.ds(r0, tq), :].astype(F32).T
        qc_ref[t] = jnp.concatenate([jnp.where(sub < DA_HEAD_DIM, qt, 0.0),
                                     jnp.where(sub >= DA_HEAD_DIM, qt, 0.0)], axis=1).astype(BF16)
        m_ref[t] = jnp.full((1, 2 * tq), NEG_INF, F32)
        l_ref[t] = jnp.zeros((1, 2 * tq), F32)
        acc_ref[t] = jnp.zeros((DA_V_DIM, 2 * tq), F32)
        return carry

    lax.fori_loop(0, n_blk, setup, 0)

    def scores(qi, kj, dst_ref, masked):
        k0 = pl.multiple_of(kj * tq, tq)
        s = _dot(k_ref[pl.ds(k0, tq), :], qc_ref[qi])
        dst_ref[...] = jnp.where(causal, s, NEG_INF) if masked else s

    def softmax(qi, s_ref, p_ref):
        m_prev = m_ref[qi]
        s = s_ref[...]
        m_new = jnp.maximum(m_prev, jnp.max(s, axis=0, keepdims=True))
        alpha = jnp.exp2(m_prev - m_new)
        p = jnp.exp2(s - m_new)
        m_ref[qi] = m_new
        l_ref[qi] = alpha * l_ref[qi] + jnp.sum(p, axis=0, keepdims=True)
        p_ref[...] = p.astype(BF16)
        return alpha

    def values(qi, kj, p_ref, alpha):
        acc_ref[qi] = alpha * acc_ref[qi] + _dot(vt_ref[kj], p_ref[...])

    def finalize(qi):
        on = acc_ref[qi] / l_ref[qi]
        ot = on[:, :tq] - lam * on[:, tq:]
        ot = ot * lax.rsqrt(jnp.mean(ot * ot, axis=0, keepdims=True) + EPS)
        o = ot.T * (gsub_ref[...] * (1.0 - LAMBDA_INIT))
        o_ref[pl.ds(pl.multiple_of(qi * tq, tq), tq), :] = o.astype(BF16)

    ones_row = jnp.ones((1, 2 * tq), F32)

    def full_step(n, u, alpha_prev):
        scores(tqi_ref[n + 2], tkj_ref[n + 2], s_bufs[(u + 2) % 4], False)
        alpha = softmax(tqi_ref[n], s_bufs[u], p_bufs[u % 2])
        prev = jnp.maximum(n - 1, 0)
        values(tqi_ref[prev], tkj_ref[prev], p_bufs[(u + 1) % 2], alpha_prev)
        return alpha

    def full_steps(t, alpha_prev):
        for u in range(ATTN_UNROLL):
            alpha_prev = full_step(ATTN_UNROLL * t + u, u, alpha_prev)
        return alpha_prev

    pb_ref[...] = jnp.zeros(pb_ref.shape, BF16)
    scores(tqi_ref[0], tkj_ref[0], s0_ref, False)
    scores(tqi_ref[1], tkj_ref[1], s1_ref, False)
    alpha_last = lax.fori_loop(0, n_full // ATTN_UNROLL, full_steps, ones_row)
    values(tqi_ref[n_full - 1], tkj_ref[n_full - 1], pb_ref, alpha_last)

    def diag_step(qi, u, alpha_prev):
        nxt = jnp.minimum(qi + 1, n_blk - 1)
        scores(nxt, nxt, s_bufs[1 - u], True)
        alpha = softmax(qi, s_bufs[u], p_bufs[u])
        prev = jnp.maximum(qi - 1, 0)
        values(prev, prev, p_bufs[1 - u], alpha_prev)
        finalize(prev)
        return alpha

    def diag_steps(t, alpha_prev):
        return diag_step(2 * t + 1, 1, diag_step(2 * t, 0, alpha_prev))

    pb_ref[...] = jnp.zeros(pb_ref.shape, BF16)
    scores(0, 0, s0_ref, True)
    alpha_last = lax.fori_loop(0, n_blk // 2, diag_steps, ones_row)
    values(n_blk - 1, n_blk - 1, pb_ref, alpha_last)
    finalize(n_blk - 1)


def _attn(q, k, v, lam_params, g_subln, *, batch, seq, tq):
    T, d_v = v.shape
    n_blk = seq // tq
    tqi, tkj = _attn_tables(n_blk)
    n_full = tqi.shape[0] - 2
    assert n_full % ATTN_UNROLL == 0 and n_blk % 2 == 0, (seq, tq)
    blk = lambda: pl.BlockSpec((seq, DA_V_DIM), lambda b, h, tqi, tkj: (b, h))
    const = lambda shape: pl.BlockSpec(shape, lambda b, h, tqi, tkj: (0,) * len(shape))
    return pl.pallas_call(
        functools.partial(_attn_kernel, seq=seq, tq=tq, n_full=n_full),
        grid_spec=pltpu.PrefetchScalarGridSpec(
            num_scalar_prefetch=2,
            grid=(batch, DA_HEADS),
            in_specs=[const((4, DA_HEAD_DIM)), blk(), blk(), blk(), const((1, DA_V_DIM))],
            out_specs=blk(),
            scratch_shapes=[pltpu.VMEM((n_blk, DA_V_DIM, tq), BF16),
                            pltpu.VMEM((n_blk, DA_V_DIM, 2 * tq), BF16),
                            pltpu.VMEM((n_blk, DA_V_DIM, 2 * tq), F32),
                            pltpu.VMEM((n_blk, 1, 2 * tq), F32),
                            pltpu.VMEM((n_blk, 1, 2 * tq), F32)]
                           + [pltpu.VMEM((tq, 2 * tq), F32)] * 4
                           + [pltpu.VMEM((tq, 2 * tq), BF16)] * 2),
        out_shape=jax.ShapeDtypeStruct((T, d_v), BF16),
        compiler_params=_cparams("arbitrary", "arbitrary"),
        name="attn",
    )(jnp.asarray(tqi), jnp.asarray(tkj), lam_params, q, k, v, g_subln.reshape(1, DA_V_DIM))


def _pack_bf16_pairs(a, b):
    hi = pltpu.bitcast(a.astype(BF16).astype(F32), jnp.uint32)
    lo = pltpu.bitcast(b.astype(BF16).astype(F32), jnp.uint32)
    return hi | (lo >> 16)


def _unpack_bf16_pairs(p):
    a = pltpu.bitcast(p & jnp.uint32(0xFFFF0000), F32)
    b = pltpu.bitcast(p << 16, F32)
    return a, b


def _merge_kernel(x_ref, oa_ref, pool_ref, memo_ref, gmix_ref, wg_ref, bg_ref, wba_ref, wbp_ref,
                  wbm_ref, wout_ref, gffn_ref, wrh_ref, wrl_ref, br_ref,
                  h_ref, hn_ref, idx_ref, rank_ref, gate_ref, cnt_ref,
                  xn_ref, mrg_ref, hs_ref, carry_ref, *, tm, d_model):
    i = pl.program_id(0)
    D = d_model

    @pl.when(i == 0)
    def _():
        carry_ref[...] = jnp.zeros(carry_ref.shape, F32)
        hs_ref[...] = jnp.zeros(hs_ref.shape, F32)

    def logits():
        hn = _rms_rows(hs_ref[...], gffn_ref[...])
        hn_ref[...] = _pack_bf16_pairs(hn[:, :D // 2], hn[:, D // 2:])
        hn_hi = hn.astype(BF16)
        hn_lo = (hn - hn_hi.astype(F32)).astype(BF16)
        return (_dot(hn_hi, wrh_ref[...]) + _dot(hn_lo, wrh_ref[...])
                + _dot(hn_hi, wrl_ref[...]) + br_ref[...])

    route = _route_pieces(logits, i > 0, idx_ref, rank_ref, gate_ref, cnt_ref, carry_ref, tm)

    x = x_ref[...]
    xn_ref[...] = _rms_rows(x, gmix_ref[...]).astype(BF16)
    branches = ((oa_ref, wba_ref), (pool_ref, wbp_ref), (memo_ref, wbm_ref))

    def slab_dots(j):
        sl = slice(j * MXU_DIM, (j + 1) * MXU_DIM)
        out = []
        for br, (y_ref, w_ref) in enumerate(branches):
            gsl = slice(br * D + j * MXU_DIM, br * D + (j + 1) * MXU_DIM)
            out.append((_dot(xn_ref[...], wg_ref[:, gsl]), _dot(y_ref[...], w_ref[:, sl])))
        return out

    def slab_epilogue(j, dots):
        merged = None
        for br, (logit, y) in enumerate(dots):
            gsl = slice(br * D + j * MXU_DIM, br * D + (j + 1) * MXU_DIM)
            term = y / (1.0 + jnp.exp(-(logit + bg_ref[:, gsl])))
            merged = term if merged is None else merged + term
        mrg_ref[:, j * MXU_DIM:(j + 1) * MXU_DIM] = merged.astype(BF16)

    n_slabs = D // MXU_DIM
    dots = slab_dots(0)
    for j in range(n_slabs):
        nxt = slab_dots(j + 1) if j + 1 < n_slabs else None
        route[j]()
        slab_epilogue(j, dots)
        dots = nxt
    for piece in route[n_slabs:-1]:
        piece()
    h = x + _dot(mrg_ref[...], wout_ref[...])
    route[-1]()
    h_ref[...] = h
    hs_ref[...] = h


def _route_pieces(logits_fn, live, idx_ref, rank_ref, gate_ref, cnt_ref, carry_ref, tm):
    lane = lax.broadcasted_iota(jnp.int32, (tm, LANES), 1)
    st = dict(vals=[], sels=[], idx_out=jnp.zeros((tm, LANES), jnp.int32))

    def start():
        st["work"] = logits_fn()

    def pick(kk):
        work = st["work"]
        mx = jnp.max(work, axis=-1, keepdims=True)
        idx = jnp.min(jnp.where(work == mx, lane, LANES), axis=-1, keepdims=True)
        sel = lane == idx
        st["vals"].append(mx)
        st["sels"].append(sel)
        st["idx_out"] = jnp.where(lane == kk, idx, st["idx_out"])
        st["work"] = jnp.where(sel, -jnp.inf, work)

    def finish():
        vals, sels = st["vals"], st["sels"]
        exps = [jnp.exp(v - vals[0]) for v in vals]
        denom = exps[0] + exps[1] + exps[2] + exps[3]
        gate_out = jnp.zeros((tm, LANES), F32)
        onehot = jnp.zeros((tm, LANES), F32)
        for kk in range(TOP_K):
            gate_out = jnp.where(lane == kk, exps[kk] / denom, gate_out)
            onehot = jnp.where(sels[kk], 1.0, onehot)
        r_i = lax.broadcasted_iota(jnp.int32, (tm, tm), 0)
        c_i = lax.broadcasted_iota(jnp.int32, (tm, tm), 1)
        lower = (c_i < r_i).astype(BF16)
        prefix = _dot(lower, onehot.astype(BF16)) + carry_ref[0:1, :]
        rank_out = jnp.zeros((tm, LANES), jnp.int32)
        for kk in range(TOP_K):
            rk = jnp.sum(jnp.where(sels[kk], prefix, 0.0), axis=-1, keepdims=True)
            rank_out = jnp.where(lane == kk, rk.astype(jnp.int32), rank_out)
        new_carry = prefix[tm - 1:tm, :] + onehot[tm - 1:tm, :]
        new_carry = jnp.where(live, new_carry, carry_ref[0:1, :])
        carry_ref[...] = jnp.broadcast_to(new_carry, carry_ref.shape)
        idx_ref[...] = st["idx_out"]
        rank_ref[...] = rank_out
        gate_ref[...] = gate_out
        cnt_ref[...] = jnp.broadcast_to(new_carry, cnt_ref.shape).astype(jnp.int32)

    return [start] + [functools.partial(pick, kk) for kk in range(TOP_K)] + [finish]


def _merge(x2, oa, pool, memo, g_mix, w_gates, b_gates, w_b_attn, w_b_pool, w_b_mem, w_out, g_ffn,
           w_router, b_router, *, tm):
    T, D = x2.shape
    d_pool = pool.shape[1]
    d_mem = memo.shape[1]
    E = w_router.shape[1]
    wr = jnp.zeros((D, LANES), F32).at[:, :E].set(w_router)
    wr_hi = wr.astype(BF16)
    wr_lo = (wr - wr_hi.astype(F32)).astype(BF16)
    br = jnp.full((1, LANES), NEG_INF, F32).at[0, :E].set(b_router)
    n_tiles = T // tm
    row = lambda width: pl.BlockSpec((tm, width), lambda i: (jnp.minimum(i, n_tiles - 1), 0))
    routed = lambda width: pl.BlockSpec((tm, width), lambda i: (jnp.maximum(i - 1, 0), 0))
    sublanes = 8
    return pl.pallas_call(
        functools.partial(_merge_kernel, tm=tm, d_model=D),
        grid=(n_tiles + 1,),
        in_specs=[row(D), row(oa.shape[1]), row(d_pool), row(d_mem),
                  _const_spec((1, D)),
                  _const_spec((D, 3 * D)), _const_spec((1, 3 * D)),
                  _const_spec((oa.shape[1], D)), _const_spec((d_pool, D)), _const_spec((d_mem, D)),
                  _const_spec((D, D)), _const_spec((1, D)),
                  _const_spec((D, LANES)), _const_spec((D, LANES)), _const_spec((1, LANES))],
        out_specs=[row(D), routed(D // 2), routed(LANES), routed(LANES), routed(LANES),
                   _const_spec((sublanes, LANES))],
        out_shape=[jax.ShapeDtypeStruct((T, D), F32),
                   jax.ShapeDtypeStruct((T, D // 2), jnp.uint32),
                   jax.ShapeDtypeStruct((T, LANES), jnp.int32),
                   jax.ShapeDtypeStruct((T, LANES), jnp.int32),
                   jax.ShapeDtypeStruct((T, LANES), F32),
                   jax.ShapeDtypeStruct((sublanes, LANES), jnp.int32)],
        scratch_shapes=[pltpu.VMEM((tm, D), BF16),
                        pltpu.VMEM((tm, D), BF16),
                        pltpu.VMEM((tm, D), F32),
                        pltpu.VMEM((sublanes, LANES), F32)],
        compiler_params=_cparams("arbitrary"),
        name="merge",
    )(x2, oa, pool, memo, g_mix.reshape(1, D), w_gates.astype(BF16), b_gates.reshape(1, 3 * D),
      w_b_attn.astype(BF16), w_b_pool.astype(BF16), w_b_mem.astype(BF16), w_out.astype(BF16),
      g_ffn.reshape(1, D), wr_hi, wr_lo, br)


def _dispatch_kernel(poff_ref, nv_ref, idx_ref, rank_ref, hn_ref, xs_ref, zero_ref, sem, zsem,
                     *, tm, bm, n_blocks):
    def row_copy(r, dest):
        return pltpu.make_async_copy(hn_ref.at[pl.ds(r, 1)], xs_ref.at[pl.ds(dest, 1)], sem)

    def zero_copy(blk):
        return pltpu.make_async_copy(zero_ref, xs_ref.at[pl.ds(pl.multiple_of(blk * bm, bm), bm)], zsem)

    @pl.when(pl.program_id(0) == 0)
    def _():
        zero_ref[...] = jnp.zeros(zero_ref.shape, zero_ref.dtype)

        def start(blk, n):
            @pl.when(nv_ref[blk] < bm)
            def _():
                zero_copy(blk).start()
            return n + (nv_ref[blk] < bm).astype(jnp.int32)

        n_zeroed = lax.fori_loop(0, n_blocks, start, 0)

        def finish(_, carry):
            zero_copy(0).wait()
            return carry

        lax.fori_loop(0, n_zeroed, finish, 0)

    def issue(r, carry):
        for kk in range(TOP_K):
            a = r * TOP_K + kk
            row_copy(r, poff_ref[idx_ref[a]] + rank_ref[a]).start(priority=kk % 2)
        return carry

    lax.fori_loop(0, tm, issue, 0, unroll=ROW_DMA_UNROLL)
    for _ in range(TOP_K):
        pltpu.make_async_copy(hn_ref, xs_ref.at[pl.ds(0, tm)], sem).wait()


def _dispatch(poff, blk_valid, idx_flat, rank_flat, hn, *, bm, tm):
    T, half = hn.shape
    n_blocks = blk_valid.shape[0]
    flat = pl.BlockSpec((tm * TOP_K,), lambda i, poff, nv: (i,), memory_space=pltpu.SMEM)
    return pl.pallas_call(
        functools.partial(_dispatch_kernel, tm=tm, bm=bm, n_blocks=n_blocks),
        grid_spec=pltpu.PrefetchScalarGridSpec(
            num_scalar_prefetch=2,
            grid=(T // tm,),
            in_specs=[flat, flat, pl.BlockSpec((tm, half), lambda i, poff, nv: (i, 0))],
            out_specs=pl.BlockSpec(memory_space=pl.ANY),
            scratch_shapes=[pltpu.VMEM((bm, half), jnp.uint32),
                            pltpu.SemaphoreType.DMA, pltpu.SemaphoreType.DMA]),
        out_shape=jax.ShapeDtypeStruct((n_blocks * bm, half), jnp.uint32),
        compiler_params=_cparams("arbitrary"),
        name="dispatch",
    )(poff, blk_valid, idx_flat, rank_flat, hn)


def _experts_kernel(be_ref, nv_ref, src_ref, x_ref, wg_ref, bg_ref, wu_ref, bu_ref, wd_ref, bd_ref,
                    y_ref, act_ref, *, bm, d_model, d_expert):
    i = pl.program_id(0)
    half = d_model // 2

    @pl.when(nv_ref[i] == 0)
    def _():
        y_ref[...] = jnp.zeros(y_ref.shape, y_ref.dtype)

    @pl.when(nv_ref[i] > 0)
    def _():
        a, b = _unpack_bf16_pairs(x_ref[...])
        valid = lax.broadcasted_iota(jnp.int32, (bm, half), 0) < nv_ref[i]
        a = jnp.where(valid, a, 0.0).astype(BF16)
        b = jnp.where(valid, b, 0.0).astype(BF16)
        for j in range(d_expert // MXU_DIM):
            sl = slice(j * MXU_DIM, (j + 1) * MXU_DIM)
            g = _dot(a, wg_ref[0, :half, sl]) + _dot(b, wg_ref[0, half:, sl]) + bg_ref[0, :, sl]
            u = _dot(a, wu_ref[0, :half, sl]) + _dot(b, wu_ref[0, half:, sl]) + bu_ref[0, :, sl]
            g = jnp.minimum(g, SWIGLU_LIMIT)
            u = jnp.clip(u, -SWIGLU_LIMIT, SWIGLU_LIMIT)
            act = g * (1.0 / (1.0 + jnp.exp(-SWIGLU_ALPHA * g))) * (u + 1.0)
            act_ref[:, sl] = act.astype(BF16)
        out = _dot(act_ref[...], wd_ref[0]) + bd_ref[0]
        y_ref[...] = _pack_bf16_pairs(out[:, :half], out[:, half:])


def _experts(blk_e, blk_valid, blk_src, xs, w_gate, b_gate, w_up, b_up, w_down, b_down, *, bm):
    rows_sorted, half = xs.shape
    E, D, De = w_gate.shape
    n_blocks = rows_sorted // bm
    wspec = lambda r, c: pl.BlockSpec((1, r, c), lambda i, be, nv, src: (be[i], 0, 0))
    xspec = pl.BlockSpec((bm, half), lambda i, be, nv, src: (src[i], 0))
    return pl.pallas_call(
        functools.partial(_experts_kernel, bm=bm, d_model=D, d_expert=De),
        grid_spec=pltpu.PrefetchScalarGridSpec(
            num_scalar_prefetch=3,
            grid=(n_blocks,),
            in_specs=[xspec, wspec(D, De), wspec(1, De), wspec(D, De), wspec(1, De),
                      wspec(De, D), wspec(1, D)],
            out_specs=pl.BlockSpec((bm, half), lambda i, be, nv, src: (i, 0)),
            scratch_shapes=[pltpu.VMEM((bm, De), BF16)]),
        out_shape=jax.ShapeDtypeStruct((rows_sorted, half), jnp.uint32),
        compiler_params=_cparams("arbitrary"),
        name="experts",
    )(blk_e, blk_valid, blk_src, xs, w_gate.astype(BF16), b_gate.reshape(E, 1, De),
      w_up.astype(BF16), b_up.reshape(E, 1, De), w_down.astype(BF16), b_down.reshape(E, 1, D))


def _combine_kernel(poff_ref, idx_ref, rank_ref, gate_ref, h_ref, ys_ref, o_ref, buf_ref, sem, *, tm, d_model):
    half = d_model // 2

    def row_copy(src, kk, r):
        return pltpu.make_async_copy(ys_ref.at[pl.ds(src, 1)], buf_ref.at[kk, pl.ds(r, 1)], sem)

    def issue(r, carry):
        for kk in range(TOP_K):
            a = r * TOP_K + kk
            row_copy(poff_ref[idx_ref[a]] + rank_ref[a], kk, r).start(priority=kk % 2)
        return carry

    lax.fori_loop(0, tm, issue, 0, unroll=ROW_DMA_UNROLL)
    for kk in range(TOP_K):
        pltpu.make_async_copy(ys_ref.at[pl.ds(0, tm)], buf_ref.at[kk], sem).wait()
    h = h_ref[...]
    lo, hi = h[:, :half], h[:, half:]
    gates = gate_ref[...]
    for kk in range(TOP_K):
        a, b = _unpack_bf16_pairs(buf_ref[kk])
        w = gates[:, kk:kk + 1]
        lo = lo + w * a
        hi = hi + w * b
    o_ref[:, :half] = lo
    o_ref[:, half:] = hi


def _combine(poff, idx_flat, rank_flat, gate, h, ys, *, tm):
    T, D = h.shape
    flat = pl.BlockSpec((tm * TOP_K,), lambda i, poff: (i,), memory_space=pltpu.SMEM)
    row = lambda width: pl.BlockSpec((tm, width), lambda i, poff: (i, 0))
    return pl.pallas_call(
        functools.partial(_combine_kernel, tm=tm, d_model=D),
        grid_spec=pltpu.PrefetchScalarGridSpec(
            num_scalar_prefetch=1,
            grid=(T // tm,),
            in_specs=[flat, flat, row(LANES), row(D), pl.BlockSpec(memory_space=pl.ANY)],
            out_specs=row(D),
            scratch_shapes=[pltpu.VMEM((TOP_K, tm, D // 2), jnp.uint32), pltpu.SemaphoreType.DMA]),
        out_shape=jax.ShapeDtypeStruct((T, D), F32),
        compiler_params=_cparams("arbitrary"),
        name="combine",
    )(poff, idx_flat, rank_flat, gate, h, ys)


SC_CORES = 2
SC_SUBCORES = 16
SC_CHUNK = 128


def _sc_mesh():
    return plsc.VectorSubcoreMesh(core_axis_name="c", subcore_axis_name="s",
                                  num_cores=SC_CORES, num_subcores=SC_SUBCORES)


def _sc_scatter_rows(rows, dest, *, rows_out):
    T, width = rows.shape
    n_chunks = T // SC_CHUNK
    per_worker = n_chunks // (SC_CORES * SC_SUBCORES)

    @functools.partial(
        pl.kernel, mesh=_sc_mesh(),
        out_type=jax.ShapeDtypeStruct((rows_out, width), rows.dtype),
        scratch_types=[pltpu.VMEM((TOP_K, SC_CHUNK), jnp.int32),
                       pltpu.VMEM((SC_CHUNK, width), rows.dtype)],
        name="sc_dispatch")
    def scatter(rows_hbm, dest_hbm, out_hbm, idx_v, rows_v):
        worker = lax.axis_index("s") * SC_CORES + lax.axis_index("c")

        @pl.loop(0, per_worker)
        def _(n):
            c = worker * per_worker + n
            pltpu.sync_copy(rows_hbm.at[pl.ds(c * SC_CHUNK, SC_CHUNK)], rows_v)
            pltpu.sync_copy(dest_hbm.at[c], idx_v)
            for kk in range(TOP_K):
                pltpu.sync_copy(rows_v, out_hbm.at[idx_v.at[kk]])

    return scatter(rows, dest)


def _sc_gather_rows(table, src, *, n_tokens):
    _, width = table.shape
    n_chunks = n_tokens // SC_CHUNK
    per_worker = n_chunks // (SC_CORES * SC_SUBCORES)

    @functools.partial(
        pl.kernel, mesh=_sc_mesh(),
        out_type=jax.ShapeDtypeStruct((TOP_K, n_tokens, width), table.dtype),
        scratch_types=[pltpu.VMEM((TOP_K, SC_CHUNK), jnp.int32),
                       pltpu.VMEM((SC_CHUNK, width), table.dtype)],
        name="sc_combine")
    def gather(table_hbm, src_hbm, out_hbm, idx_v, rows_v):
        worker = lax.axis_index("s") * SC_CORES + lax.axis_index("c")

        @pl.loop(0, per_worker)
        def _(n):
            c = worker * per_worker + n
            pltpu.sync_copy(src_hbm.at[c], idx_v)
            for kk in range(TOP_K):
                pltpu.sync_copy(table_hbm.at[idx_v.at[kk]], rows_v)
                pltpu.sync_copy(rows_v, out_hbm.at[kk, pl.ds(c * SC_CHUNK, SC_CHUNK)])

    return gather(table, src)


def _weighted_sum_kernel(gate_ref, h_ref, y_ref, o_ref, *, d_model):
    half = d_model // 2
    h = h_ref[...]
    lo, hi = h[:, :half], h[:, half:]
    gates = gate_ref[...]
    for kk in range(TOP_K):
        a, b = _unpack_bf16_pairs(y_ref[kk])
        w = gates[:, kk:kk + 1]
        lo = lo + w * a
        hi = hi + w * b
    o_ref[:, :half] = lo
    o_ref[:, half:] = hi


def _weighted_sum(gate, h, y4, *, tm):
    T, D = h.shape
    row = lambda width: pl.BlockSpec((tm, width), lambda i: (i, 0))
    return pl.pallas_call(
        functools.partial(_weighted_sum_kernel, d_model=D),
        grid=(T // tm,),
        in_specs=[row(LANES), row(D), pl.BlockSpec((TOP_K, tm, D // 2), lambda i: (0, i, 0))],
        out_specs=row(D),
        out_shape=jax.ShapeDtypeStruct((T, D), F32),
        compiler_params=_cparams("arbitrary"),
        name="weighted_sum",
    )(gate, h, y4)


def _block_tables(counts, *, bm, n_blocks):
    padded = ((counts + bm - 1) // bm) * bm
    pend = jnp.cumsum(padded)
    poff = pend - padded
    n_used = pend[-1] // bm
    src = jnp.minimum(jnp.arange(n_blocks, dtype=jnp.int32), n_used - 1)
    blk_e = jnp.sum(pend[None, :] <= (src * bm)[:, None], axis=1)
    blk_e = jnp.minimum(blk_e, N_EXPERTS - 1).astype(jnp.int32)
    blk_valid = jnp.clip(poff[blk_e] + counts[blk_e] - src * bm, 0, bm)
    blk_valid = jnp.where(jnp.arange(n_blocks) < n_used, blk_valid, 0).astype(jnp.int32)
    return poff.astype(jnp.int32), blk_e, blk_valid, src.astype(jnp.int32)


def _rope_tables(seq):
    inv_freq = ROPE_THETA ** (-jnp.arange(0, DA_HEAD_DIM, 2, dtype=F32) / DA_HEAD_DIM)
    ang = jnp.arange(seq, dtype=F32)[:, None] * inv_freq[None, :]
    reps = LANES // (DA_HEAD_DIM // 2)
    cos = jnp.tile(jnp.cos(ang), (1, reps))
    sin = jnp.tile(jnp.sin(ang), (1, reps))
    first_half = (jnp.arange(LANES) % DA_HEAD_DIM) < DA_HEAD_DIM // 2
    return cos, jnp.where(first_half[None, :], -sin, sin)


ROW_TILE = 512
ATTN_TILE = 256
EXPERT_BLOCK = 512
MOVE_TILE = 256
ROW_DMA_UNROLL = 4


def _forward(x, mem, g_mix, w_in, b_gates, q_norm, k_norm, lambda_q1, lambda_k1, lambda_q2, lambda_k2, g_subln, pool_w, pool_scale, g_mem, w_mem_kv, mq_norm, mk_norm, w_b_attn, w_b_pool, w_b_mem, w_out, g_ffn, w_router, b_router, w_gate, b_gate, w_up, b_up, w_down, b_down):
    B, S, D = x.shape
    T = B * S
    x2 = x.reshape(T, D)
    cos_tab, sin_tab = _rope_tables(S)
    km, vm = _memkv(mem, g_mem, w_mem_kv, mk_norm)
    q, k, v, pool, memo = _inproj(x2, g_mix, w_in, q_norm, k_norm, cos_tab, sin_tab, pool_w,
                                  pool_scale, km, vm, mq_norm, seq=S, tm=ROW_TILE)
    lam_params = jnp.stack([lambda_q1, lambda_k1, lambda_q2, lambda_k2])
    o = _attn(q, k, v, lam_params, g_subln, batch=B, seq=S, tq=ATTN_TILE)
    d_in = q.shape[1] + k.shape[1] + v.shape[1] + pool.shape[1] + memo.shape[1]
    h1, hn, idx, rank, gate, cnt = _merge(x2, o, pool, memo, g_mix, w_in[:, d_in:], b_gates, w_b_attn,
                                          w_b_pool, w_b_mem, w_out, g_ffn, w_router, b_router,
                                          tm=ROW_TILE)
    n_blocks = (T * TOP_K) // EXPERT_BLOCK + N_EXPERTS
    poff, blk_e, blk_valid, blk_src = _block_tables(cnt[0, :N_EXPERTS], bm=EXPERT_BLOCK, n_blocks=n_blocks)
    idx_flat = idx[:, :TOP_K].reshape(T * TOP_K)
    rank_flat = rank[:, :TOP_K].reshape(T * TOP_K)
    dest = (poff[idx[:, :TOP_K]] + rank[:, :TOP_K]).astype(jnp.int32)
    dest = dest.reshape(T // SC_CHUNK, SC_CHUNK, TOP_K).transpose(0, 2, 1)
    xs = _sc_scatter_rows(hn, dest, rows_out=n_blocks * EXPERT_BLOCK)
    ys = _experts(blk_e, blk_valid, blk_src, xs, w_gate, b_gate, w_up, b_up, w_down, b_down,
                  bm=EXPERT_BLOCK)
    y4 = _sc_gather_rows(ys, dest, n_tokens=T)
    out = _weighted_sum(gate, h1, y4, tm=ROW_TILE)
    return dict(q=q, k=k, v=v, pool=pool, memo=memo, o=o, h1=h1, hn=hn, idx=idx, rank=rank,
                gate=gate, cnt=cnt, out=out.reshape(B, S, D))


def kernel(x, mem, g_mix, w_in, b_gates, q_norm, k_norm, lambda_q1, lambda_k1, lambda_q2, lambda_k2, g_subln, pool_w, pool_scale, g_mem, w_mem_kv, mq_norm, mk_norm, w_b_attn, w_b_pool, w_b_mem, w_out, g_ffn, w_router, b_router, w_gate, b_gate, w_up, b_up, w_down, b_down):
    return _forward(x, mem, g_mix, w_in, b_gates, q_norm, k_norm, lambda_q1, lambda_k1, lambda_q2,
                    lambda_k2, g_subln, pool_w, pool_scale, g_mem, w_mem_kv, mq_norm, mk_norm,
                    w_b_attn, w_b_pool, w_b_mem, w_out, g_ffn, w_router, b_router, w_gate, b_gate,
                    w_up, b_up, w_down, b_down)["out"]
```

```python
import functools
import math

import jax
import jax.numpy as jnp
import numpy as np
from jax import lax
from jax.experimental import pallas as pl
from jax.experimental.pallas import tpu as pltpu
from jax.experimental.pallas import tpu_sc as plsc

DA_HEADS = 8
DA_HEAD_DIM = 64
DA_V_DIM = 2 * DA_HEAD_DIM
ROPE_THETA = 10000.0
POOL_WINDOWS = (2, 4, 8, 16)
POOL_GROUP_DIM = 128
POOL_HALO = 16
PROJECT_AHEAD = 1
MEM_HEADS = 4
MEM_HEAD_DIM = 128
N_EXPERTS = 32
TOP_K = 4
SWIGLU_LIMIT = 7.0
SWIGLU_ALPHA = 1.702
LAMBDA_INIT = 0.8 - 0.6 * math.exp(-0.3 * 0.0)
EPS = 1e-6
NEG_INF = -1e30
QUERY_SCALE = math.log2(math.e) / math.sqrt(DA_HEAD_DIM)

LANES = 128
MXU_DIM = 256
VMEM_LIMIT_BYTES = 56 * 1024 * 1024

BF16 = jnp.bfloat16
F32 = jnp.float32


def _cparams(*sem):
    return pltpu.CompilerParams(dimension_semantics=sem, vmem_limit_bytes=VMEM_LIMIT_BYTES)


def _const_spec(shape):
    nd = len(shape)
    return pl.BlockSpec(shape, lambda *_: (0,) * nd)


def _dot(a, b):
    return jnp.dot(a, b, preferred_element_type=F32)


def _dot_nt(a, b):
    return lax.dot_general(a, b, (((1,), (1,)), ((), ())), preferred_element_type=F32)


def _rms_rows(x, gain):
    ms = jnp.mean(x * x, axis=-1, keepdims=True)
    return x * lax.rsqrt(ms + EPS) * gain


def _memkv_kernel(mem_ref, gmem_ref, w_ref, mkn_ref, km_ref, vm_ref):
    mem_dim = MEM_HEADS * MEM_HEAD_DIM
    mn = _rms_rows(mem_ref[0], gmem_ref[...]).astype(BF16)
    kv = _dot(mn, w_ref[...])
    for h in range(MEM_HEADS):
        sl = slice(h * MEM_HEAD_DIM, (h + 1) * MEM_HEAD_DIM)
        km_ref[0, :, sl] = _rms_rows(kv[:, sl], mkn_ref[...]).astype(BF16)
    vm_ref[0] = kv[:, mem_dim:].astype(BF16)


def _memkv(mem, g_mem, w_mem_kv, mk_norm):
    B, M, D = mem.shape
    mem_dim = MEM_HEADS * MEM_HEAD_DIM
    out = jax.ShapeDtypeStruct((B, M, mem_dim), BF16)
    return pl.pallas_call(
        _memkv_kernel,
        grid=(B,),
        in_specs=[pl.BlockSpec((1, M, D), lambda b: (b, 0, 0)),
                  _const_spec((1, D)),
                  _const_spec((D, 2 * mem_dim)),
                  _const_spec((1, MEM_HEAD_DIM))],
        out_specs=[pl.BlockSpec((1, M, mem_dim), lambda b: (b, 0, 0))] * 2,
        out_shape=[out, out],
        compiler_params=_cparams("arbitrary"),
        name="memkv",
    )(mem, g_mem.reshape(1, D), w_mem_kv.astype(BF16), mk_norm.reshape(1, MEM_HEAD_DIM))


def _swap32(x):
    lane = lax.broadcasted_iota(jnp.int32, x.shape, 1)
    low = (lane & 32) == 0
    return jnp.where(low, pltpu.roll(x, LANES - 32, 1), pltpu.roll(x, 32, 1))


def _inproj_kernel(x_ref, gmix_ref, w_ref, gq_ref, gk_ref, cos_ref, sin_ref, ones_ref,
                   poolw_ref, pscale_ref, km_ref, vm_ref, mqn_ref,
                   q_ref, k_ref, v_ref, pool_ref, memo_ref,
                   xn_ref, uext_ref, *, tm, tiles_per_seq, d_qk, d_v, d_pool, d_mem):
    i = pl.program_id(0)
    pos0 = (i % tiles_per_seq) * tm
    xn_ref[...] = _rms_rows(x_ref[...], gmix_ref[...]).astype(BF16)

    cos = cos_ref[...]
    sin = sin_ref[...]
    ones = ones_ref[...]

    def qk_slab(p, c, gain_ref, out_ref, scale):
        ss = _dot((p * p).astype(BF16), ones)
        n = p * (lax.rsqrt(ss * (1.0 / DA_HEAD_DIM) + EPS) * scale) * gain_ref[...]
        for half in range(MXU_DIM // LANES):
            nh = n[:, half * LANES:(half + 1) * LANES]
            r = nh * cos + _swap32(nh) * sin
            out_ref[:, c + half * LANES:c + (half + 1) * LANES] = r.astype(BF16)

    def v_slab(p, c):
        v_ref[:, c:c + MXU_DIM] = p.astype(BF16)

    def pool_slab(p, c):
        @pl.when(pos0 == 0)
        def _():
            uext_ref[0:POOL_HALO, c:c + MXU_DIM] = jnp.zeros((POOL_HALO, MXU_DIM), F32)

        uext_ref[POOL_HALO:POOL_HALO + tm, c:c + MXU_DIM] = p
        t1 = pos0 + 1 + lax.broadcasted_iota(jnp.int32, (tm, POOL_GROUP_DIM), 0)
        for g in range(c // POOL_GROUP_DIM, (c + MXU_DIM) // POOL_GROUP_DIM):
            w = POOL_WINDOWS[g]
            sl = slice(g * POOL_GROUP_DIM, (g + 1) * POOL_GROUP_DIM)
            u = uext_ref[POOL_HALO:POOL_HALO + tm, sl]
            acc = u
            for s in range(1, w):
                acc = acc + uext_ref[POOL_HALO - s:POOL_HALO - s + tm, sl]
            cnt = jnp.minimum(t1, w).astype(F32)
            z = acc / cnt - u
            zp = _dot(z.astype(BF16), poolw_ref[g])
            pool_ref[:, sl] = (zp * pscale_ref[:, sl]).astype(BF16)
        uext_ref[0:POOL_HALO, c:c + MXU_DIM] = uext_ref[tm:tm + POOL_HALO, c:c + MXU_DIM]

    def mem_slab(p, c):
        inv_sqrt = 1.0 / math.sqrt(MEM_HEAD_DIM)
        for half in range(MXU_DIM // LANES):
            sl = slice(c + half * LANES, c + (half + 1) * LANES)
            qn = (_rms_rows(p[:, half * LANES:(half + 1) * LANES], mqn_ref[...]) * inv_sqrt).astype(BF16)
            s = _dot_nt(qn, km_ref[0, :, sl])
            m = jnp.max(s, axis=-1, keepdims=True)
            e = jnp.exp(s - m)
            l = jnp.sum(e, axis=-1, keepdims=True)
            o = _dot(e.astype(BF16), vm_ref[0, :, sl])
            memo_ref[:, sl] = (o / l).astype(BF16)

    col = 0
    q_slabs = [(col + c, functools.partial(qk_slab, c=c, gain_ref=gq_ref, out_ref=q_ref, scale=QUERY_SCALE))
               for c in range(0, d_qk, MXU_DIM)]
    col += d_qk
    k_slabs = [(col + c, functools.partial(qk_slab, c=c, gain_ref=gk_ref, out_ref=k_ref, scale=1.0))
               for c in range(0, d_qk, MXU_DIM)]
    col += d_qk
    v_slabs = [(col + c, functools.partial(v_slab, c=c)) for c in range(0, d_v, MXU_DIM)]
    col += d_v
    pool_slabs = [(col + c, functools.partial(pool_slab, c=c)) for c in range(0, d_pool, MXU_DIM)]
    col += d_pool
    mem_slabs = [(col + c, functools.partial(mem_slab, c=c)) for c in range(0, d_mem, MXU_DIM)]
    slabs = q_slabs + k_slabs + v_slabs + pool_slabs + mem_slabs

    def project(col0):
        return _dot(xn_ref[...], w_ref[:, col0:col0 + MXU_DIM])

    queue = [project(c0) for c0, _ in slabs[:PROJECT_AHEAD]]
    for n, (_, epilogue) in enumerate(slabs):
        if n + PROJECT_AHEAD < len(slabs):
            queue.append(project(slabs[n + PROJECT_AHEAD][0]))
        epilogue(queue.pop(0))


def _inproj(x2, g_mix, w_in, q_norm, k_norm, cos_tab, sin_tab, pool_w, pool_scale,
            km, vm, mq_norm, *, seq, tm):
    T, D = x2.shape
    d_qk = DA_HEADS * 2 * DA_HEAD_DIM
    d_v = DA_HEADS * DA_V_DIM
    d_pool = len(POOL_WINDOWS) * POOL_GROUP_DIM
    d_mem = MEM_HEADS * MEM_HEAD_DIM
    d_all = 2 * d_qk + d_v + d_pool + d_mem
    M = km.shape[1]
    tiles_per_seq = seq // tm
    reps = MXU_DIM // DA_HEAD_DIM
    gq = jnp.tile(q_norm, reps).reshape(1, MXU_DIM)
    gk = jnp.tile(k_norm, reps).reshape(1, MXU_DIM)
    grp = jnp.arange(MXU_DIM) // DA_HEAD_DIM
    ones = (grp[:, None] == grp[None, :]).astype(BF16)
    kern = functools.partial(_inproj_kernel, tm=tm, tiles_per_seq=tiles_per_seq,
                             d_qk=d_qk, d_v=d_v, d_pool=d_pool, d_mem=d_mem)
    row = lambda width: pl.BlockSpec((tm, width), lambda i: (i, 0))
    return pl.pallas_call(
        kern,
        grid=(T // tm,),
        in_specs=[row(D),
                  _const_spec((1, D)),
                  _const_spec((D, d_all)),
                  _const_spec((1, MXU_DIM)),
                  _const_spec((1, MXU_DIM)),
                  pl.BlockSpec((tm, LANES), lambda i: (i % tiles_per_seq, 0)),
                  pl.BlockSpec((tm, LANES), lambda i: (i % tiles_per_seq, 0)),
                  _const_spec((MXU_DIM, MXU_DIM)),
                  _const_spec((len(POOL_WINDOWS), POOL_GROUP_DIM, POOL_GROUP_DIM)),
                  _const_spec((1, d_pool)),
                  pl.BlockSpec((1, M, d_mem), lambda i: (i // tiles_per_seq, 0, 0)),
                  pl.BlockSpec((1, M, d_mem), lambda i: (i // tiles_per_seq, 0, 0)),
                  _const_spec((1, MEM_HEAD_DIM))],
        out_specs=[row(d_qk), row(d_qk), row(d_v), row(d_pool), row(d_mem)],
        out_shape=[jax.ShapeDtypeStruct((T, d_qk), BF16),
                   jax.ShapeDtypeStruct((T, d_qk), BF16),
                   jax.ShapeDtypeStruct((T, d_v), BF16),
                   jax.ShapeDtypeStruct((T, d_pool), BF16),
                   jax.ShapeDtypeStruct((T, d_mem), BF16)],
        scratch_shapes=[pltpu.VMEM((tm, D), BF16),
                        pltpu.VMEM((tm + POOL_HALO, d_pool), F32)],
        compiler_params=_cparams("arbitrary"),
        name="inproj",
    )(x2, g_mix.reshape(1, D), w_in[:, :d_all].astype(BF16), gq, gk, cos_tab, sin_tab, ones,
      pool_w.astype(BF16), pool_scale.reshape(1, d_pool), km, vm, mq_norm.reshape(1, MEM_HEAD_DIM))


def _attn_kernel_tiled(lamp_ref, q_ref, k_ref, v_ref, gsub_ref, o_ref, vt_ref, acc_ref,
                       sa_ref, sb_ref, pa_ref, pb_ref, lfin_ref, *, seq, tq):
    lp = lamp_ref[...]
    lam = (jnp.exp(jnp.sum(lp[0:1] * lp[1:2], axis=-1, keepdims=True))
           - jnp.exp(jnp.sum(lp[2:3] * lp[3:4], axis=-1, keepdims=True)) + LAMBDA_INIT)
    sub = lax.broadcasted_iota(jnp.int32, (DA_V_DIM, tq), 0)
    key_i = lax.broadcasted_iota(jnp.int32, (tq, 2 * tq), 0)
    qry_i = lax.broadcasted_iota(jnp.int32, (tq, 2 * tq), 1) % tq
    causal = key_i <= qry_i
    n_blk = seq // tq

    def transpose_v(j, carry):
        r0 = pl.multiple_of(j * tq, tq)
        vt_ref[j] = v_ref[pl.ds(r0, tq), :].astype(F32).T.astype(BF16)
        return carry

    lax.fori_loop(0, n_blk, transpose_v, 0)

    def q_block(qi, _):
        q0 = pl.multiple_of(qi * tq, tq)
        qt = q_ref[pl.ds(q0, tq), :].astype(F32).T
        qcat = jnp.concatenate([jnp.where(sub < DA_HEAD_DIM, qt, 0.0),
                                jnp.where(sub >= DA_HEAD_DIM, qt, 0.0)], axis=1).astype(BF16)
        acc_ref[...] = jnp.zeros(acc_ref.shape, F32)
        pb_ref[...] = jnp.zeros(pb_ref.shape, BF16)

        def scores(j, dst_ref):
            k0 = pl.multiple_of(j * tq, tq)
            dst_ref[...] = _dot(k_ref[pl.ds(k0, tq), :], qcat)

        def values(j, p_ref, alpha):
            acc_ref[...] = alpha * acc_ref[...] + _dot(vt_ref[j], p_ref[...])

        def step(j, state, s_cur, s_next, p_prev, p_cur, *, masked, prefetch):
            m_prev, l_prev, alpha_prev = state
            if prefetch:
                scores(j + 1, s_next)
            s = s_cur[...]
            if masked:
                s = jnp.where(causal, s, NEG_INF)
            m_new = jnp.maximum(m_prev, jnp.max(s, axis=0, keepdims=True))
            alpha = jnp.exp2(m_prev - m_new)
            p = jnp.exp2(s - m_new)
            l_new = alpha * l_prev + jnp.sum(p, axis=0, keepdims=True)
            p_cur[...] = p.astype(BF16)
            values(jnp.maximum(j - 1, 0), p_prev, alpha_prev)
            return m_new, l_new, alpha

        def pair(t, state):
            state = step(2 * t, state, sa_ref, sb_ref, pb_ref, pa_ref, masked=False, prefetch=True)
            return step(2 * t + 1, state, sb_ref, sa_ref, pa_ref, pb_ref, masked=False, prefetch=True)

        scores(0, sa_ref)
        init = (jnp.full((1, 2 * tq), NEG_INF, F32), jnp.zeros((1, 2 * tq), F32),
                jnp.ones((1, 2 * tq), F32))
        state = lax.fori_loop(0, qi // 2, pair, init)

        @pl.when(qi % 2 == 0)
        def _():
            _, l_fin, alpha = step(qi, state, sa_ref, sb_ref, pb_ref, pa_ref, masked=True, prefetch=False)
            values(qi, pa_ref, alpha)
            lfin_ref[...] = l_fin

        @pl.when(qi % 2 == 1)
        def _():
            st = step(qi - 1, state, sa_ref, sb_ref, pb_ref, pa_ref, masked=False, prefetch=True)
            _, l_fin, alpha = step(qi, st, sb_ref, sa_ref, pa_ref, pb_ref, masked=True, prefetch=False)
            values(qi, pb_ref, alpha)
            lfin_ref[...] = l_fin

        on = acc_ref[...] / lfin_ref[...]
        ot = on[:, :tq] - lam * on[:, tq:]
        ot = ot * lax.rsqrt(jnp.mean(ot * ot, axis=0, keepdims=True) + EPS)
        o = ot.T * (gsub_ref[...] * (1.0 - LAMBDA_INIT))
        o_ref[pl.ds(q0, tq), :] = o.astype(BF16)
        return 0

    lax.fori_loop(0, n_blk, q_block, 0)


def _attn_tiled(q, k, v, lam_params, g_subln, *, batch, seq, tq):
    T, d_v = v.shape
    blk = lambda: pl.BlockSpec((seq, DA_V_DIM), lambda b, h: (b, h))
    return pl.pallas_call(
        functools.partial(_attn_kernel_tiled, seq=seq, tq=tq),
        grid=(batch, DA_HEADS),
        in_specs=[_const_spec((4, DA_HEAD_DIM)), blk(), blk(), blk(), _const_spec((1, DA_V_DIM))],
        out_specs=blk(),
        out_shape=jax.ShapeDtypeStruct((T, d_v), BF16),
        scratch_shapes=[pltpu.VMEM((seq // tq, DA_V_DIM, tq), BF16),
                        pltpu.VMEM((DA_V_DIM, 2 * tq), F32),
                        pltpu.VMEM((tq, 2 * tq), F32), pltpu.VMEM((tq, 2 * tq), F32),
                        pltpu.VMEM((tq, 2 * tq), BF16), pltpu.VMEM((tq, 2 * tq), BF16),
                        pltpu.VMEM((1, 2 * tq), F32)],
        compiler_params=_cparams("arbitrary", "arbitrary"),
        name="attn",
    )(lam_params, q, k, v, g_subln.reshape(1, DA_V_DIM))


ATTN_UNROLL = 4


def _attn_tables(n_blk):
    qi = [q for q in range(1, n_blk) for _ in range(q)]
    kj = [j for q in range(1, n_blk) for j in range(q)]
    return (np.asarray(qi + qi[-1:] * 2, np.int32), np.asarray(kj + kj[-1:] * 2, np.int32))


def _attn_kernel(tqi_ref, tkj_ref, lamp_ref, q_ref, k_ref, v_ref, gsub_ref, o_ref,
                 vt_ref, qc_ref, acc_ref, m_ref, l_ref, s0_ref, s1_ref, s2_ref, s3_ref, pa_ref, pb_ref,
                 *, seq, tq, n_full):
    lp = lamp_ref[...]
    lam = (jnp.exp(jnp.sum(lp[0:1] * lp[1:2], axis=-1, keepdims=True))
           - jnp.exp(jnp.sum(lp[2:3] * lp[3:4], axis=-1, keepdims=True)) + LAMBDA_INIT)
    n_blk = seq // tq
    sub = lax.broadcasted_iota(jnp.int32, (DA_V_DIM, tq), 0)
    key_i = lax.broadcasted_iota(jnp.int32, (tq, 2 * tq), 0)
    qry_i = lax.broadcasted_iota(jnp.int32, (tq, 2 * tq), 1) % tq
    causal = key_i <= qry_i
    s_bufs = (s0_ref, s1_ref, s2_ref, s3_ref)
    p_bufs = (pa_ref, pb_ref)

    def setup(t, carry):
        r0 = pl.multiple_of(t * tq, tq)
        vt_ref[t] = v_ref[pl.ds(r0, tq), :].astype(F32).T.astype(BF16)
        qt = q_ref[pl.ds(r0, tq), :].astype(F32).T
        qc_ref[t] = jnp.concatenate([jnp.where(sub < DA_HEAD_DIM, qt, 0.0),
                                     jnp.where(sub >= DA_HEAD_DIM, qt, 0.0)], axis=1).astype(BF16)
        m_ref[t] = jnp.full((1, 2 * tq), NEG_INF, F32)
        l_ref[t] = jnp.zeros((1, 2 * tq), F32)
        acc_ref[t] = jnp.zeros((DA_V_DIM, 2 * tq), F32)
        return carry

    lax.fori_loop(0, n_blk, setup, 0)

    def scores(qi, kj, dst_ref, masked):
        k0 = pl.multiple_of(kj * tq, tq)
        s = _dot(k_ref[pl.ds(k0, tq), :], qc_ref[qi])
        dst_ref[...] = jnp.where(causal, s, NEG_INF) if masked else s

    def softmax(qi, s_ref, p_ref):
        m_prev = m_ref[qi]
        s = s_ref[...]
        m_new = jnp.maximum(m_prev, jnp.max(s, axis=0, keepdims=True))
        alpha = jnp.exp2(m_prev - m_new)
        p = jnp.exp2(s - m_new)
        m_ref[qi] = m_new
        l_ref[qi] = alpha * l_ref[qi] + jnp.sum(p, axis=0, keepdims=True)
        p_ref[...] = p.astype(BF16)
        return alpha

    def values(qi, kj, p_ref, alpha):
        acc_ref[qi] = alpha * acc_ref[qi] + _dot(vt_ref[kj], p_ref[...])

    def finalize(qi):
        on = acc_ref[qi] / l_ref[qi]
        ot = on[:, :tq] - lam * on[:, tq:]
        ot = ot * lax.rsqrt(jnp.mean(ot * ot, axis=0, keepdims=True) + EPS)
        o = ot.T * (gsub_ref[...] * (1.0 - LAMBDA_INIT))
        o_ref[pl.ds(pl.multiple_of(qi * tq, tq), tq), :] = o.astype(BF16)

    ones_row = jnp.ones((1, 2 * tq), F32)

    def full_step(n, u, alpha_prev):
        scores(tqi_ref[n + 2], tkj_ref[n + 2], s_bufs[(u + 2) % 4], False)
        alpha = softmax(tqi_ref[n], s_bufs[u], p_bufs[u % 2])
        prev = jnp.maximum(n - 1, 0)
        values(tqi_ref[prev], tkj_ref[prev], p_bufs[(u + 1) % 2], alpha_prev)
        return alpha

    def full_steps(t, alpha_prev):
        for u in range(ATTN_UNROLL):
            alpha_prev = full_step(ATTN_UNROLL * t + u, u, alpha_prev)
        return alpha_prev

    pb_ref[...] = jnp.zeros(pb_ref.shape, BF16)
    scores(tqi_ref[0], tkj_ref[0], s0_ref, False)
    scores(tqi_ref[1], tkj_ref[1], s1_ref, False)
    alpha_last = lax.fori_loop(0, n_full // ATTN_UNROLL, full_steps, ones_row)
    values(tqi_ref[n_full - 1], tkj_ref[n_full - 1], pb_ref, alpha_last)

    def diag_step(qi, u, alpha_prev):
        nxt = jnp.minimum(qi + 1, n_blk - 1)
        scores(nxt, nxt, s_bufs[1 - u], True)
        alpha = softmax(qi, s_bufs[u], p_bufs[u])
        prev = jnp.maximum(qi - 1, 0)
        values(prev, prev, p_bufs[1 - u], alpha_prev)
        finalize(prev)
        return alpha

    def diag_steps(t, alpha_prev):
        return diag_step(2 * t + 1, 1, diag_step(2 * t, 0, alpha_prev))

    pb_ref[...] = jnp.zeros(pb_ref.shape, BF16)
    scores(0, 0, s0_ref, True)
    alpha_last = lax.fori_loop(0, n_blk // 2, diag_steps, ones_row)
    values(n_blk - 1, n_blk - 1, pb_ref, alpha_last)
    finalize(n_blk - 1)


def _attn(q, k, v, lam_params, g_subln, *, batch, seq, tq):
    T, d_v = v.shape
    n_blk = seq // tq
    tqi, tkj = _attn_tables(n_blk)
    n_full = tqi.shape[0] - 2
    assert n_full % ATTN_UNROLL == 0 and n_blk % 2 == 0, (seq, tq)
    blk = lambda: pl.BlockSpec((seq, DA_V_DIM), lambda b, h, tqi, tkj: (b, h))
    const = lambda shape: pl.BlockSpec(shape, lambda b, h, tqi, tkj: (0,) * len(shape))
    return pl.pallas_call(
        functools.partial(_attn_kernel, seq=seq, tq=tq, n_full=n_full),
        grid_spec=pltpu.PrefetchScalarGridSpec(
            num_scalar_prefetch=2,
            grid=(batch, DA_HEADS),
            in_specs=[const((4, DA_HEAD_DIM)), blk(), blk(), blk(), const((1, DA_V_DIM))],
            out_specs=blk(),
            scratch_shapes=[pltpu.VMEM((n_blk, DA_V_DIM, tq), BF16),
                            pltpu.VMEM((n_blk, DA_V_DIM, 2 * tq), BF16),
                            pltpu.VMEM((n_blk, DA_V_DIM, 2 * tq), F32),
                            pltpu.VMEM((n_blk, 1, 2 * tq), F32),
                            pltpu.VMEM((n_blk, 1, 2 * tq), F32)]
                           + [pltpu.VMEM((tq, 2 * tq), F32)] * 4
                           + [pltpu.VMEM((tq, 2 * tq), BF16)] * 2),
        out_shape=jax.ShapeDtypeStruct((T, d_v), BF16),
        compiler_params=_cparams("arbitrary", "arbitrary"),
        name="attn",
    )(jnp.asarray(tqi), jnp.asarray(tkj), lam_params, q, k, v, g_subln.reshape(1, DA_V_DIM))


def _pack_bf16_pairs(a, b):
    hi = pltpu.bitcast(a.astype(BF16).astype(F32), jnp.uint32)
    lo = pltpu.bitcast(b.astype(BF16).astype(F32), jnp.uint32)
    return hi | (lo >> 16)


def _unpack_bf16_pairs(p):
    a = pltpu.bitcast(p & jnp.uint32(0xFFFF0000), F32)
    b = pltpu.bitcast(p << 16, F32)
    return a, b


def _merge_kernel(x_ref, oa_ref, pool_ref, memo_ref, gmix_ref, wg_ref, bg_ref, wba_ref, wbp_ref,
                  wbm_ref, wout_ref, gffn_ref, wrh_ref, wrl_ref, br_ref,
                  h_ref, hn_ref, idx_ref, rank_ref, gate_ref, cnt_ref,
                  xn_ref, mrg_ref, hs_ref, carry_ref, *, tm, d_model):
    i = pl.program_id(0)
    D = d_model

    @pl.when(i == 0)
    def _():
        carry_ref[...] = jnp.zeros(carry_ref.shape, F32)
        hs_ref[...] = jnp.zeros(hs_ref.shape, F32)

    def logits():
        hn = _rms_rows(hs_ref[...], gffn_ref[...])
        hn_ref[...] = _pack_bf16_pairs(hn[:, :D // 2], hn[:, D // 2:])
        hn_hi = hn.astype(BF16)
        hn_lo = (hn - hn_hi.astype(F32)).astype(BF16)
        return (_dot(hn_hi, wrh_ref[...]) + _dot(hn_lo, wrh_ref[...])
                + _dot(hn_hi, wrl_ref[...]) + br_ref[...])

    route = _route_pieces(logits, i > 0, idx_ref, rank_ref, gate_ref, cnt_ref, carry_ref, tm)

    x = x_ref[...]
    xn_ref[...] = _rms_rows(x, gmix_ref[...]).astype(BF16)
    branches = ((oa_ref, wba_ref), (pool_ref, wbp_ref), (memo_ref, wbm_ref))

    def slab_dots(j):
        sl = slice(j * MXU_DIM, (j + 1) * MXU_DIM)
        out = []
        for br, (y_ref, w_ref) in enumerate(branches):
            gsl = slice(br * D + j * MXU_DIM, br * D + (j + 1) * MXU_DIM)
            out.append((_dot(xn_ref[...], wg_ref[:, gsl]), _dot(y_ref[...], w_ref[:, sl])))
        return out

    def slab_epilogue(j, dots):
        merged = None
        for br, (logit, y) in enumerate(dots):
            gsl = slice(br * D + j * MXU_DIM, br * D + (j + 1) * MXU_DIM)
            term = y / (1.0 + jnp.exp(-(logit + bg_ref[:, gsl])))
            merged = term if merged is None else merged + term
        mrg_ref[:, j * MXU_DIM:(j + 1) * MXU_DIM] = merged.astype(BF16)

    n_slabs = D // MXU_DIM
    dots = slab_dots(0)
    for j in range(n_slabs):
        nxt = slab_dots(j + 1) if j + 1 < n_slabs else None
        route[j]()
        slab_epilogue(j, dots)
        dots = nxt
    for piece in route[n_slabs:-1]:
        piece()
    h = x + _dot(mrg_ref[...], wout_ref[...])
    route[-1]()
    h_ref[...] = h
    hs_ref[...] = h


def _route_pieces(logits_fn, live, idx_ref, rank_ref, gate_ref, cnt_ref, carry_ref, tm):
    lane = lax.broadcasted_iota(jnp.int32, (tm, LANES), 1)
    st = dict(vals=[], sels=[], idx_out=jnp.zeros((tm, LANES), jnp.int32))

    def start():
        st["work"] = logits_fn()

    def pick(kk):
        work = st["work"]
        mx = jnp.max(work, axis=-1, keepdims=True)
        idx = jnp.min(jnp.where(work == mx, lane, LANES), axis=-1, keepdims=True)
        sel = lane == idx
        st["vals"].append(mx)
        st["sels"].append(sel)
        st["idx_out"] = jnp.where(lane == kk, idx, st["idx_out"])
        st["work"] = jnp.where(sel, -jnp.inf, work)

    def finish():
        vals, sels = st["vals"], st["sels"]
        exps = [jnp.exp(v - vals[0]) for v in vals]
        denom = exps[0] + exps[1] + exps[2] + exps[3]
        gate_out = jnp.zeros((tm, LANES), F32)
        onehot = jnp.zeros((tm, LANES), F32)
        for kk in range(TOP_K):
            gate_out = jnp.where(lane == kk, exps[kk] / denom, gate_out)
            onehot = jnp.where(sels[kk], 1.0, onehot)
        r_i = lax.broadcasted_iota(jnp.int32, (tm, tm), 0)
        c_i = lax.broadcasted_iota(jnp.int32, (tm, tm), 1)
        lower = (c_i < r_i).astype(BF16)
        prefix = _dot(lower, onehot.astype(BF16)) + carry_ref[0:1, :]
        rank_out = jnp.zeros((tm, LANES), jnp.int32)
        for kk in range(TOP_K):
            rk = jnp.sum(jnp.where(sels[kk], prefix, 0.0), axis=-1, keepdims=True)
            rank_out = jnp.where(lane == kk, rk.astype(jnp.int32), rank_out)
        new_carry = prefix[tm - 1:tm, :] + onehot[tm - 1:tm, :]
        new_carry = jnp.where(live, new_carry, carry_ref[0:1, :])
        carry_ref[...] = jnp.broadcast_to(new_carry, carry_ref.shape)
        idx_ref[...] = st["idx_out"]
        rank_ref[...] = rank_out
        gate_ref[...] = gate_out
        cnt_ref[...] = jnp.broadcast_to(new_carry, cnt_ref.shape).astype(jnp.int32)

    return [start] + [functools.partial(pick, kk) for kk in range(TOP_K)] + [finish]


def _merge(x2, oa, pool, memo, g_mix, w_gates, b_gates, w_b_attn, w_b_pool, w_b_mem, w_out, g_ffn,
           w_router, b_router, *, tm):
    T, D = x2.shape
    d_pool = pool.shape[1]
    d_mem = memo.shape[1]
    E = w_router.shape[1]
    wr = jnp.zeros((D, LANES), F32).at[:, :E].set(w_router)
    wr_hi = wr.astype(BF16)
    wr_lo = (wr - wr_hi.astype(F32)).astype(BF16)
    br = jnp.full((1, LANES), NEG_INF, F32).at[0, :E].set(b_router)
    n_tiles = T // tm
    row = lambda width: pl.BlockSpec((tm, width), lambda i: (jnp.minimum(i, n_tiles - 1), 0))
    routed = lambda width: pl.BlockSpec((tm, width), lambda i: (jnp.maximum(i - 1, 0), 0))
    sublanes = 8
    return pl.pallas_call(
        functools.partial(_merge_kernel, tm=tm, d_model=D),
        grid=(n_tiles + 1,),
        in_specs=[row(D), row(oa.shape[1]), row(d_pool), row(d_mem),
                  _const_spec((1, D)),
                  _const_spec((D, 3 * D)), _const_spec((1, 3 * D)),
                  _const_spec((oa.shape[1], D)), _const_spec((d_pool, D)), _const_spec((d_mem, D)),
                  _const_spec((D, D)), _const_spec((1, D)),
                  _const_spec((D, LANES)), _const_spec((D, LANES)), _const_spec((1, LANES))],
        out_specs=[row(D), routed(D // 2), routed(LANES), routed(LANES), routed(LANES),
                   _const_spec((sublanes, LANES))],
        out_shape=[jax.ShapeDtypeStruct((T, D), F32),
                   jax.ShapeDtypeStruct((T, D // 2), jnp.uint32),
                   jax.ShapeDtypeStruct((T, LANES), jnp.int32),
                   jax.ShapeDtypeStruct((T, LANES), jnp.int32),
                   jax.ShapeDtypeStruct((T, LANES), F32),
                   jax.ShapeDtypeStruct((sublanes, LANES), jnp.int32)],
        scratch_shapes=[pltpu.VMEM((tm, D), BF16),
                        pltpu.VMEM((tm, D), BF16),
                        pltpu.VMEM((tm, D), F32),
                        pltpu.VMEM((sublanes, LANES), F32)],
        compiler_params=_cparams("arbitrary"),
        name="merge",
    )(x2, oa, pool, memo, g_mix.reshape(1, D), w_gates.astype(BF16), b_gates.reshape(1, 3 * D),
      w_b_attn.astype(BF16), w_b_pool.astype(BF16), w_b_mem.astype(BF16), w_out.astype(BF16),
      g_ffn.reshape(1, D), wr_hi, wr_lo, br)


def _dispatch_kernel(poff_ref, nv_ref, idx_ref, rank_ref, hn_ref, xs_ref, zero_ref, sem, zsem,
                     *, tm, bm, n_blocks):
    def row_copy(r, dest):
        return pltpu.make_async_copy(hn_ref.at[pl.ds(r, 1)], xs_ref.at[pl.ds(dest, 1)], sem)

    def zero_copy(blk):
        return pltpu.make_async_copy(zero_ref, xs_ref.at[pl.ds(pl.multiple_of(blk * bm, bm), bm)], zsem)

    @pl.when(pl.program_id(0) == 0)
    def _():
        zero_ref[...] = jnp.zeros(zero_ref.shape, zero_ref.dtype)

        def start(blk, n):
            @pl.when(nv_ref[blk] < bm)
            def _():
                zero_copy(blk).start()
            return n + (nv_ref[blk] < bm).astype(jnp.int32)

        n_zeroed = lax.fori_loop(0, n_blocks, start, 0)

        def finish(_, carry):
            zero_copy(0).wait()
            return carry

        lax.fori_loop(0, n_zeroed, finish, 0)

    def issue(r, carry):
        for kk in range(TOP_K):
            a = r * TOP_K + kk
            row_copy(r, poff_ref[idx_ref[a]] + rank_ref[a]).start(priority=kk % 2)
        return carry

    lax.fori_loop(0, tm, issue, 0, unroll=ROW_DMA_UNROLL)
    for _ in range(TOP_K):
        pltpu.make_async_copy(hn_ref, xs_ref.at[pl.ds(0, tm)], sem).wait()


def _dispatch(poff, blk_valid, idx_flat, rank_flat, hn, *, bm, tm):
    T, half = hn.shape
    n_blocks = blk_valid.shape[0]
    flat = pl.BlockSpec((tm * TOP_K,), lambda i, poff, nv: (i,), memory_space=pltpu.SMEM)
    return pl.pallas_call(
        functools.partial(_dispatch_kernel, tm=tm, bm=bm, n_blocks=n_blocks),
        grid_spec=pltpu.PrefetchScalarGridSpec(
            num_scalar_prefetch=2,
            grid=(T // tm,),
            in_specs=[flat, flat, pl.BlockSpec((tm, half), lambda i, poff, nv: (i, 0))],
            out_specs=pl.BlockSpec(memory_space=pl.ANY),
            scratch_shapes=[pltpu.VMEM((bm, half), jnp.uint32),
                            pltpu.SemaphoreType.DMA, pltpu.SemaphoreType.DMA]),
        out_shape=jax.ShapeDtypeStruct((n_blocks * bm, half), jnp.uint32),
        compiler_params=_cparams("arbitrary"),
        name="dispatch",
    )(poff, blk_valid, idx_flat, rank_flat, hn)


def _experts_kernel(be_ref, nv_ref, src_ref, x_ref, wg_ref, bg_ref, wu_ref, bu_ref, wd_ref, bd_ref,
                    y_ref, act_ref, *, bm, d_model, d_expert):
    i = pl.program_id(0)
    half = d_model // 2

    @pl.when(nv_ref[i] == 0)
    def _():
        y_ref[...] = jnp.zeros(y_ref.shape, y_ref.dtype)

    @pl.when(nv_ref[i] > 0)
    def _():
        a, b = _unpack_bf16_pairs(x_ref[...])
        valid = lax.broadcasted_iota(jnp.int32, (bm, half), 0) < nv_ref[i]
        a = jnp.where(valid, a, 0.0).astype(BF16)
        b = jnp.where(valid, b, 0.0).astype(BF16)
        for j in range(d_expert // MXU_DIM):
            sl = slice(j * MXU_DIM, (j + 1) * MXU_DIM)
            g = _dot(a, wg_ref[0, :half, sl]) + _dot(b, wg_ref[0, half:, sl]) + bg_ref[0, :, sl]
            u = _dot(a, wu_ref[0, :half, sl]) + _dot(b, wu_ref[0, half:, sl]) + bu_ref[0, :, sl]
            g = jnp.minimum(g, SWIGLU_LIMIT)
            u = jnp.clip(u, -SWIGLU_LIMIT, SWIGLU_LIMIT)
            act = g * (1.0 / (1.0 + jnp.exp(-SWIGLU_ALPHA * g))) * (u + 1.0)
            act_ref[:, sl] = act.astype(BF16)
        out = _dot(act_ref[...], wd_ref[0]) + bd_ref[0]
        y_ref[...] = _pack_bf16_pairs(out[:, :half], out[:, half:])


def _experts(blk_e, blk_valid, blk_src, xs, w_gate, b_gate, w_up, b_up, w_down, b_down, *, bm):
    rows_sorted, half = xs.shape
    E, D, De = w_gate.shape
    n_blocks = rows_sorted // bm
    wspec = lambda r, c: pl.BlockSpec((1, r, c), lambda i, be, nv, src: (be[i], 0, 0))
    xspec = pl.BlockSpec((bm, half), lambda i, be, nv, src: (src[i], 0))
    return pl.pallas_call(
        functools.partial(_experts_kernel, bm=bm, d_model=D, d_expert=De),
        grid_spec=pltpu.PrefetchScalarGridSpec(
            num_scalar_prefetch=3,
            grid=(n_blocks,),
            in_specs=[xspec, wspec(D, De), wspec(1, De), wspec(D, De), wspec(1, De),
                      wspec(De, D), wspec(1, D)],
            out_specs=pl.BlockSpec((bm, half), lambda i, be, nv, src: (i, 0)),
            scratch_shapes=[pltpu.VMEM((bm, De), BF16)]),
        out_shape=jax.ShapeDtypeStruct((rows_sorted, half), jnp.uint32),
        compiler_params=_cparams("arbitrary"),
        name="experts",
    )(blk_e, blk_valid, blk_src, xs, w_gate.astype(BF16), b_gate.reshape(E, 1, De),
      w_up.astype(BF16), b_up.reshape(E, 1, De), w_down.astype(BF16), b_down.reshape(E, 1, D))


def _combine_kernel(poff_ref, idx_ref, rank_ref, gate_ref, h_ref, ys_ref, o_ref, buf_ref, sem, *, tm, d_model):
    half = d_model // 2

    def row_copy(src, kk, r):
        return pltpu.make_async_copy(ys_ref.at[pl.ds(src, 1)], buf_ref.at[kk, pl.ds(r, 1)], sem)

    def issue(r, carry):
        for kk in range(TOP_K):
            a = r * TOP_K + kk
            row_copy(poff_ref[idx_ref[a]] + rank_ref[a], kk, r).start(priority=kk % 2)
        return carry

    lax.fori_loop(0, tm, issue, 0, unroll=ROW_DMA_UNROLL)
    for kk in range(TOP_K):
        pltpu.make_async_copy(ys_ref.at[pl.ds(0, tm)], buf_ref.at[kk], sem).wait()
    h = h_ref[...]
    lo, hi = h[:, :half], h[:, half:]
    gates = gate_ref[...]
    for kk in range(TOP_K):
        a, b = _unpack_bf16_pairs(buf_ref[kk])
        w = gates[:, kk:kk + 1]
        lo = lo + w * a
        hi = hi + w * b
    o_ref[:, :half] = lo
    o_ref[:, half:] = hi


def _combine(poff, idx_flat, rank_flat, gate, h, ys, *, tm):
    T, D = h.shape
    flat = pl.BlockSpec((tm * TOP_K,), lambda i, poff: (i,), memory_space=pltpu.SMEM)
    row = lambda width: pl.BlockSpec((tm, width), lambda i, poff: (i, 0))
    return pl.pallas_call(
        functools.partial(_combine_kernel, tm=tm, d_model=D),
        grid_spec=pltpu.PrefetchScalarGridSpec(
            num_scalar_prefetch=1,
            grid=(T // tm,),
            in_specs=[flat, flat, row(LANES), row(D), pl.BlockSpec(memory_space=pl.ANY)],
            out_specs=row(D),
            scratch_shapes=[pltpu.VMEM((TOP_K, tm, D // 2), jnp.uint32), pltpu.SemaphoreType.DMA]),
        out_shape=jax.ShapeDtypeStruct((T, D), F32),
        compiler_params=_cparams("arbitrary"),
        name="combine",
    )(poff, idx_flat, rank_flat, gate, h, ys)


SC_CORES = 2
SC_SUBCORES = 16
SC_CHUNK = 128


def _sc_mesh():
    return plsc.VectorSubcoreMesh(core_axis_name="c", subcore_axis_name="s",
                                  num_cores=SC_CORES, num_subcores=SC_SUBCORES)


def _sc_scatter_rows(rows, dest, *, rows_out):
    T, width = rows.shape
    n_chunks = T // SC_CHUNK
    per_worker = n_chunks // (SC_CORES * SC_SUBCORES)

    @functools.partial(
        pl.kernel, mesh=_sc_mesh(),
        out_type=jax.ShapeDtypeStruct((rows_out, width), rows.dtype),
        scratch_types=[pltpu.VMEM((TOP_K, SC_CHUNK), jnp.int32),
                       pltpu.VMEM((SC_CHUNK, width), rows.dtype)],
        name="sc_dispatch")
    def scatter(rows_hbm, dest_hbm, out_hbm, idx_v, rows_v):
        worker = lax.axis_index("s") * SC_CORES + lax.axis_index("c")

        @pl.loop(0, per_worker)
        def _(n):
            c = worker * per_worker + n
            pltpu.sync_copy(rows_hbm.at[pl.ds(c * SC_CHUNK, SC_CHUNK)], rows_v)
            pltpu.sync_copy(dest_hbm.at[c], idx_v)
            for kk in range(TOP_K):
                pltpu.sync_copy(rows_v, out_hbm.at[idx_v.at[kk]])

    return scatter(rows, dest)


def _sc_gather_rows(table, src, *, n_tokens):
    _, width = table.shape
    n_chunks = n_tokens // SC_CHUNK
    per_worker = n_chunks // (SC_CORES * SC_SUBCORES)

    @functools.partial(
        pl.kernel, mesh=_sc_mesh(),
        out_type=jax.ShapeDtypeStruct((TOP_K, n_tokens, width), table.dtype),
        scratch_types=[pltpu.VMEM((TOP_K, SC_CHUNK), jnp.int32),
                       pltpu.VMEM((SC_CHUNK, width), table.dtype)],
        name="sc_combine")
    def gather(table_hbm, src_hbm, out_hbm, idx_v, rows_v):
        worker = lax.axis_index("s") * SC_CORES + lax.axis_index("c")

        @pl.loop(0, per_worker)
        def _(n):
            c = worker * per_worker + n
            pltpu.sync_copy(src_hbm.at[c], idx_v)
            for kk in range(TOP_K):
                pltpu.sync_copy(table_hbm.at[idx_v.at[kk]], rows_v)
                pltpu.sync_copy(rows_v, out_hbm.at[kk, pl.ds(c * SC_CHUNK, SC_CHUNK)])

    return gather(table, src)


def _weighted_sum_kernel(gate_ref, h_ref, y_ref, *rest, d_model):
    o_ref = rest[-1]
    half = d_model // 2
    h = h_ref[...]
    lo, hi = h[:, :half], h[:, half:]
    gates = gate_ref[...]
    for kk in range(TOP_K):
        a, b = _unpack_bf16_pairs(y_ref[kk])
        w = gates[:, kk:kk + 1]
        lo = lo + w * a
        hi = hi + w * b
    o_ref[:, :half] = lo
    o_ref[:, half:] = hi


def _weighted_sum(gate, h, y4, out_prev, *, first_tile, tm):
    T, D = h.shape
    n_tiles = y4.shape[1] // tm
    row = lambda width: pl.BlockSpec((tm, width), lambda i: (first_tile + i, 0))
    in_specs = [row(LANES), row(D), pl.BlockSpec((TOP_K, tm, D // 2), lambda i: (0, i, 0))]
    args = [gate, h, y4]
    aliases = {}
    if out_prev is not None:
        in_specs.append(pl.BlockSpec(memory_space=pl.ANY))
        args.append(out_prev)
        aliases = {3: 0}
    return pl.pallas_call(
        functools.partial(_weighted_sum_kernel, d_model=D),
        grid=(n_tiles,),
        in_specs=in_specs,
        out_specs=row(D),
        out_shape=jax.ShapeDtypeStruct((T, D), F32),
        input_output_aliases=aliases,
        compiler_params=_cparams("arbitrary"),
        name="weighted_sum",
    )(*args)


def _block_tables(counts, *, bm, n_blocks):
    padded = ((counts + bm - 1) // bm) * bm
    pend = jnp.cumsum(padded)
    poff = pend - padded
    n_used = pend[-1] // bm
    src = jnp.minimum(jnp.arange(n_blocks, dtype=jnp.int32), n_used - 1)
    blk_e = jnp.sum(pend[None, :] <= (src * bm)[:, None], axis=1)
    blk_e = jnp.minimum(blk_e, N_EXPERTS - 1).astype(jnp.int32)
    blk_valid = jnp.clip(poff[blk_e] + counts[blk_e] - src * bm, 0, bm)
    blk_valid = jnp.where(jnp.arange(n_blocks) < n_used, blk_valid, 0).astype(jnp.int32)
    return poff.astype(jnp.int32), blk_e, blk_valid, src.astype(jnp.int32)


def _rope_tables(seq):
    inv_freq = ROPE_THETA ** (-jnp.arange(0, DA_HEAD_DIM, 2, dtype=F32) / DA_HEAD_DIM)
    ang = jnp.arange(seq, dtype=F32)[:, None] * inv_freq[None, :]
    reps = LANES // (DA_HEAD_DIM // 2)
    cos = jnp.tile(jnp.cos(ang), (1, reps))
    sin = jnp.tile(jnp.sin(ang), (1, reps))
    first_half = (jnp.arange(LANES) % DA_HEAD_DIM) < DA_HEAD_DIM // 2
    return cos, jnp.where(first_half[None, :], -sin, sin)


ROW_TILE = 512
ATTN_TILE = 256
EXPERT_BLOCK = 512
COMBINE_PARTS = 4
MOVE_TILE = 256
ROW_DMA_UNROLL = 4


def _forward(x, mem, g_mix, w_in, b_gates, q_norm, k_norm, lambda_q1, lambda_k1, lambda_q2, lambda_k2, g_subln, pool_w, pool_scale, g_mem, w_mem_kv, mq_norm, mk_norm, w_b_attn, w_b_pool, w_b_mem, w_out, g_ffn, w_router, b_router, w_gate, b_gate, w_up, b_up, w_down, b_down):
    B, S, D = x.shape
    T = B * S
    x2 = x.reshape(T, D)
    cos_tab, sin_tab = _rope_tables(S)
    km, vm = _memkv(mem, g_mem, w_mem_kv, mk_norm)
    q, k, v, pool, memo = _inproj(x2, g_mix, w_in, q_norm, k_norm, cos_tab, sin_tab, pool_w,
                                  pool_scale, km, vm, mq_norm, seq=S, tm=ROW_TILE)
    lam_params = jnp.stack([lambda_q1, lambda_k1, lambda_q2, lambda_k2])
    o = _attn(q, k, v, lam_params, g_subln, batch=B, seq=S, tq=ATTN_TILE)
    d_in = q.shape[1] + k.shape[1] + v.shape[1] + pool.shape[1] + memo.shape[1]
    h1, hn, idx, rank, gate, cnt = _merge(x2, o, pool, memo, g_mix, w_in[:, d_in:], b_gates, w_b_attn,
                                          w_b_pool, w_b_mem, w_out, g_ffn, w_router, b_router,
                                          tm=ROW_TILE)
    n_blocks = (T * TOP_K) // EXPERT_BLOCK + N_EXPERTS
    poff, blk_e, blk_valid, blk_src = _block_tables(cnt[0, :N_EXPERTS], bm=EXPERT_BLOCK, n_blocks=n_blocks)
    idx_flat = idx[:, :TOP_K].reshape(T * TOP_K)
    rank_flat = rank[:, :TOP_K].reshape(T * TOP_K)
    idx_d = idx[:, :TOP_K].reshape(T * TOP_K // LANES, LANES)
    rank_d = rank[:, :TOP_K].reshape(T * TOP_K // LANES, LANES)
    dest = (poff[idx_d] + rank_d).astype(jnp.int32)
    dest = dest.reshape(T // SC_CHUNK, SC_CHUNK, TOP_K).transpose(0, 2, 1)
    xs = _sc_scatter_rows(hn, dest, rows_out=n_blocks * EXPERT_BLOCK)
    ys = _experts(blk_e, blk_valid, blk_src, xs, w_gate, b_gate, w_up, b_up, w_down, b_down,
                  bm=EXPERT_BLOCK)
    out = None
    chunks_per_part = dest.shape[0] // COMBINE_PARTS
    tokens_per_part = T // COMBINE_PARTS
    for part in range(COMBINE_PARTS):
        y4 = _sc_gather_rows(ys, dest[part * chunks_per_part:(part + 1) * chunks_per_part],
                             n_tokens=tokens_per_part)
        out = _weighted_sum(gate, h1, y4, out, first_tile=part * (tokens_per_part // ROW_TILE), tm=ROW_TILE)
    return dict(q=q, k=k, v=v, pool=pool, memo=memo, o=o, h1=h1, hn=hn, idx=idx, rank=rank,
                gate=gate, cnt=cnt, out=out.reshape(B, S, D))


def kernel(x, mem, g_mix, w_in, b_gates, q_norm, k_norm, lambda_q1, lambda_k1, lambda_q2, lambda_k2, g_subln, pool_w, pool_scale, g_mem, w_mem_kv, mq_norm, mk_norm, w_b_attn, w_b_pool, w_b_mem, w_out, g_ffn, w_router, b_router, w_gate, b_gate, w_up, b_up, w_down, b_down):
    return _forward(x, mem, g_mix, w_in, b_gates, q_norm, k_norm, lambda_q1, lambda_k1, lambda_q2,
                    lambda_k2, g_subln, pool_w, pool_scale, g_mem, w_mem_kv, mq_norm, mk_norm,
                    w_b_attn, w_b_pool, w_b_mem, w_out, g_ffn, w_router, b_router, w_gate, b_gate,
                    w_up, b_up, w_down, b_down)["out"]
```

```python
import functools
import math

import jax
import jax.numpy as jnp
import numpy as np
from jax import lax
from jax.experimental import pallas as pl
from jax.experimental.pallas import tpu as pltpu
from jax.experimental.pallas import tpu_sc as plsc

DA_HEADS = 8
DA_HEAD_DIM = 64
DA_V_DIM = 2 * DA_HEAD_DIM
ROPE_THETA = 10000.0
POOL_WINDOWS = (2, 4, 8, 16)
POOL_GROUP_DIM = 128
POOL_HALO = 16
PROJECT_AHEAD = 1
MEM_HEADS = 4
MEM_HEAD_DIM = 128
N_EXPERTS = 32
TOP_K = 4
SWIGLU_LIMIT = 7.0
SWIGLU_ALPHA = 1.702
LAMBDA_INIT = 0.8 - 0.6 * math.exp(-0.3 * 0.0)
EPS = 1e-6
NEG_INF = -1e30
QUERY_SCALE = math.log2(math.e) / math.sqrt(DA_HEAD_DIM)

LANES = 128
MXU_DIM = 256
VMEM_LIMIT_BYTES = 56 * 1024 * 1024

BF16 = jnp.bfloat16
F32 = jnp.float32


def _cparams(*sem):
    return pltpu.CompilerParams(dimension_semantics=sem, vmem_limit_bytes=VMEM_LIMIT_BYTES)


def _const_spec(shape):
    nd = len(shape)
    return pl.BlockSpec(shape, lambda *_: (0,) * nd)


def _dot(a, b):
    return jnp.dot(a, b, preferred_element_type=F32)


def _dot_nt(a, b):
    return lax.dot_general(a, b, (((1,), (1,)), ((), ())), preferred_element_type=F32)


def _rms_rows(x, gain):
    ms = jnp.mean(x * x, axis=-1, keepdims=True)
    return x * lax.rsqrt(ms + EPS) * gain


def _memkv_kernel(mem_ref, gmem_ref, w_ref, mkn_ref, km_ref, vm_ref):
    mem_dim = MEM_HEADS * MEM_HEAD_DIM
    mn = _rms_rows(mem_ref[0], gmem_ref[...]).astype(BF16)
    kv = _dot(mn, w_ref[...])
    for h in range(MEM_HEADS):
        sl = slice(h * MEM_HEAD_DIM, (h + 1) * MEM_HEAD_DIM)
        km_ref[0, :, sl] = _rms_rows(kv[:, sl], mkn_ref[...]).astype(BF16)
    vm_ref[0] = kv[:, mem_dim:].astype(BF16)


def _memkv(mem, g_mem, w_mem_kv, mk_norm):
    B, M, D = mem.shape
    mem_dim = MEM_HEADS * MEM_HEAD_DIM
    out = jax.ShapeDtypeStruct((B, M, mem_dim), BF16)
    return pl.pallas_call(
        _memkv_kernel,
        grid=(B,),
        in_specs=[pl.BlockSpec((1, M, D), lambda b: (b, 0, 0)),
                  _const_spec((1, D)),
                  _const_spec((D, 2 * mem_dim)),
                  _const_spec((1, MEM_HEAD_DIM))],
        out_specs=[pl.BlockSpec((1, M, mem_dim), lambda b: (b, 0, 0))] * 2,
        out_shape=[out, out],
        compiler_params=_cparams("arbitrary"),
        name="memkv",
    )(mem, g_mem.reshape(1, D), w_mem_kv.astype(BF16), mk_norm.reshape(1, MEM_HEAD_DIM))


def _swap32(x):
    lane = lax.broadcasted_iota(jnp.int32, x.shape, 1)
    low = (lane & 32) == 0
    return jnp.where(low, pltpu.roll(x, LANES - 32, 1), pltpu.roll(x, 32, 1))


def _inproj_kernel(x_ref, gmix_ref, w_ref, gq_ref, gk_ref, cos_ref, sin_ref, ones_ref,
                   poolw_ref, pscale_ref, km_ref, vm_ref, mqn_ref,
                   q_ref, k_ref, v_ref, pool_ref, memo_ref,
                   xn_ref, uext_ref, *, tm, tiles_per_seq, d_qk, d_v, d_pool, d_mem):
    i = pl.program_id(0)
    pos0 = (i % tiles_per_seq) * tm
    xn_ref[...] = _rms_rows(x_ref[...], gmix_ref[...]).astype(BF16)

    cos = cos_ref[...]
    sin = sin_ref[...]
    ones = ones_ref[...]

    def qk_slab(p, c, gain_ref, out_ref, scale):
        ss = _dot((p * p).astype(BF16), ones)
        n = p * (lax.rsqrt(ss * (1.0 / DA_HEAD_DIM) + EPS) * scale) * gain_ref[...]
        for half in range(MXU_DIM // LANES):
            nh = n[:, half * LANES:(half + 1) * LANES]
            r = nh * cos + _swap32(nh) * sin
            out_ref[:, c + half * LANES:c + (half + 1) * LANES] = r.astype(BF16)

    def v_slab(p, c):
        v_ref[:, c:c + MXU_DIM] = p.astype(BF16)

    def pool_slab(p, c):
        @pl.when(pos0 == 0)
        def _():
            uext_ref[0:POOL_HALO, c:c + MXU_DIM] = jnp.zeros((POOL_HALO, MXU_DIM), F32)

        uext_ref[POOL_HALO:POOL_HALO + tm, c:c + MXU_DIM] = p
        t1 = pos0 + 1 + lax.broadcasted_iota(jnp.int32, (tm, POOL_GROUP_DIM), 0)
        for g in range(c // POOL_GROUP_DIM, (c + MXU_DIM) // POOL_GROUP_DIM):
            w = POOL_WINDOWS[g]
            sl = slice(g * POOL_GROUP_DIM, (g + 1) * POOL_GROUP_DIM)
            u = uext_ref[POOL_HALO:POOL_HALO + tm, sl]
            acc = u
            for s in range(1, w):
                acc = acc + uext_ref[POOL_HALO - s:POOL_HALO - s + tm, sl]
            cnt = jnp.minimum(t1, w).astype(F32)
            z = acc / cnt - u
            zp = _dot(z.astype(BF16), poolw_ref[g])
            pool_ref[:, sl] = (zp * pscale_ref[:, sl]).astype(BF16)
        uext_ref[0:POOL_HALO, c:c + MXU_DIM] = uext_ref[tm:tm + POOL_HALO, c:c + MXU_DIM]

    def mem_slab(p, c):
        inv_sqrt = 1.0 / math.sqrt(MEM_HEAD_DIM)
        for half in range(MXU_DIM // LANES):
            sl = slice(c + half * LANES, c + (half + 1) * LANES)
            qn = (_rms_rows(p[:, half * LANES:(half + 1) * LANES], mqn_ref[...]) * inv_sqrt).astype(BF16)
            s = _dot_nt(qn, km_ref[0, :, sl])
            m = jnp.max(s, axis=-1, keepdims=True)
            e = jnp.exp(s - m)
            l = jnp.sum(e, axis=-1, keepdims=True)
            o = _dot(e.astype(BF16), vm_ref[0, :, sl])
            memo_ref[:, sl] = (o / l).astype(BF16)

    col = 0
    q_slabs = [(col + c, functools.partial(qk_slab, c=c, gain_ref=gq_ref, out_ref=q_ref, scale=QUERY_SCALE))
               for c in range(0, d_qk, MXU_DIM)]
    col += d_qk
    k_slabs = [(col + c, functools.partial(qk_slab, c=c, gain_ref=gk_ref, out_ref=k_ref, scale=1.0))
               for c in range(0, d_qk, MXU_DIM)]
    col += d_qk
    v_slabs = [(col + c, functools.partial(v_slab, c=c)) for c in range(0, d_v, MXU_DIM)]
    col += d_v
    pool_slabs = [(col + c, functools.partial(pool_slab, c=c)) for c in range(0, d_pool, MXU_DIM)]
    col += d_pool
    mem_slabs = [(col + c, functools.partial(mem_slab, c=c)) for c in range(0, d_mem, MXU_DIM)]
    slabs = q_slabs + k_slabs + v_slabs + pool_slabs + mem_slabs

    def project(col0):
        return _dot(xn_ref[...], w_ref[:, col0:col0 + MXU_DIM])

    queue = [project(c0) for c0, _ in slabs[:PROJECT_AHEAD]]
    for n, (_, epilogue) in enumerate(slabs):
        if n + PROJECT_AHEAD < len(slabs):
            queue.append(project(slabs[n + PROJECT_AHEAD][0]))
        epilogue(queue.pop(0))


def _inproj(x2, g_mix, w_in, q_norm, k_norm, cos_tab, sin_tab, pool_w, pool_scale,
            km, vm, mq_norm, *, seq, tm):
    T, D = x2.shape
    d_qk = DA_HEADS * 2 * DA_HEAD_DIM
    d_v = DA_HEADS * DA_V_DIM
    d_pool = len(POOL_WINDOWS) * POOL_GROUP_DIM
    d_mem = MEM_HEADS * MEM_HEAD_DIM
    d_all = 2 * d_qk + d_v + d_pool + d_mem
    M = km.shape[1]
    tiles_per_seq = seq // tm
    reps = MXU_DIM // DA_HEAD_DIM
    gq = jnp.tile(q_norm, reps).reshape(1, MXU_DIM)
    gk = jnp.tile(k_norm, reps).reshape(1, MXU_DIM)
    grp = jnp.arange(MXU_DIM) // DA_HEAD_DIM
    ones = (grp[:, None] == grp[None, :]).astype(BF16)
    kern = functools.partial(_inproj_kernel, tm=tm, tiles_per_seq=tiles_per_seq,
                             d_qk=d_qk, d_v=d_v, d_pool=d_pool, d_mem=d_mem)
    row = lambda width: pl.BlockSpec((tm, width), lambda i: (i, 0))
    return pl.pallas_call(
        kern,
        grid=(T // tm,),
        in_specs=[row(D),
                  _const_spec((1, D)),
                  _const_spec((D, d_all)),
                  _const_spec((1, MXU_DIM)),
                  _const_spec((1, MXU_DIM)),
                  pl.BlockSpec((tm, LANES), lambda i: (i % tiles_per_seq, 0)),
                  pl.BlockSpec((tm, LANES), lambda i: (i % tiles_per_seq, 0)),
                  _const_spec((MXU_DIM, MXU_DIM)),
                  _const_spec((len(POOL_WINDOWS), POOL_GROUP_DIM, POOL_GROUP_DIM)),
                  _const_spec((1, d_pool)),
                  pl.BlockSpec((1, M, d_mem), lambda i: (i // tiles_per_seq, 0, 0)),
                  pl.BlockSpec((1, M, d_mem), lambda i: (i // tiles_per_seq, 0, 0)),
                  _const_spec((1, MEM_HEAD_DIM))],
        out_specs=[row(d_qk), row(d_qk), row(d_v), row(d_pool), row(d_mem)],
        out_shape=[jax.ShapeDtypeStruct((T, d_qk), BF16),
                   jax.ShapeDtypeStruct((T, d_qk), BF16),
                   jax.ShapeDtypeStruct((T, d_v), BF16),
                   jax.ShapeDtypeStruct((T, d_pool), BF16),
                   jax.ShapeDtypeStruct((T, d_mem), BF16)],
        scratch_shapes=[pltpu.VMEM((tm, D), BF16),
                        pltpu.VMEM((tm + POOL_HALO, d_pool), F32)],
        compiler_params=_cparams("arbitrary"),
        name="inproj",
    )(x2, g_mix.reshape(1, D), w_in[:, :d_all].astype(BF16), gq, gk, cos_tab, sin_tab, ones,
      pool_w.astype(BF16), pool_scale.reshape(1, d_pool), km, vm, mq_norm.reshape(1, MEM_HEAD_DIM))


ATTN_UNROLL = 4


def _attn_tables(n_blk):
    qi = [q for q in range(1, n_blk) for _ in range(q)]
    kj = [j for q in range(1, n_blk) for j in range(q)]
    return (np.asarray(qi + qi[-1:] * 2, np.int32), np.asarray(kj + kj[-1:] * 2, np.int32))


def _attn_kernel(tqi_ref, tkj_ref, lamp_ref, q_ref, k_ref, v_ref, gsub_ref, o_ref,
                 qc_ref, acc_ref, m_ref, l_ref, s0_ref, s1_ref, s2_ref, s3_ref, pa_ref, pb_ref,
                 *, seq, tq, n_full):
    lp = lamp_ref[...]
    lam = (jnp.exp(jnp.sum(lp[0:1] * lp[1:2], axis=-1, keepdims=True))
           - jnp.exp(jnp.sum(lp[2:3] * lp[3:4], axis=-1, keepdims=True)) + LAMBDA_INIT)
    n_blk = seq // tq
    lane = lax.broadcasted_iota(jnp.int32, (tq, DA_V_DIM), 1)
    key_i = lax.broadcasted_iota(jnp.int32, (tq, 2 * tq), 0)
    qry_i = lax.broadcasted_iota(jnp.int32, (tq, 2 * tq), 1) % tq
    causal = key_i <= qry_i
    s_bufs = (s0_ref, s1_ref, s2_ref, s3_ref)
    p_bufs = (pa_ref, pb_ref)

    def setup(t, carry):
        r0 = pl.multiple_of(t * tq, tq)
        q = q_ref[pl.ds(r0, tq), :]
        qc_ref[t] = jnp.concatenate([jnp.where(lane < DA_HEAD_DIM, q, jnp.zeros_like(q)),
                                     jnp.where(lane >= DA_HEAD_DIM, q, jnp.zeros_like(q))], axis=0)
        m_ref[t] = jnp.full((1, 2 * tq), NEG_INF, F32)
        l_ref[t] = jnp.zeros((1, 2 * tq), F32)
        acc_ref[t] = jnp.zeros((DA_V_DIM, 2 * tq), F32)
        return carry

    lax.fori_loop(0, n_blk, setup, 0)

    def scores(qi, kj, dst_ref, masked):
        k0 = pl.multiple_of(kj * tq, tq)
        s = _dot_nt(k_ref[pl.ds(k0, tq), :], qc_ref[qi])
        dst_ref[...] = jnp.where(causal, s, NEG_INF) if masked else s

    def softmax(qi, s_ref, p_ref):
        m_prev = m_ref[qi]
        s = s_ref[...]
        m_new = jnp.maximum(m_prev, jnp.max(s, axis=0, keepdims=True))
        alpha = jnp.exp2(m_prev - m_new)
        p = jnp.exp2(s - m_new)
        m_ref[qi] = m_new
        l_ref[qi] = alpha * l_ref[qi] + jnp.sum(p, axis=0, keepdims=True)
        p_ref[...] = p.astype(BF16)
        return alpha

    def values(qi, kj, p_ref, alpha):
        v = v_ref[pl.ds(pl.multiple_of(kj * tq, tq), tq), :]
        pv = lax.dot_general(v, p_ref[...], (((0,), (0,)), ((), ())), preferred_element_type=F32)
        acc_ref[qi] = alpha * acc_ref[qi] + pv

    def finalize(qi):
        on = acc_ref[qi] / l_ref[qi]
        ot = on[:, :tq] - lam * on[:, tq:]
        ot = ot * lax.rsqrt(jnp.mean(ot * ot, axis=0, keepdims=True) + EPS)
        o = ot.T * (gsub_ref[...] * (1.0 - LAMBDA_INIT))
        o_ref[pl.ds(pl.multiple_of(qi * tq, tq), tq), :] = o.astype(BF16)

    ones_row = jnp.ones((1, 2 * tq), F32)

    def full_step(n, u, alpha_prev):
        scores(tqi_ref[n + 2], tkj_ref[n + 2], s_bufs[(u + 2) % 4], False)
        alpha = softmax(tqi_ref[n], s_bufs[u], p_bufs[u % 2])
        prev = jnp.maximum(n - 1, 0)
        values(tqi_ref[prev], tkj_ref[prev], p_bufs[(u + 1) % 2], alpha_prev)
        return alpha

    def full_steps(t, alpha_prev):
        for u in range(ATTN_UNROLL):
            alpha_prev = full_step(ATTN_UNROLL * t + u, u, alpha_prev)
        return alpha_prev

    pb_ref[...] = jnp.zeros(pb_ref.shape, BF16)
    scores(tqi_ref[0], tkj_ref[0], s0_ref, False)
    scores(tqi_ref[1], tkj_ref[1], s1_ref, False)
    alpha_last = lax.fori_loop(0, n_full // ATTN_UNROLL, full_steps, ones_row)
    values(tqi_ref[n_full - 1], tkj_ref[n_full - 1], pb_ref, alpha_last)

    def diag_step(qi, u, alpha_prev):
        nxt = jnp.minimum(qi + 1, n_blk - 1)
        scores(nxt, nxt, s_bufs[1 - u], True)
        alpha = softmax(qi, s_bufs[u], p_bufs[u])
        prev = jnp.maximum(qi - 1, 0)
        values(prev, prev, p_bufs[1 - u], alpha_prev)
        finalize(prev)
        return alpha

    def diag_steps(t, alpha_prev):
        return diag_step(2 * t + 1, 1, diag_step(2 * t, 0, alpha_prev))

    pb_ref[...] = jnp.zeros(pb_ref.shape, BF16)
    scores(0, 0, s0_ref, True)
    alpha_last = lax.fori_loop(0, n_blk // 2, diag_steps, ones_row)
    values(n_blk - 1, n_blk - 1, pb_ref, alpha_last)
    finalize(n_blk - 1)


def _attn(q, k, v, lam_params, g_subln, *, batch, seq, tq):
    T, d_v = v.shape
    n_blk = seq // tq
    tqi, tkj = _attn_tables(n_blk)
    n_full = tqi.shape[0] - 2
    assert n_full % ATTN_UNROLL == 0 and n_blk % 2 == 0, (seq, tq)
    blk = lambda: pl.BlockSpec((seq, DA_V_DIM), lambda b, h, tqi, tkj: (b, h))
    const = lambda shape: pl.BlockSpec(shape, lambda b, h, tqi, tkj: (0,) * len(shape))
    return pl.pallas_call(
        functools.partial(_attn_kernel, seq=seq, tq=tq, n_full=n_full),
        grid_spec=pltpu.PrefetchScalarGridSpec(
            num_scalar_prefetch=2,
            grid=(batch, DA_HEADS),
            in_specs=[const((4, DA_HEAD_DIM)), blk(), blk(), blk(), const((1, DA_V_DIM))],
            out_specs=blk(),
            scratch_shapes=[pltpu.VMEM((n_blk, 2 * tq, DA_V_DIM), BF16),
                            pltpu.VMEM((n_blk, DA_V_DIM, 2 * tq), F32),
                            pltpu.VMEM((n_blk, 1, 2 * tq), F32),
                            pltpu.VMEM((n_blk, 1, 2 * tq), F32)]
                           + [pltpu.VMEM((tq, 2 * tq), F32)] * 4
                           + [pltpu.VMEM((tq, 2 * tq), BF16)] * 2),
        out_shape=jax.ShapeDtypeStruct((T, d_v), BF16),
        compiler_params=_cparams("arbitrary", "arbitrary"),
        name="attn",
    )(jnp.asarray(tqi), jnp.asarray(tkj), lam_params, q, k, v, g_subln.reshape(1, DA_V_DIM))


def _pack_bf16_pairs(a, b):
    hi = pltpu.bitcast(a.astype(BF16).astype(F32), jnp.uint32)
    lo = pltpu.bitcast(b.astype(BF16).astype(F32), jnp.uint32)
    return hi | (lo >> 16)


def _unpack_bf16_pairs(p):
    a = pltpu.bitcast(p & jnp.uint32(0xFFFF0000), F32)
    b = pltpu.bitcast(p << 16, F32)
    return a, b


def _merge_kernel(x_ref, oa_ref, pool_ref, memo_ref, gmix_ref, wg_ref, bg_ref, wba_ref, wbp_ref,
                  wbm_ref, wout_ref, gffn_ref, wrh_ref, wrl_ref, br_ref,
                  h_ref, hn_ref, idx_ref, rank_ref, gate_ref, cnt_ref,
                  xn_ref, mrg_ref, hs_ref, carry_ref, *, tm, d_model):
    i = pl.program_id(0)
    D = d_model

    @pl.when(i == 0)
    def _():
        carry_ref[...] = jnp.zeros(carry_ref.shape, F32)
        hs_ref[...] = jnp.zeros(hs_ref.shape, F32)

    def logits():
        hn = _rms_rows(hs_ref[...], gffn_ref[...])
        hn_ref[...] = _pack_bf16_pairs(hn[:, :D // 2], hn[:, D // 2:])
        hn_hi = hn.astype(BF16)
        hn_lo = (hn - hn_hi.astype(F32)).astype(BF16)
        return (_dot(hn_hi, wrh_ref[...]) + _dot(hn_lo, wrh_ref[...])
                + _dot(hn_hi, wrl_ref[...]) + br_ref[...])

    route = _route_pieces(logits, i > 0, idx_ref, rank_ref, gate_ref, cnt_ref, carry_ref, tm)

    x = x_ref[...]
    xn_ref[...] = _rms_rows(x, gmix_ref[...]).astype(BF16)
    branches = ((oa_ref, wba_ref), (pool_ref, wbp_ref), (memo_ref, wbm_ref))

    def slab_dots(j):
        sl = slice(j * MXU_DIM, (j + 1) * MXU_DIM)
        out = []
        for br, (y_ref, w_ref) in enumerate(branches):
            gsl = slice(br * D + j * MXU_DIM, br * D + (j + 1) * MXU_DIM)
            out.append((_dot(xn_ref[...], wg_ref[:, gsl]), _dot(y_ref[...], w_ref[:, sl])))
        return out

    def slab_epilogue(j, dots):
        merged = None
        for br, (logit, y) in enumerate(dots):
            gsl = slice(br * D + j * MXU_DIM, br * D + (j + 1) * MXU_DIM)
            term = y / (1.0 + jnp.exp(-(logit + bg_ref[:, gsl])))
            merged = term if merged is None else merged + term
        mrg_ref[:, j * MXU_DIM:(j + 1) * MXU_DIM] = merged.astype(BF16)

    n_slabs = D // MXU_DIM
    dots = slab_dots(0)
    for j in range(n_slabs):
        nxt = slab_dots(j + 1) if j + 1 < n_slabs else None
        route[j]()
        slab_epilogue(j, dots)
        dots = nxt
    for piece in route[n_slabs:-1]:
        piece()
    h = x + _dot(mrg_ref[...], wout_ref[...])
    route[-1]()
    h_ref[...] = h
    hs_ref[...] = h


def _route_pieces(logits_fn, live, idx_ref, rank_ref, gate_ref, cnt_ref, carry_ref, tm):
    lane = lax.broadcasted_iota(jnp.int32, (tm, LANES), 1)
    st = dict(vals=[], sels=[], idx_out=jnp.zeros((tm, LANES), jnp.int32))

    def start():
        st["work"] = logits_fn()

    def pick(kk):
        work = st["work"]
        mx = jnp.max(work, axis=-1, keepdims=True)
        idx = jnp.min(jnp.where(work == mx, lane, LANES), axis=-1, keepdims=True)
        sel = lane == idx
        st["vals"].append(mx)
        st["sels"].append(sel)
        st["idx_out"] = jnp.where(lane == kk, idx, st["idx_out"])
        st["work"] = jnp.where(sel, -jnp.inf, work)

    def finish():
        vals, sels = st["vals"], st["sels"]
        exps = [jnp.exp(v - vals[0]) for v in vals]
        denom = exps[0] + exps[1] + exps[2] + exps[3]
        gate_out = jnp.zeros((tm, LANES), F32)
        onehot = jnp.zeros((tm, LANES), F32)
        for kk in range(TOP_K):
            gate_out = jnp.where(lane == kk, exps[kk] / denom, gate_out)
            onehot = jnp.where(sels[kk], 1.0, onehot)
        r_i = lax.broadcasted_iota(jnp.int32, (tm, tm), 0)
        c_i = lax.broadcasted_iota(jnp.int32, (tm, tm), 1)
        lower = (c_i < r_i).astype(BF16)
        prefix = _dot(lower, onehot.astype(BF16)) + carry_ref[0:1, :]
        rank_out = jnp.zeros((tm, LANES), jnp.int32)
        for kk in range(TOP_K):
            rk = jnp.sum(jnp.where(sels[kk], prefix, 0.0), axis=-1, keepdims=True)
            rank_out = jnp.where(lane == kk, rk.astype(jnp.int32), rank_out)
        new_carry = prefix[tm - 1:tm, :] + onehot[tm - 1:tm, :]
        new_carry = jnp.where(live, new_carry, carry_ref[0:1, :])
        carry_ref[...] = jnp.broadcast_to(new_carry, carry_ref.shape)
        idx_ref[...] = st["idx_out"]
        rank_ref[...] = rank_out
        gate_ref[...] = gate_out
        cnt_ref[...] = jnp.broadcast_to(new_carry, cnt_ref.shape).astype(jnp.int32)

    return [start] + [functools.partial(pick, kk) for kk in range(TOP_K)] + [finish]


def _merge(x2, oa, pool, memo, g_mix, w_gates, b_gates, w_b_attn, w_b_pool, w_b_mem, w_out, g_ffn,
           w_router, b_router, *, tm):
    T, D = x2.shape
    d_pool = pool.shape[1]
    d_mem = memo.shape[1]
    E = w_router.shape[1]
    wr = jnp.zeros((D, LANES), F32).at[:, :E].set(w_router)
    wr_hi = wr.astype(BF16)
    wr_lo = (wr - wr_hi.astype(F32)).astype(BF16)
    br = jnp.full((1, LANES), NEG_INF, F32).at[0, :E].set(b_router)
    n_tiles = T // tm
    row = lambda width: pl.BlockSpec((tm, width), lambda i: (jnp.minimum(i, n_tiles - 1), 0))
    routed = lambda width: pl.BlockSpec((tm, width), lambda i: (jnp.maximum(i - 1, 0), 0))
    sublanes = 8
    return pl.pallas_call(
        functools.partial(_merge_kernel, tm=tm, d_model=D),
        grid=(n_tiles + 1,),
        in_specs=[row(D), row(oa.shape[1]), row(d_pool), row(d_mem),
                  _const_spec((1, D)),
                  _const_spec((D, 3 * D)), _const_spec((1, 3 * D)),
                  _const_spec((oa.shape[1], D)), _const_spec((d_pool, D)), _const_spec((d_mem, D)),
                  _const_spec((D, D)), _const_spec((1, D)),
                  _const_spec((D, LANES)), _const_spec((D, LANES)), _const_spec((1, LANES))],
        out_specs=[row(D), routed(D // 2), routed(LANES), routed(LANES), routed(LANES),
                   _const_spec((sublanes, LANES))],
        out_shape=[jax.ShapeDtypeStruct((T, D), F32),
                   jax.ShapeDtypeStruct((T, D // 2), jnp.uint32),
                   jax.ShapeDtypeStruct((T, LANES), jnp.int32),
                   jax.ShapeDtypeStruct((T, LANES), jnp.int32),
                   jax.ShapeDtypeStruct((T, LANES), F32),
                   jax.ShapeDtypeStruct((sublanes, LANES), jnp.int32)],
        scratch_shapes=[pltpu.VMEM((tm, D), BF16),
                        pltpu.VMEM((tm, D), BF16),
                        pltpu.VMEM((tm, D), F32),
                        pltpu.VMEM((sublanes, LANES), F32)],
        compiler_params=_cparams("arbitrary"),
        name="merge",
    )(x2, oa, pool, memo, g_mix.reshape(1, D), w_gates.astype(BF16), b_gates.reshape(1, 3 * D),
      w_b_attn.astype(BF16), w_b_pool.astype(BF16), w_b_mem.astype(BF16), w_out.astype(BF16),
      g_ffn.reshape(1, D), wr_hi, wr_lo, br)


def _experts_kernel(be_ref, nv_ref, src_ref, x_ref, wg_ref, bg_ref, wu_ref, bu_ref, wd_ref, bd_ref,
                    y_ref, act_ref, *, bm, d_model, d_expert):
    i = pl.program_id(0)
    half = d_model // 2

    @pl.when(nv_ref[i] == 0)
    def _():
        y_ref[...] = jnp.zeros(y_ref.shape, y_ref.dtype)

    @pl.when(nv_ref[i] > 0)
    def _():
        a, b = _unpack_bf16_pairs(x_ref[...])
        valid = lax.broadcasted_iota(jnp.int32, (bm, half), 0) < nv_ref[i]
        a = jnp.where(valid, a, 0.0).astype(BF16)
        b = jnp.where(valid, b, 0.0).astype(BF16)
        for j in range(d_expert // MXU_DIM):
            sl = slice(j * MXU_DIM, (j + 1) * MXU_DIM)
            g = _dot(a, wg_ref[0, :half, sl]) + _dot(b, wg_ref[0, half:, sl]) + bg_ref[0, :, sl]
            u = _dot(a, wu_ref[0, :half, sl]) + _dot(b, wu_ref[0, half:, sl]) + bu_ref[0, :, sl]
            g = jnp.minimum(g, SWIGLU_LIMIT)
            u = jnp.clip(u, -SWIGLU_LIMIT, SWIGLU_LIMIT)
            act = g * (1.0 / (1.0 + jnp.exp(-SWIGLU_ALPHA * g))) * (u + 1.0)
            act_ref[:, sl] = act.astype(BF16)
        out = _dot(act_ref[...], wd_ref[0]) + bd_ref[0]
        y_ref[...] = _pack_bf16_pairs(out[:, :half], out[:, half:])


def _experts(blk_e, blk_valid, blk_src, xs, w_gate, b_gate, w_up, b_up, w_down, b_down, *, bm):
    rows_sorted, half = xs.shape
    E, D, De = w_gate.shape
    n_blocks = rows_sorted // bm
    wspec = lambda r, c: pl.BlockSpec((1, r, c), lambda i, be, nv, src: (be[i], 0, 0))
    xspec = pl.BlockSpec((bm, half), lambda i, be, nv, src: (src[i], 0))
    return pl.pallas_call(
        functools.partial(_experts_kernel, bm=bm, d_model=D, d_expert=De),
        grid_spec=pltpu.PrefetchScalarGridSpec(
            num_scalar_prefetch=3,
            grid=(n_blocks,),
            in_specs=[xspec, wspec(D, De), wspec(1, De), wspec(D, De), wspec(1, De),
                      wspec(De, D), wspec(1, D)],
            out_specs=pl.BlockSpec((bm, half), lambda i, be, nv, src: (i, 0)),
            scratch_shapes=[pltpu.VMEM((bm, De), BF16)]),
        out_shape=jax.ShapeDtypeStruct((rows_sorted, half), jnp.uint32),
        compiler_params=_cparams("arbitrary"),
        name="experts",
    )(blk_e, blk_valid, blk_src, xs, w_gate.astype(BF16), b_gate.reshape(E, 1, De),
      w_up.astype(BF16), b_up.reshape(E, 1, De), w_down.astype(BF16), b_down.reshape(E, 1, D))


SC_CORES = 2
SC_SUBCORES = 16
SC_CHUNK = 128


def _sc_mesh():
    return plsc.VectorSubcoreMesh(core_axis_name="c", subcore_axis_name="s",
                                  num_cores=SC_CORES, num_subcores=SC_SUBCORES)


def _sc_scatter_rows(rows, dest, *, rows_out):
    T, width = rows.shape
    n_chunks = T // SC_CHUNK
    per_worker = n_chunks // (SC_CORES * SC_SUBCORES)

    @functools.partial(
        pl.kernel, mesh=_sc_mesh(),
        out_type=jax.ShapeDtypeStruct((rows_out, width), rows.dtype),
        scratch_types=[pltpu.VMEM((TOP_K, SC_CHUNK), jnp.int32),
                       pltpu.VMEM((SC_CHUNK, width), rows.dtype)],
        name="sc_dispatch")
    def scatter(rows_hbm, dest_hbm, out_hbm, idx_v, rows_v):
        worker = lax.axis_index("s") * SC_CORES + lax.axis_index("c")

        @pl.loop(0, per_worker)
        def _(n):
            c = worker * per_worker + n
            pltpu.sync_copy(rows_hbm.at[pl.ds(c * SC_CHUNK, SC_CHUNK)], rows_v)
            pltpu.sync_copy(dest_hbm.at[c], idx_v)
            for kk in range(TOP_K):
                pltpu.sync_copy(rows_v, out_hbm.at[idx_v.at[kk]])

    return scatter(rows, dest)


def _sc_gather_rows(table, src, *, n_tokens):
    _, width = table.shape
    n_chunks = n_tokens // SC_CHUNK
    per_worker = n_chunks // (SC_CORES * SC_SUBCORES)

    @functools.partial(
        pl.kernel, mesh=_sc_mesh(),
        out_type=jax.ShapeDtypeStruct((TOP_K, n_tokens, width), table.dtype),
        scratch_types=[pltpu.VMEM((TOP_K, SC_CHUNK), jnp.int32),
                       pltpu.VMEM((SC_CHUNK, width), table.dtype)],
        name="sc_combine")
    def gather(table_hbm, src_hbm, out_hbm, idx_v, rows_v):
        worker = lax.axis_index("s") * SC_CORES + lax.axis_index("c")

        @pl.loop(0, per_worker)
        def _(n):
            c = worker * per_worker + n
            pltpu.sync_copy(src_hbm.at[c], idx_v)
            for kk in range(TOP_K):
                pltpu.sync_copy(table_hbm.at[idx_v.at[kk]], rows_v)
                pltpu.sync_copy(rows_v, out_hbm.at[kk, pl.ds(c * SC_CHUNK, SC_CHUNK)])

    return gather(table, src)


def _weighted_sum_kernel(gate_ref, h_ref, y_ref, *rest, d_model):
    o_ref = rest[-1]
    half = d_model // 2
    h = h_ref[...]
    lo, hi = h[:, :half], h[:, half:]
    gates = gate_ref[...]
    for kk in range(TOP_K):
        a, b = _unpack_bf16_pairs(y_ref[kk])
        w = gates[:, kk:kk + 1]
        lo = lo + w * a
        hi = hi + w * b
    o_ref[:, :half] = lo
    o_ref[:, half:] = hi


def _weighted_sum(gate, h, y4, out_prev, *, first_tile, tm):
    T, D = h.shape
    n_tiles = y4.shape[1] // tm
    row = lambda width: pl.BlockSpec((tm, width), lambda i: (first_tile + i, 0))
    in_specs = [row(LANES), row(D), pl.BlockSpec((TOP_K, tm, D // 2), lambda i: (0, i, 0))]
    args = [gate, h, y4]
    aliases = {}
    if out_prev is not None:
        in_specs.append(pl.BlockSpec(memory_space=pl.ANY))
        args.append(out_prev)
        aliases = {3: 0}
    return pl.pallas_call(
        functools.partial(_weighted_sum_kernel, d_model=D),
        grid=(n_tiles,),
        in_specs=in_specs,
        out_specs=row(D),
        out_shape=jax.ShapeDtypeStruct((T, D), F32),
        input_output_aliases=aliases,
        compiler_params=_cparams("arbitrary"),
        name="weighted_sum",
    )(*args)


def _block_tables(counts, *, bm, n_blocks):
    padded = ((counts + bm - 1) // bm) * bm
    pend = jnp.cumsum(padded)
    poff = pend - padded
    n_used = pend[-1] // bm
    src = jnp.minimum(jnp.arange(n_blocks, dtype=jnp.int32), n_used - 1)
    blk_e = jnp.sum(pend[None, :] <= (src * bm)[:, None], axis=1)
    blk_e = jnp.minimum(blk_e, N_EXPERTS - 1).astype(jnp.int32)
    blk_valid = jnp.clip(poff[blk_e] + counts[blk_e] - src * bm, 0, bm)
    blk_valid = jnp.where(jnp.arange(n_blocks) < n_used, blk_valid, 0).astype(jnp.int32)
    return poff.astype(jnp.int32), blk_e, blk_valid, src.astype(jnp.int32)


def _rope_tables(seq):
    inv_freq = ROPE_THETA ** (-jnp.arange(0, DA_HEAD_DIM, 2, dtype=F32) / DA_HEAD_DIM)
    ang = jnp.arange(seq, dtype=F32)[:, None] * inv_freq[None, :]
    reps = LANES // (DA_HEAD_DIM // 2)
    cos = jnp.tile(jnp.cos(ang), (1, reps))
    sin = jnp.tile(jnp.sin(ang), (1, reps))
    first_half = (jnp.arange(LANES) % DA_HEAD_DIM) < DA_HEAD_DIM // 2
    return cos, jnp.where(first_half[None, :], -sin, sin)


ROW_TILE = 512
ATTN_TILE = 256
EXPERT_BLOCK = 512
COMBINE_PARTS = 4


def _forward(x, mem, g_mix, w_in, b_gates, q_norm, k_norm, lambda_q1, lambda_k1, lambda_q2, lambda_k2, g_subln, pool_w, pool_scale, g_mem, w_mem_kv, mq_norm, mk_norm, w_b_attn, w_b_pool, w_b_mem, w_out, g_ffn, w_router, b_router, w_gate, b_gate, w_up, b_up, w_down, b_down):
    B, S, D = x.shape
    T = B * S
    x2 = x.reshape(T, D)
    cos_tab, sin_tab = _rope_tables(S)
    km, vm = _memkv(mem, g_mem, w_mem_kv, mk_norm)
    q, k, v, pool, memo = _inproj(x2, g_mix, w_in, q_norm, k_norm, cos_tab, sin_tab, pool_w,
                                  pool_scale, km, vm, mq_norm, seq=S, tm=ROW_TILE)
    lam_params = jnp.stack([lambda_q1, lambda_k1, lambda_q2, lambda_k2])
    o = _attn(q, k, v, lam_params, g_subln, batch=B, seq=S, tq=ATTN_TILE)
    d_in = q.shape[1] + k.shape[1] + v.shape[1] + pool.shape[1] + memo.shape[1]
    h1, hn, idx, rank, gate, cnt = _merge(x2, o, pool, memo, g_mix, w_in[:, d_in:], b_gates, w_b_attn,
                                          w_b_pool, w_b_mem, w_out, g_ffn, w_router, b_router,
                                          tm=ROW_TILE)
    n_blocks = (T * TOP_K) // EXPERT_BLOCK + N_EXPERTS
    poff, blk_e, blk_valid, blk_src = _block_tables(cnt[0, :N_EXPERTS], bm=EXPERT_BLOCK, n_blocks=n_blocks)
    idx_d = idx[:, :TOP_K].reshape(T * TOP_K // LANES, LANES)
    rank_d = rank[:, :TOP_K].reshape(T * TOP_K // LANES, LANES)
    dest = (poff[idx_d] + rank_d).astype(jnp.int32)
    dest = dest.reshape(T // SC_CHUNK, SC_CHUNK, TOP_K).transpose(0, 2, 1)
    xs = _sc_scatter_rows(hn, dest, rows_out=n_blocks * EXPERT_BLOCK)
    ys = _experts(blk_e, blk_valid, blk_src, xs, w_gate, b_gate, w_up, b_up, w_down, b_down,
                  bm=EXPERT_BLOCK)
    out = None
    chunks_per_part = dest.shape[0] // COMBINE_PARTS
    tokens_per_part = T // COMBINE_PARTS
    for part in range(COMBINE_PARTS):
        y4 = _sc_gather_rows(ys, dest[part * chunks_per_part:(part + 1) * chunks_per_part],
                             n_tokens=tokens_per_part)
        out = _weighted_sum(gate, h1, y4, out, first_tile=part * (tokens_per_part // ROW_TILE), tm=ROW_TILE)
    return dict(q=q, k=k, v=v, pool=pool, memo=memo, o=o, h1=h1, hn=hn, idx=idx, rank=rank,
                gate=gate, cnt=cnt, out=out.reshape(B, S, D))


def kernel(x, mem, g_mix, w_in, b_gates, q_norm, k_norm, lambda_q1, lambda_k1, lambda_q2, lambda_k2, g_subln, pool_w, pool_scale, g_mem, w_mem_kv, mq_norm, mk_norm, w_b_attn, w_b_pool, w_b_mem, w_out, g_ffn, w_router, b_router, w_gate, b_gate, w_up, b_up, w_down, b_down):
    return _forward(x, mem, g_mix, w_in, b_gates, q_norm, k_norm, lambda_q1, lambda_k1, lambda_q2,
                    lambda_k2, g_subln, pool_w, pool_scale, g_mem, w_mem_kv, mq_norm, mk_norm,
                    w_b_attn, w_b_pool, w_b_mem, w_out, g_ffn, w_router, b_router, w_gate, b_gate,
                    w_up, b_up, w_down, b_down)["out"]
```

```python
import functools
import math

import jax
import jax.numpy as jnp
import numpy as np
from jax import lax
from jax.experimental import pallas as pl
from jax.experimental.pallas import tpu as pltpu
from jax.experimental.pallas import tpu_sc as plsc

DA_HEADS = 8
DA_HEAD_DIM = 64
DA_V_DIM = 2 * DA_HEAD_DIM
ROPE_THETA = 10000.0
POOL_WINDOWS = (2, 4, 8, 16)
POOL_GROUP_DIM = 128
POOL_HALO = 16
PROJECT_AHEAD = 1
MEM_HEADS = 4
MEM_HEAD_DIM = 128
N_EXPERTS = 32
TOP_K = 4
SWIGLU_LIMIT = 7.0
SWIGLU_ALPHA = 1.702
LAMBDA_INIT = 0.8 - 0.6 * math.exp(-0.3 * 0.0)
EPS = 1e-6
NEG_INF = -1e30
QUERY_SCALE = math.log2(math.e) / math.sqrt(DA_HEAD_DIM)

LANES = 128
MXU_DIM = 256
VMEM_LIMIT_BYTES = 56 * 1024 * 1024

BF16 = jnp.bfloat16
F32 = jnp.float32


def _cparams(*sem):
    return pltpu.CompilerParams(dimension_semantics=sem, vmem_limit_bytes=VMEM_LIMIT_BYTES)


def _const_spec(shape):
    nd = len(shape)
    return pl.BlockSpec(shape, lambda *_: (0,) * nd)


def _dot(a, b):
    return jnp.dot(a, b, preferred_element_type=F32)


def _dot_nt(a, b):
    return lax.dot_general(a, b, (((1,), (1,)), ((), ())), preferred_element_type=F32)


def _rms_rows(x, gain):
    ms = jnp.mean(x * x, axis=-1, keepdims=True)
    return x * lax.rsqrt(ms + EPS) * gain


def _memkv_kernel(mem_ref, gmem_ref, w_ref, mkn_ref, km_ref, vm_ref):
    mem_dim = MEM_HEADS * MEM_HEAD_DIM
    mn = _rms_rows(mem_ref[0], gmem_ref[...]).astype(BF16)
    kv = _dot(mn, w_ref[...])
    for h in range(MEM_HEADS):
        sl = slice(h * MEM_HEAD_DIM, (h + 1) * MEM_HEAD_DIM)
        km_ref[0, :, sl] = _rms_rows(kv[:, sl], mkn_ref[...]).astype(BF16)
    vm_ref[0] = kv[:, mem_dim:].astype(BF16)


def _memkv(mem, g_mem, w_mem_kv, mk_norm):
    B, M, D = mem.shape
    mem_dim = MEM_HEADS * MEM_HEAD_DIM
    out = jax.ShapeDtypeStruct((B, M, mem_dim), BF16)
    return pl.pallas_call(
        _memkv_kernel,
        grid=(B,),
        in_specs=[pl.BlockSpec((1, M, D), lambda b: (b, 0, 0)),
                  _const_spec((1, D)),
                  _const_spec((D, 2 * mem_dim)),
                  _const_spec((1, MEM_HEAD_DIM))],
        out_specs=[pl.BlockSpec((1, M, mem_dim), lambda b: (b, 0, 0))] * 2,
        out_shape=[out, out],
        compiler_params=_cparams("arbitrary"),
        name="memkv",
    )(mem, g_mem.reshape(1, D), w_mem_kv.astype(BF16), mk_norm.reshape(1, MEM_HEAD_DIM))


def _swap32(x):
    lane = lax.broadcasted_iota(jnp.int32, x.shape, 1)
    low = (lane & 32) == 0
    return jnp.where(low, pltpu.roll(x, LANES - 32, 1), pltpu.roll(x, 32, 1))


def _inproj_kernel(x_ref, gmix_ref, w_ref, gq_ref, gk_ref, cos_ref, sin_ref, ones_ref,
                   poolw_ref, pscale_ref, km_ref, vm_ref, mqn_ref,
                   q_ref, k_ref, v_ref, pool_ref, memo_ref,
                   xn_ref, uext_ref, *, tm, tiles_per_seq, d_qk, d_v, d_pool, d_mem):
    i = pl.program_id(0)
    pos0 = (i % tiles_per_seq) * tm
    xn_ref[...] = _rms_rows(x_ref[...], gmix_ref[...]).astype(BF16)

    cos = cos_ref[...]
    sin = sin_ref[...]
    ones = ones_ref[...]

    def qk_slab(p, c, gain_ref, out_ref, scale):
        ss = _dot((p * p).astype(BF16), ones)
        n = p * (lax.rsqrt(ss * (1.0 / DA_HEAD_DIM) + EPS) * scale) * gain_ref[...]
        for half in range(MXU_DIM // LANES):
            nh = n[:, half * LANES:(half + 1) * LANES]
            r = nh * cos + _swap32(nh) * sin
            out_ref[:, c + half * LANES:c + (half + 1) * LANES] = r.astype(BF16)

    def v_slab(p, c):
        v_ref[:, c:c + MXU_DIM] = p.astype(BF16)

    def pool_slab(p, c):
        @pl.when(pos0 == 0)
        def _():
            uext_ref[0:POOL_HALO, c:c + MXU_DIM] = jnp.zeros((POOL_HALO, MXU_DIM), F32)

        uext_ref[POOL_HALO:POOL_HALO + tm, c:c + MXU_DIM] = p
        t1 = pos0 + 1 + lax.broadcasted_iota(jnp.int32, (tm, POOL_GROUP_DIM), 0)
        for g in range(c // POOL_GROUP_DIM, (c + MXU_DIM) // POOL_GROUP_DIM):
            w = POOL_WINDOWS[g]
            sl = slice(g * POOL_GROUP_DIM, (g + 1) * POOL_GROUP_DIM)
            u = uext_ref[POOL_HALO:POOL_HALO + tm, sl]
            acc = u
            for s in range(1, w):
                acc = acc + uext_ref[POOL_HALO - s:POOL_HALO - s + tm, sl]
            cnt = jnp.minimum(t1, w).astype(F32)
            z = acc / cnt - u
            zp = _dot(z.astype(BF16), poolw_ref[g])
            pool_ref[:, sl] = (zp * pscale_ref[:, sl]).astype(BF16)
        uext_ref[0:POOL_HALO, c:c + MXU_DIM] = uext_ref[tm:tm + POOL_HALO, c:c + MXU_DIM]

    def mem_slab(p, c):
        inv_sqrt = 1.0 / math.sqrt(MEM_HEAD_DIM)
        for half in range(MXU_DIM // LANES):
            sl = slice(c + half * LANES, c + (half + 1) * LANES)
            qn = (_rms_rows(p[:, half * LANES:(half + 1) * LANES], mqn_ref[...]) * inv_sqrt).astype(BF16)
            s = _dot_nt(qn, km_ref[0, :, sl])
            m = jnp.max(s, axis=-1, keepdims=True)
            e = jnp.exp(s - m)
            l = jnp.sum(e, axis=-1, keepdims=True)
            o = _dot(e.astype(BF16), vm_ref[0, :, sl])
            memo_ref[:, sl] = (o / l).astype(BF16)

    col = 0
    q_slabs = [(col + c, functools.partial(qk_slab, c=c, gain_ref=gq_ref, out_ref=q_ref, scale=QUERY_SCALE))
               for c in range(0, d_qk, MXU_DIM)]
    col += d_qk
    k_slabs = [(col + c, functools.partial(qk_slab, c=c, gain_ref=gk_ref, out_ref=k_ref, scale=1.0))
               for c in range(0, d_qk, MXU_DIM)]
    col += d_qk
    v_slabs = [(col + c, functools.partial(v_slab, c=c)) for c in range(0, d_v, MXU_DIM)]
    col += d_v
    pool_slabs = [(col + c, functools.partial(pool_slab, c=c)) for c in range(0, d_pool, MXU_DIM)]
    col += d_pool
    mem_slabs = [(col + c, functools.partial(mem_slab, c=c)) for c in range(0, d_mem, MXU_DIM)]
    slabs = q_slabs + k_slabs + v_slabs + pool_slabs + mem_slabs

    def project(col0):
        return _dot(xn_ref[...], w_ref[:, col0:col0 + MXU_DIM])

    queue = [project(c0) for c0, _ in slabs[:PROJECT_AHEAD]]
    for n, (_, epilogue) in enumerate(slabs):
        if n + PROJECT_AHEAD < len(slabs):
            queue.append(project(slabs[n + PROJECT_AHEAD][0]))
        epilogue(queue.pop(0))


def _inproj(x2, g_mix, w_in, q_norm, k_norm, cos_tab, sin_tab, pool_w, pool_scale,
            km, vm, mq_norm, *, seq, tm):
    T, D = x2.shape
    d_qk = DA_HEADS * 2 * DA_HEAD_DIM
    d_v = DA_HEADS * DA_V_DIM
    d_pool = len(POOL_WINDOWS) * POOL_GROUP_DIM
    d_mem = MEM_HEADS * MEM_HEAD_DIM
    d_all = 2 * d_qk + d_v + d_pool + d_mem
    M = km.shape[1]
    tiles_per_seq = seq // tm
    reps = MXU_DIM // DA_HEAD_DIM
    gq = jnp.tile(q_norm, reps).reshape(1, MXU_DIM)
    gk = jnp.tile(k_norm, reps).reshape(1, MXU_DIM)
    grp = jnp.arange(MXU_DIM) // DA_HEAD_DIM
    ones = (grp[:, None] == grp[None, :]).astype(BF16)
    kern = functools.partial(_inproj_kernel, tm=tm, tiles_per_seq=tiles_per_seq,
                             d_qk=d_qk, d_v=d_v, d_pool=d_pool, d_mem=d_mem)
    row = lambda width: pl.BlockSpec((tm, width), lambda i: (i, 0))
    return pl.pallas_call(
        kern,
        grid=(T // tm,),
        in_specs=[row(D),
                  _const_spec((1, D)),
                  _const_spec((D, d_all)),
                  _const_spec((1, MXU_DIM)),
                  _const_spec((1, MXU_DIM)),
                  pl.BlockSpec((tm, LANES), lambda i: (i % tiles_per_seq, 0)),
                  pl.BlockSpec((tm, LANES), lambda i: (i % tiles_per_seq, 0)),
                  _const_spec((MXU_DIM, MXU_DIM)),
                  _const_spec((len(POOL_WINDOWS), POOL_GROUP_DIM, POOL_GROUP_DIM)),
                  _const_spec((1, d_pool)),
                  pl.BlockSpec((1, M, d_mem), lambda i: (i // tiles_per_seq, 0, 0)),
                  pl.BlockSpec((1, M, d_mem), lambda i: (i // tiles_per_seq, 0, 0)),
                  _const_spec((1, MEM_HEAD_DIM))],
        out_specs=[row(d_qk), row(d_qk), row(d_v), row(d_pool), row(d_mem)],
        out_shape=[jax.ShapeDtypeStruct((T, d_qk), BF16),
                   jax.ShapeDtypeStruct((T, d_qk), BF16),
                   jax.ShapeDtypeStruct((T, d_v), BF16),
                   jax.ShapeDtypeStruct((T, d_pool), BF16),
                   jax.ShapeDtypeStruct((T, d_mem), BF16)],
        scratch_shapes=[pltpu.VMEM((tm, D), BF16),
                        pltpu.VMEM((tm + POOL_HALO, d_pool), F32)],
        compiler_params=_cparams("arbitrary"),
        name="inproj",
    )(x2, g_mix.reshape(1, D), w_in[:, :d_all].astype(BF16), gq, gk, cos_tab, sin_tab, ones,
      pool_w.astype(BF16), pool_scale.reshape(1, d_pool), km, vm, mq_norm.reshape(1, MEM_HEAD_DIM))


ATTN_UNROLL = 8


def _attn_tables(n_blk):
    qi = [q for q in range(1, n_blk) for _ in range(q)]
    kj = [j for q in range(1, n_blk) for j in range(q)]
    return (np.asarray(qi + qi[-1:] * 2, np.int32), np.asarray(kj + kj[-1:] * 2, np.int32))


def _attn_kernel(tqi_ref, tkj_ref, lamp_ref, q_ref, k_ref, v_ref, gsub_ref, o_ref,
                 vt_ref, qc_ref, acc_ref, m_ref, l_ref, s0_ref, s1_ref, s2_ref, s3_ref, pa_ref, pb_ref,
                 *, seq, tq, n_full):
    lp = lamp_ref[...]
    lam = (jnp.exp(jnp.sum(lp[0:1] * lp[1:2], axis=-1, keepdims=True))
           - jnp.exp(jnp.sum(lp[2:3] * lp[3:4], axis=-1, keepdims=True)) + LAMBDA_INIT)
    n_blk = seq // tq
    sub = lax.broadcasted_iota(jnp.int32, (DA_V_DIM, tq), 0)
    key_i = lax.broadcasted_iota(jnp.int32, (tq, 2 * tq), 0)
    qry_i = lax.broadcasted_iota(jnp.int32, (tq, 2 * tq), 1) % tq
    causal = key_i <= qry_i
    s_bufs = (s0_ref, s1_ref, s2_ref, s3_ref)
    p_bufs = (pa_ref, pb_ref)

    def setup(t, carry):
        r0 = pl.multiple_of(t * tq, tq)
        vt_ref[t] = v_ref[pl.ds(r0, tq), :].astype(F32).T.astype(BF16)
        qt = q_ref[pl.ds(r0, tq), :].astype(F32).T
        qc_ref[t] = jnp.concatenate([jnp.where(sub < DA_HEAD_DIM, qt, 0.0),
                                     jnp.where(sub >= DA_HEAD_DIM, qt, 0.0)], axis=1).astype(BF16)
        m_ref[t] = jnp.full((1, 2 * tq), NEG_INF, F32)
        l_ref[t] = jnp.zeros((1, 2 * tq), F32)
        acc_ref[t] = jnp.zeros((DA_V_DIM, 2 * tq), F32)
        return carry

    lax.fori_loop(0, n_blk, setup, 0)

    def scores(qi, kj, dst_ref, masked):
        k0 = pl.multiple_of(kj * tq, tq)
        s = _dot(k_ref[pl.ds(k0, tq), :], qc_ref[qi])
        dst_ref[...] = jnp.where(causal, s, NEG_INF) if masked else s

    def softmax(qi, s_ref, p_ref):
        m_prev = m_ref[qi]
        s = s_ref[...]
        m_new = jnp.maximum(m_prev, jnp.max(s, axis=0, keepdims=True))
        alpha = jnp.exp2(m_prev - m_new)
        p = jnp.exp2(s - m_new)
        m_ref[qi] = m_new
        l_ref[qi] = alpha * l_ref[qi] + jnp.sum(p, axis=0, keepdims=True)
        p_ref[...] = p.astype(BF16)
        return alpha

    def values(qi, kj, p_ref, alpha):
        acc_ref[qi] = alpha * acc_ref[qi] + _dot(vt_ref[kj], p_ref[...])

    def finalize(qi):
        on = acc_ref[qi] / l_ref[qi]
        ot = on[:, :tq] - lam * on[:, tq:]
        ot = ot * lax.rsqrt(jnp.mean(ot * ot, axis=0, keepdims=True) + EPS)
        o = ot.T * (gsub_ref[...] * (1.0 - LAMBDA_INIT))
        o_ref[pl.ds(pl.multiple_of(qi * tq, tq), tq), :] = o.astype(BF16)

    ones_row = jnp.ones((1, 2 * tq), F32)

    def full_step(n, u, alpha_prev):
        scores(tqi_ref[n + 2], tkj_ref[n + 2], s_bufs[(u + 2) % 4], False)
        alpha = softmax(tqi_ref[n], s_bufs[u % 4], p_bufs[u % 2])
        prev = jnp.maximum(n - 1, 0)
        values(tqi_ref[prev], tkj_ref[prev], p_bufs[(u + 1) % 2], alpha_prev)
        return alpha

    def full_steps(t, alpha_prev):
        for u in range(ATTN_UNROLL):
            alpha_prev = full_step(ATTN_UNROLL * t + u, u, alpha_prev)
        return alpha_prev

    pb_ref[...] = jnp.zeros(pb_ref.shape, BF16)
    scores(tqi_ref[0], tkj_ref[0], s0_ref, False)
    scores(tqi_ref[1], tkj_ref[1], s1_ref, False)
    alpha_last = lax.fori_loop(0, n_full // ATTN_UNROLL, full_steps, ones_row)
    values(tqi_ref[n_full - 1], tkj_ref[n_full - 1], pb_ref, alpha_last)

    def diag_step(qi, u, alpha_prev):
        nxt = jnp.minimum(qi + 1, n_blk - 1)
        scores(nxt, nxt, s_bufs[1 - u], True)
        alpha = softmax(qi, s_bufs[u], p_bufs[u])
        prev = jnp.maximum(qi - 1, 0)
        values(prev, prev, p_bufs[1 - u], alpha_prev)
        finalize(prev)
        return alpha

    def diag_steps(t, alpha_prev):
        return diag_step(2 * t + 1, 1, diag_step(2 * t, 0, alpha_prev))

    pb_ref[...] = jnp.zeros(pb_ref.shape, BF16)
    scores(0, 0, s0_ref, True)
    alpha_last = lax.fori_loop(0, n_blk // 2, diag_steps, ones_row)
    values(n_blk - 1, n_blk - 1, pb_ref, alpha_last)
    finalize(n_blk - 1)


def _attn(q, k, v, lam_params, g_subln, *, batch, seq, tq):
    T, d_v = v.shape
    n_blk = seq // tq
    tqi, tkj = _attn_tables(n_blk)
    n_full = tqi.shape[0] - 2
    assert n_full % ATTN_UNROLL == 0 and n_blk % 2 == 0, (seq, tq)
    blk = lambda: pl.BlockSpec((seq, DA_V_DIM), lambda b, h, tqi, tkj: (b, h))
    const = lambda shape: pl.BlockSpec(shape, lambda b, h, tqi, tkj: (0,) * len(shape))
    return pl.pallas_call(
        functools.partial(_attn_kernel, seq=seq, tq=tq, n_full=n_full),
        grid_spec=pltpu.PrefetchScalarGridSpec(
            num_scalar_prefetch=2,
            grid=(batch, DA_HEADS),
            in_specs=[const((4, DA_HEAD_DIM)), blk(), blk(), blk(), const((1, DA_V_DIM))],
            out_specs=blk(),
            scratch_shapes=[pltpu.VMEM((n_blk, DA_V_DIM, tq), BF16),
                            pltpu.VMEM((n_blk, DA_V_DIM, 2 * tq), BF16),
                            pltpu.VMEM((n_blk, DA_V_DIM, 2 * tq), F32),
                            pltpu.VMEM((n_blk, 1, 2 * tq), F32),
                            pltpu.VMEM((n_blk, 1, 2 * tq), F32)]
                           + [pltpu.VMEM((tq, 2 * tq), F32)] * 4
                           + [pltpu.VMEM((tq, 2 * tq), BF16)] * 2),
        out_shape=jax.ShapeDtypeStruct((T, d_v), BF16),
        compiler_params=_cparams("arbitrary", "arbitrary"),
        name="attn",
    )(jnp.asarray(tqi), jnp.asarray(tkj), lam_params, q, k, v, g_subln.reshape(1, DA_V_DIM))


def _pack_bf16_pairs(a, b):
    hi = pltpu.bitcast(a.astype(BF16).astype(F32), jnp.uint32)
    lo = pltpu.bitcast(b.astype(BF16).astype(F32), jnp.uint32)
    return hi | (lo >> 16)


def _unpack_bf16_pairs(p):
    a = pltpu.bitcast(p & jnp.uint32(0xFFFF0000), F32)
    b = pltpu.bitcast(p << 16, F32)
    return a, b


def _merge_kernel(x_ref, oa_ref, pool_ref, memo_ref, gmix_ref, wg_ref, bg_ref, wba_ref, wbp_ref,
                  wbm_ref, wout_ref, gffn_ref, wrh_ref, wrl_ref, br_ref,
                  h_ref, hn_ref, idx_ref, rank_ref, gate_ref, cnt_ref,
                  xn_ref, mrg_ref, hs_ref, carry_ref, *, tm, d_model):
    i = pl.program_id(0)
    D = d_model

    @pl.when(i == 0)
    def _():
        carry_ref[...] = jnp.zeros(carry_ref.shape, F32)
        hs_ref[...] = jnp.zeros(hs_ref.shape, F32)

    def logits():
        hn = _rms_rows(hs_ref[...], gffn_ref[...])
        hn_ref[...] = _pack_bf16_pairs(hn[:, :D // 2], hn[:, D // 2:])
        hn_hi = hn.astype(BF16)
        hn_lo = (hn - hn_hi.astype(F32)).astype(BF16)
        return (_dot(hn_hi, wrh_ref[...]) + _dot(hn_lo, wrh_ref[...])
                + _dot(hn_hi, wrl_ref[...]) + br_ref[...])

    route = _route_pieces(logits, i > 0, idx_ref, rank_ref, gate_ref, cnt_ref, carry_ref, tm)

    x = x_ref[...]
    xn_ref[...] = _rms_rows(x, gmix_ref[...]).astype(BF16)
    branches = ((oa_ref, wba_ref), (pool_ref, wbp_ref), (memo_ref, wbm_ref))

    def slab_dots(j):
        sl = slice(j * MXU_DIM, (j + 1) * MXU_DIM)
        out = []
        for br, (y_ref, w_ref) in enumerate(branches):
            gsl = slice(br * D + j * MXU_DIM, br * D + (j + 1) * MXU_DIM)
            out.append((_dot(xn_ref[...], wg_ref[:, gsl]), _dot(y_ref[...], w_ref[:, sl])))
        return out

    def slab_epilogue(j, dots):
        merged = None
        for br, (logit, y) in enumerate(dots):
            gsl = slice(br * D + j * MXU_DIM, br * D + (j + 1) * MXU_DIM)
            term = y / (1.0 + jnp.exp(-(logit + bg_ref[:, gsl])))
            merged = term if merged is None else merged + term
        mrg_ref[:, j * MXU_DIM:(j + 1) * MXU_DIM] = merged.astype(BF16)

    n_slabs = D // MXU_DIM
    dots = slab_dots(0)
    for j in range(n_slabs):
        nxt = slab_dots(j + 1) if j + 1 < n_slabs else None
        route[j]()
        slab_epilogue(j, dots)
        dots = nxt
    for piece in route[n_slabs:-1]:
        piece()
    h = x + _dot(mrg_ref[...], wout_ref[...])
    route[-1]()
    h_ref[...] = h
    hs_ref[...] = h


def _route_pieces(logits_fn, live, idx_ref, rank_ref, gate_ref, cnt_ref, carry_ref, tm):
    lane = lax.broadcasted_iota(jnp.int32, (tm, LANES), 1)
    st = dict(vals=[], sels=[], idx_out=jnp.zeros((tm, LANES), jnp.int32))

    def start():
        st["work"] = logits_fn()

    def pick(kk):
        work = st["work"]
        mx = jnp.max(work, axis=-1, keepdims=True)
        idx = jnp.min(jnp.where(work == mx, lane, LANES), axis=-1, keepdims=True)
        sel = lane == idx
        st["vals"].append(mx)
        st["sels"].append(sel)
        st["idx_out"] = jnp.where(lane == kk, idx, st["idx_out"])
        st["work"] = jnp.where(sel, -jnp.inf, work)

    def finish():
        vals, sels = st["vals"], st["sels"]
        exps = [jnp.exp(v - vals[0]) for v in vals]
        denom = exps[0] + exps[1] + exps[2] + exps[3]
        gate_out = jnp.zeros((tm, LANES), F32)
        onehot = jnp.zeros((tm, LANES), F32)
        for kk in range(TOP_K):
            gate_out = jnp.where(lane == kk, exps[kk] / denom, gate_out)
            onehot = jnp.where(sels[kk], 1.0, onehot)
        r_i = lax.broadcasted_iota(jnp.int32, (tm, tm), 0)
        c_i = lax.broadcasted_iota(jnp.int32, (tm, tm), 1)
        lower = (c_i < r_i).astype(BF16)
        prefix = _dot(lower, onehot.astype(BF16)) + carry_ref[0:1, :]
        rank_out = jnp.zeros((tm, LANES), jnp.int32)
        for kk in range(TOP_K):
            rk = jnp.sum(jnp.where(sels[kk], prefix, 0.0), axis=-1, keepdims=True)
            rank_out = jnp.where(lane == kk, rk.astype(jnp.int32), rank_out)
        new_carry = prefix[tm - 1:tm, :] + onehot[tm - 1:tm, :]
        new_carry = jnp.where(live, new_carry, carry_ref[0:1, :])
        carry_ref[...] = jnp.broadcast_to(new_carry, carry_ref.shape)
        idx_ref[...] = st["idx_out"]
        rank_ref[...] = rank_out
        gate_ref[...] = gate_out
        cnt_ref[...] = jnp.broadcast_to(new_carry, cnt_ref.shape).astype(jnp.int32)

    return [start] + [functools.partial(pick, kk) for kk in range(TOP_K)] + [finish]


def _merge(x2, oa, pool, memo, g_mix, w_gates, b_gates, w_b_attn, w_b_pool, w_b_mem, w_out, g_ffn,
           w_router, b_router, *, tm):
    T, D = x2.shape
    d_pool = pool.shape[1]
    d_mem = memo.shape[1]
    E = w_router.shape[1]
    wr = jnp.zeros((D, LANES), F32).at[:, :E].set(w_router)
    wr_hi = wr.astype(BF16)
    wr_lo = (wr - wr_hi.astype(F32)).astype(BF16)
    br = jnp.full((1, LANES), NEG_INF, F32).at[0, :E].set(b_router)
    n_tiles = T // tm
    row = lambda width: pl.BlockSpec((tm, width), lambda i: (jnp.minimum(i, n_tiles - 1), 0))
    routed = lambda width: pl.BlockSpec((tm, width), lambda i: (jnp.maximum(i - 1, 0), 0))
    sublanes = 8
    return pl.pallas_call(
        functools.partial(_merge_kernel, tm=tm, d_model=D),
        grid=(n_tiles + 1,),
        in_specs=[row(D), row(oa.shape[1]), row(d_pool), row(d_mem),
                  _const_spec((1, D)),
                  _const_spec((D, 3 * D)), _const_spec((1, 3 * D)),
                  _const_spec((oa.shape[1], D)), _const_spec((d_pool, D)), _const_spec((d_mem, D)),
                  _const_spec((D, D)), _const_spec((1, D)),
                  _const_spec((D, LANES)), _const_spec((D, LANES)), _const_spec((1, LANES))],
        out_specs=[row(D), routed(D // 2), routed(LANES), routed(LANES), routed(LANES),
                   _const_spec((sublanes, LANES))],
        out_shape=[jax.ShapeDtypeStruct((T, D), F32),
                   jax.ShapeDtypeStruct((T, D // 2), jnp.uint32),
                   jax.ShapeDtypeStruct((T, LANES), jnp.int32),
                   jax.ShapeDtypeStruct((T, LANES), jnp.int32),
                   jax.ShapeDtypeStruct((T, LANES), F32),
                   jax.ShapeDtypeStruct((sublanes, LANES), jnp.int32)],
        scratch_shapes=[pltpu.VMEM((tm, D), BF16),
                        pltpu.VMEM((tm, D), BF16),
                        pltpu.VMEM((tm, D), F32),
                        pltpu.VMEM((sublanes, LANES), F32)],
        compiler_params=_cparams("arbitrary"),
        name="merge",
    )(x2, oa, pool, memo, g_mix.reshape(1, D), w_gates.astype(BF16), b_gates.reshape(1, 3 * D),
      w_b_attn.astype(BF16), w_b_pool.astype(BF16), w_b_mem.astype(BF16), w_out.astype(BF16),
      g_ffn.reshape(1, D), wr_hi, wr_lo, br)


def _experts_kernel(be_ref, nv_ref, src_ref, x_ref, wg_ref, bg_ref, wu_ref, bu_ref, wd_ref, bd_ref,
                    y_ref, act_ref, *, bm, d_model, d_expert):
    i = pl.program_id(0)
    half = d_model // 2

    @pl.when(nv_ref[i] == 0)
    def _():
        y_ref[...] = jnp.zeros(y_ref.shape, y_ref.dtype)

    @pl.when(nv_ref[i] > 0)
    def _():
        a, b = _unpack_bf16_pairs(x_ref[...])
        valid = lax.broadcasted_iota(jnp.int32, (bm, half), 0) < nv_ref[i]
        a = jnp.where(valid, a, 0.0).astype(BF16)
        b = jnp.where(valid, b, 0.0).astype(BF16)
        for j in range(d_expert // MXU_DIM):
            sl = slice(j * MXU_DIM, (j + 1) * MXU_DIM)
            g = _dot(a, wg_ref[0, :half, sl]) + _dot(b, wg_ref[0, half:, sl]) + bg_ref[0, :, sl]
            u = _dot(a, wu_ref[0, :half, sl]) + _dot(b, wu_ref[0, half:, sl]) + bu_ref[0, :, sl]
            g = jnp.minimum(g, SWIGLU_LIMIT)
            u = jnp.clip(u, -SWIGLU_LIMIT, SWIGLU_LIMIT)
            act = g * (1.0 / (1.0 + jnp.exp(-SWIGLU_ALPHA * g))) * (u + 1.0)
            act_ref[:, sl] = act.astype(BF16)
        out = _dot(act_ref[...], wd_ref[0]) + bd_ref[0]
        y_ref[...] = _pack_bf16_pairs(out[:, :half], out[:, half:])


def _experts(blk_e, blk_valid, blk_src, xs, w_gate, b_gate, w_up, b_up, w_down, b_down, *, bm):
    rows_sorted, half = xs.shape
    E, D, De = w_gate.shape
    n_blocks = rows_sorted // bm
    wspec = lambda r, c: pl.BlockSpec((1, r, c), lambda i, be, nv, src: (be[i], 0, 0))
    xspec = pl.BlockSpec((bm, half), lambda i, be, nv, src: (src[i], 0))
    return pl.pallas_call(
        functools.partial(_experts_kernel, bm=bm, d_model=D, d_expert=De),
        grid_spec=pltpu.PrefetchScalarGridSpec(
            num_scalar_prefetch=3,
            grid=(n_blocks,),
            in_specs=[xspec, wspec(D, De), wspec(1, De), wspec(D, De), wspec(1, De),
                      wspec(De, D), wspec(1, D)],
            out_specs=pl.BlockSpec((bm, half), lambda i, be, nv, src: (i, 0)),
            scratch_shapes=[pltpu.VMEM((bm, De), BF16)]),
        out_shape=jax.ShapeDtypeStruct((rows_sorted, half), jnp.uint32),
        compiler_params=_cparams("arbitrary"),
        name="experts",
    )(blk_e, blk_valid, blk_src, xs, w_gate.astype(BF16), b_gate.reshape(E, 1, De),
      w_up.astype(BF16), b_up.reshape(E, 1, De), w_down.astype(BF16), b_down.reshape(E, 1, D))


SC_CORES = 2
SC_SUBCORES = 16
SC_CHUNK = 128


def _sc_mesh():
    return plsc.VectorSubcoreMesh(core_axis_name="c", subcore_axis_name="s",
                                  num_cores=SC_CORES, num_subcores=SC_SUBCORES)


def _sc_scatter_rows(rows, dest, *, rows_out):
    T, width = rows.shape
    n_chunks = T // SC_CHUNK
    per_worker = n_chunks // (SC_CORES * SC_SUBCORES)

    @functools.partial(
        pl.kernel, mesh=_sc_mesh(),
        out_type=jax.ShapeDtypeStruct((rows_out, width), rows.dtype),
        scratch_types=[pltpu.VMEM((TOP_K, SC_CHUNK), jnp.int32),
                       pltpu.VMEM((SC_CHUNK, width), rows.dtype)],
        name="sc_dispatch")
    def scatter(rows_hbm, dest_hbm, out_hbm, idx_v, rows_v):
        worker = lax.axis_index("s") * SC_CORES + lax.axis_index("c")

        @pl.loop(0, per_worker)
        def _(n):
            c = worker * per_worker + n
            pltpu.sync_copy(rows_hbm.at[pl.ds(c * SC_CHUNK, SC_CHUNK)], rows_v)
            pltpu.sync_copy(dest_hbm.at[c], idx_v)
            for kk in range(TOP_K):
                pltpu.sync_copy(rows_v, out_hbm.at[idx_v.at[kk]])

    return scatter(rows, dest)


def _sc_gather_rows(table, src, *, n_tokens):
    _, width = table.shape
    n_chunks = n_tokens // SC_CHUNK
    per_worker = n_chunks // (SC_CORES * SC_SUBCORES)

    @functools.partial(
        pl.kernel, mesh=_sc_mesh(),
        out_type=jax.ShapeDtypeStruct((TOP_K, n_tokens, width), table.dtype),
        scratch_types=[pltpu.VMEM((TOP_K, SC_CHUNK), jnp.int32),
                       pltpu.VMEM((SC_CHUNK, width), table.dtype)],
        name="sc_combine")
    def gather(table_hbm, src_hbm, out_hbm, idx_v, rows_v):
        worker = lax.axis_index("s") * SC_CORES + lax.axis_index("c")

        @pl.loop(0, per_worker)
        def _(n):
            c = worker * per_worker + n
            pltpu.sync_copy(src_hbm.at[c], idx_v)
            for kk in range(TOP_K):
                pltpu.sync_copy(table_hbm.at[idx_v.at[kk]], rows_v)
                pltpu.sync_copy(rows_v, out_hbm.at[kk, pl.ds(c * SC_CHUNK, SC_CHUNK)])

    return gather(table, src)


def _weighted_sum_kernel(gate_ref, h_ref, y_ref, *rest, d_model):
    o_ref = rest[-1]
    half = d_model // 2
    h = h_ref[...]
    lo, hi = h[:, :half], h[:, half:]
    gates = gate_ref[...]
    for kk in range(TOP_K):
        a, b = _unpack_bf16_pairs(y_ref[kk])
        w = gates[:, kk:kk + 1]
        lo = lo + w * a
        hi = hi + w * b
    o_ref[:, :half] = lo
    o_ref[:, half:] = hi


def _weighted_sum(gate, h, y4, out_prev, *, first_tile, tm):
    T, D = h.shape
    n_tiles = y4.shape[1] // tm
    row = lambda width: pl.BlockSpec((tm, width), lambda i: (first_tile + i, 0))
    in_specs = [row(LANES), row(D), pl.BlockSpec((TOP_K, tm, D // 2), lambda i: (0, i, 0))]
    args = [gate, h, y4]
    aliases = {}
    if out_prev is not None:
        in_specs.append(pl.BlockSpec(memory_space=pl.ANY))
        args.append(out_prev)
        aliases = {3: 0}
    return pl.pallas_call(
        functools.partial(_weighted_sum_kernel, d_model=D),
        grid=(n_tiles,),
        in_specs=in_specs,
        out_specs=row(D),
        out_shape=jax.ShapeDtypeStruct((T, D), F32),
        input_output_aliases=aliases,
        compiler_params=_cparams("arbitrary"),
        name="weighted_sum",
    )(*args)


def _block_tables(counts, *, bm, n_blocks):
    padded = ((counts + bm - 1) // bm) * bm
    pend = jnp.cumsum(padded)
    poff = pend - padded
    n_used = pend[-1] // bm
    src = jnp.minimum(jnp.arange(n_blocks, dtype=jnp.int32), n_used - 1)
    blk_e = jnp.sum(pend[None, :] <= (src * bm)[:, None], axis=1)
    blk_e = jnp.minimum(blk_e, N_EXPERTS - 1).astype(jnp.int32)
    blk_valid = jnp.clip(poff[blk_e] + counts[blk_e] - src * bm, 0, bm)
    blk_valid = jnp.where(jnp.arange(n_blocks) < n_used, blk_valid, 0).astype(jnp.int32)
    return poff.astype(jnp.int32), blk_e, blk_valid, src.astype(jnp.int32)


def _rope_tables(seq):
    inv_freq = ROPE_THETA ** (-jnp.arange(0, DA_HEAD_DIM, 2, dtype=F32) / DA_HEAD_DIM)
    ang = jnp.arange(seq, dtype=F32)[:, None] * inv_freq[None, :]
    reps = LANES // (DA_HEAD_DIM // 2)
    cos = jnp.tile(jnp.cos(ang), (1, reps))
    sin = jnp.tile(jnp.sin(ang), (1, reps))
    first_half = (jnp.arange(LANES) % DA_HEAD_DIM) < DA_HEAD_DIM // 2
    return cos, jnp.where(first_half[None, :], -sin, sin)


ROW_TILE = 512
ATTN_TILE = 256
EXPERT_BLOCK = 512
COMBINE_PARTS = 4


def _forward(x, mem, g_mix, w_in, b_gates, q_norm, k_norm, lambda_q1, lambda_k1, lambda_q2, lambda_k2, g_subln, pool_w, pool_scale, g_mem, w_mem_kv, mq_norm, mk_norm, w_b_attn, w_b_pool, w_b_mem, w_out, g_ffn, w_router, b_router, w_gate, b_gate, w_up, b_up, w_down, b_down):
    B, S, D = x.shape
    T = B * S
    x2 = x.reshape(T, D)
    cos_tab, sin_tab = _rope_tables(S)
    km, vm = _memkv(mem, g_mem, w_mem_kv, mk_norm)
    q, k, v, pool, memo = _inproj(x2, g_mix, w_in, q_norm, k_norm, cos_tab, sin_tab, pool_w,
                                  pool_scale, km, vm, mq_norm, seq=S, tm=ROW_TILE)
    lam_params = jnp.stack([lambda_q1, lambda_k1, lambda_q2, lambda_k2])
    o = _attn(q, k, v, lam_params, g_subln, batch=B, seq=S, tq=ATTN_TILE)
    d_in = q.shape[1] + k.shape[1] + v.shape[1] + pool.shape[1] + memo.shape[1]
    h1, hn, idx, rank, gate, cnt = _merge(x2, o, pool, memo, g_mix, w_in[:, d_in:], b_gates, w_b_attn,
                                          w_b_pool, w_b_mem, w_out, g_ffn, w_router, b_router,
                                          tm=ROW_TILE)
    n_blocks = (T * TOP_K) // EXPERT_BLOCK + N_EXPERTS
    poff, blk_e, blk_valid, blk_src = _block_tables(cnt[0, :N_EXPERTS], bm=EXPERT_BLOCK, n_blocks=n_blocks)
    idx_d = idx[:, :TOP_K].reshape(T * TOP_K // LANES, LANES)
    rank_d = rank[:, :TOP_K].reshape(T * TOP_K // LANES, LANES)
    dest = (poff[idx_d] + rank_d).astype(jnp.int32)
    dest = dest.reshape(T // SC_CHUNK, SC_CHUNK, TOP_K).transpose(0, 2, 1)
    xs = _sc_scatter_rows(hn, dest, rows_out=n_blocks * EXPERT_BLOCK)
    ys = _experts(blk_e, blk_valid, blk_src, xs, w_gate, b_gate, w_up, b_up, w_down, b_down,
                  bm=EXPERT_BLOCK)
    out = None
    chunks_per_part = dest.shape[0] // COMBINE_PARTS
    tokens_per_part = T // COMBINE_PARTS
    for part in range(COMBINE_PARTS):
        y4 = _sc_gather_rows(ys, dest[part * chunks_per_part:(part + 1) * chunks_per_part],
                             n_tokens=tokens_per_part)
        out = _weighted_sum(gate, h1, y4, out, first_tile=part * (tokens_per_part // ROW_TILE), tm=ROW_TILE)
    return dict(q=q, k=k, v=v, pool=pool, memo=memo, o=o, h1=h1, hn=hn, idx=idx, rank=rank,
                gate=gate, cnt=cnt, out=out.reshape(B, S, D))


def kernel(x, mem, g_mix, w_in, b_gates, q_norm, k_norm, lambda_q1, lambda_k1, lambda_q2, lambda_k2, g_subln, pool_w, pool_scale, g_mem, w_mem_kv, mq_norm, mk_norm, w_b_attn, w_b_pool, w_b_mem, w_out, g_ffn, w_router, b_router, w_gate, b_gate, w_up, b_up, w_down, b_down):
    return _forward(x, mem, g_mix, w_in, b_gates, q_norm, k_norm, lambda_q1, lambda_k1, lambda_q2,
                    lambda_k2, g_subln, pool_w, pool_scale, g_mem, w_mem_kv, mq_norm, mk_norm,
                    w_b_attn, w_b_pool, w_b_mem, w_out, g_ffn, w_router, b_router, w_gate, b_gate,
                    w_up, b_up, w_down, b_down)["out"]
```

```python
import functools
import math

import jax
import jax.numpy as jnp
import numpy as np
from jax import lax
from jax.experimental import pallas as pl
from jax.experimental.pallas import tpu as pltpu
from jax.experimental.pallas import tpu_sc as plsc

DA_HEADS = 8
DA_HEAD_DIM = 64
DA_V_DIM = 2 * DA_HEAD_DIM
ROPE_THETA = 10000.0
POOL_WINDOWS = (2, 4, 8, 16)
POOL_GROUP_DIM = 128
POOL_HALO = 16
PROJECT_AHEAD = 1
MEM_HEADS = 4
MEM_HEAD_DIM = 128
N_EXPERTS = 32
TOP_K = 4
SWIGLU_LIMIT = 7.0
SWIGLU_ALPHA = 1.702
LAMBDA_INIT = 0.8 - 0.6 * math.exp(-0.3 * 0.0)
EPS = 1e-6
NEG_INF = -1e30
QUERY_SCALE = math.log2(math.e) / math.sqrt(DA_HEAD_DIM)

LANES = 128
MXU_DIM = 256
VMEM_LIMIT_BYTES = 56 * 1024 * 1024

BF16 = jnp.bfloat16
F32 = jnp.float32


def _cparams(*sem):
    return pltpu.CompilerParams(dimension_semantics=sem, vmem_limit_bytes=VMEM_LIMIT_BYTES)


def _const_spec(shape):
    nd = len(shape)
    return pl.BlockSpec(shape, lambda *_: (0,) * nd)


def _dot(a, b):
    return jnp.dot(a, b, preferred_element_type=F32)


def _dot_nt(a, b):
    return lax.dot_general(a, b, (((1,), (1,)), ((), ())), preferred_element_type=F32)


def _rms_rows(x, gain):
    ms = jnp.mean(x * x, axis=-1, keepdims=True)
    return x * lax.rsqrt(ms + EPS) * gain


def _memkv_kernel(mem_ref, gmem_ref, w_ref, mkn_ref, km_ref, vm_ref):
    mem_dim = MEM_HEADS * MEM_HEAD_DIM
    mn = _rms_rows(mem_ref[0], gmem_ref[...]).astype(BF16)
    kv = _dot(mn, w_ref[...])
    for h in range(MEM_HEADS):
        sl = slice(h * MEM_HEAD_DIM, (h + 1) * MEM_HEAD_DIM)
        km_ref[0, :, sl] = _rms_rows(kv[:, sl], mkn_ref[...]).astype(BF16)
    vm_ref[0] = kv[:, mem_dim:].astype(BF16)


def _memkv(mem, g_mem, w_mem_kv, mk_norm):
    B, M, D = mem.shape
    mem_dim = MEM_HEADS * MEM_HEAD_DIM
    out = jax.ShapeDtypeStruct((B, M, mem_dim), BF16)
    return pl.pallas_call(
        _memkv_kernel,
        grid=(B,),
        in_specs=[pl.BlockSpec((1, M, D), lambda b: (b, 0, 0)),
                  _const_spec((1, D)),
                  _const_spec((D, 2 * mem_dim)),
                  _const_spec((1, MEM_HEAD_DIM))],
        out_specs=[pl.BlockSpec((1, M, mem_dim), lambda b: (b, 0, 0))] * 2,
        out_shape=[out, out],
        compiler_params=_cparams("arbitrary"),
        name="memkv",
    )(mem, g_mem.reshape(1, D), w_mem_kv.astype(BF16), mk_norm.reshape(1, MEM_HEAD_DIM))


def _swap32(x):
    lane = lax.broadcasted_iota(jnp.int32, x.shape, 1)
    low = (lane & 32) == 0
    return jnp.where(low, pltpu.roll(x, LANES - 32, 1), pltpu.roll(x, 32, 1))


def _inproj_kernel(x_ref, gmix_ref, w_ref, gq_ref, gk_ref, cos_ref, sin_ref, ones_ref,
                   poolw_ref, pscale_ref, km_ref, vm_ref, mqn_ref,
                   q_ref, k_ref, v_ref, pool_ref, memo_ref,
                   xn_ref, uext_ref, *, tm, tiles_per_seq, d_qk, d_v, d_pool, d_mem):
    i = pl.program_id(0)
    pos0 = (i % tiles_per_seq) * tm
    xn_ref[...] = _rms_rows(x_ref[...], gmix_ref[...]).astype(BF16)

    cos = cos_ref[...]
    sin = sin_ref[...]
    ones = ones_ref[...]

    def qk_slab(p, c, gain_ref, out_ref, scale):
        ss = _dot((p * p).astype(BF16), ones)
        n = p * (lax.rsqrt(ss * (1.0 / DA_HEAD_DIM) + EPS) * scale) * gain_ref[...]
        for half in range(MXU_DIM // LANES):
            nh = n[:, half * LANES:(half + 1) * LANES]
            r = nh * cos + _swap32(nh) * sin
            out_ref[:, c + half * LANES:c + (half + 1) * LANES] = r.astype(BF16)

    def v_slab(p, c):
        v_ref[:, c:c + MXU_DIM] = p.astype(BF16)

    def pool_slab(p, c):
        @pl.when(pos0 == 0)
        def _():
            uext_ref[0:POOL_HALO, c:c + MXU_DIM] = jnp.zeros((POOL_HALO, MXU_DIM), F32)

        uext_ref[POOL_HALO:POOL_HALO + tm, c:c + MXU_DIM] = p
        t1 = pos0 + 1 + lax.broadcasted_iota(jnp.int32, (tm, POOL_GROUP_DIM), 0)
        for g in range(c // POOL_GROUP_DIM, (c + MXU_DIM) // POOL_GROUP_DIM):
            w = POOL_WINDOWS[g]
            sl = slice(g * POOL_GROUP_DIM, (g + 1) * POOL_GROUP_DIM)
            u = uext_ref[POOL_HALO:POOL_HALO + tm, sl]
            acc = u
            for s in range(1, w):
                acc = acc + uext_ref[POOL_HALO - s:POOL_HALO - s + tm, sl]
            cnt = jnp.minimum(t1, w).astype(F32)
            z = acc / cnt - u
            zp = _dot(z.astype(BF16), poolw_ref[g])
            pool_ref[:, sl] = (zp * pscale_ref[:, sl]).astype(BF16)
        uext_ref[0:POOL_HALO, c:c + MXU_DIM] = uext_ref[tm:tm + POOL_HALO, c:c + MXU_DIM]

    def mem_slab(p, c):
        inv_sqrt = 1.0 / math.sqrt(MEM_HEAD_DIM)
        for half in range(MXU_DIM // LANES):
            sl = slice(c + half * LANES, c + (half + 1) * LANES)
            qn = (_rms_rows(p[:, half * LANES:(half + 1) * LANES], mqn_ref[...]) * inv_sqrt).astype(BF16)
            s = _dot_nt(qn, km_ref[0, :, sl])
            m = jnp.max(s, axis=-1, keepdims=True)
            e = jnp.exp(s - m)
            l = jnp.sum(e, axis=-1, keepdims=True)
            o = _dot(e.astype(BF16), vm_ref[0, :, sl])
            memo_ref[:, sl] = (o / l).astype(BF16)

    col = 0
    q_slabs = [(col + c, functools.partial(qk_slab, c=c, gain_ref=gq_ref, out_ref=q_ref, scale=QUERY_SCALE))
               for c in range(0, d_qk, MXU_DIM)]
    col += d_qk
    k_slabs = [(col + c, functools.partial(qk_slab, c=c, gain_ref=gk_ref, out_ref=k_ref, scale=1.0))
               for c in range(0, d_qk, MXU_DIM)]
    col += d_qk
    v_slabs = [(col + c, functools.partial(v_slab, c=c)) for c in range(0, d_v, MXU_DIM)]
    col += d_v
    pool_slabs = [(col + c, functools.partial(pool_slab, c=c)) for c in range(0, d_pool, MXU_DIM)]
    col += d_pool
    mem_slabs = [(col + c, functools.partial(mem_slab, c=c)) for c in range(0, d_mem, MXU_DIM)]
    slabs = q_slabs + k_slabs + v_slabs + pool_slabs + mem_slabs

    def project(col0):
        return _dot(xn_ref[...], w_ref[:, col0:col0 + MXU_DIM])

    queue = [project(c0) for c0, _ in slabs[:PROJECT_AHEAD]]
    for n, (_, epilogue) in enumerate(slabs):
        if n + PROJECT_AHEAD < len(slabs):
            queue.append(project(slabs[n + PROJECT_AHEAD][0]))
        epilogue(queue.pop(0))


def _inproj(x2, g_mix, w_in, q_norm, k_norm, cos_tab, sin_tab, pool_w, pool_scale,
            km, vm, mq_norm, *, seq, tm):
    T, D = x2.shape
    d_qk = DA_HEADS * 2 * DA_HEAD_DIM
    d_v = DA_HEADS * DA_V_DIM
    d_pool = len(POOL_WINDOWS) * POOL_GROUP_DIM
    d_mem = MEM_HEADS * MEM_HEAD_DIM
    d_all = 2 * d_qk + d_v + d_pool + d_mem
    M = km.shape[1]
    tiles_per_seq = seq // tm
    reps = MXU_DIM // DA_HEAD_DIM
    gq = jnp.tile(q_norm, reps).reshape(1, MXU_DIM)
    gk = jnp.tile(k_norm, reps).reshape(1, MXU_DIM)
    grp = jnp.arange(MXU_DIM) // DA_HEAD_DIM
    ones = (grp[:, None] == grp[None, :]).astype(BF16)
    kern = functools.partial(_inproj_kernel, tm=tm, tiles_per_seq=tiles_per_seq,
                             d_qk=d_qk, d_v=d_v, d_pool=d_pool, d_mem=d_mem)
    row = lambda width: pl.BlockSpec((tm, width), lambda i: (i, 0))
    return pl.pallas_call(
        kern,
        grid=(T // tm,),
        in_specs=[row(D),
                  _const_spec((1, D)),
                  _const_spec((D, d_all)),
                  _const_spec((1, MXU_DIM)),
                  _const_spec((1, MXU_DIM)),
                  pl.BlockSpec((tm, LANES), lambda i: (i % tiles_per_seq, 0)),
                  pl.BlockSpec((tm, LANES), lambda i: (i % tiles_per_seq, 0)),
                  _const_spec((MXU_DIM, MXU_DIM)),
                  _const_spec((len(POOL_WINDOWS), POOL_GROUP_DIM, POOL_GROUP_DIM)),
                  _const_spec((1, d_pool)),
                  pl.BlockSpec((1, M, d_mem), lambda i: (i // tiles_per_seq, 0, 0)),
                  pl.BlockSpec((1, M, d_mem), lambda i: (i // tiles_per_seq, 0, 0)),
                  _const_spec((1, MEM_HEAD_DIM))],
        out_specs=[row(d_qk), row(d_qk), row(d_v), row(d_pool), row(d_mem)],
        out_shape=[jax.ShapeDtypeStruct((T, d_qk), BF16),
                   jax.ShapeDtypeStruct((T, d_qk), BF16),
                   jax.ShapeDtypeStruct((T, d_v), BF16),
                   jax.ShapeDtypeStruct((T, d_pool), BF16),
                   jax.ShapeDtypeStruct((T, d_mem), BF16)],
        scratch_shapes=[pltpu.VMEM((tm, D), BF16),
                        pltpu.VMEM((tm + POOL_HALO, d_pool), F32)],
        compiler_params=_cparams("arbitrary"),
        name="inproj",
    )(x2, g_mix.reshape(1, D), w_in[:, :d_all].astype(BF16), gq, gk, cos_tab, sin_tab, ones,
      pool_w.astype(BF16), pool_scale.reshape(1, d_pool), km, vm, mq_norm.reshape(1, MEM_HEAD_DIM))


ATTN_UNROLL = 8
ATTN_BOUND_MARGIN = 1.01
ATTN_BOUND_MAX = 30.0


def _attn_tables(n_blk):
    qi = [q for q in range(1, n_blk) for _ in range(q)]
    kj = [j for q in range(1, n_blk) for j in range(q)]
    return (np.asarray(qi + qi[-1:] * 2, np.int32), np.asarray(kj + kj[-1:] * 2, np.int32))


def _attn_kernel(tqi_ref, tkj_ref, lamp_ref, q_ref, k_ref, v_ref, gsub_ref, o_ref,
                 vt_ref, qc_ref, acc_ref, m_ref, l_ref, s0_ref, s1_ref, s2_ref, s3_ref, pa_ref, pb_ref,
                 *, seq, tq, n_full):
    lp = lamp_ref[...]
    lam = (jnp.exp(jnp.sum(lp[0:1] * lp[1:2], axis=-1, keepdims=True))
           - jnp.exp(jnp.sum(lp[2:3] * lp[3:4], axis=-1, keepdims=True)) + LAMBDA_INIT)
    n_blk = seq // tq
    sub = lax.broadcasted_iota(jnp.int32, (DA_V_DIM, tq), 0)
    key_i = lax.broadcasted_iota(jnp.int32, (tq, 2 * tq), 0)
    qry_i = lax.broadcasted_iota(jnp.int32, (tq, 2 * tq), 1) % tq
    causal = key_i <= qry_i
    s_bufs = (s0_ref, s1_ref, s2_ref, s3_ref)
    p_bufs = (pa_ref, pb_ref)

    def setup(t, carry):
        r0 = pl.multiple_of(t * tq, tq)
        vt_ref[t] = v_ref[pl.ds(r0, tq), :].astype(F32).T.astype(BF16)
        qt = q_ref[pl.ds(r0, tq), :].astype(F32).T
        qc_ref[t] = jnp.concatenate([jnp.where(sub < DA_HEAD_DIM, qt, 0.0),
                                     jnp.where(sub >= DA_HEAD_DIM, qt, 0.0)], axis=1).astype(BF16)
        m_ref[t] = jnp.full((1, 2 * tq), NEG_INF, F32)
        l_ref[t] = jnp.zeros((1, 2 * tq), F32)
        acc_ref[t] = jnp.zeros((DA_V_DIM, 2 * tq), F32)
        return carry

    lax.fori_loop(0, n_blk, setup, 0)

    def scores(qi, kj, dst_ref, masked):
        k0 = pl.multiple_of(kj * tq, tq)
        s = _dot(k_ref[pl.ds(k0, tq), :], qc_ref[qi])
        dst_ref[...] = jnp.where(causal, s, NEG_INF) if masked else s

    def softmax(qi, s_ref, p_ref):
        m_prev = m_ref[qi]
        s = s_ref[...]
        m_new = jnp.maximum(m_prev, jnp.max(s, axis=0, keepdims=True))
        alpha = jnp.exp2(m_prev - m_new)
        p = jnp.exp2(s - m_new)
        m_ref[qi] = m_new
        l_ref[qi] = alpha * l_ref[qi] + jnp.sum(p, axis=0, keepdims=True)
        p_ref[...] = p.astype(BF16)
        return alpha

    def values(qi, kj, p_ref, alpha):
        acc_ref[qi] = alpha * acc_ref[qi] + _dot(vt_ref[kj], p_ref[...])

    def finalize(qi):
        on = acc_ref[qi] / l_ref[qi]
        ot = on[:, :tq] - lam * on[:, tq:]
        ot = ot * lax.rsqrt(jnp.mean(ot * ot, axis=0, keepdims=True) + EPS)
        o = ot.T * (gsub_ref[...] * (1.0 - LAMBDA_INIT))
        o_ref[pl.ds(pl.multiple_of(qi * tq, tq), tq), :] = o.astype(BF16)

    ones_row = jnp.ones((1, 2 * tq), F32)

    def full_step(n, u, alpha_prev):
        scores(tqi_ref[n + 2], tkj_ref[n + 2], s_bufs[(u + 2) % 4], False)
        alpha = softmax(tqi_ref[n], s_bufs[u % 4], p_bufs[u % 2])
        prev = jnp.maximum(n - 1, 0)
        values(tqi_ref[prev], tkj_ref[prev], p_bufs[(u + 1) % 2], alpha_prev)
        return alpha

    def full_steps(t, alpha_prev):
        for u in range(ATTN_UNROLL):
            alpha_prev = full_step(ATTN_UNROLL * t + u, u, alpha_prev)
        return alpha_prev

    pb_ref[...] = jnp.zeros(pb_ref.shape, BF16)
    scores(tqi_ref[0], tkj_ref[0], s0_ref, False)
    scores(tqi_ref[1], tkj_ref[1], s1_ref, False)
    alpha_last = lax.fori_loop(0, n_full // ATTN_UNROLL, full_steps, ones_row)
    values(tqi_ref[n_full - 1], tkj_ref[n_full - 1], pb_ref, alpha_last)

    def diag_step(qi, u, alpha_prev):
        nxt = jnp.minimum(qi + 1, n_blk - 1)
        scores(nxt, nxt, s_bufs[1 - u], True)
        alpha = softmax(qi, s_bufs[u], p_bufs[u])
        prev = jnp.maximum(qi - 1, 0)
        values(prev, prev, p_bufs[1 - u], alpha_prev)
        finalize(prev)
        return alpha

    def diag_steps(t, alpha_prev):
        return diag_step(2 * t + 1, 1, diag_step(2 * t, 0, alpha_prev))

    pb_ref[...] = jnp.zeros(pb_ref.shape, BF16)
    scores(0, 0, s0_ref, True)
    alpha_last = lax.fori_loop(0, n_blk // 2, diag_steps, ones_row)
    values(n_blk - 1, n_blk - 1, pb_ref, alpha_last)
    finalize(n_blk - 1)


def _attn_bounded_kernel(tqi_ref, tkj_ref, bound_ref, lamp_ref, q_ref, k_ref, v_ref, gsub_ref, o_ref,
                         vt_ref, qc_ref, acc_ref, l_ref, pa_ref, pb_ref, *, seq, tq, n_full):
    lp = lamp_ref[...]
    lam = (jnp.exp(jnp.sum(lp[0:1] * lp[1:2], axis=-1, keepdims=True))
           - jnp.exp(jnp.sum(lp[2:3] * lp[3:4], axis=-1, keepdims=True)) + LAMBDA_INIT)
    n_blk = seq // tq
    bound = bound_ref[0]
    sub = lax.broadcasted_iota(jnp.int32, (DA_V_DIM, tq), 0)
    key_i = lax.broadcasted_iota(jnp.int32, (tq, 2 * tq), 0)
    qry_i = lax.broadcasted_iota(jnp.int32, (tq, 2 * tq), 1) % tq
    causal = key_i <= qry_i
    p_bufs = (pa_ref, pb_ref)

    def setup(t, carry):
        r0 = pl.multiple_of(t * tq, tq)
        vt_ref[t] = v_ref[pl.ds(r0, tq), :].astype(F32).T.astype(BF16)
        qt = q_ref[pl.ds(r0, tq), :].astype(F32).T
        qc_ref[t] = jnp.concatenate([jnp.where(sub < DA_HEAD_DIM, qt, 0.0),
                                     jnp.where(sub >= DA_HEAD_DIM, qt, 0.0)], axis=1).astype(BF16)
        l_ref[t] = jnp.zeros((1, 2 * tq), F32)
        acc_ref[t] = jnp.zeros((DA_V_DIM, 2 * tq), F32)
        return carry

    lax.fori_loop(0, n_blk, setup, 0)

    def probs(qi, kj, p_ref, masked):
        k0 = pl.multiple_of(kj * tq, tq)
        s = _dot(k_ref[pl.ds(k0, tq), :], qc_ref[qi])
        if masked:
            s = jnp.where(causal, s, NEG_INF)
        p = jnp.exp2(s - bound)
        l_ref[qi] = l_ref[qi] + jnp.sum(p, axis=0, keepdims=True)
        p_ref[...] = p.astype(BF16)

    def values(qi, kj, p_ref):
        acc_ref[qi] = acc_ref[qi] + _dot(vt_ref[kj], p_ref[...])

    def finalize(qi):
        on = acc_ref[qi] / l_ref[qi]
        ot = on[:, :tq] - lam * on[:, tq:]
        ot = ot * lax.rsqrt(jnp.mean(ot * ot, axis=0, keepdims=True) + EPS)
        o = ot.T * (gsub_ref[...] * (1.0 - LAMBDA_INIT))
        o_ref[pl.ds(pl.multiple_of(qi * tq, tq), tq), :] = o.astype(BF16)

    def full_steps(t, carry):
        for u in range(ATTN_UNROLL):
            n = ATTN_UNROLL * t + u
            probs(tqi_ref[n], tkj_ref[n], p_bufs[u % 2], False)
            prev = jnp.maximum(n - 1, 0)
            values(tqi_ref[prev], tkj_ref[prev], p_bufs[(u + 1) % 2])
        return carry

    pb_ref[...] = jnp.zeros(pb_ref.shape, BF16)
    lax.fori_loop(0, n_full // ATTN_UNROLL, full_steps, 0)
    values(tqi_ref[n_full - 1], tkj_ref[n_full - 1], pb_ref)

    def diag_steps(t, carry):
        for u in range(2):
            qi = 2 * t + u
            probs(qi, qi, p_bufs[u], True)
            prev = jnp.maximum(qi - 1, 0)
            values(prev, prev, p_bufs[1 - u])
            finalize(prev)
        return carry

    pb_ref[...] = jnp.zeros(pb_ref.shape, BF16)
    lax.fori_loop(0, n_blk // 2, diag_steps, 0)
    values(n_blk - 1, n_blk - 1, pb_ref)
    finalize(n_blk - 1)


def _attn_bounded(bound, q, k, v, lam_params, g_subln, *, batch, seq, tq):
    T, d_v = v.shape
    n_blk = seq // tq
    tqi, tkj = _attn_tables(n_blk)
    n_full = tqi.shape[0] - 2
    assert n_full % ATTN_UNROLL == 0 and n_blk % 2 == 0, (seq, tq)
    blk = lambda: pl.BlockSpec((seq, DA_V_DIM), lambda b, h, *_: (b, h))
    const = lambda shape: pl.BlockSpec(shape, lambda b, h, *_: (0,) * len(shape))
    return pl.pallas_call(
        functools.partial(_attn_bounded_kernel, seq=seq, tq=tq, n_full=n_full),
        grid_spec=pltpu.PrefetchScalarGridSpec(
            num_scalar_prefetch=3,
            grid=(batch, DA_HEADS),
            in_specs=[const((4, DA_HEAD_DIM)), blk(), blk(), blk(), const((1, DA_V_DIM))],
            out_specs=blk(),
            scratch_shapes=[pltpu.VMEM((n_blk, DA_V_DIM, tq), BF16),
                            pltpu.VMEM((n_blk, DA_V_DIM, 2 * tq), BF16),
                            pltpu.VMEM((n_blk, DA_V_DIM, 2 * tq), F32),
                            pltpu.VMEM((n_blk, 1, 2 * tq), F32)]
                           + [pltpu.VMEM((tq, 2 * tq), BF16)] * 2),
        out_shape=jax.ShapeDtypeStruct((T, d_v), BF16),
        compiler_params=_cparams("arbitrary", "arbitrary"),
        name="attn_bounded",
    )(jnp.asarray(tqi), jnp.asarray(tkj), bound.reshape(1), lam_params, q, k, v,
      g_subln.reshape(1, DA_V_DIM))


def _attn(q, k, v, lam_params, g_subln, *, batch, seq, tq):
    T, d_v = v.shape
    n_blk = seq // tq
    tqi, tkj = _attn_tables(n_blk)
    n_full = tqi.shape[0] - 2
    assert n_full % ATTN_UNROLL == 0 and n_blk % 2 == 0, (seq, tq)
    blk = lambda: pl.BlockSpec((seq, DA_V_DIM), lambda b, h, tqi, tkj: (b, h))
    const = lambda shape: pl.BlockSpec(shape, lambda b, h, tqi, tkj: (0,) * len(shape))
    return pl.pallas_call(
        functools.partial(_attn_kernel, seq=seq, tq=tq, n_full=n_full),
        grid_spec=pltpu.PrefetchScalarGridSpec(
            num_scalar_prefetch=2,
            grid=(batch, DA_HEADS),
            in_specs=[const((4, DA_HEAD_DIM)), blk(), blk(), blk(), const((1, DA_V_DIM))],
            out_specs=blk(),
            scratch_shapes=[pltpu.VMEM((n_blk, DA_V_DIM, tq), BF16),
                            pltpu.VMEM((n_blk, DA_V_DIM, 2 * tq), BF16),
                            pltpu.VMEM((n_blk, DA_V_DIM, 2 * tq), F32),
                            pltpu.VMEM((n_blk, 1, 2 * tq), F32),
                            pltpu.VMEM((n_blk, 1, 2 * tq), F32)]
                           + [pltpu.VMEM((tq, 2 * tq), F32)] * 4
                           + [pltpu.VMEM((tq, 2 * tq), BF16)] * 2),
        out_shape=jax.ShapeDtypeStruct((T, d_v), BF16),
        compiler_params=_cparams("arbitrary", "arbitrary"),
        name="attn",
    )(jnp.asarray(tqi), jnp.asarray(tkj), lam_params, q, k, v, g_subln.reshape(1, DA_V_DIM))


def _pack_bf16_pairs(a, b):
    hi = pltpu.bitcast(a.astype(BF16).astype(F32), jnp.uint32)
    lo = pltpu.bitcast(b.astype(BF16).astype(F32), jnp.uint32)
    return hi | (lo >> 16)


def _unpack_bf16_pairs(p):
    a = pltpu.bitcast(p & jnp.uint32(0xFFFF0000), F32)
    b = pltpu.bitcast(p << 16, F32)
    return a, b


def _merge_kernel(x_ref, oa_ref, pool_ref, memo_ref, gmix_ref, wg_ref, bg_ref, wba_ref, wbp_ref,
                  wbm_ref, wout_ref, gffn_ref, wrh_ref, wrl_ref, br_ref,
                  h_ref, hn_ref, idx_ref, rank_ref, gate_ref, cnt_ref,
                  xn_ref, mrg_ref, hs_ref, carry_ref, *, tm, d_model):
    i = pl.program_id(0)
    D = d_model

    @pl.when(i == 0)
    def _():
        carry_ref[...] = jnp.zeros(carry_ref.shape, F32)
        hs_ref[...] = jnp.zeros(hs_ref.shape, F32)

    def logits():
        hn = _rms_rows(hs_ref[...], gffn_ref[...])
        hn_ref[...] = _pack_bf16_pairs(hn[:, :D // 2], hn[:, D // 2:])
        hn_hi = hn.astype(BF16)
        hn_lo = (hn - hn_hi.astype(F32)).astype(BF16)
        return (_dot(hn_hi, wrh_ref[...]) + _dot(hn_lo, wrh_ref[...])
                + _dot(hn_hi, wrl_ref[...]) + br_ref[...])

    route = _route_pieces(logits, i > 0, idx_ref, rank_ref, gate_ref, cnt_ref, carry_ref, tm)

    x = x_ref[...]
    xn_ref[...] = _rms_rows(x, gmix_ref[...]).astype(BF16)
    branches = ((oa_ref, wba_ref), (pool_ref, wbp_ref), (memo_ref, wbm_ref))

    def slab_dots(j):
        sl = slice(j * MXU_DIM, (j + 1) * MXU_DIM)
        out = []
        for br, (y_ref, w_ref) in enumerate(branches):
            gsl = slice(br * D + j * MXU_DIM, br * D + (j + 1) * MXU_DIM)
            out.append((_dot(xn_ref[...], wg_ref[:, gsl]), _dot(y_ref[...], w_ref[:, sl])))
        return out

    def slab_epilogue(j, dots):
        merged = None
        for br, (logit, y) in enumerate(dots):
            gsl = slice(br * D + j * MXU_DIM, br * D + (j + 1) * MXU_DIM)
            term = y / (1.0 + jnp.exp(-(logit + bg_ref[:, gsl])))
            merged = term if merged is None else merged + term
        mrg_ref[:, j * MXU_DIM:(j + 1) * MXU_DIM] = merged.astype(BF16)

    n_slabs = D // MXU_DIM
    dots = slab_dots(0)
    for j in range(n_slabs):
        nxt = slab_dots(j + 1) if j + 1 < n_slabs else None
        route[j]()
        slab_epilogue(j, dots)
        dots = nxt
    for piece in route[n_slabs:-1]:
        piece()
    h = x + _dot(mrg_ref[...], wout_ref[...])
    route[-1]()
    h_ref[...] = h
    hs_ref[...] = h


def _route_pieces(logits_fn, live, idx_ref, rank_ref, gate_ref, cnt_ref, carry_ref, tm):
    lane = lax.broadcasted_iota(jnp.int32, (tm, LANES), 1)
    st = dict(vals=[], sels=[], idx_out=jnp.zeros((tm, LANES), jnp.int32))

    def start():
        st["work"] = logits_fn()

    def pick(kk):
        work = st["work"]
        mx = jnp.max(work, axis=-1, keepdims=True)
        idx = jnp.min(jnp.where(work == mx, lane, LANES), axis=-1, keepdims=True)
        sel = lane == idx
        st["vals"].append(mx)
        st["sels"].append(sel)
        st["idx_out"] = jnp.where(lane == kk, idx, st["idx_out"])
        st["work"] = jnp.where(sel, -jnp.inf, work)

    def finish():
        vals, sels = st["vals"], st["sels"]
        exps = [jnp.exp(v - vals[0]) for v in vals]
        denom = exps[0] + exps[1] + exps[2] + exps[3]
        gate_out = jnp.zeros((tm, LANES), F32)
        onehot = jnp.zeros((tm, LANES), F32)
        for kk in range(TOP_K):
            gate_out = jnp.where(lane == kk, exps[kk] / denom, gate_out)
            onehot = jnp.where(sels[kk], 1.0, onehot)
        r_i = lax.broadcasted_iota(jnp.int32, (tm, tm), 0)
        c_i = lax.broadcasted_iota(jnp.int32, (tm, tm), 1)
        lower = (c_i < r_i).astype(BF16)
        prefix = _dot(lower, onehot.astype(BF16)) + carry_ref[0:1, :]
        rank_out = jnp.zeros((tm, LANES), jnp.int32)
        for kk in range(TOP_K):
            rk = jnp.sum(jnp.where(sels[kk], prefix, 0.0), axis=-1, keepdims=True)
            rank_out = jnp.where(lane == kk, rk.astype(jnp.int32), rank_out)
        new_carry = prefix[tm - 1:tm, :] + onehot[tm - 1:tm, :]
        new_carry = jnp.where(live, new_carry, carry_ref[0:1, :])
        carry_ref[...] = jnp.broadcast_to(new_carry, carry_ref.shape)
        idx_ref[...] = st["idx_out"]
        rank_ref[...] = rank_out
        gate_ref[...] = gate_out
        cnt_ref[...] = jnp.broadcast_to(new_carry, cnt_ref.shape).astype(jnp.int32)

    return [start] + [functools.partial(pick, kk) for kk in range(TOP_K)] + [finish]


def _merge(x2, oa, pool, memo, g_mix, w_gates, b_gates, w_b_attn, w_b_pool, w_b_mem, w_out, g_ffn,
           w_router, b_router, *, tm):
    T, D = x2.shape
    d_pool = pool.shape[1]
    d_mem = memo.shape[1]
    E = w_router.shape[1]
    wr = jnp.zeros((D, LANES), F32).at[:, :E].set(w_router)
    wr_hi = wr.astype(BF16)
    wr_lo = (wr - wr_hi.astype(F32)).astype(BF16)
    br = jnp.full((1, LANES), NEG_INF, F32).at[0, :E].set(b_router)
    n_tiles = T // tm
    row = lambda width: pl.BlockSpec((tm, width), lambda i: (jnp.minimum(i, n_tiles - 1), 0))
    routed = lambda width: pl.BlockSpec((tm, width), lambda i: (jnp.maximum(i - 1, 0), 0))
    sublanes = 8
    return pl.pallas_call(
        functools.partial(_merge_kernel, tm=tm, d_model=D),
        grid=(n_tiles + 1,),
        in_specs=[row(D), row(oa.shape[1]), row(d_pool), row(d_mem),
                  _const_spec((1, D)),
                  _const_spec((D, 3 * D)), _const_spec((1, 3 * D)),
                  _const_spec((oa.shape[1], D)), _const_spec((d_pool, D)), _const_spec((d_mem, D)),
                  _const_spec((D, D)), _const_spec((1, D)),
                  _const_spec((D, LANES)), _const_spec((D, LANES)), _const_spec((1, LANES))],
        out_specs=[row(D), routed(D // 2), routed(LANES), routed(LANES), routed(LANES),
                   _const_spec((sublanes, LANES))],
        out_shape=[jax.ShapeDtypeStruct((T, D), F32),
                   jax.ShapeDtypeStruct((T, D // 2), jnp.uint32),
                   jax.ShapeDtypeStruct((T, LANES), jnp.int32),
                   jax.ShapeDtypeStruct((T, LANES), jnp.int32),
                   jax.ShapeDtypeStruct((T, LANES), F32),
                   jax.ShapeDtypeStruct((sublanes, LANES), jnp.int32)],
        scratch_shapes=[pltpu.VMEM((tm, D), BF16),
                        pltpu.VMEM((tm, D), BF16),
                        pltpu.VMEM((tm, D), F32),
                        pltpu.VMEM((sublanes, LANES), F32)],
        compiler_params=_cparams("arbitrary"),
        name="merge",
    )(x2, oa, pool, memo, g_mix.reshape(1, D), w_gates.astype(BF16), b_gates.reshape(1, 3 * D),
      w_b_attn.astype(BF16), w_b_pool.astype(BF16), w_b_mem.astype(BF16), w_out.astype(BF16),
      g_ffn.reshape(1, D), wr_hi, wr_lo, br)


def _experts_kernel(be_ref, nv_ref, src_ref, x_ref, wg_ref, bg_ref, wu_ref, bu_ref, wd_ref, bd_ref,
                    y_ref, act_ref, *, bm, d_model, d_expert):
    i = pl.program_id(0)
    half = d_model // 2

    @pl.when(nv_ref[i] == 0)
    def _():
        y_ref[...] = jnp.zeros(y_ref.shape, y_ref.dtype)

    @pl.when(nv_ref[i] > 0)
    def _():
        a, b = _unpack_bf16_pairs(x_ref[...])
        valid = lax.broadcasted_iota(jnp.int32, (bm, half), 0) < nv_ref[i]
        a = jnp.where(valid, a, 0.0).astype(BF16)
        b = jnp.where(valid, b, 0.0).astype(BF16)
        for j in range(d_expert // MXU_DIM):
            sl = slice(j * MXU_DIM, (j + 1) * MXU_DIM)
            g = _dot(a, wg_ref[0, :half, sl]) + _dot(b, wg_ref[0, half:, sl]) + bg_ref[0, :, sl]
            u = _dot(a, wu_ref[0, :half, sl]) + _dot(b, wu_ref[0, half:, sl]) + bu_ref[0, :, sl]
            g = jnp.minimum(g, SWIGLU_LIMIT)
            u = jnp.clip(u, -SWIGLU_LIMIT, SWIGLU_LIMIT)
            act = g * (1.0 / (1.0 + jnp.exp(-SWIGLU_ALPHA * g))) * (u + 1.0)
            act_ref[:, sl] = act.astype(BF16)
        out = _dot(act_ref[...], wd_ref[0]) + bd_ref[0]
        y_ref[...] = _pack_bf16_pairs(out[:, :half], out[:, half:])


def _experts(blk_e, blk_valid, blk_src, xs, w_gate, b_gate, w_up, b_up, w_down, b_down, *, bm):
    rows_sorted, half = xs.shape
    E, D, De = w_gate.shape
    n_blocks = rows_sorted // bm
    wspec = lambda r, c: pl.BlockSpec((1, r, c), lambda i, be, nv, src: (be[i], 0, 0))
    xspec = pl.BlockSpec((bm, half), lambda i, be, nv, src: (src[i], 0))
    return pl.pallas_call(
        functools.partial(_experts_kernel, bm=bm, d_model=D, d_expert=De),
        grid_spec=pltpu.PrefetchScalarGridSpec(
            num_scalar_prefetch=3,
            grid=(n_blocks,),
            in_specs=[xspec, wspec(D, De), wspec(1, De), wspec(D, De), wspec(1, De),
                      wspec(De, D), wspec(1, D)],
            out_specs=pl.BlockSpec((bm, half), lambda i, be, nv, src: (i, 0)),
            scratch_shapes=[pltpu.VMEM((bm, De), BF16)]),
        out_shape=jax.ShapeDtypeStruct((rows_sorted, half), jnp.uint32),
        compiler_params=_cparams("arbitrary"),
        name="experts",
    )(blk_e, blk_valid, blk_src, xs, w_gate.astype(BF16), b_gate.reshape(E, 1, De),
      w_up.astype(BF16), b_up.reshape(E, 1, De), w_down.astype(BF16), b_down.reshape(E, 1, D))


SC_CORES = 2
SC_SUBCORES = 16
SC_CHUNK = 128


def _sc_mesh():
    return plsc.VectorSubcoreMesh(core_axis_name="c", subcore_axis_name="s",
                                  num_cores=SC_CORES, num_subcores=SC_SUBCORES)


def _sc_scatter_rows(rows, dest, *, rows_out):
    T, width = rows.shape
    n_chunks = T // SC_CHUNK
    per_worker = n_chunks // (SC_CORES * SC_SUBCORES)

    @functools.partial(
        pl.kernel, mesh=_sc_mesh(),
        out_type=jax.ShapeDtypeStruct((rows_out, width), rows.dtype),
        scratch_types=[pltpu.VMEM((TOP_K, SC_CHUNK), jnp.int32),
                       pltpu.VMEM((SC_CHUNK, width), rows.dtype)],
        name="sc_dispatch")
    def scatter(rows_hbm, dest_hbm, out_hbm, idx_v, rows_v):
        worker = lax.axis_index("s") * SC_CORES + lax.axis_index("c")

        @pl.loop(0, per_worker)
        def _(n):
            c = worker * per_worker + n
            pltpu.sync_copy(rows_hbm.at[pl.ds(c * SC_CHUNK, SC_CHUNK)], rows_v)
            pltpu.sync_copy(dest_hbm.at[c], idx_v)
            for kk in range(TOP_K):
                pltpu.sync_copy(rows_v, out_hbm.at[idx_v.at[kk]])

    return scatter(rows, dest)


def _sc_gather_rows(table, src, *, n_tokens):
    _, width = table.shape
    n_chunks = n_tokens // SC_CHUNK
    per_worker = n_chunks // (SC_CORES * SC_SUBCORES)

    @functools.partial(
        pl.kernel, mesh=_sc_mesh(),
        out_type=jax.ShapeDtypeStruct((TOP_K, n_tokens, width), table.dtype),
        scratch_types=[pltpu.VMEM((TOP_K, SC_CHUNK), jnp.int32),
                       pltpu.VMEM((SC_CHUNK, width), table.dtype)],
        name="sc_combine")
    def gather(table_hbm, src_hbm, out_hbm, idx_v, rows_v):
        worker = lax.axis_index("s") * SC_CORES + lax.axis_index("c")

        @pl.loop(0, per_worker)
        def _(n):
            c = worker * per_worker + n
            pltpu.sync_copy(src_hbm.at[c], idx_v)
            for kk in range(TOP_K):
                pltpu.sync_copy(table_hbm.at[idx_v.at[kk]], rows_v)
                pltpu.sync_copy(rows_v, out_hbm.at[kk, pl.ds(c * SC_CHUNK, SC_CHUNK)])

    return gather(table, src)


def _weighted_sum_kernel(gate_ref, h_ref, y_ref, *rest, d_model):
    o_ref = rest[-1]
    half = d_model // 2
    h = h_ref[...]
    lo, hi = h[:, :half], h[:, half:]
    gates = gate_ref[...]
    for kk in range(TOP_K):
        a, b = _unpack_bf16_pairs(y_ref[kk])
        w = gates[:, kk:kk + 1]
        lo = lo + w * a
        hi = hi + w * b
    o_ref[:, :half] = lo
    o_ref[:, half:] = hi


def _weighted_sum(gate, h, y4, out_prev, *, first_tile, tm):
    T, D = h.shape
    n_tiles = y4.shape[1] // tm
    row = lambda width: pl.BlockSpec((tm, width), lambda i: (first_tile + i, 0))
    in_specs = [row(LANES), row(D), pl.BlockSpec((TOP_K, tm, D // 2), lambda i: (0, i, 0))]
    args = [gate, h, y4]
    aliases = {}
    if out_prev is not None:
        in_specs.append(pl.BlockSpec(memory_space=pl.ANY))
        args.append(out_prev)
        aliases = {3: 0}
    return pl.pallas_call(
        functools.partial(_weighted_sum_kernel, d_model=D),
        grid=(n_tiles,),
        in_specs=in_specs,
        out_specs=row(D),
        out_shape=jax.ShapeDtypeStruct((T, D), F32),
        input_output_aliases=aliases,
        compiler_params=_cparams("arbitrary"),
        name="weighted_sum",
    )(*args)


def _block_tables(counts, *, bm, n_blocks):
    padded = ((counts + bm - 1) // bm) * bm
    pend = jnp.cumsum(padded)
    poff = pend - padded
    n_used = pend[-1] // bm
    src = jnp.minimum(jnp.arange(n_blocks, dtype=jnp.int32), n_used - 1)
    blk_e = jnp.sum(pend[None, :] <= (src * bm)[:, None], axis=1)
    blk_e = jnp.minimum(blk_e, N_EXPERTS - 1).astype(jnp.int32)
    blk_valid = jnp.clip(poff[blk_e] + counts[blk_e] - src * bm, 0, bm)
    blk_valid = jnp.where(jnp.arange(n_blocks) < n_used, blk_valid, 0).astype(jnp.int32)
    return poff.astype(jnp.int32), blk_e, blk_valid, src.astype(jnp.int32)


def _rope_tables(seq):
    inv_freq = ROPE_THETA ** (-jnp.arange(0, DA_HEAD_DIM, 2, dtype=F32) / DA_HEAD_DIM)
    ang = jnp.arange(seq, dtype=F32)[:, None] * inv_freq[None, :]
    reps = LANES // (DA_HEAD_DIM // 2)
    cos = jnp.tile(jnp.cos(ang), (1, reps))
    sin = jnp.tile(jnp.sin(ang), (1, reps))
    first_half = (jnp.arange(LANES) % DA_HEAD_DIM) < DA_HEAD_DIM // 2
    return cos, jnp.where(first_half[None, :], -sin, sin)


ROW_TILE = 512
ATTN_TILE = 256
EXPERT_BLOCK = 512
COMBINE_PARTS = 4


def _forward(x, mem, g_mix, w_in, b_gates, q_norm, k_norm, lambda_q1, lambda_k1, lambda_q2, lambda_k2, g_subln, pool_w, pool_scale, g_mem, w_mem_kv, mq_norm, mk_norm, w_b_attn, w_b_pool, w_b_mem, w_out, g_ffn, w_router, b_router, w_gate, b_gate, w_up, b_up, w_down, b_down):
    B, S, D = x.shape
    T = B * S
    x2 = x.reshape(T, D)
    cos_tab, sin_tab = _rope_tables(S)
    km, vm = _memkv(mem, g_mem, w_mem_kv, mk_norm)
    q, k, v, pool, memo = _inproj(x2, g_mix, w_in, q_norm, k_norm, cos_tab, sin_tab, pool_w,
                                  pool_scale, km, vm, mq_norm, seq=S, tm=ROW_TILE)
    lam_params = jnp.stack([lambda_q1, lambda_k1, lambda_q2, lambda_k2])
    score_bound = (ATTN_BOUND_MARGIN * DA_HEAD_DIM * QUERY_SCALE
                   * jnp.max(jnp.abs(q_norm)) * jnp.max(jnp.abs(k_norm))).astype(F32)
    o = lax.cond(score_bound <= ATTN_BOUND_MAX,
                 lambda: _attn_bounded(score_bound, q, k, v, lam_params, g_subln, batch=B, seq=S, tq=ATTN_TILE),
                 lambda: _attn(q, k, v, lam_params, g_subln, batch=B, seq=S, tq=ATTN_TILE))
    d_in = q.shape[1] + k.shape[1] + v.shape[1] + pool.shape[1] + memo.shape[1]
    h1, hn, idx, rank, gate, cnt = _merge(x2, o, pool, memo, g_mix, w_in[:, d_in:], b_gates, w_b_attn,
                                          w_b_pool, w_b_mem, w_out, g_ffn, w_router, b_router,
                                          tm=ROW_TILE)
    n_blocks = (T * TOP_K) // EXPERT_BLOCK + N_EXPERTS
    poff, blk_e, blk_valid, blk_src = _block_tables(cnt[0, :N_EXPERTS], bm=EXPERT_BLOCK, n_blocks=n_blocks)
    idx_d = idx[:, :TOP_K].reshape(T * TOP_K // LANES, LANES)
    rank_d = rank[:, :TOP_K].reshape(T * TOP_K // LANES, LANES)
    dest = (poff[idx_d] + rank_d).astype(jnp.int32)
    dest = dest.reshape(T // SC_CHUNK, SC_CHUNK, TOP_K).transpose(0, 2, 1)
    xs = _sc_scatter_rows(hn, dest, rows_out=n_blocks * EXPERT_BLOCK)
    ys = _experts(blk_e, blk_valid, blk_src, xs, w_gate, b_gate, w_up, b_up, w_down, b_down,
                  bm=EXPERT_BLOCK)
    out = None
    chunks_per_part = dest.shape[0] // COMBINE_PARTS
    tokens_per_part = T // COMBINE_PARTS
    for part in range(COMBINE_PARTS):
        y4 = _sc_gather_rows(ys, dest[part * chunks_per_part:(part + 1) * chunks_per_part],
                             n_tokens=tokens_per_part)
        out = _weighted_sum(gate, h1, y4, out, first_tile=part * (tokens_per_part // ROW_TILE), tm=ROW_TILE)
    return dict(q=q, k=k, v=v, pool=pool, memo=memo, o=o, h1=h1, hn=hn, idx=idx, rank=rank,
                gate=gate, cnt=cnt, out=out.reshape(B, S, D))


def kernel(x, mem, g_mix, w_in, b_gates, q_norm, k_norm, lambda_q1, lambda_k1, lambda_q2, lambda_k2, g_subln, pool_w, pool_scale, g_mem, w_mem_kv, mq_norm, mk_norm, w_b_attn, w_b_pool, w_b_mem, w_out, g_ffn, w_router, b_router, w_gate, b_gate, w_up, b_up, w_down, b_down):
    return _forward(x, mem, g_mix, w_in, b_gates, q_norm, k_norm, lambda_q1, lambda_k1, lambda_q2,
                    lambda_k2, g_subln, pool_w, pool_scale, g_mem, w_mem_kv, mq_norm, mk_norm,
                    w_b_attn, w_b_pool, w_b_mem, w_out, g_ffn, w_router, b_router, w_gate, b_gate,
                    w_up, b_up, w_down, b_down)["out"]
```

```python
import functools
import math

import jax
import jax.numpy as jnp
import numpy as np
from jax import lax
from jax.experimental import pallas as pl
from jax.experimental.pallas import tpu as pltpu
from jax.experimental.pallas import tpu_sc as plsc

DA_HEADS = 8
DA_HEAD_DIM = 64
DA_V_DIM = 2 * DA_HEAD_DIM
ROPE_THETA = 10000.0
POOL_WINDOWS = (2, 4, 8, 16)
POOL_GROUP_DIM = 128
POOL_HALO = 16
PROJECT_AHEAD = 1
MEM_HEADS = 4
MEM_HEAD_DIM = 128
N_EXPERTS = 32
TOP_K = 4
SWIGLU_LIMIT = 7.0
SWIGLU_ALPHA = 1.702
LAMBDA_INIT = 0.8 - 0.6 * math.exp(-0.3 * 0.0)
EPS = 1e-6
NEG_INF = -1e30
QUERY_SCALE = math.log2(math.e) / math.sqrt(DA_HEAD_DIM)

LANES = 128
MXU_DIM = 256
VMEM_LIMIT_BYTES = 56 * 1024 * 1024

BF16 = jnp.bfloat16
F32 = jnp.float32


def _cparams(*sem):
    return pltpu.CompilerParams(dimension_semantics=sem, vmem_limit_bytes=VMEM_LIMIT_BYTES)


def _const_spec(shape):
    nd = len(shape)
    return pl.BlockSpec(shape, lambda *_: (0,) * nd)


def _dot(a, b):
    return jnp.dot(a, b, preferred_element_type=F32)


def _dot_nt(a, b):
    return lax.dot_general(a, b, (((1,), (1,)), ((), ())), preferred_element_type=F32)


def _rms_rows(x, gain):
    ms = jnp.mean(x * x, axis=-1, keepdims=True)
    return x * lax.rsqrt(ms + EPS) * gain


def _memkv_kernel(mem_ref, gmem_ref, w_ref, mkn_ref, km_ref, vm_ref):
    mem_dim = MEM_HEADS * MEM_HEAD_DIM
    mn = _rms_rows(mem_ref[0], gmem_ref[...]).astype(BF16)
    kv = _dot(mn, w_ref[...])
    for h in range(MEM_HEADS):
        sl = slice(h * MEM_HEAD_DIM, (h + 1) * MEM_HEAD_DIM)
        km_ref[0, :, sl] = _rms_rows(kv[:, sl], mkn_ref[...]).astype(BF16)
    vm_ref[0] = kv[:, mem_dim:].astype(BF16)


def _memkv(mem, g_mem, w_mem_kv, mk_norm):
    B, M, D = mem.shape
    mem_dim = MEM_HEADS * MEM_HEAD_DIM
    out = jax.ShapeDtypeStruct((B, M, mem_dim), BF16)
    return pl.pallas_call(
        _memkv_kernel,
        grid=(B,),
        in_specs=[pl.BlockSpec((1, M, D), lambda b: (b, 0, 0)),
                  _const_spec((1, D)),
                  _const_spec((D, 2 * mem_dim)),
                  _const_spec((1, MEM_HEAD_DIM))],
        out_specs=[pl.BlockSpec((1, M, mem_dim), lambda b: (b, 0, 0))] * 2,
        out_shape=[out, out],
        compiler_params=_cparams("arbitrary"),
        name="memkv",
    )(mem, g_mem.reshape(1, D), w_mem_kv.astype(BF16), mk_norm.reshape(1, MEM_HEAD_DIM))


def _swap32(x):
    lane = lax.broadcasted_iota(jnp.int32, x.shape, 1)
    low = (lane & 32) == 0
    return jnp.where(low, pltpu.roll(x, LANES - 32, 1), pltpu.roll(x, 32, 1))


def _inproj_kernel(x_ref, gmix_ref, w_ref, gq_ref, gk_ref, cos_ref, sin_ref, ones_ref,
                   poolw_ref, pscale_ref, km_ref, vm_ref, mqn_ref,
                   q_ref, k_ref, v_ref, pool_ref, memo_ref,
                   xn_ref, uext_ref, *, tm, tiles_per_seq, d_qk, d_v, d_pool, d_mem):
    i = pl.program_id(0)
    pos0 = (i % tiles_per_seq) * tm
    xn_ref[...] = _rms_rows(x_ref[...], gmix_ref[...]).astype(BF16)

    cos = cos_ref[...]
    sin = sin_ref[...]
    ones = ones_ref[...]

    def qk_slab(p, c, gain_ref, out_ref, scale):
        ss = _dot((p * p).astype(BF16), ones)
        n = p * (lax.rsqrt(ss * (1.0 / DA_HEAD_DIM) + EPS) * scale) * gain_ref[...]
        for half in range(MXU_DIM // LANES):
            nh = n[:, half * LANES:(half + 1) * LANES]
            r = nh * cos + _swap32(nh) * sin
            out_ref[:, c + half * LANES:c + (half + 1) * LANES] = r.astype(BF16)

    def v_slab(p, c):
        v_ref[:, c:c + MXU_DIM] = p.astype(BF16)

    def pool_slab(p, c):
        @pl.when(pos0 == 0)
        def _():
            uext_ref[0:POOL_HALO, c:c + MXU_DIM] = jnp.zeros((POOL_HALO, MXU_DIM), F32)

        uext_ref[POOL_HALO:POOL_HALO + tm, c:c + MXU_DIM] = p
        t1 = pos0 + 1 + lax.broadcasted_iota(jnp.int32, (tm, POOL_GROUP_DIM), 0)
        for g in range(c // POOL_GROUP_DIM, (c + MXU_DIM) // POOL_GROUP_DIM):
            w = POOL_WINDOWS[g]
            sl = slice(g * POOL_GROUP_DIM, (g + 1) * POOL_GROUP_DIM)
            u = uext_ref[POOL_HALO:POOL_HALO + tm, sl]
            acc = u
            for s in range(1, w):
                acc = acc + uext_ref[POOL_HALO - s:POOL_HALO - s + tm, sl]
            cnt = jnp.minimum(t1, w).astype(F32)
            z = acc / cnt - u
            zp = _dot(z.astype(BF16), poolw_ref[g])
            pool_ref[:, sl] = (zp * pscale_ref[:, sl]).astype(BF16)
        uext_ref[0:POOL_HALO, c:c + MXU_DIM] = uext_ref[tm:tm + POOL_HALO, c:c + MXU_DIM]

    def mem_slab(p, c):
        inv_sqrt = 1.0 / math.sqrt(MEM_HEAD_DIM)
        for half in range(MXU_DIM // LANES):
            sl = slice(c + half * LANES, c + (half + 1) * LANES)
            qn = (_rms_rows(p[:, half * LANES:(half + 1) * LANES], mqn_ref[...]) * inv_sqrt).astype(BF16)
            s = _dot_nt(qn, km_ref[0, :, sl])
            m = jnp.max(s, axis=-1, keepdims=True)
            e = jnp.exp(s - m)
            l = jnp.sum(e, axis=-1, keepdims=True)
            o = _dot(e.astype(BF16), vm_ref[0, :, sl])
            memo_ref[:, sl] = (o / l).astype(BF16)

    col = 0
    q_slabs = [(col + c, functools.partial(qk_slab, c=c, gain_ref=gq_ref, out_ref=q_ref, scale=QUERY_SCALE))
               for c in range(0, d_qk, MXU_DIM)]
    col += d_qk
    k_slabs = [(col + c, functools.partial(qk_slab, c=c, gain_ref=gk_ref, out_ref=k_ref, scale=1.0))
               for c in range(0, d_qk, MXU_DIM)]
    col += d_qk
    v_slabs = [(col + c, functools.partial(v_slab, c=c)) for c in range(0, d_v, MXU_DIM)]
    col += d_v
    pool_slabs = [(col + c, functools.partial(pool_slab, c=c)) for c in range(0, d_pool, MXU_DIM)]
    col += d_pool
    mem_slabs = [(col + c, functools.partial(mem_slab, c=c)) for c in range(0, d_mem, MXU_DIM)]
    slabs = q_slabs + k_slabs + v_slabs + pool_slabs + mem_slabs

    def project(col0):
        return _dot(xn_ref[...], w_ref[:, col0:col0 + MXU_DIM])

    queue = [project(c0) for c0, _ in slabs[:PROJECT_AHEAD]]
    for n, (_, epilogue) in enumerate(slabs):
        if n + PROJECT_AHEAD < len(slabs):
            queue.append(project(slabs[n + PROJECT_AHEAD][0]))
        epilogue(queue.pop(0))


def _inproj(x2, g_mix, w_in, q_norm, k_norm, cos_tab, sin_tab, pool_w, pool_scale,
            km, vm, mq_norm, *, seq, tm):
    T, D = x2.shape
    d_qk = DA_HEADS * 2 * DA_HEAD_DIM
    d_v = DA_HEADS * DA_V_DIM
    d_pool = len(POOL_WINDOWS) * POOL_GROUP_DIM
    d_mem = MEM_HEADS * MEM_HEAD_DIM
    d_all = 2 * d_qk + d_v + d_pool + d_mem
    M = km.shape[1]
    tiles_per_seq = seq // tm
    reps = MXU_DIM // DA_HEAD_DIM
    gq = jnp.tile(q_norm, reps).reshape(1, MXU_DIM)
    gk = jnp.tile(k_norm, reps).reshape(1, MXU_DIM)
    grp = jnp.arange(MXU_DIM) // DA_HEAD_DIM
    ones = (grp[:, None] == grp[None, :]).astype(BF16)
    kern = functools.partial(_inproj_kernel, tm=tm, tiles_per_seq=tiles_per_seq,
                             d_qk=d_qk, d_v=d_v, d_pool=d_pool, d_mem=d_mem)
    row = lambda width: pl.BlockSpec((tm, width), lambda i: (i, 0))
    return pl.pallas_call(
        kern,
        grid=(T // tm,),
        in_specs=[row(D),
                  _const_spec((1, D)),
                  _const_spec((D, d_all)),
                  _const_spec((1, MXU_DIM)),
                  _const_spec((1, MXU_DIM)),
                  pl.BlockSpec((tm, LANES), lambda i: (i % tiles_per_seq, 0)),
                  pl.BlockSpec((tm, LANES), lambda i: (i % tiles_per_seq, 0)),
                  _const_spec((MXU_DIM, MXU_DIM)),
                  _const_spec((len(POOL_WINDOWS), POOL_GROUP_DIM, POOL_GROUP_DIM)),
                  _const_spec((1, d_pool)),
                  pl.BlockSpec((1, M, d_mem), lambda i: (i // tiles_per_seq, 0, 0)),
                  pl.BlockSpec((1, M, d_mem), lambda i: (i // tiles_per_seq, 0, 0)),
                  _const_spec((1, MEM_HEAD_DIM))],
        out_specs=[row(d_qk), row(d_qk), row(d_v), row(d_pool), row(d_mem)],
        out_shape=[jax.ShapeDtypeStruct((T, d_qk), BF16),
                   jax.ShapeDtypeStruct((T, d_qk), BF16),
                   jax.ShapeDtypeStruct((T, d_v), BF16),
                   jax.ShapeDtypeStruct((T, d_pool), BF16),
                   jax.ShapeDtypeStruct((T, d_mem), BF16)],
        scratch_shapes=[pltpu.VMEM((tm, D), BF16),
                        pltpu.VMEM((tm + POOL_HALO, d_pool), F32)],
        compiler_params=_cparams("arbitrary"),
        name="inproj",
    )(x2, g_mix.reshape(1, D), w_in[:, :d_all].astype(BF16), gq, gk, cos_tab, sin_tab, ones,
      pool_w.astype(BF16), pool_scale.reshape(1, d_pool), km, vm, mq_norm.reshape(1, MEM_HEAD_DIM))


ATTN_UNROLL = 8
DIAG_UNROLL = 4
ATTN_BOUND_MARGIN = 1.01
ATTN_BOUND_MAX = 30.0


def _attn_tables(n_blk):
    qi = [q for q in range(1, n_blk) for _ in range(q)]
    kj = [j for q in range(1, n_blk) for j in range(q)]
    return (np.asarray(qi + qi[-1:] * 2, np.int32), np.asarray(kj + kj[-1:] * 2, np.int32))


def _attn_kernel(tqi_ref, tkj_ref, lamp_ref, q_ref, k_ref, v_ref, gsub_ref, o_ref,
                 vt_ref, qc_ref, acc_ref, m_ref, l_ref, s0_ref, s1_ref, s2_ref, s3_ref, pa_ref, pb_ref,
                 *, seq, tq, n_full):
    lp = lamp_ref[...]
    lam = (jnp.exp(jnp.sum(lp[0:1] * lp[1:2], axis=-1, keepdims=True))
           - jnp.exp(jnp.sum(lp[2:3] * lp[3:4], axis=-1, keepdims=True)) + LAMBDA_INIT)
    n_blk = seq // tq
    sub = lax.broadcasted_iota(jnp.int32, (DA_V_DIM, tq), 0)
    key_i = lax.broadcasted_iota(jnp.int32, (tq, 2 * tq), 0)
    qry_i = lax.broadcasted_iota(jnp.int32, (tq, 2 * tq), 1) % tq
    causal = key_i <= qry_i
    s_bufs = (s0_ref, s1_ref, s2_ref, s3_ref)
    p_bufs = (pa_ref, pb_ref)

    def setup(t, carry):
        r0 = pl.multiple_of(t * tq, tq)
        vt_ref[t] = v_ref[pl.ds(r0, tq), :].astype(F32).T.astype(BF16)
        qt = q_ref[pl.ds(r0, tq), :].astype(F32).T
        qc_ref[t] = jnp.concatenate([jnp.where(sub < DA_HEAD_DIM, qt, 0.0),
                                     jnp.where(sub >= DA_HEAD_DIM, qt, 0.0)], axis=1).astype(BF16)
        m_ref[t] = jnp.full((1, 2 * tq), NEG_INF, F32)
        l_ref[t] = jnp.zeros((1, 2 * tq), F32)
        acc_ref[t] = jnp.zeros((DA_V_DIM, 2 * tq), F32)
        return carry

    lax.fori_loop(0, n_blk, setup, 0)

    def scores(qi, kj, dst_ref, masked):
        k0 = pl.multiple_of(kj * tq, tq)
        s = _dot(k_ref[pl.ds(k0, tq), :], qc_ref[qi])
        dst_ref[...] = jnp.where(causal, s, NEG_INF) if masked else s

    def softmax(qi, s_ref, p_ref):
        m_prev = m_ref[qi]
        s = s_ref[...]
        m_new = jnp.maximum(m_prev, jnp.max(s, axis=0, keepdims=True))
        alpha = jnp.exp2(m_prev - m_new)
        p = jnp.exp2(s - m_new)
        m_ref[qi] = m_new
        l_ref[qi] = alpha * l_ref[qi] + jnp.sum(p, axis=0, keepdims=True)
        p_ref[...] = p.astype(BF16)
        return alpha

    def values(qi, kj, p_ref, alpha):
        acc_ref[qi] = alpha * acc_ref[qi] + _dot(vt_ref[kj], p_ref[...])

    def finalize(qi):
        on = acc_ref[qi] / l_ref[qi]
        ot = on[:, :tq] - lam * on[:, tq:]
        ot = ot * lax.rsqrt(jnp.mean(ot * ot, axis=0, keepdims=True) + EPS)
        o = ot.T * (gsub_ref[...] * (1.0 - LAMBDA_INIT))
        o_ref[pl.ds(pl.multiple_of(qi * tq, tq), tq), :] = o.astype(BF16)

    ones_row = jnp.ones((1, 2 * tq), F32)

    def full_step(n, u, alpha_prev):
        scores(tqi_ref[n + 2], tkj_ref[n + 2], s_bufs[(u + 2) % 4], False)
        alpha = softmax(tqi_ref[n], s_bufs[u % 4], p_bufs[u % 2])
        prev = jnp.maximum(n - 1, 0)
        values(tqi_ref[prev], tkj_ref[prev], p_bufs[(u + 1) % 2], alpha_prev)
        return alpha

    def full_steps(t, alpha_prev):
        for u in range(ATTN_UNROLL):
            alpha_prev = full_step(ATTN_UNROLL * t + u, u, alpha_prev)
        return alpha_prev

    pb_ref[...] = jnp.zeros(pb_ref.shape, BF16)
    scores(tqi_ref[0], tkj_ref[0], s0_ref, False)
    scores(tqi_ref[1], tkj_ref[1], s1_ref, False)
    alpha_last = lax.fori_loop(0, n_full // ATTN_UNROLL, full_steps, ones_row)
    values(tqi_ref[n_full - 1], tkj_ref[n_full - 1], pb_ref, alpha_last)

    def diag_step(qi, u, alpha_prev):
        nxt = jnp.minimum(qi + 1, n_blk - 1)
        scores(nxt, nxt, s_bufs[1 - u], True)
        alpha = softmax(qi, s_bufs[u], p_bufs[u])
        prev = jnp.maximum(qi - 1, 0)
        values(prev, prev, p_bufs[1 - u], alpha_prev)
        finalize(prev)
        return alpha

    def diag_steps(t, alpha_prev):
        return diag_step(2 * t + 1, 1, diag_step(2 * t, 0, alpha_prev))

    pb_ref[...] = jnp.zeros(pb_ref.shape, BF16)
    scores(0, 0, s0_ref, True)
    alpha_last = lax.fori_loop(0, n_blk // 2, diag_steps, ones_row)
    values(n_blk - 1, n_blk - 1, pb_ref, alpha_last)
    finalize(n_blk - 1)


def _attn_bounded_kernel(tqi_ref, tkj_ref, bound_ref, lamp_ref, q_ref, k_ref, v_ref, gsub_ref, o_ref,
                         vt_ref, qc_ref, acc_ref, l_ref, pa_ref, pb_ref, *, seq, tq, n_full):
    lp = lamp_ref[...]
    lam = (jnp.exp(jnp.sum(lp[0:1] * lp[1:2], axis=-1, keepdims=True))
           - jnp.exp(jnp.sum(lp[2:3] * lp[3:4], axis=-1, keepdims=True)) + LAMBDA_INIT)
    n_blk = seq // tq
    bound = bound_ref[0]
    sub = lax.broadcasted_iota(jnp.int32, (DA_V_DIM, tq), 0)
    key_i = lax.broadcasted_iota(jnp.int32, (tq, 2 * tq), 0)
    qry_i = lax.broadcasted_iota(jnp.int32, (tq, 2 * tq), 1) % tq
    causal = key_i <= qry_i
    p_bufs = (pa_ref, pb_ref)

    def setup(t, carry):
        r0 = pl.multiple_of(t * tq, tq)
        vt_ref[t] = v_ref[pl.ds(r0, tq), :].astype(F32).T.astype(BF16)
        qt = q_ref[pl.ds(r0, tq), :].astype(F32).T
        qc_ref[t] = jnp.concatenate([jnp.where(sub < DA_HEAD_DIM, qt, 0.0),
                                     jnp.where(sub >= DA_HEAD_DIM, qt, 0.0)], axis=1).astype(BF16)
        l_ref[t] = jnp.zeros((1, 2 * tq), F32)
        acc_ref[t] = jnp.zeros((DA_V_DIM, 2 * tq), F32)
        return carry

    lax.fori_loop(0, n_blk, setup, 0, unroll=4)

    def probs(qi, kj, p_ref, masked):
        k0 = pl.multiple_of(kj * tq, tq)
        s = _dot(k_ref[pl.ds(k0, tq), :], qc_ref[qi])
        if masked:
            s = jnp.where(causal, s, NEG_INF)
        p = jnp.exp2(s - bound)
        l_ref[qi] = l_ref[qi] + jnp.sum(p, axis=0, keepdims=True)
        p_ref[...] = p.astype(BF16)

    def values(qi, kj, p_ref):
        acc_ref[qi] = acc_ref[qi] + _dot(vt_ref[kj], p_ref[...])

    def finalize(qi):
        on = acc_ref[qi] / l_ref[qi]
        ot = on[:, :tq] - lam * on[:, tq:]
        ot = ot * lax.rsqrt(jnp.mean(ot * ot, axis=0, keepdims=True) + EPS)
        o = ot.T * (gsub_ref[...] * (1.0 - LAMBDA_INIT))
        o_ref[pl.ds(pl.multiple_of(qi * tq, tq), tq), :] = o.astype(BF16)

    def full_steps(t, carry):
        for u in range(ATTN_UNROLL):
            n = ATTN_UNROLL * t + u
            probs(tqi_ref[n], tkj_ref[n], p_bufs[u % 2], False)
            prev = jnp.maximum(n - 1, 0)
            values(tqi_ref[prev], tkj_ref[prev], p_bufs[(u + 1) % 2])
        return carry

    pb_ref[...] = jnp.zeros(pb_ref.shape, BF16)
    lax.fori_loop(0, n_full // ATTN_UNROLL, full_steps, 0)
    values(tqi_ref[n_full - 1], tkj_ref[n_full - 1], pb_ref)

    def diag_steps(t, carry):
        for u in range(DIAG_UNROLL):
            qi = DIAG_UNROLL * t + u
            probs(qi, qi, p_bufs[u % 2], True)
            prev = jnp.maximum(qi - 1, 0)
            values(prev, prev, p_bufs[(u + 1) % 2])
            finalize(prev)
        return carry

    pb_ref[...] = jnp.zeros(pb_ref.shape, BF16)
    lax.fori_loop(0, n_blk // DIAG_UNROLL, diag_steps, 0)
    values(n_blk - 1, n_blk - 1, pb_ref)
    finalize(n_blk - 1)


def _attn_bounded(bound, q, k, v, lam_params, g_subln, *, batch, seq, tq):
    T, d_v = v.shape
    n_blk = seq // tq
    tqi, tkj = _attn_tables(n_blk)
    n_full = tqi.shape[0] - 2
    assert n_full % ATTN_UNROLL == 0 and n_blk % DIAG_UNROLL == 0, (seq, tq)
    blk = lambda: pl.BlockSpec((seq, DA_V_DIM), lambda b, h, *_: (b, h))
    const = lambda shape: pl.BlockSpec(shape, lambda b, h, *_: (0,) * len(shape))
    return pl.pallas_call(
        functools.partial(_attn_bounded_kernel, seq=seq, tq=tq, n_full=n_full),
        grid_spec=pltpu.PrefetchScalarGridSpec(
            num_scalar_prefetch=3,
            grid=(batch, DA_HEADS),
            in_specs=[const((4, DA_HEAD_DIM)), blk(), blk(), blk(), const((1, DA_V_DIM))],
            out_specs=blk(),
            scratch_shapes=[pltpu.VMEM((n_blk, DA_V_DIM, tq), BF16),
                            pltpu.VMEM((n_blk, DA_V_DIM, 2 * tq), BF16),
                            pltpu.VMEM((n_blk, DA_V_DIM, 2 * tq), F32),
                            pltpu.VMEM((n_blk, 1, 2 * tq), F32)]
                           + [pltpu.VMEM((tq, 2 * tq), BF16)] * 2),
        out_shape=jax.ShapeDtypeStruct((T, d_v), BF16),
        compiler_params=_cparams("arbitrary", "arbitrary"),
        name="attn_bounded",
    )(jnp.asarray(tqi), jnp.asarray(tkj), bound.reshape(1), lam_params, q, k, v,
      g_subln.reshape(1, DA_V_DIM))


def _attn(q, k, v, lam_params, g_subln, *, batch, seq, tq):
    T, d_v = v.shape
    n_blk = seq // tq
    tqi, tkj = _attn_tables(n_blk)
    n_full = tqi.shape[0] - 2
    assert n_full % ATTN_UNROLL == 0 and n_blk % 2 == 0, (seq, tq)
    blk = lambda: pl.BlockSpec((seq, DA_V_DIM), lambda b, h, tqi, tkj: (b, h))
    const = lambda shape: pl.BlockSpec(shape, lambda b, h, tqi, tkj: (0,) * len(shape))
    return pl.pallas_call(
        functools.partial(_attn_kernel, seq=seq, tq=tq, n_full=n_full),
        grid_spec=pltpu.PrefetchScalarGridSpec(
            num_scalar_prefetch=2,
            grid=(batch, DA_HEADS),
            in_specs=[const((4, DA_HEAD_DIM)), blk(), blk(), blk(), const((1, DA_V_DIM))],
            out_specs=blk(),
            scratch_shapes=[pltpu.VMEM((n_blk, DA_V_DIM, tq), BF16),
                            pltpu.VMEM((n_blk, DA_V_DIM, 2 * tq), BF16),
                            pltpu.VMEM((n_blk, DA_V_DIM, 2 * tq), F32),
                            pltpu.VMEM((n_blk, 1, 2 * tq), F32),
                            pltpu.VMEM((n_blk, 1, 2 * tq), F32)]
                           + [pltpu.VMEM((tq, 2 * tq), F32)] * 4
                           + [pltpu.VMEM((tq, 2 * tq), BF16)] * 2),
        out_shape=jax.ShapeDtypeStruct((T, d_v), BF16),
        compiler_params=_cparams("arbitrary", "arbitrary"),
        name="attn",
    )(jnp.asarray(tqi), jnp.asarray(tkj), lam_params, q, k, v, g_subln.reshape(1, DA_V_DIM))


def _pack_bf16_pairs(a, b):
    hi = pltpu.bitcast(a.astype(BF16).astype(F32), jnp.uint32)
    lo = pltpu.bitcast(b.astype(BF16).astype(F32), jnp.uint32)
    return hi | (lo >> 16)


def _unpack_bf16_pairs(p):
    a = pltpu.bitcast(p & jnp.uint32(0xFFFF0000), F32)
    b = pltpu.bitcast(p << 16, F32)
    return a, b


def _merge_kernel(x_ref, oa_ref, pool_ref, memo_ref, gmix_ref, wg_ref, bg_ref, wba_ref, wbp_ref,
                  wbm_ref, wout_ref, gffn_ref, wrh_ref, wrl_ref, br_ref,
                  h_ref, hn_ref, idx_ref, rank_ref, gate_ref, cnt_ref,
                  xn_ref, mrg_ref, hs_ref, carry_ref, *, tm, d_model):
    i = pl.program_id(0)
    D = d_model

    @pl.when(i == 0)
    def _():
        carry_ref[...] = jnp.zeros(carry_ref.shape, F32)
        hs_ref[...] = jnp.zeros(hs_ref.shape, F32)

    def logits():
        hn = _rms_rows(hs_ref[...], gffn_ref[...])
        hn_ref[...] = _pack_bf16_pairs(hn[:, :D // 2], hn[:, D // 2:])
        hn_hi = hn.astype(BF16)
        hn_lo = (hn - hn_hi.astype(F32)).astype(BF16)
        return (_dot(hn_hi, wrh_ref[...]) + _dot(hn_lo, wrh_ref[...])
                + _dot(hn_hi, wrl_ref[...]) + br_ref[...])

    route = _route_pieces(logits, i > 0, idx_ref, rank_ref, gate_ref, cnt_ref, carry_ref, tm)

    x = x_ref[...]
    xn_ref[...] = _rms_rows(x, gmix_ref[...]).astype(BF16)
    branches = ((oa_ref, wba_ref), (pool_ref, wbp_ref), (memo_ref, wbm_ref))

    def slab_dots(j):
        sl = slice(j * MXU_DIM, (j + 1) * MXU_DIM)
        out = []
        for br, (y_ref, w_ref) in enumerate(branches):
            gsl = slice(br * D + j * MXU_DIM, br * D + (j + 1) * MXU_DIM)
            out.append((_dot(xn_ref[...], wg_ref[:, gsl]), _dot(y_ref[...], w_ref[:, sl])))
        return out

    def slab_epilogue(j, dots):
        merged = None
        for br, (logit, y) in enumerate(dots):
            gsl = slice(br * D + j * MXU_DIM, br * D + (j + 1) * MXU_DIM)
            term = y / (1.0 + jnp.exp(-(logit + bg_ref[:, gsl])))
            merged = term if merged is None else merged + term
        mrg_ref[:, j * MXU_DIM:(j + 1) * MXU_DIM] = merged.astype(BF16)

    n_slabs = D // MXU_DIM
    dots = slab_dots(0)
    for j in range(n_slabs):
        nxt = slab_dots(j + 1) if j + 1 < n_slabs else None
        route[j]()
        slab_epilogue(j, dots)
        dots = nxt
    for piece in route[n_slabs:-1]:
        piece()
    h = x + _dot(mrg_ref[...], wout_ref[...])
    route[-1]()
    h_ref[...] = h
    hs_ref[...] = h


def _route_pieces(logits_fn, live, idx_ref, rank_ref, gate_ref, cnt_ref, carry_ref, tm):
    lane = lax.broadcasted_iota(jnp.int32, (tm, LANES), 1)
    st = dict(vals=[], sels=[], idx_out=jnp.zeros((tm, LANES), jnp.int32))

    def start():
        st["work"] = logits_fn()

    def pick(kk):
        work = st["work"]
        mx = jnp.max(work, axis=-1, keepdims=True)
        idx = jnp.min(jnp.where(work == mx, lane, LANES), axis=-1, keepdims=True)
        sel = lane == idx
        st["vals"].append(mx)
        st["sels"].append(sel)
        st["idx_out"] = jnp.where(lane == kk, idx, st["idx_out"])
        st["work"] = jnp.where(sel, -jnp.inf, work)

    def finish():
        vals, sels = st["vals"], st["sels"]
        exps = [jnp.exp(v - vals[0]) for v in vals]
        denom = exps[0] + exps[1] + exps[2] + exps[3]
        gate_out = jnp.zeros((tm, LANES), F32)
        onehot = jnp.zeros((tm, LANES), F32)
        for kk in range(TOP_K):
            gate_out = jnp.where(lane == kk, exps[kk] / denom, gate_out)
            onehot = jnp.where(sels[kk], 1.0, onehot)
        r_i = lax.broadcasted_iota(jnp.int32, (tm, tm), 0)
        c_i = lax.broadcasted_iota(jnp.int32, (tm, tm), 1)
        lower = (c_i < r_i).astype(BF16)
        prefix = _dot(lower, onehot.astype(BF16)) + carry_ref[0:1, :]
        rank_out = jnp.zeros((tm, LANES), jnp.int32)
        for kk in range(TOP_K):
            rk = jnp.sum(jnp.where(sels[kk], prefix, 0.0), axis=-1, keepdims=True)
            rank_out = jnp.where(lane == kk, rk.astype(jnp.int32), rank_out)
        new_carry = prefix[tm - 1:tm, :] + onehot[tm - 1:tm, :]
        new_carry = jnp.where(live, new_carry, carry_ref[0:1, :])
        carry_ref[...] = jnp.broadcast_to(new_carry, carry_ref.shape)
        idx_ref[...] = st["idx_out"]
        rank_ref[...] = rank_out
        gate_ref[...] = gate_out
        cnt_ref[...] = jnp.broadcast_to(new_carry, cnt_ref.shape).astype(jnp.int32)

    return [start] + [functools.partial(pick, kk) for kk in range(TOP_K)] + [finish]


def _merge(x2, oa, pool, memo, g_mix, w_gates, b_gates, w_b_attn, w_b_pool, w_b_mem, w_out, g_ffn,
           w_router, b_router, *, tm):
    T, D = x2.shape
    d_pool = pool.shape[1]
    d_mem = memo.shape[1]
    E = w_router.shape[1]
    wr = jnp.zeros((D, LANES), F32).at[:, :E].set(w_router)
    wr_hi = wr.astype(BF16)
    wr_lo = (wr - wr_hi.astype(F32)).astype(BF16)
    br = jnp.full((1, LANES), NEG_INF, F32).at[0, :E].set(b_router)
    n_tiles = T // tm
    row = lambda width: pl.BlockSpec((tm, width), lambda i: (jnp.minimum(i, n_tiles - 1), 0))
    routed = lambda width: pl.BlockSpec((tm, width), lambda i: (jnp.maximum(i - 1, 0), 0))
    sublanes = 8
    return pl.pallas_call(
        functools.partial(_merge_kernel, tm=tm, d_model=D),
        grid=(n_tiles + 1,),
        in_specs=[row(D), row(oa.shape[1]), row(d_pool), row(d_mem),
                  _const_spec((1, D)),
                  _const_spec((D, 3 * D)), _const_spec((1, 3 * D)),
                  _const_spec((oa.shape[1], D)), _const_spec((d_pool, D)), _const_spec((d_mem, D)),
                  _const_spec((D, D)), _const_spec((1, D)),
                  _const_spec((D, LANES)), _const_spec((D, LANES)), _const_spec((1, LANES))],
        out_specs=[row(D), routed(D // 2), routed(LANES), routed(LANES), routed(LANES),
                   _const_spec((sublanes, LANES))],
        out_shape=[jax.ShapeDtypeStruct((T, D), F32),
                   jax.ShapeDtypeStruct((T, D // 2), jnp.uint32),
                   jax.ShapeDtypeStruct((T, LANES), jnp.int32),
                   jax.ShapeDtypeStruct((T, LANES), jnp.int32),
                   jax.ShapeDtypeStruct((T, LANES), F32),
                   jax.ShapeDtypeStruct((sublanes, LANES), jnp.int32)],
        scratch_shapes=[pltpu.VMEM((tm, D), BF16),
                        pltpu.VMEM((tm, D), BF16),
                        pltpu.VMEM((tm, D), F32),
                        pltpu.VMEM((sublanes, LANES), F32)],
        compiler_params=_cparams("arbitrary"),
        name="merge",
    )(x2, oa, pool, memo, g_mix.reshape(1, D), w_gates.astype(BF16), b_gates.reshape(1, 3 * D),
      w_b_attn.astype(BF16), w_b_pool.astype(BF16), w_b_mem.astype(BF16), w_out.astype(BF16),
      g_ffn.reshape(1, D), wr_hi, wr_lo, br)


def _experts_kernel(be_ref, nv_ref, src_ref, x_ref, wg_ref, bg_ref, wu_ref, bu_ref, wd_ref, bd_ref,
                    y_ref, act_ref, *, bm, d_model, d_expert):
    i = pl.program_id(0)
    half = d_model // 2

    @pl.when(nv_ref[i] == 0)
    def _():
        y_ref[...] = jnp.zeros(y_ref.shape, y_ref.dtype)

    @pl.when(nv_ref[i] > 0)
    def _():
        a, b = _unpack_bf16_pairs(x_ref[...])
        valid = lax.broadcasted_iota(jnp.int32, (bm, half), 0) < nv_ref[i]
        a = jnp.where(valid, a, 0.0).astype(BF16)
        b = jnp.where(valid, b, 0.0).astype(BF16)
        for j in range(d_expert // MXU_DIM):
            sl = slice(j * MXU_DIM, (j + 1) * MXU_DIM)
            g = _dot(a, wg_ref[0, :half, sl]) + _dot(b, wg_ref[0, half:, sl]) + bg_ref[0, :, sl]
            u = _dot(a, wu_ref[0, :half, sl]) + _dot(b, wu_ref[0, half:, sl]) + bu_ref[0, :, sl]
            g = jnp.minimum(g, SWIGLU_LIMIT)
            u = jnp.clip(u, -SWIGLU_LIMIT, SWIGLU_LIMIT)
            act = g * (1.0 / (1.0 + jnp.exp(-SWIGLU_ALPHA * g))) * (u + 1.0)
            act_ref[:, sl] = act.astype(BF16)
        out = _dot(act_ref[...], wd_ref[0]) + bd_ref[0]
        y_ref[...] = _pack_bf16_pairs(out[:, :half], out[:, half:])


def _experts(blk_e, blk_valid, blk_src, xs, w_gate, b_gate, w_up, b_up, w_down, b_down, *, bm):
    rows_sorted, half = xs.shape
    E, D, De = w_gate.shape
    n_blocks = rows_sorted // bm
    wspec = lambda r, c: pl.BlockSpec((1, r, c), lambda i, be, nv, src: (be[i], 0, 0))
    xspec = pl.BlockSpec((bm, half), lambda i, be, nv, src: (src[i], 0))
    return pl.pallas_call(
        functools.partial(_experts_kernel, bm=bm, d_model=D, d_expert=De),
        grid_spec=pltpu.PrefetchScalarGridSpec(
            num_scalar_prefetch=3,
            grid=(n_blocks,),
            in_specs=[xspec, wspec(D, De), wspec(1, De), wspec(D, De), wspec(1, De),
                      wspec(De, D), wspec(1, D)],
            out_specs=pl.BlockSpec((bm, half), lambda i, be, nv, src: (i, 0)),
            scratch_shapes=[pltpu.VMEM((bm, De), BF16)]),
        out_shape=jax.ShapeDtypeStruct((rows_sorted, half), jnp.uint32),
        compiler_params=_cparams("arbitrary"),
        name="experts",
    )(blk_e, blk_valid, blk_src, xs, w_gate.astype(BF16), b_gate.reshape(E, 1, De),
      w_up.astype(BF16), b_up.reshape(E, 1, De), w_down.astype(BF16), b_down.reshape(E, 1, D))


SC_CORES = 2
SC_SUBCORES = 16
SC_CHUNK = 128


def _sc_mesh():
    return plsc.VectorSubcoreMesh(core_axis_name="c", subcore_axis_name="s",
                                  num_cores=SC_CORES, num_subcores=SC_SUBCORES)


def _sc_scatter_rows(rows, dest, *, rows_out):
    T, width = rows.shape
    n_chunks = T // SC_CHUNK
    per_worker = n_chunks // (SC_CORES * SC_SUBCORES)

    @functools.partial(
        pl.kernel, mesh=_sc_mesh(),
        out_type=jax.ShapeDtypeStruct((rows_out, width), rows.dtype),
        scratch_types=[pltpu.VMEM((TOP_K, SC_CHUNK), jnp.int32),
                       pltpu.VMEM((SC_CHUNK, width), rows.dtype)],
        name="sc_dispatch")
    def scatter(rows_hbm, dest_hbm, out_hbm, idx_v, rows_v):
        worker = lax.axis_index("s") * SC_CORES + lax.axis_index("c")

        @pl.loop(0, per_worker)
        def _(n):
            c = worker * per_worker + n
            pltpu.sync_copy(rows_hbm.at[pl.ds(c * SC_CHUNK, SC_CHUNK)], rows_v)
            pltpu.sync_copy(dest_hbm.at[c], idx_v)
            for kk in range(TOP_K):
                pltpu.sync_copy(rows_v, out_hbm.at[idx_v.at[kk]])

    return scatter(rows, dest)


def _sc_gather_rows(table, src, *, n_tokens):
    _, width = table.shape
    n_chunks = n_tokens // SC_CHUNK
    per_worker = n_chunks // (SC_CORES * SC_SUBCORES)

    @functools.partial(
        pl.kernel, mesh=_sc_mesh(),
        out_type=jax.ShapeDtypeStruct((TOP_K, n_tokens, width), table.dtype),
        scratch_types=[pltpu.VMEM((TOP_K, SC_CHUNK), jnp.int32),
                       pltpu.VMEM((SC_CHUNK, width), table.dtype)],
        name="sc_combine")
    def gather(table_hbm, src_hbm, out_hbm, idx_v, rows_v):
        worker = lax.axis_index("s") * SC_CORES + lax.axis_index("c")

        @pl.loop(0, per_worker)
        def _(n):
            c = worker * per_worker + n
            pltpu.sync_copy(src_hbm.at[c], idx_v)
            for kk in range(TOP_K):
                pltpu.sync_copy(table_hbm.at[idx_v.at[kk]], rows_v)
                pltpu.sync_copy(rows_v, out_hbm.at[kk, pl.ds(c * SC_CHUNK, SC_CHUNK)])

    return gather(table, src)


def _weighted_sum_kernel(gate_ref, h_ref, y_ref, *rest, d_model):
    o_ref = rest[-1]
    half = d_model // 2
    h = h_ref[...]
    lo, hi = h[:, :half], h[:, half:]
    gates = gate_ref[...]
    for kk in range(TOP_K):
        a, b = _unpack_bf16_pairs(y_ref[kk])
        w = gates[:, kk:kk + 1]
        lo = lo + w * a
        hi = hi + w * b
    o_ref[:, :half] = lo
    o_ref[:, half:] = hi


def _weighted_sum(gate, h, y4, out_prev, *, first_tile, tm):
    T, D = h.shape
    n_tiles = y4.shape[1] // tm
    row = lambda width: pl.BlockSpec((tm, width), lambda i: (first_tile + i, 0))
    in_specs = [row(LANES), row(D), pl.BlockSpec((TOP_K, tm, D // 2), lambda i: (0, i, 0))]
    args = [gate, h, y4]
    aliases = {}
    if out_prev is not None:
        in_specs.append(pl.BlockSpec(memory_space=pl.ANY))
        args.append(out_prev)
        aliases = {3: 0}
    return pl.pallas_call(
        functools.partial(_weighted_sum_kernel, d_model=D),
        grid=(n_tiles,),
        in_specs=in_specs,
        out_specs=row(D),
        out_shape=jax.ShapeDtypeStruct((T, D), F32),
        input_output_aliases=aliases,
        compiler_params=_cparams("arbitrary"),
        name="weighted_sum",
    )(*args)


def _block_tables(counts, *, bm, n_blocks):
    padded = ((counts + bm - 1) // bm) * bm
    pend = jnp.cumsum(padded)
    poff = pend - padded
    n_used = pend[-1] // bm
    src = jnp.minimum(jnp.arange(n_blocks, dtype=jnp.int32), n_used - 1)
    blk_e = jnp.sum(pend[None, :] <= (src * bm)[:, None], axis=1)
    blk_e = jnp.minimum(blk_e, N_EXPERTS - 1).astype(jnp.int32)
    blk_valid = jnp.clip(poff[blk_e] + counts[blk_e] - src * bm, 0, bm)
    blk_valid = jnp.where(jnp.arange(n_blocks) < n_used, blk_valid, 0).astype(jnp.int32)
    return poff.astype(jnp.int32), blk_e, blk_valid, src.astype(jnp.int32)


def _rope_tables(seq):
    inv_freq = ROPE_THETA ** (-jnp.arange(0, DA_HEAD_DIM, 2, dtype=F32) / DA_HEAD_DIM)
    ang = jnp.arange(seq, dtype=F32)[:, None] * inv_freq[None, :]
    reps = LANES // (DA_HEAD_DIM // 2)
    cos = jnp.tile(jnp.cos(ang), (1, reps))
    sin = jnp.tile(jnp.sin(ang), (1, reps))
    first_half = (jnp.arange(LANES) % DA_HEAD_DIM) < DA_HEAD_DIM // 2
    return cos, jnp.where(first_half[None, :], -sin, sin)


ROW_TILE = 512
ATTN_TILE = 256
EXPERT_BLOCK = 512
COMBINE_PARTS = 4


def _forward(x, mem, g_mix, w_in, b_gates, q_norm, k_norm, lambda_q1, lambda_k1, lambda_q2, lambda_k2, g_subln, pool_w, pool_scale, g_mem, w_mem_kv, mq_norm, mk_norm, w_b_attn, w_b_pool, w_b_mem, w_out, g_ffn, w_router, b_router, w_gate, b_gate, w_up, b_up, w_down, b_down):
    B, S, D = x.shape
    T = B * S
    x2 = x.reshape(T, D)
    cos_tab, sin_tab = _rope_tables(S)
    km, vm = _memkv(mem, g_mem, w_mem_kv, mk_norm)
    q, k, v, pool, memo = _inproj(x2, g_mix, w_in, q_norm, k_norm, cos_tab, sin_tab, pool_w,
                                  pool_scale, km, vm, mq_norm, seq=S, tm=ROW_TILE)
    lam_params = jnp.stack([lambda_q1, lambda_k1, lambda_q2, lambda_k2])
    score_bound = (ATTN_BOUND_MARGIN * DA_HEAD_DIM * QUERY_SCALE
                   * jnp.max(jnp.abs(q_norm)) * jnp.max(jnp.abs(k_norm))).astype(F32)
    o = lax.cond(score_bound <= ATTN_BOUND_MAX,
                 lambda: _attn_bounded(score_bound, q, k, v, lam_params, g_subln, batch=B, seq=S, tq=ATTN_TILE),
                 lambda: _attn(q, k, v, lam_params, g_subln, batch=B, seq=S, tq=ATTN_TILE))
    d_in = q.shape[1] + k.shape[1] + v.shape[1] + pool.shape[1] + memo.shape[1]
    h1, hn, idx, rank, gate, cnt = _merge(x2, o, pool, memo, g_mix, w_in[:, d_in:], b_gates, w_b_attn,
                                          w_b_pool, w_b_mem, w_out, g_ffn, w_router, b_router,
                                          tm=ROW_TILE)
    n_blocks = (T * TOP_K) // EXPERT_BLOCK + N_EXPERTS
    poff, blk_e, blk_valid, blk_src = _block_tables(cnt[0, :N_EXPERTS], bm=EXPERT_BLOCK, n_blocks=n_blocks)
    idx_d = idx[:, :TOP_K].reshape(T * TOP_K // LANES, LANES)
    rank_d = rank[:, :TOP_K].reshape(T * TOP_K // LANES, LANES)
    dest = (poff[idx_d] + rank_d).astype(jnp.int32)
    dest = dest.reshape(T // SC_CHUNK, SC_CHUNK, TOP_K).transpose(0, 2, 1)
    xs = _sc_scatter_rows(hn, dest, rows_out=n_blocks * EXPERT_BLOCK)
    ys = _experts(blk_e, blk_valid, blk_src, xs, w_gate, b_gate, w_up, b_up, w_down, b_down,
                  bm=EXPERT_BLOCK)
    out = None
    chunks_per_part = dest.shape[0] // COMBINE_PARTS
    tokens_per_part = T // COMBINE_PARTS
    for part in range(COMBINE_PARTS):
        y4 = _sc_gather_rows(ys, dest[part * chunks_per_part:(part + 1) * chunks_per_part],
                             n_tokens=tokens_per_part)
        out = _weighted_sum(gate, h1, y4, out, first_tile=part * (tokens_per_part // ROW_TILE), tm=ROW_TILE)
    return dict(q=q, k=k, v=v, pool=pool, memo=memo, o=o, h1=h1, hn=hn, idx=idx, rank=rank,
                gate=gate, cnt=cnt, out=out.reshape(B, S, D))


def kernel(x, mem, g_mix, w_in, b_gates, q_norm, k_norm, lambda_q1, lambda_k1, lambda_q2, lambda_k2, g_subln, pool_w, pool_scale, g_mem, w_mem_kv, mq_norm, mk_norm, w_b_attn, w_b_pool, w_b_mem, w_out, g_ffn, w_router, b_router, w_gate, b_gate, w_up, b_up, w_down, b_down):
    return _forward(x, mem, g_mix, w_in, b_gates, q_norm, k_norm, lambda_q1, lambda_k1, lambda_q2,
                    lambda_k2, g_subln, pool_w, pool_scale, g_mem, w_mem_kv, mq_norm, mk_norm,
                    w_b_attn, w_b_pool, w_b_mem, w_out, g_ffn, w_router, b_router, w_gate, b_gate,
                    w_up, b_up, w_down, b_down)["out"]
```

```python
import functools
import math

import jax
import jax.numpy as jnp
import numpy as np
from jax import lax
from jax.experimental import pallas as pl
from jax.experimental.pallas import tpu as pltpu
from jax.experimental.pallas import tpu_sc as plsc

DA_HEADS = 8
DA_HEAD_DIM = 64
DA_V_DIM = 2 * DA_HEAD_DIM
ROPE_THETA = 10000.0
POOL_WINDOWS = (2, 4, 8, 16)
POOL_GROUP_DIM = 128
POOL_HALO = 16
PROJECT_AHEAD = 1
MEM_HEADS = 4
MEM_HEAD_DIM = 128
N_EXPERTS = 32
TOP_K = 4
SWIGLU_LIMIT = 7.0
SWIGLU_ALPHA = 1.702
LAMBDA_INIT = 0.8 - 0.6 * math.exp(-0.3 * 0.0)
EPS = 1e-6
NEG_INF = -1e30
QUERY_SCALE = math.log2(math.e) / math.sqrt(DA_HEAD_DIM)

LANES = 128
MXU_DIM = 256
VMEM_LIMIT_BYTES = 56 * 1024 * 1024

BF16 = jnp.bfloat16
F32 = jnp.float32


def _cparams(*sem):
    return pltpu.CompilerParams(dimension_semantics=sem, vmem_limit_bytes=VMEM_LIMIT_BYTES)


def _const_spec(shape):
    nd = len(shape)
    return pl.BlockSpec(shape, lambda *_: (0,) * nd)


def _dot(a, b):
    return jnp.dot(a, b, preferred_element_type=F32)


def _dot_nt(a, b):
    return lax.dot_general(a, b, (((1,), (1,)), ((), ())), preferred_element_type=F32)


def _rms_rows(x, gain):
    ms = jnp.mean(x * x, axis=-1, keepdims=True)
    return x * lax.rsqrt(ms + EPS) * gain


def _memkv_kernel(mem_ref, gmem_ref, w_ref, mkn_ref, km_ref, vm_ref):
    mem_dim = MEM_HEADS * MEM_HEAD_DIM
    mn = _rms_rows(mem_ref[0], gmem_ref[...]).astype(BF16)
    kv = _dot(mn, w_ref[...])
    for h in range(MEM_HEADS):
        sl = slice(h * MEM_HEAD_DIM, (h + 1) * MEM_HEAD_DIM)
        km_ref[0, :, sl] = _rms_rows(kv[:, sl], mkn_ref[...]).astype(BF16)
    vm_ref[0] = kv[:, mem_dim:].astype(BF16)


def _memkv(mem, g_mem, w_mem_kv, mk_norm):
    B, M, D = mem.shape
    mem_dim = MEM_HEADS * MEM_HEAD_DIM
    out = jax.ShapeDtypeStruct((B, M, mem_dim), BF16)
    return pl.pallas_call(
        _memkv_kernel,
        grid=(B,),
        in_specs=[pl.BlockSpec((1, M, D), lambda b: (b, 0, 0)),
                  _const_spec((1, D)),
                  _const_spec((D, 2 * mem_dim)),
                  _const_spec((1, MEM_HEAD_DIM))],
        out_specs=[pl.BlockSpec((1, M, mem_dim), lambda b: (b, 0, 0))] * 2,
        out_shape=[out, out],
        compiler_params=_cparams("arbitrary"),
        name="memkv",
    )(mem, g_mem.reshape(1, D), w_mem_kv.astype(BF16), mk_norm.reshape(1, MEM_HEAD_DIM))


def _swap32(x):
    lane = lax.broadcasted_iota(jnp.int32, x.shape, 1)
    low = (lane & 32) == 0
    return jnp.where(low, pltpu.roll(x, LANES - 32, 1), pltpu.roll(x, 32, 1))


def _inproj_kernel(x_ref, gmix_ref, w_ref, gq_ref, gk_ref, cos_ref, sin_ref, ones_ref,
                   poolw_ref, pscale_ref, km_ref, vm_ref, mqn_ref,
                   q_ref, k_ref, v_ref, pool_ref, memo_ref,
                   xn_ref, uext_ref, *, tm, tiles_per_seq, d_qk, d_v, d_pool, d_mem):
    i = pl.program_id(0)
    pos0 = (i % tiles_per_seq) * tm
    xn_ref[...] = _rms_rows(x_ref[...], gmix_ref[...]).astype(BF16)

    cos = cos_ref[...]
    sin = sin_ref[...]
    ones = ones_ref[...]

    def qk_slab(p, c, gain_ref, out_ref, scale):
        ss = _dot((p * p).astype(BF16), ones)
        n = p * (lax.rsqrt(ss * (1.0 / DA_HEAD_DIM) + EPS) * scale) * gain_ref[...]
        for half in range(MXU_DIM // LANES):
            nh = n[:, half * LANES:(half + 1) * LANES]
            r = nh * cos + _swap32(nh) * sin
            out_ref[:, c + half * LANES:c + (half + 1) * LANES] = r.astype(BF16)

    def v_slab(p, c):
        v_ref[:, c:c + MXU_DIM] = p.astype(BF16)

    def pool_slab(p, c):
        @pl.when(pos0 == 0)
        def _():
            uext_ref[0:POOL_HALO, c:c + MXU_DIM] = jnp.zeros((POOL_HALO, MXU_DIM), F32)

        uext_ref[POOL_HALO:POOL_HALO + tm, c:c + MXU_DIM] = p
        t1 = pos0 + 1 + lax.broadcasted_iota(jnp.int32, (tm, POOL_GROUP_DIM), 0)
        for g in range(c // POOL_GROUP_DIM, (c + MXU_DIM) // POOL_GROUP_DIM):
            w = POOL_WINDOWS[g]
            sl = slice(g * POOL_GROUP_DIM, (g + 1) * POOL_GROUP_DIM)
            u = uext_ref[POOL_HALO:POOL_HALO + tm, sl]
            acc = u
            for s in range(1, w):
                acc = acc + uext_ref[POOL_HALO - s:POOL_HALO - s + tm, sl]
            cnt = jnp.minimum(t1, w).astype(F32)
            z = acc / cnt - u
            zp = _dot(z.astype(BF16), poolw_ref[g])
            pool_ref[:, sl] = (zp * pscale_ref[:, sl]).astype(BF16)
        uext_ref[0:POOL_HALO, c:c + MXU_DIM] = uext_ref[tm:tm + POOL_HALO, c:c + MXU_DIM]

    def mem_slab(p, c):
        inv_sqrt = 1.0 / math.sqrt(MEM_HEAD_DIM)
        for half in range(MXU_DIM // LANES):
            sl = slice(c + half * LANES, c + (half + 1) * LANES)
            qn = (_rms_rows(p[:, half * LANES:(half + 1) * LANES], mqn_ref[...]) * inv_sqrt).astype(BF16)
            s = _dot_nt(qn, km_ref[0, :, sl])
            m = jnp.max(s, axis=-1, keepdims=True)
            e = jnp.exp(s - m)
            l = jnp.sum(e, axis=-1, keepdims=True)
            o = _dot(e.astype(BF16), vm_ref[0, :, sl])
            memo_ref[:, sl] = (o / l).astype(BF16)

    col = 0
    q_slabs = [(col + c, functools.partial(qk_slab, c=c, gain_ref=gq_ref, out_ref=q_ref, scale=QUERY_SCALE))
               for c in range(0, d_qk, MXU_DIM)]
    col += d_qk
    k_slabs = [(col + c, functools.partial(qk_slab, c=c, gain_ref=gk_ref, out_ref=k_ref, scale=1.0))
               for c in range(0, d_qk, MXU_DIM)]
    col += d_qk
    v_slabs = [(col + c, functools.partial(v_slab, c=c)) for c in range(0, d_v, MXU_DIM)]
    col += d_v
    pool_slabs = [(col + c, functools.partial(pool_slab, c=c)) for c in range(0, d_pool, MXU_DIM)]
    col += d_pool
    mem_slabs = [(col + c, functools.partial(mem_slab, c=c)) for c in range(0, d_mem, MXU_DIM)]
    slabs = q_slabs + k_slabs + v_slabs + pool_slabs + mem_slabs

    def project(col0):
        return _dot(xn_ref[...], w_ref[:, col0:col0 + MXU_DIM])

    queue = [project(c0) for c0, _ in slabs[:PROJECT_AHEAD]]
    for n, (_, epilogue) in enumerate(slabs):
        if n + PROJECT_AHEAD < len(slabs):
            queue.append(project(slabs[n + PROJECT_AHEAD][0]))
        epilogue(queue.pop(0))


def _inproj(x2, g_mix, w_in, q_norm, k_norm, cos_tab, sin_tab, pool_w, pool_scale,
            km, vm, mq_norm, *, seq, tm):
    T, D = x2.shape
    d_qk = DA_HEADS * 2 * DA_HEAD_DIM
    d_v = DA_HEADS * DA_V_DIM
    d_pool = len(POOL_WINDOWS) * POOL_GROUP_DIM
    d_mem = MEM_HEADS * MEM_HEAD_DIM
    d_all = 2 * d_qk + d_v + d_pool + d_mem
    M = km.shape[1]
    tiles_per_seq = seq // tm
    reps = MXU_DIM // DA_HEAD_DIM
    gq = jnp.tile(q_norm, reps).reshape(1, MXU_DIM)
    gk = jnp.tile(k_norm, reps).reshape(1, MXU_DIM)
    grp = jnp.arange(MXU_DIM) // DA_HEAD_DIM
    ones = (grp[:, None] == grp[None, :]).astype(BF16)
    kern = functools.partial(_inproj_kernel, tm=tm, tiles_per_seq=tiles_per_seq,
                             d_qk=d_qk, d_v=d_v, d_pool=d_pool, d_mem=d_mem)
    row = lambda width: pl.BlockSpec((tm, width), lambda i: (i, 0))
    return pl.pallas_call(
        kern,
        grid=(T // tm,),
        in_specs=[row(D),
                  _const_spec((1, D)),
                  _const_spec((D, d_all)),
                  _const_spec((1, MXU_DIM)),
                  _const_spec((1, MXU_DIM)),
                  pl.BlockSpec((tm, LANES), lambda i: (i % tiles_per_seq, 0)),
                  pl.BlockSpec((tm, LANES), lambda i: (i % tiles_per_seq, 0)),
                  _const_spec((MXU_DIM, MXU_DIM)),
                  _const_spec((len(POOL_WINDOWS), POOL_GROUP_DIM, POOL_GROUP_DIM)),
                  _const_spec((1, d_pool)),
                  pl.BlockSpec((1, M, d_mem), lambda i: (i // tiles_per_seq, 0, 0)),
                  pl.BlockSpec((1, M, d_mem), lambda i: (i // tiles_per_seq, 0, 0)),
                  _const_spec((1, MEM_HEAD_DIM))],
        out_specs=[row(d_qk), row(d_qk), row(d_v), row(d_pool), row(d_mem)],
        out_shape=[jax.ShapeDtypeStruct((T, d_qk), BF16),
                   jax.ShapeDtypeStruct((T, d_qk), BF16),
                   jax.ShapeDtypeStruct((T, d_v), BF16),
                   jax.ShapeDtypeStruct((T, d_pool), BF16),
                   jax.ShapeDtypeStruct((T, d_mem), BF16)],
        scratch_shapes=[pltpu.VMEM((tm, D), BF16),
                        pltpu.VMEM((tm + POOL_HALO, d_pool), F32)],
        compiler_params=_cparams("arbitrary"),
        name="inproj",
    )(x2, g_mix.reshape(1, D), w_in[:, :d_all].astype(BF16), gq, gk, cos_tab, sin_tab, ones,
      pool_w.astype(BF16), pool_scale.reshape(1, d_pool), km, vm, mq_norm.reshape(1, MEM_HEAD_DIM))


ATTN_UNROLL = 8
BOUNDED_UNROLL = 24
DIAG_UNROLL = 4
ATTN_BOUND_MARGIN = 1.01
ATTN_BOUND_MAX = 30.0


def _attn_tables(n_blk):
    qi = [q for q in range(1, n_blk) for _ in range(q)]
    kj = [j for q in range(1, n_blk) for j in range(q)]
    return (np.asarray(qi + qi[-1:] * 2, np.int32), np.asarray(kj + kj[-1:] * 2, np.int32))


def _attn_kernel(tqi_ref, tkj_ref, lamp_ref, q_ref, k_ref, v_ref, gsub_ref, o_ref,
                 vt_ref, qc_ref, acc_ref, m_ref, l_ref, s0_ref, s1_ref, s2_ref, s3_ref, pa_ref, pb_ref,
                 *, seq, tq, n_full):
    lp = lamp_ref[...]
    lam = (jnp.exp(jnp.sum(lp[0:1] * lp[1:2], axis=-1, keepdims=True))
           - jnp.exp(jnp.sum(lp[2:3] * lp[3:4], axis=-1, keepdims=True)) + LAMBDA_INIT)
    n_blk = seq // tq
    sub = lax.broadcasted_iota(jnp.int32, (DA_V_DIM, tq), 0)
    key_i = lax.broadcasted_iota(jnp.int32, (tq, 2 * tq), 0)
    qry_i = lax.broadcasted_iota(jnp.int32, (tq, 2 * tq), 1) % tq
    causal = key_i <= qry_i
    s_bufs = (s0_ref, s1_ref, s2_ref, s3_ref)
    p_bufs = (pa_ref, pb_ref)

    def setup(t, carry):
        r0 = pl.multiple_of(t * tq, tq)
        vt_ref[t] = v_ref[pl.ds(r0, tq), :].astype(F32).T.astype(BF16)
        qt = q_ref[pl.ds(r0, tq), :].astype(F32).T
        qc_ref[t] = jnp.concatenate([jnp.where(sub < DA_HEAD_DIM, qt, 0.0),
                                     jnp.where(sub >= DA_HEAD_DIM, qt, 0.0)], axis=1).astype(BF16)
        m_ref[t] = jnp.full((1, 2 * tq), NEG_INF, F32)
        l_ref[t] = jnp.zeros((1, 2 * tq), F32)
        acc_ref[t] = jnp.zeros((DA_V_DIM, 2 * tq), F32)
        return carry

    lax.fori_loop(0, n_blk, setup, 0)

    def scores(qi, kj, dst_ref, masked):
        k0 = pl.multiple_of(kj * tq, tq)
        s = _dot(k_ref[pl.ds(k0, tq), :], qc_ref[qi])
        dst_ref[...] = jnp.where(causal, s, NEG_INF) if masked else s

    def softmax(qi, s_ref, p_ref):
        m_prev = m_ref[qi]
        s = s_ref[...]
        m_new = jnp.maximum(m_prev, jnp.max(s, axis=0, keepdims=True))
        alpha = jnp.exp2(m_prev - m_new)
        p = jnp.exp2(s - m_new)
        m_ref[qi] = m_new
        l_ref[qi] = alpha * l_ref[qi] + jnp.sum(p, axis=0, keepdims=True)
        p_ref[...] = p.astype(BF16)
        return alpha

    def values(qi, kj, p_ref, alpha):
        acc_ref[qi] = alpha * acc_ref[qi] + _dot(vt_ref[kj], p_ref[...])

    def finalize(qi):
        on = acc_ref[qi] / l_ref[qi]
        ot = on[:, :tq] - lam * on[:, tq:]
        ot = ot * lax.rsqrt(jnp.mean(ot * ot, axis=0, keepdims=True) + EPS)
        o = ot.T * (gsub_ref[...] * (1.0 - LAMBDA_INIT))
        o_ref[pl.ds(pl.multiple_of(qi * tq, tq), tq), :] = o.astype(BF16)

    ones_row = jnp.ones((1, 2 * tq), F32)

    def full_step(n, u, alpha_prev):
        scores(tqi_ref[n + 2], tkj_ref[n + 2], s_bufs[(u + 2) % 4], False)
        alpha = softmax(tqi_ref[n], s_bufs[u % 4], p_bufs[u % 2])
        prev = jnp.maximum(n - 1, 0)
        values(tqi_ref[prev], tkj_ref[prev], p_bufs[(u + 1) % 2], alpha_prev)
        return alpha

    def full_steps(t, alpha_prev):
        for u in range(ATTN_UNROLL):
            alpha_prev = full_step(ATTN_UNROLL * t + u, u, alpha_prev)
        return alpha_prev

    pb_ref[...] = jnp.zeros(pb_ref.shape, BF16)
    scores(tqi_ref[0], tkj_ref[0], s0_ref, False)
    scores(tqi_ref[1], tkj_ref[1], s1_ref, False)
    alpha_last = lax.fori_loop(0, n_full // ATTN_UNROLL, full_steps, ones_row)
    values(tqi_ref[n_full - 1], tkj_ref[n_full - 1], pb_ref, alpha_last)

    def diag_step(qi, u, alpha_prev):
        nxt = jnp.minimum(qi + 1, n_blk - 1)
        scores(nxt, nxt, s_bufs[1 - u], True)
        alpha = softmax(qi, s_bufs[u], p_bufs[u])
        prev = jnp.maximum(qi - 1, 0)
        values(prev, prev, p_bufs[1 - u], alpha_prev)
        finalize(prev)
        return alpha

    def diag_steps(t, alpha_prev):
        return diag_step(2 * t + 1, 1, diag_step(2 * t, 0, alpha_prev))

    pb_ref[...] = jnp.zeros(pb_ref.shape, BF16)
    scores(0, 0, s0_ref, True)
    alpha_last = lax.fori_loop(0, n_blk // 2, diag_steps, ones_row)
    values(n_blk - 1, n_blk - 1, pb_ref, alpha_last)
    finalize(n_blk - 1)


def _attn_bounded_kernel(tqi_ref, tkj_ref, bound_ref, lamp_ref, q_ref, k_ref, v_ref, gsub_ref, o_ref,
                         vt_ref, qc_ref, acc_ref, l_ref, pa_ref, pb_ref, *, seq, tq, n_full):
    lp = lamp_ref[...]
    lam = (jnp.exp(jnp.sum(lp[0:1] * lp[1:2], axis=-1, keepdims=True))
           - jnp.exp(jnp.sum(lp[2:3] * lp[3:4], axis=-1, keepdims=True)) + LAMBDA_INIT)
    n_blk = seq // tq
    bound = bound_ref[0]
    sub = lax.broadcasted_iota(jnp.int32, (DA_V_DIM, tq), 0)
    key_i = lax.broadcasted_iota(jnp.int32, (tq, 2 * tq), 0)
    qry_i = lax.broadcasted_iota(jnp.int32, (tq, 2 * tq), 1) % tq
    causal = key_i <= qry_i
    p_bufs = (pa_ref, pb_ref)

    def setup(t, carry):
        r0 = pl.multiple_of(t * tq, tq)
        vt_ref[t] = v_ref[pl.ds(r0, tq), :].astype(F32).T.astype(BF16)
        qt = q_ref[pl.ds(r0, tq), :].astype(F32).T
        qc_ref[t] = jnp.concatenate([jnp.where(sub < DA_HEAD_DIM, qt, 0.0),
                                     jnp.where(sub >= DA_HEAD_DIM, qt, 0.0)], axis=1).astype(BF16)
        l_ref[t] = jnp.zeros((1, 2 * tq), F32)
        acc_ref[t] = jnp.zeros((DA_V_DIM, 2 * tq), F32)
        return carry

    lax.fori_loop(0, n_blk, setup, 0, unroll=4)

    def probs(qi, kj, p_ref, masked):
        k0 = pl.multiple_of(kj * tq, tq)
        s = _dot(k_ref[pl.ds(k0, tq), :], qc_ref[qi])
        if masked:
            s = jnp.where(causal, s, NEG_INF)
        p = jnp.exp2(s - bound)
        l_ref[qi] = l_ref[qi] + jnp.sum(p, axis=0, keepdims=True)
        p_ref[...] = p.astype(BF16)

    def values(qi, kj, p_ref):
        acc_ref[qi] = acc_ref[qi] + _dot(vt_ref[kj], p_ref[...])

    def finalize(qi):
        on = acc_ref[qi] / l_ref[qi]
        ot = on[:, :tq] - lam * on[:, tq:]
        ot = ot * lax.rsqrt(jnp.mean(ot * ot, axis=0, keepdims=True) + EPS)
        o = ot.T * (gsub_ref[...] * (1.0 - LAMBDA_INIT))
        o_ref[pl.ds(pl.multiple_of(qi * tq, tq), tq), :] = o.astype(BF16)

    def full_steps(t, carry):
        for u in range(BOUNDED_UNROLL):
            n = BOUNDED_UNROLL * t + u
            probs(tqi_ref[n], tkj_ref[n], p_bufs[u % 2], False)
            prev = jnp.maximum(n - 1, 0)
            values(tqi_ref[prev], tkj_ref[prev], p_bufs[(u + 1) % 2])
        return carry

    pb_ref[...] = jnp.zeros(pb_ref.shape, BF16)
    lax.fori_loop(0, n_full // BOUNDED_UNROLL, full_steps, 0)
    values(tqi_ref[n_full - 1], tkj_ref[n_full - 1], pb_ref)

    def diag_steps(t, carry):
        for u in range(DIAG_UNROLL):
            qi = DIAG_UNROLL * t + u
            probs(qi, qi, p_bufs[u % 2], True)
            prev = jnp.maximum(qi - 1, 0)
            values(prev, prev, p_bufs[(u + 1) % 2])
            finalize(prev)
        return carry

    pb_ref[...] = jnp.zeros(pb_ref.shape, BF16)
    lax.fori_loop(0, n_blk // DIAG_UNROLL, diag_steps, 0)
    values(n_blk - 1, n_blk - 1, pb_ref)
    finalize(n_blk - 1)


def _attn_bounded(bound, q, k, v, lam_params, g_subln, *, batch, seq, tq):
    T, d_v = v.shape
    n_blk = seq // tq
    tqi, tkj = _attn_tables(n_blk)
    n_full = tqi.shape[0] - 2
    assert n_full % BOUNDED_UNROLL == 0 and n_blk % DIAG_UNROLL == 0, (seq, tq)
    blk = lambda: pl.BlockSpec((seq, DA_V_DIM), lambda b, h, *_: (b, h))
    const = lambda shape: pl.BlockSpec(shape, lambda b, h, *_: (0,) * len(shape))
    return pl.pallas_call(
        functools.partial(_attn_bounded_kernel, seq=seq, tq=tq, n_full=n_full),
        grid_spec=pltpu.PrefetchScalarGridSpec(
            num_scalar_prefetch=3,
            grid=(batch, DA_HEADS),
            in_specs=[const((4, DA_HEAD_DIM)), blk(), blk(), blk(), const((1, DA_V_DIM))],
            out_specs=blk(),
            scratch_shapes=[pltpu.VMEM((n_blk, DA_V_DIM, tq), BF16),
                            pltpu.VMEM((n_blk, DA_V_DIM, 2 * tq), BF16),
                            pltpu.VMEM((n_blk, DA_V_DIM, 2 * tq), F32),
                            pltpu.VMEM((n_blk, 1, 2 * tq), F32)]
                           + [pltpu.VMEM((tq, 2 * tq), BF16)] * 2),
        out_shape=jax.ShapeDtypeStruct((T, d_v), BF16),
        compiler_params=_cparams("arbitrary", "arbitrary"),
        name="attn_bounded",
    )(jnp.asarray(tqi), jnp.asarray(tkj), bound.reshape(1), lam_params, q, k, v,
      g_subln.reshape(1, DA_V_DIM))


def _attn(q, k, v, lam_params, g_subln, *, batch, seq, tq):
    T, d_v = v.shape
    n_blk = seq // tq
    tqi, tkj = _attn_tables(n_blk)
    n_full = tqi.shape[0] - 2
    assert n_full % ATTN_UNROLL == 0 and n_blk % 2 == 0, (seq, tq)
    blk = lambda: pl.BlockSpec((seq, DA_V_DIM), lambda b, h, tqi, tkj: (b, h))
    const = lambda shape: pl.BlockSpec(shape, lambda b, h, tqi, tkj: (0,) * len(shape))
    return pl.pallas_call(
        functools.partial(_attn_kernel, seq=seq, tq=tq, n_full=n_full),
        grid_spec=pltpu.PrefetchScalarGridSpec(
            num_scalar_prefetch=2,
            grid=(batch, DA_HEADS),
            in_specs=[const((4, DA_HEAD_DIM)), blk(), blk(), blk(), const((1, DA_V_DIM))],
            out_specs=blk(),
            scratch_shapes=[pltpu.VMEM((n_blk, DA_V_DIM, tq), BF16),
                            pltpu.VMEM((n_blk, DA_V_DIM, 2 * tq), BF16),
                            pltpu.VMEM((n_blk, DA_V_DIM, 2 * tq), F32),
                            pltpu.VMEM((n_blk, 1, 2 * tq), F32),
                            pltpu.VMEM((n_blk, 1, 2 * tq), F32)]
                           + [pltpu.VMEM((tq, 2 * tq), F32)] * 4
                           + [pltpu.VMEM((tq, 2 * tq), BF16)] * 2),
        out_shape=jax.ShapeDtypeStruct((T, d_v), BF16),
        compiler_params=_cparams("arbitrary", "arbitrary"),
        name="attn",
    )(jnp.asarray(tqi), jnp.asarray(tkj), lam_params, q, k, v, g_subln.reshape(1, DA_V_DIM))


def _pack_bf16_pairs(a, b):
    hi = pltpu.bitcast(a.astype(BF16).astype(F32), jnp.uint32)
    lo = pltpu.bitcast(b.astype(BF16).astype(F32), jnp.uint32)
    return hi | (lo >> 16)


def _unpack_bf16_pairs(p):
    a = pltpu.bitcast(p & jnp.uint32(0xFFFF0000), F32)
    b = pltpu.bitcast(p << 16, F32)
    return a, b


def _merge_kernel(x_ref, oa_ref, pool_ref, memo_ref, gmix_ref, wg_ref, bg_ref, wba_ref, wbp_ref,
                  wbm_ref, wout_ref, gffn_ref, wrh_ref, wrl_ref, br_ref,
                  h_ref, hn_ref, idx_ref, rank_ref, gate_ref, cnt_ref,
                  xn_ref, mrg_ref, hs_ref, carry_ref, *, tm, d_model):
    i = pl.program_id(0)
    D = d_model

    @pl.when(i == 0)
    def _():
        carry_ref[...] = jnp.zeros(carry_ref.shape, F32)
        hs_ref[...] = jnp.zeros(hs_ref.shape, F32)

    def logits():
        hn = _rms_rows(hs_ref[...], gffn_ref[...])
        hn_ref[...] = _pack_bf16_pairs(hn[:, :D // 2], hn[:, D // 2:])
        hn_hi = hn.astype(BF16)
        hn_lo = (hn - hn_hi.astype(F32)).astype(BF16)
        return (_dot(hn_hi, wrh_ref[...]) + _dot(hn_lo, wrh_ref[...])
                + _dot(hn_hi, wrl_ref[...]) + br_ref[...])

    route = _route_pieces(logits, i > 0, idx_ref, rank_ref, gate_ref, cnt_ref, carry_ref, tm)

    x = x_ref[...]
    xn_ref[...] = _rms_rows(x, gmix_ref[...]).astype(BF16)
    branches = ((oa_ref, wba_ref), (pool_ref, wbp_ref), (memo_ref, wbm_ref))

    def slab_dots(j):
        sl = slice(j * MXU_DIM, (j + 1) * MXU_DIM)
        out = []
        for br, (y_ref, w_ref) in enumerate(branches):
            gsl = slice(br * D + j * MXU_DIM, br * D + (j + 1) * MXU_DIM)
            out.append((_dot(xn_ref[...], wg_ref[:, gsl]), _dot(y_ref[...], w_ref[:, sl])))
        return out

    def slab_epilogue(j, dots):
        merged = None
        for br, (logit, y) in enumerate(dots):
            gsl = slice(br * D + j * MXU_DIM, br * D + (j + 1) * MXU_DIM)
            term = y / (1.0 + jnp.exp(-(logit + bg_ref[:, gsl])))
            merged = term if merged is None else merged + term
        mrg_ref[:, j * MXU_DIM:(j + 1) * MXU_DIM] = merged.astype(BF16)

    n_slabs = D // MXU_DIM
    dots = slab_dots(0)
    for j in range(n_slabs):
        nxt = slab_dots(j + 1) if j + 1 < n_slabs else None
        route[j]()
        slab_epilogue(j, dots)
        dots = nxt
    for piece in route[n_slabs:-1]:
        piece()
    h = x + _dot(mrg_ref[...], wout_ref[...])
    route[-1]()
    h_ref[...] = h
    hs_ref[...] = h


def _route_pieces(logits_fn, live, idx_ref, rank_ref, gate_ref, cnt_ref, carry_ref, tm):
    lane = lax.broadcasted_iota(jnp.int32, (tm, LANES), 1)
    st = dict(vals=[], sels=[], idx_out=jnp.zeros((tm, LANES), jnp.int32))

    def start():
        st["work"] = logits_fn()

    def pick(kk):
        work = st["work"]
        mx = jnp.max(work, axis=-1, keepdims=True)
        idx = jnp.min(jnp.where(work == mx, lane, LANES), axis=-1, keepdims=True)
        sel = lane == idx
        st["vals"].append(mx)
        st["sels"].append(sel)
        st["idx_out"] = jnp.where(lane == kk, idx, st["idx_out"])
        st["work"] = jnp.where(sel, -jnp.inf, work)

    def finish():
        vals, sels = st["vals"], st["sels"]
        exps = [jnp.exp(v - vals[0]) for v in vals]
        denom = exps[0] + exps[1] + exps[2] + exps[3]
        gate_out = jnp.zeros((tm, LANES), F32)
        onehot = jnp.zeros((tm, LANES), F32)
        for kk in range(TOP_K):
            gate_out = jnp.where(lane == kk, exps[kk] / denom, gate_out)
            onehot = jnp.where(sels[kk], 1.0, onehot)
        r_i = lax.broadcasted_iota(jnp.int32, (tm, tm), 0)
        c_i = lax.broadcasted_iota(jnp.int32, (tm, tm), 1)
        lower = (c_i < r_i).astype(BF16)
        prefix = _dot(lower, onehot.astype(BF16)) + carry_ref[0:1, :]
        rank_out = jnp.zeros((tm, LANES), jnp.int32)
        for kk in range(TOP_K):
            rk = jnp.sum(jnp.where(sels[kk], prefix, 0.0), axis=-1, keepdims=True)
            rank_out = jnp.where(lane == kk, rk.astype(jnp.int32), rank_out)
        new_carry = prefix[tm - 1:tm, :] + onehot[tm - 1:tm, :]
        new_carry = jnp.where(live, new_carry, carry_ref[0:1, :])
        carry_ref[...] = jnp.broadcast_to(new_carry, carry_ref.shape)
        idx_ref[...] = st["idx_out"]
        rank_ref[...] = rank_out
        gate_ref[...] = gate_out
        cnt_ref[...] = jnp.broadcast_to(new_carry, cnt_ref.shape).astype(jnp.int32)

    return [start] + [functools.partial(pick, kk) for kk in range(TOP_K)] + [finish]


def _merge(x2, oa, pool, memo, g_mix, w_gates, b_gates, w_b_attn, w_b_pool, w_b_mem, w_out, g_ffn,
           w_router, b_router, *, tm):
    T, D = x2.shape
    d_pool = pool.shape[1]
    d_mem = memo.shape[1]
    E = w_router.shape[1]
    wr = jnp.zeros((D, LANES), F32).at[:, :E].set(w_router)
    wr_hi = wr.astype(BF16)
    wr_lo = (wr - wr_hi.astype(F32)).astype(BF16)
    br = jnp.full((1, LANES), NEG_INF, F32).at[0, :E].set(b_router)
    n_tiles = T // tm
    row = lambda width: pl.BlockSpec((tm, width), lambda i: (jnp.minimum(i, n_tiles - 1), 0))
    routed = lambda width: pl.BlockSpec((tm, width), lambda i: (jnp.maximum(i - 1, 0), 0))
    sublanes = 8
    return pl.pallas_call(
        functools.partial(_merge_kernel, tm=tm, d_model=D),
        grid=(n_tiles + 1,),
        in_specs=[row(D), row(oa.shape[1]), row(d_pool), row(d_mem),
                  _const_spec((1, D)),
                  _const_spec((D, 3 * D)), _const_spec((1, 3 * D)),
                  _const_spec((oa.shape[1], D)), _const_spec((d_pool, D)), _const_spec((d_mem, D)),
                  _const_spec((D, D)), _const_spec((1, D)),
                  _const_spec((D, LANES)), _const_spec((D, LANES)), _const_spec((1, LANES))],
        out_specs=[row(D), routed(D // 2), routed(LANES), routed(LANES), routed(LANES),
                   _const_spec((sublanes, LANES))],
        out_shape=[jax.ShapeDtypeStruct((T, D), F32),
                   jax.ShapeDtypeStruct((T, D // 2), jnp.uint32),
                   jax.ShapeDtypeStruct((T, LANES), jnp.int32),
                   jax.ShapeDtypeStruct((T, LANES), jnp.int32),
                   jax.ShapeDtypeStruct((T, LANES), F32),
                   jax.ShapeDtypeStruct((sublanes, LANES), jnp.int32)],
        scratch_shapes=[pltpu.VMEM((tm, D), BF16),
                        pltpu.VMEM((tm, D), BF16),
                        pltpu.VMEM((tm, D), F32),
                        pltpu.VMEM((sublanes, LANES), F32)],
        compiler_params=_cparams("arbitrary"),
        name="merge",
    )(x2, oa, pool, memo, g_mix.reshape(1, D), w_gates.astype(BF16), b_gates.reshape(1, 3 * D),
      w_b_attn.astype(BF16), w_b_pool.astype(BF16), w_b_mem.astype(BF16), w_out.astype(BF16),
      g_ffn.reshape(1, D), wr_hi, wr_lo, br)


def _experts_kernel(be_ref, nv_ref, src_ref, x_ref, wg_ref, bg_ref, wu_ref, bu_ref, wd_ref, bd_ref,
                    y_ref, act_ref, *, bm, d_model, d_expert):
    i = pl.program_id(0)
    half = d_model // 2

    @pl.when(nv_ref[i] == 0)
    def _():
        y_ref[...] = jnp.zeros(y_ref.shape, y_ref.dtype)

    @pl.when(nv_ref[i] > 0)
    def _():
        a, b = _unpack_bf16_pairs(x_ref[...])
        valid = lax.broadcasted_iota(jnp.int32, (bm, half), 0) < nv_ref[i]
        a = jnp.where(valid, a, 0.0).astype(BF16)
        b = jnp.where(valid, b, 0.0).astype(BF16)
        for j in range(d_expert // MXU_DIM):
            sl = slice(j * MXU_DIM, (j + 1) * MXU_DIM)
            g = _dot(a, wg_ref[0, :half, sl]) + _dot(b, wg_ref[0, half:, sl]) + bg_ref[0, :, sl]
            u = _dot(a, wu_ref[0, :half, sl]) + _dot(b, wu_ref[0, half:, sl]) + bu_ref[0, :, sl]
            g = jnp.minimum(g, SWIGLU_LIMIT)
            u = jnp.clip(u, -SWIGLU_LIMIT, SWIGLU_LIMIT)
            act = g * (1.0 / (1.0 + jnp.exp(-SWIGLU_ALPHA * g))) * (u + 1.0)
            act_ref[:, sl] = act.astype(BF16)
        out = _dot(act_ref[...], wd_ref[0]) + bd_ref[0]
        y_ref[...] = _pack_bf16_pairs(out[:, :half], out[:, half:])


def _experts(blk_e, blk_valid, blk_src, xs, w_gate, b_gate, w_up, b_up, w_down, b_down, *, bm):
    rows_sorted, half = xs.shape
    E, D, De = w_gate.shape
    n_blocks = rows_sorted // bm
    wspec = lambda r, c: pl.BlockSpec((1, r, c), lambda i, be, nv, src: (be[i], 0, 0))
    xspec = pl.BlockSpec((bm, half), lambda i, be, nv, src: (src[i], 0))
    return pl.pallas_call(
        functools.partial(_experts_kernel, bm=bm, d_model=D, d_expert=De),
        grid_spec=pltpu.PrefetchScalarGridSpec(
            num_scalar_prefetch=3,
            grid=(n_blocks,),
            in_specs=[xspec, wspec(D, De), wspec(1, De), wspec(D, De), wspec(1, De),
                      wspec(De, D), wspec(1, D)],
            out_specs=pl.BlockSpec((bm, half), lambda i, be, nv, src: (i, 0)),
            scratch_shapes=[pltpu.VMEM((bm, De), BF16)]),
        out_shape=jax.ShapeDtypeStruct((rows_sorted, half), jnp.uint32),
        compiler_params=_cparams("arbitrary"),
        name="experts",
    )(blk_e, blk_valid, blk_src, xs, w_gate.astype(BF16), b_gate.reshape(E, 1, De),
      w_up.astype(BF16), b_up.reshape(E, 1, De), w_down.astype(BF16), b_down.reshape(E, 1, D))


SC_CORES = 2
SC_SUBCORES = 16
SC_CHUNK = 128


def _sc_mesh():
    return plsc.VectorSubcoreMesh(core_axis_name="c", subcore_axis_name="s",
                                  num_cores=SC_CORES, num_subcores=SC_SUBCORES)


def _sc_scatter_rows(rows, dest, *, rows_out):
    T, width = rows.shape
    n_chunks = T // SC_CHUNK
    per_worker = n_chunks // (SC_CORES * SC_SUBCORES)

    @functools.partial(
        pl.kernel, mesh=_sc_mesh(),
        out_type=jax.ShapeDtypeStruct((rows_out, width), rows.dtype),
        scratch_types=[pltpu.VMEM((TOP_K, SC_CHUNK), jnp.int32),
                       pltpu.VMEM((SC_CHUNK, width), rows.dtype)],
        name="sc_dispatch")
    def scatter(rows_hbm, dest_hbm, out_hbm, idx_v, rows_v):
        worker = lax.axis_index("s") * SC_CORES + lax.axis_index("c")

        @pl.loop(0, per_worker)
        def _(n):
            c = worker * per_worker + n
            pltpu.sync_copy(rows_hbm.at[pl.ds(c * SC_CHUNK, SC_CHUNK)], rows_v)
            pltpu.sync_copy(dest_hbm.at[c], idx_v)
            for kk in range(TOP_K):
                pltpu.sync_copy(rows_v, out_hbm.at[idx_v.at[kk]])

    return scatter(rows, dest)


def _sc_gather_rows(table, src, *, n_tokens):
    _, width = table.shape
    n_chunks = n_tokens // SC_CHUNK
    per_worker = n_chunks // (SC_CORES * SC_SUBCORES)

    @functools.partial(
        pl.kernel, mesh=_sc_mesh(),
        out_type=jax.ShapeDtypeStruct((TOP_K, n_tokens, width), table.dtype),
        scratch_types=[pltpu.VMEM((TOP_K, SC_CHUNK), jnp.int32),
                       pltpu.VMEM((SC_CHUNK, width), table.dtype)],
        name="sc_combine")
    def gather(table_hbm, src_hbm, out_hbm, idx_v, rows_v):
        worker = lax.axis_index("s") * SC_CORES + lax.axis_index("c")

        @pl.loop(0, per_worker)
        def _(n):
            c = worker * per_worker + n
            pltpu.sync_copy(src_hbm.at[c], idx_v)
            for kk in range(TOP_K):
                pltpu.sync_copy(table_hbm.at[idx_v.at[kk]], rows_v)
                pltpu.sync_copy(rows_v, out_hbm.at[kk, pl.ds(c * SC_CHUNK, SC_CHUNK)])

    return gather(table, src)


def _weighted_sum_kernel(gate_ref, h_ref, y_ref, *rest, d_model):
    o_ref = rest[-1]
    half = d_model // 2
    h = h_ref[...]
    lo, hi = h[:, :half], h[:, half:]
    gates = gate_ref[...]
    for kk in range(TOP_K):
        a, b = _unpack_bf16_pairs(y_ref[kk])
        w = gates[:, kk:kk + 1]
        lo = lo + w * a
        hi = hi + w * b
    o_ref[:, :half] = lo
    o_ref[:, half:] = hi


def _weighted_sum(gate, h, y4, out_prev, *, first_tile, tm):
    T, D = h.shape
    n_tiles = y4.shape[1] // tm
    row = lambda width: pl.BlockSpec((tm, width), lambda i: (first_tile + i, 0))
    in_specs = [row(LANES), row(D), pl.BlockSpec((TOP_K, tm, D // 2), lambda i: (0, i, 0))]
    args = [gate, h, y4]
    aliases = {}
    if out_prev is not None:
        in_specs.append(pl.BlockSpec(memory_space=pl.ANY))
        args.append(out_prev)
        aliases = {3: 0}
    return pl.pallas_call(
        functools.partial(_weighted_sum_kernel, d_model=D),
        grid=(n_tiles,),
        in_specs=in_specs,
        out_specs=row(D),
        out_shape=jax.ShapeDtypeStruct((T, D), F32),
        input_output_aliases=aliases,
        compiler_params=_cparams("arbitrary"),
        name="weighted_sum",
    )(*args)


def _block_tables(counts, *, bm, n_blocks):
    padded = ((counts + bm - 1) // bm) * bm
    pend = jnp.cumsum(padded)
    poff = pend - padded
    n_used = pend[-1] // bm
    src = jnp.minimum(jnp.arange(n_blocks, dtype=jnp.int32), n_used - 1)
    blk_e = jnp.sum(pend[None, :] <= (src * bm)[:, None], axis=1)
    blk_e = jnp.minimum(blk_e, N_EXPERTS - 1).astype(jnp.int32)
    blk_valid = jnp.clip(poff[blk_e] + counts[blk_e] - src * bm, 0, bm)
    blk_valid = jnp.where(jnp.arange(n_blocks) < n_used, blk_valid, 0).astype(jnp.int32)
    return poff.astype(jnp.int32), blk_e, blk_valid, src.astype(jnp.int32)


def _rope_tables(seq):
    inv_freq = ROPE_THETA ** (-jnp.arange(0, DA_HEAD_DIM, 2, dtype=F32) / DA_HEAD_DIM)
    ang = jnp.arange(seq, dtype=F32)[:, None] * inv_freq[None, :]
    reps = LANES // (DA_HEAD_DIM // 2)
    cos = jnp.tile(jnp.cos(ang), (1, reps))
    sin = jnp.tile(jnp.sin(ang), (1, reps))
    first_half = (jnp.arange(LANES) % DA_HEAD_DIM) < DA_HEAD_DIM // 2
    return cos, jnp.where(first_half[None, :], -sin, sin)


ROW_TILE = 512
ATTN_TILE = 256
EXPERT_BLOCK = 512
COMBINE_PARTS = 4


def _forward(x, mem, g_mix, w_in, b_gates, q_norm, k_norm, lambda_q1, lambda_k1, lambda_q2, lambda_k2, g_subln, pool_w, pool_scale, g_mem, w_mem_kv, mq_norm, mk_norm, w_b_attn, w_b_pool, w_b_mem, w_out, g_ffn, w_router, b_router, w_gate, b_gate, w_up, b_up, w_down, b_down):
    B, S, D = x.shape
    T = B * S
    x2 = x.reshape(T, D)
    cos_tab, sin_tab = _rope_tables(S)
    km, vm = _memkv(mem, g_mem, w_mem_kv, mk_norm)
    q, k, v, pool, memo = _inproj(x2, g_mix, w_in, q_norm, k_norm, cos_tab, sin_tab, pool_w,
                                  pool_scale, km, vm, mq_norm, seq=S, tm=ROW_TILE)
    lam_params = jnp.stack([lambda_q1, lambda_k1, lambda_q2, lambda_k2])
    score_bound = (ATTN_BOUND_MARGIN * DA_HEAD_DIM * QUERY_SCALE
                   * jnp.max(jnp.abs(q_norm)) * jnp.max(jnp.abs(k_norm))).astype(F32)
    o = lax.cond(score_bound <= ATTN_BOUND_MAX,
                 lambda: _attn_bounded(score_bound, q, k, v, lam_params, g_subln, batch=B, seq=S, tq=ATTN_TILE),
                 lambda: _attn(q, k, v, lam_params, g_subln, batch=B, seq=S, tq=ATTN_TILE))
    d_in = q.shape[1] + k.shape[1] + v.shape[1] + pool.shape[1] + memo.shape[1]
    h1, hn, idx, rank, gate, cnt = _merge(x2, o, pool, memo, g_mix, w_in[:, d_in:], b_gates, w_b_attn,
                                          w_b_pool, w_b_mem, w_out, g_ffn, w_router, b_router,
                                          tm=ROW_TILE)
    n_blocks = (T * TOP_K) // EXPERT_BLOCK + N_EXPERTS
    poff, blk_e, blk_valid, blk_src = _block_tables(cnt[0, :N_EXPERTS], bm=EXPERT_BLOCK, n_blocks=n_blocks)
    idx_d = idx[:, :TOP_K].reshape(T * TOP_K // LANES, LANES)
    rank_d = rank[:, :TOP_K].reshape(T * TOP_K // LANES, LANES)
    dest = (poff[idx_d] + rank_d).astype(jnp.int32)
    dest = dest.reshape(T // SC_CHUNK, SC_CHUNK, TOP_K).transpose(0, 2, 1)
    xs = _sc_scatter_rows(hn, dest, rows_out=n_blocks * EXPERT_BLOCK)
    ys = _experts(blk_e, blk_valid, blk_src, xs, w_gate, b_gate, w_up, b_up, w_down, b_down,
                  bm=EXPERT_BLOCK)
    out = None
    chunks_per_part = dest.shape[0] // COMBINE_PARTS
    tokens_per_part = T // COMBINE_PARTS
    for part in range(COMBINE_PARTS):
        y4 = _sc_gather_rows(ys, dest[part * chunks_per_part:(part + 1) * chunks_per_part],
                             n_tokens=tokens_per_part)
        out = _weighted_sum(gate, h1, y4, out, first_tile=part * (tokens_per_part // ROW_TILE), tm=ROW_TILE)
    return dict(q=q, k=k, v=v, pool=pool, memo=memo, o=o, h1=h1, hn=hn, idx=idx, rank=rank,
                gate=gate, cnt=cnt, out=out.reshape(B, S, D))


def kernel(x, mem, g_mix, w_in, b_gates, q_norm, k_norm, lambda_q1, lambda_k1, lambda_q2, lambda_k2, g_subln, pool_w, pool_scale, g_mem, w_mem_kv, mq_norm, mk_norm, w_b_attn, w_b_pool, w_b_mem, w_out, g_ffn, w_router, b_router, w_gate, b_gate, w_up, b_up, w_down, b_down):
    return _forward(x, mem, g_mix, w_in, b_gates, q_norm, k_norm, lambda_q1, lambda_k1, lambda_q2,
                    lambda_k2, g_subln, pool_w, pool_scale, g_mem, w_mem_kv, mq_norm, mk_norm,
                    w_b_attn, w_b_pool, w_b_mem, w_out, g_ffn, w_router, b_router, w_gate, b_gate,
                    w_up, b_up, w_down, b_down)["out"]
```

```python
import functools
import math

import jax
import jax.numpy as jnp
import numpy as np
from jax import lax
from jax.experimental import pallas as pl
from jax.experimental.pallas import tpu as pltpu
from jax.experimental.pallas import tpu_sc as plsc

DA_HEADS = 8
DA_HEAD_DIM = 64
DA_V_DIM = 2 * DA_HEAD_DIM
ROPE_THETA = 10000.0
POOL_WINDOWS = (2, 4, 8, 16)
POOL_GROUP_DIM = 128
POOL_HALO = 16
PROJECT_AHEAD = 1
MEM_HEADS = 4
MEM_HEAD_DIM = 128
N_EXPERTS = 32
TOP_K = 4
SWIGLU_LIMIT = 7.0
SWIGLU_ALPHA = 1.702
LAMBDA_INIT = 0.8 - 0.6 * math.exp(-0.3 * 0.0)
EPS = 1e-6
NEG_INF = -1e30
QUERY_SCALE = math.log2(math.e) / math.sqrt(DA_HEAD_DIM)

LANES = 128
MXU_DIM = 256
VMEM_LIMIT_BYTES = 56 * 1024 * 1024

BF16 = jnp.bfloat16
F32 = jnp.float32


def _cparams(*sem):
    return pltpu.CompilerParams(dimension_semantics=sem, vmem_limit_bytes=VMEM_LIMIT_BYTES)


def _const_spec(shape):
    nd = len(shape)
    return pl.BlockSpec(shape, lambda *_: (0,) * nd)


def _dot(a, b):
    return jnp.dot(a, b, preferred_element_type=F32)


def _dot_nt(a, b):
    return lax.dot_general(a, b, (((1,), (1,)), ((), ())), preferred_element_type=F32)


def _rms_rows(x, gain):
    ms = jnp.mean(x * x, axis=-1, keepdims=True)
    return x * lax.rsqrt(ms + EPS) * gain


def _memkv_kernel(mem_ref, gmem_ref, w_ref, mkn_ref, km_ref, vm_ref):
    mem_dim = MEM_HEADS * MEM_HEAD_DIM
    mn = _rms_rows(mem_ref[0], gmem_ref[...]).astype(BF16)
    kv = _dot(mn, w_ref[...])
    for h in range(MEM_HEADS):
        sl = slice(h * MEM_HEAD_DIM, (h + 1) * MEM_HEAD_DIM)
        km_ref[0, :, sl] = _rms_rows(kv[:, sl], mkn_ref[...]).astype(BF16)
    vm_ref[0] = kv[:, mem_dim:].astype(BF16)


def _memkv(mem, g_mem, w_mem_kv, mk_norm):
    B, M, D = mem.shape
    mem_dim = MEM_HEADS * MEM_HEAD_DIM
    out = jax.ShapeDtypeStruct((B, M, mem_dim), BF16)
    return pl.pallas_call(
        _memkv_kernel,
        grid=(B,),
        in_specs=[pl.BlockSpec((1, M, D), lambda b: (b, 0, 0)),
                  _const_spec((1, D)),
                  _const_spec((D, 2 * mem_dim)),
                  _const_spec((1, MEM_HEAD_DIM))],
        out_specs=[pl.BlockSpec((1, M, mem_dim), lambda b: (b, 0, 0))] * 2,
        out_shape=[out, out],
        compiler_params=_cparams("arbitrary"),
        name="memkv",
    )(mem, g_mem.reshape(1, D), w_mem_kv.astype(BF16), mk_norm.reshape(1, MEM_HEAD_DIM))


def _swap32(x):
    lane = lax.broadcasted_iota(jnp.int32, x.shape, 1)
    low = (lane & 32) == 0
    return jnp.where(low, pltpu.roll(x, LANES - 32, 1), pltpu.roll(x, 32, 1))


def _inproj_kernel(x_ref, gmix_ref, w_ref, gq_ref, gk_ref, cos_ref, sin_ref, ones_ref,
                   poolw_ref, pscale_ref, km_ref, vm_ref, mqn_ref,
                   q_ref, k_ref, v_ref, pool_ref, memo_ref,
                   xn_ref, uext_ref, *, tm, tiles_per_seq, d_qk, d_v, d_pool, d_mem):
    i = pl.program_id(0)
    pos0 = (i % tiles_per_seq) * tm
    xn_ref[...] = _rms_rows(x_ref[...], gmix_ref[...]).astype(BF16)

    cos = cos_ref[...]
    sin = sin_ref[...]
    ones = ones_ref[...]

    def qk_slab(p, c, gain_ref, out_ref, scale):
        ss = _dot((p * p).astype(BF16), ones)
        n = p * (lax.rsqrt(ss * (1.0 / DA_HEAD_DIM) + EPS) * scale) * gain_ref[...]
        for half in range(MXU_DIM // LANES):
            nh = n[:, half * LANES:(half + 1) * LANES]
            r = nh * cos + _swap32(nh) * sin
            out_ref[:, c + half * LANES:c + (half + 1) * LANES] = r.astype(BF16)

    def v_slab(p, c):
        v_ref[:, c:c + MXU_DIM] = p.astype(BF16)

    def pool_slab(p, c):
        @pl.when(pos0 == 0)
        def _():
            uext_ref[0:POOL_HALO, c:c + MXU_DIM] = jnp.zeros((POOL_HALO, MXU_DIM), F32)

        uext_ref[POOL_HALO:POOL_HALO + tm, c:c + MXU_DIM] = p
        t1 = pos0 + 1 + lax.broadcasted_iota(jnp.int32, (tm, POOL_GROUP_DIM), 0)
        for g in range(c // POOL_GROUP_DIM, (c + MXU_DIM) // POOL_GROUP_DIM):
            w = POOL_WINDOWS[g]
            sl = slice(g * POOL_GROUP_DIM, (g + 1) * POOL_GROUP_DIM)
            u = uext_ref[POOL_HALO:POOL_HALO + tm, sl]
            acc = u
            for s in range(1, w):
                acc = acc + uext_ref[POOL_HALO - s:POOL_HALO - s + tm, sl]
            cnt = jnp.minimum(t1, w).astype(F32)
            z = acc / cnt - u
            zp = _dot(z.astype(BF16), poolw_ref[g])
            pool_ref[:, sl] = (zp * pscale_ref[:, sl]).astype(BF16)
        uext_ref[0:POOL_HALO, c:c + MXU_DIM] = uext_ref[tm:tm + POOL_HALO, c:c + MXU_DIM]

    def mem_slab(p, c):
        inv_sqrt = 1.0 / math.sqrt(MEM_HEAD_DIM)
        for half in range(MXU_DIM // LANES):
            sl = slice(c + half * LANES, c + (half + 1) * LANES)
            qn = (_rms_rows(p[:, half * LANES:(half + 1) * LANES], mqn_ref[...]) * inv_sqrt).astype(BF16)
            s = _dot_nt(qn, km_ref[0, :, sl])
            m = jnp.max(s, axis=-1, keepdims=True)
            e = jnp.exp(s - m)
            l = jnp.sum(e, axis=-1, keepdims=True)
            o = _dot(e.astype(BF16), vm_ref[0, :, sl])
            memo_ref[:, sl] = (o / l).astype(BF16)

    col = 0
    q_slabs = [(col + c, functools.partial(qk_slab, c=c, gain_ref=gq_ref, out_ref=q_ref, scale=QUERY_SCALE))
               for c in range(0, d_qk, MXU_DIM)]
    col += d_qk
    k_slabs = [(col + c, functools.partial(qk_slab, c=c, gain_ref=gk_ref, out_ref=k_ref, scale=1.0))
               for c in range(0, d_qk, MXU_DIM)]
    col += d_qk
    v_slabs = [(col + c, functools.partial(v_slab, c=c)) for c in range(0, d_v, MXU_DIM)]
    col += d_v
    pool_slabs = [(col + c, functools.partial(pool_slab, c=c)) for c in range(0, d_pool, MXU_DIM)]
    col += d_pool
    mem_slabs = [(col + c, functools.partial(mem_slab, c=c)) for c in range(0, d_mem, MXU_DIM)]
    slabs = q_slabs + k_slabs + v_slabs + pool_slabs + mem_slabs

    def project(col0):
        return _dot(xn_ref[...], w_ref[:, col0:col0 + MXU_DIM])

    queue = [project(c0) for c0, _ in slabs[:PROJECT_AHEAD]]
    for n, (_, epilogue) in enumerate(slabs):
        if n + PROJECT_AHEAD < len(slabs):
            queue.append(project(slabs[n + PROJECT_AHEAD][0]))
        epilogue(queue.pop(0))


def _inproj(x2, g_mix, w_in, q_norm, k_norm, cos_tab, sin_tab, pool_w, pool_scale,
            km, vm, mq_norm, *, seq, tm):
    T, D = x2.shape
    d_qk = DA_HEADS * 2 * DA_HEAD_DIM
    d_v = DA_HEADS * DA_V_DIM
    d_pool = len(POOL_WINDOWS) * POOL_GROUP_DIM
    d_mem = MEM_HEADS * MEM_HEAD_DIM
    d_all = 2 * d_qk + d_v + d_pool + d_mem
    M = km.shape[1]
    tiles_per_seq = seq // tm
    reps = MXU_DIM // DA_HEAD_DIM
    gq = jnp.tile(q_norm, reps).reshape(1, MXU_DIM)
    gk = jnp.tile(k_norm, reps).reshape(1, MXU_DIM)
    grp = jnp.arange(MXU_DIM) // DA_HEAD_DIM
    ones = (grp[:, None] == grp[None, :]).astype(BF16)
    kern = functools.partial(_inproj_kernel, tm=tm, tiles_per_seq=tiles_per_seq,
                             d_qk=d_qk, d_v=d_v, d_pool=d_pool, d_mem=d_mem)
    row = lambda width: pl.BlockSpec((tm, width), lambda i: (i, 0))
    return pl.pallas_call(
        kern,
        grid=(T // tm,),
        in_specs=[row(D),
                  _const_spec((1, D)),
                  _const_spec((D, d_all)),
                  _const_spec((1, MXU_DIM)),
                  _const_spec((1, MXU_DIM)),
                  pl.BlockSpec((tm, LANES), lambda i: (i % tiles_per_seq, 0)),
                  pl.BlockSpec((tm, LANES), lambda i: (i % tiles_per_seq, 0)),
                  _const_spec((MXU_DIM, MXU_DIM)),
                  _const_spec((len(POOL_WINDOWS), POOL_GROUP_DIM, POOL_GROUP_DIM)),
                  _const_spec((1, d_pool)),
                  pl.BlockSpec((1, M, d_mem), lambda i: (i // tiles_per_seq, 0, 0)),
                  pl.BlockSpec((1, M, d_mem), lambda i: (i // tiles_per_seq, 0, 0)),
                  _const_spec((1, MEM_HEAD_DIM))],
        out_specs=[row(d_qk), row(d_qk), row(d_v), row(d_pool), row(d_mem)],
        out_shape=[jax.ShapeDtypeStruct((T, d_qk), BF16),
                   jax.ShapeDtypeStruct((T, d_qk), BF16),
                   jax.ShapeDtypeStruct((T, d_v), BF16),
                   jax.ShapeDtypeStruct((T, d_pool), BF16),
                   jax.ShapeDtypeStruct((T, d_mem), BF16)],
        scratch_shapes=[pltpu.VMEM((tm, D), BF16),
                        pltpu.VMEM((tm + POOL_HALO, d_pool), F32)],
        compiler_params=_cparams("arbitrary"),
        name="inproj",
    )(x2, g_mix.reshape(1, D), w_in[:, :d_all].astype(BF16), gq, gk, cos_tab, sin_tab, ones,
      pool_w.astype(BF16), pool_scale.reshape(1, d_pool), km, vm, mq_norm.reshape(1, MEM_HEAD_DIM))


ATTN_UNROLL = 8
BOUNDED_UNROLL = 24
DIAG_UNROLL = 8
ATTN_BOUND_MARGIN = 1.01
ATTN_BOUND_MAX = 30.0


def _attn_tables(n_blk):
    qi = [q for q in range(1, n_blk) for _ in range(q)]
    kj = [j for q in range(1, n_blk) for j in range(q)]
    return (np.asarray(qi + qi[-1:] * 2, np.int32), np.asarray(kj + kj[-1:] * 2, np.int32))


def _attn_kernel(tqi_ref, tkj_ref, lamp_ref, q_ref, k_ref, v_ref, gsub_ref, o_ref,
                 vt_ref, qc_ref, acc_ref, m_ref, l_ref, s0_ref, s1_ref, s2_ref, s3_ref, pa_ref, pb_ref,
                 *, seq, tq, n_full):
    lp = lamp_ref[...]
    lam = (jnp.exp(jnp.sum(lp[0:1] * lp[1:2], axis=-1, keepdims=True))
           - jnp.exp(jnp.sum(lp[2:3] * lp[3:4], axis=-1, keepdims=True)) + LAMBDA_INIT)
    n_blk = seq // tq
    sub = lax.broadcasted_iota(jnp.int32, (DA_V_DIM, tq), 0)
    key_i = lax.broadcasted_iota(jnp.int32, (tq, 2 * tq), 0)
    qry_i = lax.broadcasted_iota(jnp.int32, (tq, 2 * tq), 1) % tq
    causal = key_i <= qry_i
    s_bufs = (s0_ref, s1_ref, s2_ref, s3_ref)
    p_bufs = (pa_ref, pb_ref)

    def setup(t, carry):
        r0 = pl.multiple_of(t * tq, tq)
        vt_ref[t] = v_ref[pl.ds(r0, tq), :].astype(F32).T.astype(BF16)
        qt = q_ref[pl.ds(r0, tq), :].astype(F32).T
        qc_ref[t] = jnp.concatenate([jnp.where(sub < DA_HEAD_DIM, qt, 0.0),
                                     jnp.where(sub >= DA_HEAD_DIM, qt, 0.0)], axis=1).astype(BF16)
        m_ref[t] = jnp.full((1, 2 * tq), NEG_INF, F32)
        l_ref[t] = jnp.zeros((1, 2 * tq), F32)
        acc_ref[t] = jnp.zeros((DA_V_DIM, 2 * tq), F32)
        return carry

    lax.fori_loop(0, n_blk, setup, 0)

    def scores(qi, kj, dst_ref, masked):
        k0 = pl.multiple_of(kj * tq, tq)
        s = _dot(k_ref[pl.ds(k0, tq), :], qc_ref[qi])
        dst_ref[...] = jnp.where(causal, s, NEG_INF) if masked else s

    def softmax(qi, s_ref, p_ref):
        m_prev = m_ref[qi]
        s = s_ref[...]
        m_new = jnp.maximum(m_prev, jnp.max(s, axis=0, keepdims=True))
        alpha = jnp.exp2(m_prev - m_new)
        p = jnp.exp2(s - m_new)
        m_ref[qi] = m_new
        l_ref[qi] = alpha * l_ref[qi] + jnp.sum(p, axis=0, keepdims=True)
        p_ref[...] = p.astype(BF16)
        return alpha

    def values(qi, kj, p_ref, alpha):
        acc_ref[qi] = alpha * acc_ref[qi] + _dot(vt_ref[kj], p_ref[...])

    def finalize(qi):
        on = acc_ref[qi] / l_ref[qi]
        ot = on[:, :tq] - lam * on[:, tq:]
        ot = ot * lax.rsqrt(jnp.mean(ot * ot, axis=0, keepdims=True) + EPS)
        o = ot.T * (gsub_ref[...] * (1.0 - LAMBDA_INIT))
        o_ref[pl.ds(pl.multiple_of(qi * tq, tq), tq), :] = o.astype(BF16)

    ones_row = jnp.ones((1, 2 * tq), F32)

    def full_step(n, u, alpha_prev):
        scores(tqi_ref[n + 2], tkj_ref[n + 2], s_bufs[(u + 2) % 4], False)
        alpha = softmax(tqi_ref[n], s_bufs[u % 4], p_bufs[u % 2])
        prev = jnp.maximum(n - 1, 0)
        values(tqi_ref[prev], tkj_ref[prev], p_bufs[(u + 1) % 2], alpha_prev)
        return alpha

    def full_steps(t, alpha_prev):
        for u in range(ATTN_UNROLL):
            alpha_prev = full_step(ATTN_UNROLL * t + u, u, alpha_prev)
        return alpha_prev

    pb_ref[...] = jnp.zeros(pb_ref.shape, BF16)
    scores(tqi_ref[0], tkj_ref[0], s0_ref, False)
    scores(tqi_ref[1], tkj_ref[1], s1_ref, False)
    alpha_last = lax.fori_loop(0, n_full // ATTN_UNROLL, full_steps, ones_row)
    values(tqi_ref[n_full - 1], tkj_ref[n_full - 1], pb_ref, alpha_last)

    def diag_step(qi, u, alpha_prev):
        nxt = jnp.minimum(qi + 1, n_blk - 1)
        scores(nxt, nxt, s_bufs[1 - u], True)
        alpha = softmax(qi, s_bufs[u], p_bufs[u])
        prev = jnp.maximum(qi - 1, 0)
        values(prev, prev, p_bufs[1 - u], alpha_prev)
        finalize(prev)
        return alpha

    def diag_steps(t, alpha_prev):
        return diag_step(2 * t + 1, 1, diag_step(2 * t, 0, alpha_prev))

    pb_ref[...] = jnp.zeros(pb_ref.shape, BF16)
    scores(0, 0, s0_ref, True)
    alpha_last = lax.fori_loop(0, n_blk // 2, diag_steps, ones_row)
    values(n_blk - 1, n_blk - 1, pb_ref, alpha_last)
    finalize(n_blk - 1)


def _attn_bounded_kernel(tqi_ref, tkj_ref, bound_ref, lamp_ref, q_ref, k_ref, v_ref, gsub_ref, o_ref,
                         vt_ref, qc_ref, acc_ref, l_ref, pa_ref, pb_ref, *, seq, tq, n_full):
    lp = lamp_ref[...]
    lam = (jnp.exp(jnp.sum(lp[0:1] * lp[1:2], axis=-1, keepdims=True))
           - jnp.exp(jnp.sum(lp[2:3] * lp[3:4], axis=-1, keepdims=True)) + LAMBDA_INIT)
    n_blk = seq // tq
    bound = bound_ref[0]
    sub = lax.broadcasted_iota(jnp.int32, (DA_V_DIM, tq), 0)
    key_i = lax.broadcasted_iota(jnp.int32, (tq, 2 * tq), 0)
    qry_i = lax.broadcasted_iota(jnp.int32, (tq, 2 * tq), 1) % tq
    causal = key_i <= qry_i
    p_bufs = (pa_ref, pb_ref)

    def setup(t, carry):
        r0 = pl.multiple_of(t * tq, tq)
        vt_ref[t] = v_ref[pl.ds(r0, tq), :].astype(F32).T.astype(BF16)
        qt = q_ref[pl.ds(r0, tq), :].astype(F32).T
        qc_ref[t] = jnp.concatenate([jnp.where(sub < DA_HEAD_DIM, qt, 0.0),
                                     jnp.where(sub >= DA_HEAD_DIM, qt, 0.0)], axis=1).astype(BF16)
        l_ref[t] = jnp.zeros((1, 2 * tq), F32)
        acc_ref[t] = jnp.zeros((DA_V_DIM, 2 * tq), F32)
        return carry

    lax.fori_loop(0, n_blk, setup, 0, unroll=8)

    def probs(qi, kj, p_ref, masked):
        k0 = pl.multiple_of(kj * tq, tq)
        s = _dot(k_ref[pl.ds(k0, tq), :], qc_ref[qi])
        if masked:
            s = jnp.where(causal, s, NEG_INF)
        p = jnp.exp2(s - bound)
        l_ref[qi] = l_ref[qi] + jnp.sum(p, axis=0, keepdims=True)
        p_ref[...] = p.astype(BF16)

    def values(qi, kj, p_ref):
        acc_ref[qi] = acc_ref[qi] + _dot(vt_ref[kj], p_ref[...])

    def finalize(qi):
        on = acc_ref[qi] / l_ref[qi]
        ot = on[:, :tq] - lam * on[:, tq:]
        ot = ot * lax.rsqrt(jnp.mean(ot * ot, axis=0, keepdims=True) + EPS)
        o = ot.T * (gsub_ref[...] * (1.0 - LAMBDA_INIT))
        o_ref[pl.ds(pl.multiple_of(qi * tq, tq), tq), :] = o.astype(BF16)

    def full_steps(t, carry):
        for u in range(BOUNDED_UNROLL):
            n = BOUNDED_UNROLL * t + u
            probs(tqi_ref[n], tkj_ref[n], p_bufs[u % 2], False)
            prev = jnp.maximum(n - 1, 0)
            values(tqi_ref[prev], tkj_ref[prev], p_bufs[(u + 1) % 2])
        return carry

    pb_ref[...] = jnp.zeros(pb_ref.shape, BF16)
    lax.fori_loop(0, n_full // BOUNDED_UNROLL, full_steps, 0)
    values(tqi_ref[n_full - 1], tkj_ref[n_full - 1], pb_ref)

    def diag_steps(t, carry):
        for u in range(DIAG_UNROLL):
            qi = DIAG_UNROLL * t + u
            probs(qi, qi, p_bufs[u % 2], True)
            prev = jnp.maximum(qi - 1, 0)
            values(prev, prev, p_bufs[(u + 1) % 2])
            finalize(prev)
        return carry

    pb_ref[...] = jnp.zeros(pb_ref.shape, BF16)
    lax.fori_loop(0, n_blk // DIAG_UNROLL, diag_steps, 0)
    values(n_blk - 1, n_blk - 1, pb_ref)
    finalize(n_blk - 1)


def _attn_bounded(bound, q, k, v, lam_params, g_subln, *, batch, seq, tq):
    T, d_v = v.shape
    n_blk = seq // tq
    tqi, tkj = _attn_tables(n_blk)
    n_full = tqi.shape[0] - 2
    assert n_full % BOUNDED_UNROLL == 0 and n_blk % DIAG_UNROLL == 0, (seq, tq)
    blk = lambda: pl.BlockSpec((seq, DA_V_DIM), lambda b, h, *_: (b, h))
    const = lambda shape: pl.BlockSpec(shape, lambda b, h, *_: (0,) * len(shape))
    return pl.pallas_call(
        functools.partial(_attn_bounded_kernel, seq=seq, tq=tq, n_full=n_full),
        grid_spec=pltpu.PrefetchScalarGridSpec(
            num_scalar_prefetch=3,
            grid=(batch, DA_HEADS),
            in_specs=[const((4, DA_HEAD_DIM)), blk(), blk(), blk(), const((1, DA_V_DIM))],
            out_specs=blk(),
            scratch_shapes=[pltpu.VMEM((n_blk, DA_V_DIM, tq), BF16),
                            pltpu.VMEM((n_blk, DA_V_DIM, 2 * tq), BF16),
                            pltpu.VMEM((n_blk, DA_V_DIM, 2 * tq), F32),
                            pltpu.VMEM((n_blk, 1, 2 * tq), F32)]
                           + [pltpu.VMEM((tq, 2 * tq), BF16)] * 2),
        out_shape=jax.ShapeDtypeStruct((T, d_v), BF16),
        compiler_params=_cparams("arbitrary", "arbitrary"),
        name="attn_bounded",
    )(jnp.asarray(tqi), jnp.asarray(tkj), bound.reshape(1), lam_params, q, k, v,
      g_subln.reshape(1, DA_V_DIM))


def _attn(q, k, v, lam_params, g_subln, *, batch, seq, tq):
    T, d_v = v.shape
    n_blk = seq // tq
    tqi, tkj = _attn_tables(n_blk)
    n_full = tqi.shape[0] - 2
    assert n_full % ATTN_UNROLL == 0 and n_blk % 2 == 0, (seq, tq)
    blk = lambda: pl.BlockSpec((seq, DA_V_DIM), lambda b, h, tqi, tkj: (b, h))
    const = lambda shape: pl.BlockSpec(shape, lambda b, h, tqi, tkj: (0,) * len(shape))
    return pl.pallas_call(
        functools.partial(_attn_kernel, seq=seq, tq=tq, n_full=n_full),
        grid_spec=pltpu.PrefetchScalarGridSpec(
            num_scalar_prefetch=2,
            grid=(batch, DA_HEADS),
            in_specs=[const((4, DA_HEAD_DIM)), blk(), blk(), blk(), const((1, DA_V_DIM))],
            out_specs=blk(),
            scratch_shapes=[pltpu.VMEM((n_blk, DA_V_DIM, tq), BF16),
                            pltpu.VMEM((n_blk, DA_V_DIM, 2 * tq), BF16),
                            pltpu.VMEM((n_blk, DA_V_DIM, 2 * tq), F32),
                            pltpu.VMEM((n_blk, 1, 2 * tq), F32),
                            pltpu.VMEM((n_blk, 1, 2 * tq), F32)]
                           + [pltpu.VMEM((tq, 2 * tq), F32)] * 4
                           + [pltpu.VMEM((tq, 2 * tq), BF16)] * 2),
        out_shape=jax.ShapeDtypeStruct((T, d_v), BF16),
        compiler_params=_cparams("arbitrary", "arbitrary"),
        name="attn",
    )(jnp.asarray(tqi), jnp.asarray(tkj), lam_params, q, k, v, g_subln.reshape(1, DA_V_DIM))


def _pack_bf16_pairs(a, b):
    hi = pltpu.bitcast(a.astype(BF16).astype(F32), jnp.uint32)
    lo = pltpu.bitcast(b.astype(BF16).astype(F32), jnp.uint32)
    return hi | (lo >> 16)


def _unpack_bf16_pairs(p):
    a = pltpu.bitcast(p & jnp.uint32(0xFFFF0000), F32)
    b = pltpu.bitcast(p << 16, F32)
    return a, b


def _merge_kernel(x_ref, oa_ref, pool_ref, memo_ref, gmix_ref, wg_ref, bg_ref, wba_ref, wbp_ref,
                  wbm_ref, wout_ref, gffn_ref, wrh_ref, wrl_ref, br_ref,
                  h_ref, hn_ref, idx_ref, rank_ref, gate_ref, cnt_ref,
                  xn_ref, mrg_ref, hs_ref, carry_ref, *, tm, d_model):
    i = pl.program_id(0)
    D = d_model

    @pl.when(i == 0)
    def _():
        carry_ref[...] = jnp.zeros(carry_ref.shape, F32)
        hs_ref[...] = jnp.zeros(hs_ref.shape, F32)

    def logits():
        hn = _rms_rows(hs_ref[...], gffn_ref[...])
        hn_ref[...] = _pack_bf16_pairs(hn[:, :D // 2], hn[:, D // 2:])
        hn_hi = hn.astype(BF16)
        hn_lo = (hn - hn_hi.astype(F32)).astype(BF16)
        return (_dot(hn_hi, wrh_ref[...]) + _dot(hn_lo, wrh_ref[...])
                + _dot(hn_hi, wrl_ref[...]) + br_ref[...])

    route = _route_pieces(logits, i > 0, idx_ref, rank_ref, gate_ref, cnt_ref, carry_ref, tm)

    x = x_ref[...]
    xn_ref[...] = _rms_rows(x, gmix_ref[...]).astype(BF16)
    branches = ((oa_ref, wba_ref), (pool_ref, wbp_ref), (memo_ref, wbm_ref))

    def slab_dots(j):
        sl = slice(j * MXU_DIM, (j + 1) * MXU_DIM)
        out = []
        for br, (y_ref, w_ref) in enumerate(branches):
            gsl = slice(br * D + j * MXU_DIM, br * D + (j + 1) * MXU_DIM)
            out.append((_dot(xn_ref[...], wg_ref[:, gsl]), _dot(y_ref[...], w_ref[:, sl])))
        return out

    def slab_epilogue(j, dots):
        merged = None
        for br, (logit, y) in enumerate(dots):
            gsl = slice(br * D + j * MXU_DIM, br * D + (j + 1) * MXU_DIM)
            term = y / (1.0 + jnp.exp(-(logit + bg_ref[:, gsl])))
            merged = term if merged is None else merged + term
        mrg_ref[:, j * MXU_DIM:(j + 1) * MXU_DIM] = merged.astype(BF16)

    n_slabs = D // MXU_DIM
    dots = slab_dots(0)
    for j in range(n_slabs):
        nxt = slab_dots(j + 1) if j + 1 < n_slabs else None
        route[j]()
        slab_epilogue(j, dots)
        dots = nxt
    for piece in route[n_slabs:-1]:
        piece()
    h = x + _dot(mrg_ref[...], wout_ref[...])
    route[-1]()
    h_ref[...] = h
    hs_ref[...] = h


def _route_pieces(logits_fn, live, idx_ref, rank_ref, gate_ref, cnt_ref, carry_ref, tm):
    lane = lax.broadcasted_iota(jnp.int32, (tm, LANES), 1)
    st = dict(vals=[], sels=[], idx_out=jnp.zeros((tm, LANES), jnp.int32))

    def start():
        st["work"] = logits_fn()

    def pick(kk):
        work = st["work"]
        mx = jnp.max(work, axis=-1, keepdims=True)
        idx = jnp.min(jnp.where(work == mx, lane, LANES), axis=-1, keepdims=True)
        sel = lane == idx
        st["vals"].append(mx)
        st["sels"].append(sel)
        st["idx_out"] = jnp.where(lane == kk, idx, st["idx_out"])
        st["work"] = jnp.where(sel, -jnp.inf, work)

    def finish():
        vals, sels = st["vals"], st["sels"]
        exps = [jnp.exp(v - vals[0]) for v in vals]
        denom = exps[0] + exps[1] + exps[2] + exps[3]
        gate_out = jnp.zeros((tm, LANES), F32)
        onehot = jnp.zeros((tm, LANES), F32)
        for kk in range(TOP_K):
            gate_out = jnp.where(lane == kk, exps[kk] / denom, gate_out)
            onehot = jnp.where(sels[kk], 1.0, onehot)
        r_i = lax.broadcasted_iota(jnp.int32, (tm, tm), 0)
        c_i = lax.broadcasted_iota(jnp.int32, (tm, tm), 1)
        lower = (c_i < r_i).astype(BF16)
        prefix = _dot(lower, onehot.astype(BF16)) + carry_ref[0:1, :]
        rank_out = jnp.zeros((tm, LANES), jnp.int32)
        for kk in range(TOP_K):
            rk = jnp.sum(jnp.where(sels[kk], prefix, 0.0), axis=-1, keepdims=True)
            rank_out = jnp.where(lane == kk, rk.astype(jnp.int32), rank_out)
        new_carry = prefix[tm - 1:tm, :] + onehot[tm - 1:tm, :]
        new_carry = jnp.where(live, new_carry, carry_ref[0:1, :])
        carry_ref[...] = jnp.broadcast_to(new_carry, carry_ref.shape)
        idx_ref[...] = st["idx_out"]
        rank_ref[...] = rank_out
        gate_ref[...] = gate_out
        cnt_ref[...] = jnp.broadcast_to(new_carry, cnt_ref.shape).astype(jnp.int32)

    return [start] + [functools.partial(pick, kk) for kk in range(TOP_K)] + [finish]


def _merge(x2, oa, pool, memo, g_mix, w_gates, b_gates, w_b_attn, w_b_pool, w_b_mem, w_out, g_ffn,
           w_router, b_router, *, tm):
    T, D = x2.shape
    d_pool = pool.shape[1]
    d_mem = memo.shape[1]
    E = w_router.shape[1]
    wr = jnp.zeros((D, LANES), F32).at[:, :E].set(w_router)
    wr_hi = wr.astype(BF16)
    wr_lo = (wr - wr_hi.astype(F32)).astype(BF16)
    br = jnp.full((1, LANES), NEG_INF, F32).at[0, :E].set(b_router)
    n_tiles = T // tm
    row = lambda width: pl.BlockSpec((tm, width), lambda i: (jnp.minimum(i, n_tiles - 1), 0))
    routed = lambda width: pl.BlockSpec((tm, width), lambda i: (jnp.maximum(i - 1, 0), 0))
    sublanes = 8
    return pl.pallas_call(
        functools.partial(_merge_kernel, tm=tm, d_model=D),
        grid=(n_tiles + 1,),
        in_specs=[row(D), row(oa.shape[1]), row(d_pool), row(d_mem),
                  _const_spec((1, D)),
                  _const_spec((D, 3 * D)), _const_spec((1, 3 * D)),
                  _const_spec((oa.shape[1], D)), _const_spec((d_pool, D)), _const_spec((d_mem, D)),
                  _const_spec((D, D)), _const_spec((1, D)),
                  _const_spec((D, LANES)), _const_spec((D, LANES)), _const_spec((1, LANES))],
        out_specs=[row(D), routed(D // 2), routed(LANES), routed(LANES), routed(LANES),
                   _const_spec((sublanes, LANES))],
        out_shape=[jax.ShapeDtypeStruct((T, D), F32),
                   jax.ShapeDtypeStruct((T, D // 2), jnp.uint32),
                   jax.ShapeDtypeStruct((T, LANES), jnp.int32),
                   jax.ShapeDtypeStruct((T, LANES), jnp.int32),
                   jax.ShapeDtypeStruct((T, LANES), F32),
                   jax.ShapeDtypeStruct((sublanes, LANES), jnp.int32)],
        scratch_shapes=[pltpu.VMEM((tm, D), BF16),
                        pltpu.VMEM((tm, D), BF16),
                        pltpu.VMEM((tm, D), F32),
                        pltpu.VMEM((sublanes, LANES), F32)],
        compiler_params=_cparams("arbitrary"),
        name="merge",
    )(x2, oa, pool, memo, g_mix.reshape(1, D), w_gates.astype(BF16), b_gates.reshape(1, 3 * D),
      w_b_attn.astype(BF16), w_b_pool.astype(BF16), w_b_mem.astype(BF16), w_out.astype(BF16),
      g_ffn.reshape(1, D), wr_hi, wr_lo, br)


def _experts_kernel(be_ref, nv_ref, src_ref, x_ref, wg_ref, bg_ref, wu_ref, bu_ref, wd_ref, bd_ref,
                    y_ref, act_ref, *, bm, d_model, d_expert):
    i = pl.program_id(0)
    half = d_model // 2

    @pl.when(nv_ref[i] == 0)
    def _():
        y_ref[...] = jnp.zeros(y_ref.shape, y_ref.dtype)

    @pl.when(nv_ref[i] > 0)
    def _():
        a, b = _unpack_bf16_pairs(x_ref[...])
        valid = lax.broadcasted_iota(jnp.int32, (bm, half), 0) < nv_ref[i]
        a = jnp.where(valid, a, 0.0).astype(BF16)
        b = jnp.where(valid, b, 0.0).astype(BF16)
        for j in range(d_expert // MXU_DIM):
            sl = slice(j * MXU_DIM, (j + 1) * MXU_DIM)
            g = _dot(a, wg_ref[0, :half, sl]) + _dot(b, wg_ref[0, half:, sl]) + bg_ref[0, :, sl]
            u = _dot(a, wu_ref[0, :half, sl]) + _dot(b, wu_ref[0, half:, sl]) + bu_ref[0, :, sl]
            g = jnp.minimum(g, SWIGLU_LIMIT)
            u = jnp.clip(u, -SWIGLU_LIMIT, SWIGLU_LIMIT)
            act = g * (1.0 / (1.0 + jnp.exp(-SWIGLU_ALPHA * g))) * (u + 1.0)
            act_ref[:, sl] = act.astype(BF16)
        out = _dot(act_ref[...], wd_ref[0]) + bd_ref[0]
        y_ref[...] = _pack_bf16_pairs(out[:, :half], out[:, half:])


def _experts(blk_e, blk_valid, blk_src, xs, w_gate, b_gate, w_up, b_up, w_down, b_down, *, bm):
    rows_sorted, half = xs.shape
    E, D, De = w_gate.shape
    n_blocks = rows_sorted // bm
    wspec = lambda r, c: pl.BlockSpec((1, r, c), lambda i, be, nv, src: (be[i], 0, 0))
    xspec = pl.BlockSpec((bm, half), lambda i, be, nv, src: (src[i], 0))
    return pl.pallas_call(
        functools.partial(_experts_kernel, bm=bm, d_model=D, d_expert=De),
        grid_spec=pltpu.PrefetchScalarGridSpec(
            num_scalar_prefetch=3,
            grid=(n_blocks,),
            in_specs=[xspec, wspec(D, De), wspec(1, De), wspec(D, De), wspec(1, De),
                      wspec(De, D), wspec(1, D)],
            out_specs=pl.BlockSpec((bm, half), lambda i, be, nv, src: (i, 0)),
            scratch_shapes=[pltpu.VMEM((bm, De), BF16)]),
        out_shape=jax.ShapeDtypeStruct((rows_sorted, half), jnp.uint32),
        compiler_params=_cparams("arbitrary"),
        name="experts",
    )(blk_e, blk_valid, blk_src, xs, w_gate.astype(BF16), b_gate.reshape(E, 1, De),
      w_up.astype(BF16), b_up.reshape(E, 1, De), w_down.astype(BF16), b_down.reshape(E, 1, D))


SC_CORES = 2
SC_SUBCORES = 16
SC_CHUNK = 128


def _sc_mesh():
    return plsc.VectorSubcoreMesh(core_axis_name="c", subcore_axis_name="s",
                                  num_cores=SC_CORES, num_subcores=SC_SUBCORES)


def _sc_scatter_rows(rows, dest, *, rows_out):
    T, width = rows.shape
    n_chunks = T // SC_CHUNK
    per_worker = n_chunks // (SC_CORES * SC_SUBCORES)

    @functools.partial(
        pl.kernel, mesh=_sc_mesh(),
        out_type=jax.ShapeDtypeStruct((rows_out, width), rows.dtype),
        scratch_types=[pltpu.VMEM((TOP_K, SC_CHUNK), jnp.int32),
                       pltpu.VMEM((SC_CHUNK, width), rows.dtype)],
        name="sc_dispatch")
    def scatter(rows_hbm, dest_hbm, out_hbm, idx_v, rows_v):
        worker = lax.axis_index("s") * SC_CORES + lax.axis_index("c")

        @pl.loop(0, per_worker)
        def _(n):
            c = worker * per_worker + n
            pltpu.sync_copy(rows_hbm.at[pl.ds(c * SC_CHUNK, SC_CHUNK)], rows_v)
            pltpu.sync_copy(dest_hbm.at[c], idx_v)
            for kk in range(TOP_K):
                pltpu.sync_copy(rows_v, out_hbm.at[idx_v.at[kk]])

    return scatter(rows, dest)


def _sc_gather_rows(table, src, *, n_tokens):
    _, width = table.shape
    n_chunks = n_tokens // SC_CHUNK
    per_worker = n_chunks // (SC_CORES * SC_SUBCORES)

    @functools.partial(
        pl.kernel, mesh=_sc_mesh(),
        out_type=jax.ShapeDtypeStruct((TOP_K, n_tokens, width), table.dtype),
        scratch_types=[pltpu.VMEM((TOP_K, SC_CHUNK), jnp.int32),
                       pltpu.VMEM((SC_CHUNK, width), table.dtype)],
        name="sc_combine")
    def gather(table_hbm, src_hbm, out_hbm, idx_v, rows_v):
        worker = lax.axis_index("s") * SC_CORES + lax.axis_index("c")

        @pl.loop(0, per_worker)
        def _(n):
            c = worker * per_worker + n
            pltpu.sync_copy(src_hbm.at[c], idx_v)
            for kk in range(TOP_K):
                pltpu.sync_copy(table_hbm.at[idx_v.at[kk]], rows_v)
                pltpu.sync_copy(rows_v, out_hbm.at[kk, pl.ds(c * SC_CHUNK, SC_CHUNK)])

    return gather(table, src)


def _weighted_sum_kernel(gate_ref, h_ref, y_ref, *rest, d_model):
    o_ref = rest[-1]
    half = d_model // 2
    h = h_ref[...]
    lo, hi = h[:, :half], h[:, half:]
    gates = gate_ref[...]
    for kk in range(TOP_K):
        a, b = _unpack_bf16_pairs(y_ref[kk])
        w = gates[:, kk:kk + 1]
        lo = lo + w * a
        hi = hi + w * b
    o_ref[:, :half] = lo
    o_ref[:, half:] = hi


def _weighted_sum(gate, h, y4, out_prev, *, first_tile, tm):
    T, D = h.shape
    n_tiles = y4.shape[1] // tm
    row = lambda width: pl.BlockSpec((tm, width), lambda i: (first_tile + i, 0))
    in_specs = [row(LANES), row(D), pl.BlockSpec((TOP_K, tm, D // 2), lambda i: (0, i, 0))]
    args = [gate, h, y4]
    aliases = {}
    if out_prev is not None:
        in_specs.append(pl.BlockSpec(memory_space=pl.ANY))
        args.append(out_prev)
        aliases = {3: 0}
    return pl.pallas_call(
        functools.partial(_weighted_sum_kernel, d_model=D),
        grid=(n_tiles,),
        in_specs=in_specs,
        out_specs=row(D),
        out_shape=jax.ShapeDtypeStruct((T, D), F32),
        input_output_aliases=aliases,
        compiler_params=_cparams("arbitrary"),
        name="weighted_sum",
    )(*args)


def _block_tables(counts, *, bm, n_blocks):
    padded = ((counts + bm - 1) // bm) * bm
    pend = jnp.cumsum(padded)
    poff = pend - padded
    n_used = pend[-1] // bm
    src = jnp.minimum(jnp.arange(n_blocks, dtype=jnp.int32), n_used - 1)
    blk_e = jnp.sum(pend[None, :] <= (src * bm)[:, None], axis=1)
    blk_e = jnp.minimum(blk_e, N_EXPERTS - 1).astype(jnp.int32)
    blk_valid = jnp.clip(poff[blk_e] + counts[blk_e] - src * bm, 0, bm)
    blk_valid = jnp.where(jnp.arange(n_blocks) < n_used, blk_valid, 0).astype(jnp.int32)
    return poff.astype(jnp.int32), blk_e, blk_valid, src.astype(jnp.int32)


def _rope_tables(seq):
    inv_freq = ROPE_THETA ** (-jnp.arange(0, DA_HEAD_DIM, 2, dtype=F32) / DA_HEAD_DIM)
    ang = jnp.arange(seq, dtype=F32)[:, None] * inv_freq[None, :]
    reps = LANES // (DA_HEAD_DIM // 2)
    cos = jnp.tile(jnp.cos(ang), (1, reps))
    sin = jnp.tile(jnp.sin(ang), (1, reps))
    first_half = (jnp.arange(LANES) % DA_HEAD_DIM) < DA_HEAD_DIM // 2
    return cos, jnp.where(first_half[None, :], -sin, sin)


ROW_TILE = 512
ATTN_TILE = 256
EXPERT_BLOCK = 512
COMBINE_PARTS = 4


def _forward(x, mem, g_mix, w_in, b_gates, q_norm, k_norm, lambda_q1, lambda_k1, lambda_q2, lambda_k2, g_subln, pool_w, pool_scale, g_mem, w_mem_kv, mq_norm, mk_norm, w_b_attn, w_b_pool, w_b_mem, w_out, g_ffn, w_router, b_router, w_gate, b_gate, w_up, b_up, w_down, b_down):
    B, S, D = x.shape
    T = B * S
    x2 = x.reshape(T, D)
    cos_tab, sin_tab = _rope_tables(S)
    km, vm = _memkv(mem, g_mem, w_mem_kv, mk_norm)
    q, k, v, pool, memo = _inproj(x2, g_mix, w_in, q_norm, k_norm, cos_tab, sin_tab, pool_w,
                                  pool_scale, km, vm, mq_norm, seq=S, tm=ROW_TILE)
    lam_params = jnp.stack([lambda_q1, lambda_k1, lambda_q2, lambda_k2])
    score_bound = (ATTN_BOUND_MARGIN * DA_HEAD_DIM * QUERY_SCALE
                   * jnp.max(jnp.abs(q_norm)) * jnp.max(jnp.abs(k_norm))).astype(F32)
    o = lax.cond(score_bound <= ATTN_BOUND_MAX,
                 lambda: _attn_bounded(score_bound, q, k, v, lam_params, g_subln, batch=B, seq=S, tq=ATTN_TILE),
                 lambda: _attn(q, k, v, lam_params, g_subln, batch=B, seq=S, tq=ATTN_TILE))
    d_in = q.shape[1] + k.shape[1] + v.shape[1] + pool.shape[1] + memo.shape[1]
    h1, hn, idx, rank, gate, cnt = _merge(x2, o, pool, memo, g_mix, w_in[:, d_in:], b_gates, w_b_attn,
                                          w_b_pool, w_b_mem, w_out, g_ffn, w_router, b_router,
                                          tm=ROW_TILE)
    n_blocks = (T * TOP_K) // EXPERT_BLOCK + N_EXPERTS
    poff, blk_e, blk_valid, blk_src = _block_tables(cnt[0, :N_EXPERTS], bm=EXPERT_BLOCK, n_blocks=n_blocks)
    idx_d = idx[:, :TOP_K].reshape(T * TOP_K // LANES, LANES)
    rank_d = rank[:, :TOP_K].reshape(T * TOP_K // LANES, LANES)
    dest = (poff[idx_d] + rank_d).astype(jnp.int32)
    dest = dest.reshape(T // SC_CHUNK, SC_CHUNK, TOP_K).transpose(0, 2, 1)
    xs = _sc_scatter_rows(hn, dest, rows_out=n_blocks * EXPERT_BLOCK)
    ys = _experts(blk_e, blk_valid, blk_src, xs, w_gate, b_gate, w_up, b_up, w_down, b_down,
                  bm=EXPERT_BLOCK)
    out = None
    chunks_per_part = dest.shape[0] // COMBINE_PARTS
    tokens_per_part = T // COMBINE_PARTS
    for part in range(COMBINE_PARTS):
        y4 = _sc_gather_rows(ys, dest[part * chunks_per_part:(part + 1) * chunks_per_part],
                             n_tokens=tokens_per_part)
        out = _weighted_sum(gate, h1, y4, out, first_tile=part * (tokens_per_part // ROW_TILE), tm=ROW_TILE)
    return dict(q=q, k=k, v=v, pool=pool, memo=memo, o=o, h1=h1, hn=hn, idx=idx, rank=rank,
                gate=gate, cnt=cnt, out=out.reshape(B, S, D))


def kernel(x, mem, g_mix, w_in, b_gates, q_norm, k_norm, lambda_q1, lambda_k1, lambda_q2, lambda_k2, g_subln, pool_w, pool_scale, g_mem, w_mem_kv, mq_norm, mk_norm, w_b_attn, w_b_pool, w_b_mem, w_out, g_ffn, w_router, b_router, w_gate, b_gate, w_up, b_up, w_down, b_down):
    return _forward(x, mem, g_mix, w_in, b_gates, q_norm, k_norm, lambda_q1, lambda_k1, lambda_q2,
                    lambda_k2, g_subln, pool_w, pool_scale, g_mem, w_mem_kv, mq_norm, mk_norm,
                    w_b_attn, w_b_pool, w_b_mem, w_out, g_ffn, w_router, b_router, w_gate, b_gate,
                    w_up, b_up, w_down, b_down)["out"]
```

```python
import functools
import math

import jax
import jax.numpy as jnp
import numpy as np
from jax import lax
from jax.experimental import pallas as pl
from jax.experimental.pallas import tpu as pltpu
from jax.experimental.pallas import tpu_sc as plsc

DA_HEADS = 8
DA_HEAD_DIM = 64
DA_V_DIM = 2 * DA_HEAD_DIM
ROPE_THETA = 10000.0
POOL_WINDOWS = (2, 4, 8, 16)
POOL_GROUP_DIM = 128
POOL_HALO = 16
PROJECT_AHEAD = 1
MEM_HEADS = 4
MEM_HEAD_DIM = 128
N_EXPERTS = 32
TOP_K = 4
SWIGLU_LIMIT = 7.0
SWIGLU_ALPHA = 1.702
LAMBDA_INIT = 0.8 - 0.6 * math.exp(-0.3 * 0.0)
EPS = 1e-6
NEG_INF = -1e30
QUERY_SCALE = math.log2(math.e) / math.sqrt(DA_HEAD_DIM)

LANES = 128
MXU_DIM = 256
VMEM_LIMIT_BYTES = 56 * 1024 * 1024

BF16 = jnp.bfloat16
F32 = jnp.float32


def _cparams(*sem):
    return pltpu.CompilerParams(dimension_semantics=sem, vmem_limit_bytes=VMEM_LIMIT_BYTES)


def _const_spec(shape):
    nd = len(shape)
    return pl.BlockSpec(shape, lambda *_: (0,) * nd)


def _dot(a, b):
    return jnp.dot(a, b, preferred_element_type=F32)


def _dot_nt(a, b):
    return lax.dot_general(a, b, (((1,), (1,)), ((), ())), preferred_element_type=F32)


def _rms_rows(x, gain):
    ms = jnp.mean(x * x, axis=-1, keepdims=True)
    return x * lax.rsqrt(ms + EPS) * gain


def _memkv_kernel(mem_ref, gmem_ref, w_ref, mkn_ref, km_ref, vm_ref):
    mem_dim = MEM_HEADS * MEM_HEAD_DIM
    mn = _rms_rows(mem_ref[0], gmem_ref[...]).astype(BF16)
    kv = _dot(mn, w_ref[...])
    for h in range(MEM_HEADS):
        sl = slice(h * MEM_HEAD_DIM, (h + 1) * MEM_HEAD_DIM)
        km_ref[0, :, sl] = _rms_rows(kv[:, sl], mkn_ref[...]).astype(BF16)
    vm_ref[0] = kv[:, mem_dim:].astype(BF16)


def _memkv(mem, g_mem, w_mem_kv, mk_norm):
    B, M, D = mem.shape
    mem_dim = MEM_HEADS * MEM_HEAD_DIM
    out = jax.ShapeDtypeStruct((B, M, mem_dim), BF16)
    return pl.pallas_call(
        _memkv_kernel,
        grid=(B,),
        in_specs=[pl.BlockSpec((1, M, D), lambda b: (b, 0, 0)),
                  _const_spec((1, D)),
                  _const_spec((D, 2 * mem_dim)),
                  _const_spec((1, MEM_HEAD_DIM))],
        out_specs=[pl.BlockSpec((1, M, mem_dim), lambda b: (b, 0, 0))] * 2,
        out_shape=[out, out],
        compiler_params=_cparams("arbitrary"),
        name="memkv",
    )(mem, g_mem.reshape(1, D), w_mem_kv.astype(BF16), mk_norm.reshape(1, MEM_HEAD_DIM))


def _swap32(x):
    lane = lax.broadcasted_iota(jnp.int32, x.shape, 1)
    low = (lane & 32) == 0
    return jnp.where(low, pltpu.roll(x, LANES - 32, 1), pltpu.roll(x, 32, 1))


def _inproj_kernel(x_ref, gmix_ref, w_ref, gq_ref, gk_ref, cos_ref, sin_ref, ones_ref,
                   poolw_ref, pscale_ref, km_ref, vm_ref, mqn_ref,
                   q_ref, k_ref, v_ref, pool_ref, memo_ref,
                   xn_ref, uext_ref, *, tm, tiles_per_seq, d_qk, d_v, d_pool, d_mem):
    i = pl.program_id(0)
    pos0 = (i % tiles_per_seq) * tm
    xn_ref[...] = _rms_rows(x_ref[...], gmix_ref[...]).astype(BF16)

    cos = cos_ref[...]
    sin = sin_ref[...]
    ones = ones_ref[...]

    def qk_slab(p, c, gain_ref, out_ref, scale):
        ss = _dot((p * p).astype(BF16), ones)
        n = p * (lax.rsqrt(ss * (1.0 / DA_HEAD_DIM) + EPS) * scale) * gain_ref[...]
        for half in range(MXU_DIM // LANES):
            nh = n[:, half * LANES:(half + 1) * LANES]
            r = nh * cos + _swap32(nh) * sin
            out_ref[:, c + half * LANES:c + (half + 1) * LANES] = r.astype(BF16)

    def v_slab(p, c):
        v_ref[:, c:c + MXU_DIM] = p.astype(BF16)

    def pool_slab(p, c):
        @pl.when(pos0 == 0)
        def _():
            uext_ref[0:POOL_HALO, c:c + MXU_DIM] = jnp.zeros((POOL_HALO, MXU_DIM), F32)

        uext_ref[POOL_HALO:POOL_HALO + tm, c:c + MXU_DIM] = p
        t1 = pos0 + 1 + lax.broadcasted_iota(jnp.int32, (tm, POOL_GROUP_DIM), 0)
        for g in range(c // POOL_GROUP_DIM, (c + MXU_DIM) // POOL_GROUP_DIM):
            w = POOL_WINDOWS[g]
            sl = slice(g * POOL_GROUP_DIM, (g + 1) * POOL_GROUP_DIM)
            u = uext_ref[POOL_HALO:POOL_HALO + tm, sl]
            acc = u
            for s in range(1, w):
                acc = acc + uext_ref[POOL_HALO - s:POOL_HALO - s + tm, sl]
            cnt = jnp.minimum(t1, w).astype(F32)
            z = acc / cnt - u
            zp = _dot(z.astype(BF16), poolw_ref[g])
            pool_ref[:, sl] = (zp * pscale_ref[:, sl]).astype(BF16)
        uext_ref[0:POOL_HALO, c:c + MXU_DIM] = uext_ref[tm:tm + POOL_HALO, c:c + MXU_DIM]

    def mem_slab(p, c):
        inv_sqrt = 1.0 / math.sqrt(MEM_HEAD_DIM)
        for half in range(MXU_DIM // LANES):
            sl = slice(c + half * LANES, c + (half + 1) * LANES)
            qn = (_rms_rows(p[:, half * LANES:(half + 1) * LANES], mqn_ref[...]) * inv_sqrt).astype(BF16)
            s = _dot_nt(qn, km_ref[0, :, sl])
            m = jnp.max(s, axis=-1, keepdims=True)
            e = jnp.exp(s - m)
            l = jnp.sum(e, axis=-1, keepdims=True)
            o = _dot(e.astype(BF16), vm_ref[0, :, sl])
            memo_ref[:, sl] = (o / l).astype(BF16)

    col = 0
    q_slabs = [(col + c, functools.partial(qk_slab, c=c, gain_ref=gq_ref, out_ref=q_ref, scale=QUERY_SCALE))
               for c in range(0, d_qk, MXU_DIM)]
    col += d_qk
    k_slabs = [(col + c, functools.partial(qk_slab, c=c, gain_ref=gk_ref, out_ref=k_ref, scale=1.0))
               for c in range(0, d_qk, MXU_DIM)]
    col += d_qk
    v_slabs = [(col + c, functools.partial(v_slab, c=c)) for c in range(0, d_v, MXU_DIM)]
    col += d_v
    pool_slabs = [(col + c, functools.partial(pool_slab, c=c)) for c in range(0, d_pool, MXU_DIM)]
    col += d_pool
    mem_slabs = [(col + c, functools.partial(mem_slab, c=c)) for c in range(0, d_mem, MXU_DIM)]
    slabs = q_slabs + k_slabs + v_slabs + pool_slabs + mem_slabs

    def project(col0):
        return _dot(xn_ref[...], w_ref[:, col0:col0 + MXU_DIM])

    queue = [project(c0) for c0, _ in slabs[:PROJECT_AHEAD]]
    for n, (_, epilogue) in enumerate(slabs):
        if n + PROJECT_AHEAD < len(slabs):
            queue.append(project(slabs[n + PROJECT_AHEAD][0]))
        epilogue(queue.pop(0))


def _inproj(x2, g_mix, w_in, q_norm, k_norm, cos_tab, sin_tab, pool_w, pool_scale,
            km, vm, mq_norm, *, seq, tm):
    T, D = x2.shape
    d_qk = DA_HEADS * 2 * DA_HEAD_DIM
    d_v = DA_HEADS * DA_V_DIM
    d_pool = len(POOL_WINDOWS) * POOL_GROUP_DIM
    d_mem = MEM_HEADS * MEM_HEAD_DIM
    d_all = 2 * d_qk + d_v + d_pool + d_mem
    M = km.shape[1]
    tiles_per_seq = seq // tm
    reps = MXU_DIM // DA_HEAD_DIM
    gq = jnp.tile(q_norm, reps).reshape(1, MXU_DIM)
    gk = jnp.tile(k_norm, reps).reshape(1, MXU_DIM)
    grp = jnp.arange(MXU_DIM) // DA_HEAD_DIM
    ones = (grp[:, None] == grp[None, :]).astype(BF16)
    kern = functools.partial(_inproj_kernel, tm=tm, tiles_per_seq=tiles_per_seq,
                             d_qk=d_qk, d_v=d_v, d_pool=d_pool, d_mem=d_mem)
    row = lambda width: pl.BlockSpec((tm, width), lambda i: (i, 0))
    return pl.pallas_call(
        kern,
        grid=(T // tm,),
        in_specs=[row(D),
                  _const_spec((1, D)),
                  _const_spec((D, d_all)),
                  _const_spec((1, MXU_DIM)),
                  _const_spec((1, MXU_DIM)),
                  pl.BlockSpec((tm, LANES), lambda i: (i % tiles_per_seq, 0)),
                  pl.BlockSpec((tm, LANES), lambda i: (i % tiles_per_seq, 0)),
                  _const_spec((MXU_DIM, MXU_DIM)),
                  _const_spec((len(POOL_WINDOWS), POOL_GROUP_DIM, POOL_GROUP_DIM)),
                  _const_spec((1, d_pool)),
                  pl.BlockSpec((1, M, d_mem), lambda i: (i // tiles_per_seq, 0, 0)),
                  pl.BlockSpec((1, M, d_mem), lambda i: (i // tiles_per_seq, 0, 0)),
                  _const_spec((1, MEM_HEAD_DIM))],
        out_specs=[row(d_qk), row(d_qk), row(d_v), row(d_pool), row(d_mem)],
        out_shape=[jax.ShapeDtypeStruct((T, d_qk), BF16),
                   jax.ShapeDtypeStruct((T, d_qk), BF16),
                   jax.ShapeDtypeStruct((T, d_v), BF16),
                   jax.ShapeDtypeStruct((T, d_pool), BF16),
                   jax.ShapeDtypeStruct((T, d_mem), BF16)],
        scratch_shapes=[pltpu.VMEM((tm, D), BF16),
                        pltpu.VMEM((tm + POOL_HALO, d_pool), F32)],
        compiler_params=_cparams("arbitrary"),
        name="inproj",
    )(x2, g_mix.reshape(1, D), w_in[:, :d_all].astype(BF16), gq, gk, cos_tab, sin_tab, ones,
      pool_w.astype(BF16), pool_scale.reshape(1, d_pool), km, vm, mq_norm.reshape(1, MEM_HEAD_DIM))


ATTN_UNROLL = 8
BOUNDED_UNROLL = 40
DIAG_UNROLL = 16
ATTN_BOUND_MARGIN = 1.01
ATTN_BOUND_MAX = 30.0


def _attn_tables(n_blk):
    qi = [q for q in range(1, n_blk) for _ in range(q)]
    kj = [j for q in range(1, n_blk) for j in range(q)]
    return (np.asarray(qi + qi[-1:] * 2, np.int32), np.asarray(kj + kj[-1:] * 2, np.int32))


def _attn_kernel(tqi_ref, tkj_ref, lamp_ref, q_ref, k_ref, v_ref, gsub_ref, o_ref,
                 vt_ref, qc_ref, acc_ref, m_ref, l_ref, s0_ref, s1_ref, s2_ref, s3_ref, pa_ref, pb_ref,
                 *, seq, tq, n_full):
    lp = lamp_ref[...]
    lam = (jnp.exp(jnp.sum(lp[0:1] * lp[1:2], axis=-1, keepdims=True))
           - jnp.exp(jnp.sum(lp[2:3] * lp[3:4], axis=-1, keepdims=True)) + LAMBDA_INIT)
    n_blk = seq // tq
    sub = lax.broadcasted_iota(jnp.int32, (DA_V_DIM, tq), 0)
    key_i = lax.broadcasted_iota(jnp.int32, (tq, 2 * tq), 0)
    qry_i = lax.broadcasted_iota(jnp.int32, (tq, 2 * tq), 1) % tq
    causal = key_i <= qry_i
    s_bufs = (s0_ref, s1_ref, s2_ref, s3_ref)
    p_bufs = (pa_ref, pb_ref)

    def setup(t, carry):
        r0 = pl.multiple_of(t * tq, tq)
        vt_ref[t] = v_ref[pl.ds(r0, tq), :].astype(F32).T.astype(BF16)
        qt = q_ref[pl.ds(r0, tq), :].astype(F32).T
        qc_ref[t] = jnp.concatenate([jnp.where(sub < DA_HEAD_DIM, qt, 0.0),
                                     jnp.where(sub >= DA_HEAD_DIM, qt, 0.0)], axis=1).astype(BF16)
        m_ref[t] = jnp.full((1, 2 * tq), NEG_INF, F32)
        l_ref[t] = jnp.zeros((1, 2 * tq), F32)
        acc_ref[t] = jnp.zeros((DA_V_DIM, 2 * tq), F32)
        return carry

    lax.fori_loop(0, n_blk, setup, 0)

    def scores(qi, kj, dst_ref, masked):
        k0 = pl.multiple_of(kj * tq, tq)
        s = _dot(k_ref[pl.ds(k0, tq), :], qc_ref[qi])
        dst_ref[...] = jnp.where(causal, s, NEG_INF) if masked else s

    def softmax(qi, s_ref, p_ref):
        m_prev = m_ref[qi]
        s = s_ref[...]
        m_new = jnp.maximum(m_prev, jnp.max(s, axis=0, keepdims=True))
        alpha = jnp.exp2(m_prev - m_new)
        p = jnp.exp2(s - m_new)
        m_ref[qi] = m_new
        l_ref[qi] = alpha * l_ref[qi] + jnp.sum(p, axis=0, keepdims=True)
        p_ref[...] = p.astype(BF16)
        return alpha

    def values(qi, kj, p_ref, alpha):
        acc_ref[qi] = alpha * acc_ref[qi] + _dot(vt_ref[kj], p_ref[...])

    def finalize(qi):
        on = acc_ref[qi] / l_ref[qi]
        ot = on[:, :tq] - lam * on[:, tq:]
        ot = ot * lax.rsqrt(jnp.mean(ot * ot, axis=0, keepdims=True) + EPS)
        o = ot.T * (gsub_ref[...] * (1.0 - LAMBDA_INIT))
        o_ref[pl.ds(pl.multiple_of(qi * tq, tq), tq), :] = o.astype(BF16)

    ones_row = jnp.ones((1, 2 * tq), F32)

    def full_step(n, u, alpha_prev):
        scores(tqi_ref[n + 2], tkj_ref[n + 2], s_bufs[(u + 2) % 4], False)
        alpha = softmax(tqi_ref[n], s_bufs[u % 4], p_bufs[u % 2])
        prev = jnp.maximum(n - 1, 0)
        values(tqi_ref[prev], tkj_ref[prev], p_bufs[(u + 1) % 2], alpha_prev)
        return alpha

    def full_steps(t, alpha_prev):
        for u in range(ATTN_UNROLL):
            alpha_prev = full_step(ATTN_UNROLL * t + u, u, alpha_prev)
        return alpha_prev

    pb_ref[...] = jnp.zeros(pb_ref.shape, BF16)
    scores(tqi_ref[0], tkj_ref[0], s0_ref, False)
    scores(tqi_ref[1], tkj_ref[1], s1_ref, False)
    alpha_last = lax.fori_loop(0, n_full // ATTN_UNROLL, full_steps, ones_row)
    values(tqi_ref[n_full - 1], tkj_ref[n_full - 1], pb_ref, alpha_last)

    def diag_step(qi, u, alpha_prev):
        nxt = jnp.minimum(qi + 1, n_blk - 1)
        scores(nxt, nxt, s_bufs[1 - u], True)
        alpha = softmax(qi, s_bufs[u], p_bufs[u])
        prev = jnp.maximum(qi - 1, 0)
        values(prev, prev, p_bufs[1 - u], alpha_prev)
        finalize(prev)
        return alpha

    def diag_steps(t, alpha_prev):
        return diag_step(2 * t + 1, 1, diag_step(2 * t, 0, alpha_prev))

    pb_ref[...] = jnp.zeros(pb_ref.shape, BF16)
    scores(0, 0, s0_ref, True)
    alpha_last = lax.fori_loop(0, n_blk // 2, diag_steps, ones_row)
    values(n_blk - 1, n_blk - 1, pb_ref, alpha_last)
    finalize(n_blk - 1)


def _attn_bounded_kernel(tqi_ref, tkj_ref, bound_ref, lamp_ref, q_ref, k_ref, v_ref, gsub_ref, o_ref,
                         vt_ref, qc_ref, acc_ref, l_ref, pa_ref, pb_ref, *, seq, tq, n_full):
    lp = lamp_ref[...]
    lam = (jnp.exp(jnp.sum(lp[0:1] * lp[1:2], axis=-1, keepdims=True))
           - jnp.exp(jnp.sum(lp[2:3] * lp[3:4], axis=-1, keepdims=True)) + LAMBDA_INIT)
    n_blk = seq // tq
    bound = bound_ref[0]
    sub = lax.broadcasted_iota(jnp.int32, (DA_V_DIM, tq), 0)
    key_i = lax.broadcasted_iota(jnp.int32, (tq, 2 * tq), 0)
    qry_i = lax.broadcasted_iota(jnp.int32, (tq, 2 * tq), 1) % tq
    causal = key_i <= qry_i
    p_bufs = (pa_ref, pb_ref)

    def setup(t, carry):
        r0 = pl.multiple_of(t * tq, tq)
        vt_ref[t] = v_ref[pl.ds(r0, tq), :].astype(F32).T.astype(BF16)
        qt = q_ref[pl.ds(r0, tq), :].astype(F32).T
        qc_ref[t] = jnp.concatenate([jnp.where(sub < DA_HEAD_DIM, qt, 0.0),
                                     jnp.where(sub >= DA_HEAD_DIM, qt, 0.0)], axis=1).astype(BF16)
        l_ref[t] = jnp.zeros((1, 2 * tq), F32)
        acc_ref[t] = jnp.zeros((DA_V_DIM, 2 * tq), F32)
        return carry

    lax.fori_loop(0, n_blk, setup, 0, unroll=8)

    def probs(qi, kj, p_ref, masked):
        k0 = pl.multiple_of(kj * tq, tq)
        s = _dot(k_ref[pl.ds(k0, tq), :], qc_ref[qi])
        if masked:
            s = jnp.where(causal, s, NEG_INF)
        p = jnp.exp2(s - bound)
        l_ref[qi] = l_ref[qi] + jnp.sum(p, axis=0, keepdims=True)
        p_ref[...] = p.astype(BF16)

    def values(qi, kj, p_ref):
        acc_ref[qi] = acc_ref[qi] + _dot(vt_ref[kj], p_ref[...])

    def finalize(qi):
        on = acc_ref[qi] / l_ref[qi]
        ot = on[:, :tq] - lam * on[:, tq:]
        ot = ot * lax.rsqrt(jnp.mean(ot * ot, axis=0, keepdims=True) + EPS)
        o = ot.T * (gsub_ref[...] * (1.0 - LAMBDA_INIT))
        o_ref[pl.ds(pl.multiple_of(qi * tq, tq), tq), :] = o.astype(BF16)

    def full_steps(t, carry):
        for u in range(BOUNDED_UNROLL):
            n = BOUNDED_UNROLL * t + u
            probs(tqi_ref[n], tkj_ref[n], p_bufs[u % 2], False)
            prev = jnp.maximum(n - 1, 0)
            values(tqi_ref[prev], tkj_ref[prev], p_bufs[(u + 1) % 2])
        return carry

    pb_ref[...] = jnp.zeros(pb_ref.shape, BF16)
    lax.fori_loop(0, n_full // BOUNDED_UNROLL, full_steps, 0)
    values(tqi_ref[n_full - 1], tkj_ref[n_full - 1], pb_ref)

    def diag_steps(t, carry):
        for u in range(DIAG_UNROLL):
            qi = DIAG_UNROLL * t + u
            probs(qi, qi, p_bufs[u % 2], True)
            prev = jnp.maximum(qi - 1, 0)
            values(prev, prev, p_bufs[(u + 1) % 2])
            finalize(prev)
        return carry

    pb_ref[...] = jnp.zeros(pb_ref.shape, BF16)
    lax.fori_loop(0, n_blk // DIAG_UNROLL, diag_steps, 0)
    values(n_blk - 1, n_blk - 1, pb_ref)
    finalize(n_blk - 1)


def _attn_bounded(bound, q, k, v, lam_params, g_subln, *, batch, seq, tq):
    T, d_v = v.shape
    n_blk = seq // tq
    tqi, tkj = _attn_tables(n_blk)
    n_full = tqi.shape[0] - 2
    assert n_full % BOUNDED_UNROLL == 0 and n_blk % DIAG_UNROLL == 0, (seq, tq)
    blk = lambda: pl.BlockSpec((seq, DA_V_DIM), lambda b, h, *_: (b, h))
    const = lambda shape: pl.BlockSpec(shape, lambda b, h, *_: (0,) * len(shape))
    return pl.pallas_call(
        functools.partial(_attn_bounded_kernel, seq=seq, tq=tq, n_full=n_full),
        grid_spec=pltpu.PrefetchScalarGridSpec(
            num_scalar_prefetch=3,
            grid=(batch, DA_HEADS),
            in_specs=[const((4, DA_HEAD_DIM)), blk(), blk(), blk(), const((1, DA_V_DIM))],
            out_specs=blk(),
            scratch_shapes=[pltpu.VMEM((n_blk, DA_V_DIM, tq), BF16),
                            pltpu.VMEM((n_blk, DA_V_DIM, 2 * tq), BF16),
                            pltpu.VMEM((n_blk, DA_V_DIM, 2 * tq), F32),
                            pltpu.VMEM((n_blk, 1, 2 * tq), F32)]
                           + [pltpu.VMEM((tq, 2 * tq), BF16)] * 2),
        out_shape=jax.ShapeDtypeStruct((T, d_v), BF16),
        compiler_params=_cparams("arbitrary", "arbitrary"),
        name="attn_bounded",
    )(jnp.asarray(tqi), jnp.asarray(tkj), bound.reshape(1), lam_params, q, k, v,
      g_subln.reshape(1, DA_V_DIM))


def _attn(q, k, v, lam_params, g_subln, *, batch, seq, tq):
    T, d_v = v.shape
    n_blk = seq // tq
    tqi, tkj = _attn_tables(n_blk)
    n_full = tqi.shape[0] - 2
    assert n_full % ATTN_UNROLL == 0 and n_blk % 2 == 0, (seq, tq)
    blk = lambda: pl.BlockSpec((seq, DA_V_DIM), lambda b, h, tqi, tkj: (b, h))
    const = lambda shape: pl.BlockSpec(shape, lambda b, h, tqi, tkj: (0,) * len(shape))
    return pl.pallas_call(
        functools.partial(_attn_kernel, seq=seq, tq=tq, n_full=n_full),
        grid_spec=pltpu.PrefetchScalarGridSpec(
            num_scalar_prefetch=2,
            grid=(batch, DA_HEADS),
            in_specs=[const((4, DA_HEAD_DIM)), blk(), blk(), blk(), const((1, DA_V_DIM))],
            out_specs=blk(),
            scratch_shapes=[pltpu.VMEM((n_blk, DA_V_DIM, tq), BF16),
                            pltpu.VMEM((n_blk, DA_V_DIM, 2 * tq), BF16),
                            pltpu.VMEM((n_blk, DA_V_DIM, 2 * tq), F32),
                            pltpu.VMEM((n_blk, 1, 2 * tq), F32),
                            pltpu.VMEM((n_blk, 1, 2 * tq), F32)]
                           + [pltpu.VMEM((tq, 2 * tq), F32)] * 4
                           + [pltpu.VMEM((tq, 2 * tq), BF16)] * 2),
        out_shape=jax.ShapeDtypeStruct((T, d_v), BF16),
        compiler_params=_cparams("arbitrary", "arbitrary"),
        name="attn",
    )(jnp.asarray(tqi), jnp.asarray(tkj), lam_params, q, k, v, g_subln.reshape(1, DA_V_DIM))


def _pack_bf16_pairs(a, b):
    hi = pltpu.bitcast(a.astype(BF16).astype(F32), jnp.uint32)
    lo = pltpu.bitcast(b.astype(BF16).astype(F32), jnp.uint32)
    return hi | (lo >> 16)


def _unpack_bf16_pairs(p):
    a = pltpu.bitcast(p & jnp.uint32(0xFFFF0000), F32)
    b = pltpu.bitcast(p << 16, F32)
    return a, b


def _merge_kernel(x_ref, oa_ref, pool_ref, memo_ref, gmix_ref, wg_ref, bg_ref, wba_ref, wbp_ref,
                  wbm_ref, wout_ref, gffn_ref, wrh_ref, wrl_ref, br_ref,
                  h_ref, hn_ref, idx_ref, rank_ref, gate_ref, cnt_ref,
                  xn_ref, mrg_ref, hs_ref, carry_ref, *, tm, d_model):
    i = pl.program_id(0)
    D = d_model

    @pl.when(i == 0)
    def _():
        carry_ref[...] = jnp.zeros(carry_ref.shape, F32)
        hs_ref[...] = jnp.zeros(hs_ref.shape, F32)

    def logits():
        hn = _rms_rows(hs_ref[...], gffn_ref[...])
        hn_ref[...] = _pack_bf16_pairs(hn[:, :D // 2], hn[:, D // 2:])
        hn_hi = hn.astype(BF16)
        hn_lo = (hn - hn_hi.astype(F32)).astype(BF16)
        return (_dot(hn_hi, wrh_ref[...]) + _dot(hn_lo, wrh_ref[...])
                + _dot(hn_hi, wrl_ref[...]) + br_ref[...])

    route = _route_pieces(logits, i > 0, idx_ref, rank_ref, gate_ref, cnt_ref, carry_ref, tm)

    x = x_ref[...]
    xn_ref[...] = _rms_rows(x, gmix_ref[...]).astype(BF16)
    branches = ((oa_ref, wba_ref), (pool_ref, wbp_ref), (memo_ref, wbm_ref))

    def slab_dots(j):
        sl = slice(j * MXU_DIM, (j + 1) * MXU_DIM)
        out = []
        for br, (y_ref, w_ref) in enumerate(branches):
            gsl = slice(br * D + j * MXU_DIM, br * D + (j + 1) * MXU_DIM)
            out.append((_dot(xn_ref[...], wg_ref[:, gsl]), _dot(y_ref[...], w_ref[:, sl])))
        return out

    def slab_epilogue(j, dots):
        merged = None
        for br, (logit, y) in enumerate(dots):
            gsl = slice(br * D + j * MXU_DIM, br * D + (j + 1) * MXU_DIM)
            term = y / (1.0 + jnp.exp(-(logit + bg_ref[:, gsl])))
            merged = term if merged is None else merged + term
        mrg_ref[:, j * MXU_DIM:(j + 1) * MXU_DIM] = merged.astype(BF16)

    n_slabs = D // MXU_DIM
    dots = slab_dots(0)
    for j in range(n_slabs):
        nxt = slab_dots(j + 1) if j + 1 < n_slabs else None
        route[j]()
        slab_epilogue(j, dots)
        dots = nxt
    for piece in route[n_slabs:-1]:
        piece()
    h = x + _dot(mrg_ref[...], wout_ref[...])
    route[-1]()
    h_ref[...] = h
    hs_ref[...] = h


def _route_pieces(logits_fn, live, idx_ref, rank_ref, gate_ref, cnt_ref, carry_ref, tm):
    lane = lax.broadcasted_iota(jnp.int32, (tm, LANES), 1)
    st = dict(vals=[], sels=[], idx_out=jnp.zeros((tm, LANES), jnp.int32))

    def start():
        st["work"] = logits_fn()

    def pick(kk):
        work = st["work"]
        mx = jnp.max(work, axis=-1, keepdims=True)
        idx = jnp.min(jnp.where(work == mx, lane, LANES), axis=-1, keepdims=True)
        sel = lane == idx
        st["vals"].append(mx)
        st["sels"].append(sel)
        st["idx_out"] = jnp.where(lane == kk, idx, st["idx_out"])
        st["work"] = jnp.where(sel, -jnp.inf, work)

    def finish():
        vals, sels = st["vals"], st["sels"]
        exps = [jnp.exp(v - vals[0]) for v in vals]
        denom = exps[0] + exps[1] + exps[2] + exps[3]
        gate_out = jnp.zeros((tm, LANES), F32)
        onehot = jnp.zeros((tm, LANES), F32)
        for kk in range(TOP_K):
            gate_out = jnp.where(lane == kk, exps[kk] / denom, gate_out)
            onehot = jnp.where(sels[kk], 1.0, onehot)
        r_i = lax.broadcasted_iota(jnp.int32, (tm, tm), 0)
        c_i = lax.broadcasted_iota(jnp.int32, (tm, tm), 1)
        lower = (c_i < r_i).astype(BF16)
        prefix = _dot(lower, onehot.astype(BF16)) + carry_ref[0:1, :]
        rank_out = jnp.zeros((tm, LANES), jnp.int32)
        for kk in range(TOP_K):
            rk = jnp.sum(jnp.where(sels[kk], prefix, 0.0), axis=-1, keepdims=True)
            rank_out = jnp.where(lane == kk, rk.astype(jnp.int32), rank_out)
        new_carry = prefix[tm - 1:tm, :] + onehot[tm - 1:tm, :]
        new_carry = jnp.where(live, new_carry, carry_ref[0:1, :])
        carry_ref[...] = jnp.broadcast_to(new_carry, carry_ref.shape)
        idx_ref[...] = st["idx_out"]
        rank_ref[...] = rank_out
        gate_ref[...] = gate_out
        cnt_ref[...] = jnp.broadcast_to(new_carry, cnt_ref.shape).astype(jnp.int32)

    return [start] + [functools.partial(pick, kk) for kk in range(TOP_K)] + [finish]


def _merge(x2, oa, pool, memo, g_mix, w_gates, b_gates, w_b_attn, w_b_pool, w_b_mem, w_out, g_ffn,
           w_router, b_router, *, tm):
    T, D = x2.shape
    d_pool = pool.shape[1]
    d_mem = memo.shape[1]
    E = w_router.shape[1]
    wr = jnp.zeros((D, LANES), F32).at[:, :E].set(w_router)
    wr_hi = wr.astype(BF16)
    wr_lo = (wr - wr_hi.astype(F32)).astype(BF16)
    br = jnp.full((1, LANES), NEG_INF, F32).at[0, :E].set(b_router)
    n_tiles = T // tm
    row = lambda width: pl.BlockSpec((tm, width), lambda i: (jnp.minimum(i, n_tiles - 1), 0))
    routed = lambda width: pl.BlockSpec((tm, width), lambda i: (jnp.maximum(i - 1, 0), 0))
    sublanes = 8
    return pl.pallas_call(
        functools.partial(_merge_kernel, tm=tm, d_model=D),
        grid=(n_tiles + 1,),
        in_specs=[row(D), row(oa.shape[1]), row(d_pool), row(d_mem),
                  _const_spec((1, D)),
                  _const_spec((D, 3 * D)), _const_spec((1, 3 * D)),
                  _const_spec((oa.shape[1], D)), _const_spec((d_pool, D)), _const_spec((d_mem, D)),
                  _const_spec((D, D)), _const_spec((1, D)),
                  _const_spec((D, LANES)), _const_spec((D, LANES)), _const_spec((1, LANES))],
        out_specs=[row(D), routed(D // 2), routed(LANES), routed(LANES), routed(LANES),
                   _const_spec((sublanes, LANES))],
        out_shape=[jax.ShapeDtypeStruct((T, D), F32),
                   jax.ShapeDtypeStruct((T, D // 2), jnp.uint32),
                   jax.ShapeDtypeStruct((T, LANES), jnp.int32),
                   jax.ShapeDtypeStruct((T, LANES), jnp.int32),
                   jax.ShapeDtypeStruct((T, LANES), F32),
                   jax.ShapeDtypeStruct((sublanes, LANES), jnp.int32)],
        scratch_shapes=[pltpu.VMEM((tm, D), BF16),
                        pltpu.VMEM((tm, D), BF16),
                        pltpu.VMEM((tm, D), F32),
                        pltpu.VMEM((sublanes, LANES), F32)],
        compiler_params=_cparams("arbitrary"),
        name="merge",
    )(x2, oa, pool, memo, g_mix.reshape(1, D), w_gates.astype(BF16), b_gates.reshape(1, 3 * D),
      w_b_attn.astype(BF16), w_b_pool.astype(BF16), w_b_mem.astype(BF16), w_out.astype(BF16),
      g_ffn.reshape(1, D), wr_hi, wr_lo, br)


def _experts_kernel(be_ref, nv_ref, src_ref, x_ref, wg_ref, bg_ref, wu_ref, bu_ref, wd_ref, bd_ref,
                    y_ref, act_ref, *, bm, d_model, d_expert):
    i = pl.program_id(0)
    half = d_model // 2

    @pl.when(nv_ref[i] == 0)
    def _():
        y_ref[...] = jnp.zeros(y_ref.shape, y_ref.dtype)

    @pl.when(nv_ref[i] > 0)
    def _():
        a, b = _unpack_bf16_pairs(x_ref[...])
        valid = lax.broadcasted_iota(jnp.int32, (bm, half), 0) < nv_ref[i]
        a = jnp.where(valid, a, 0.0).astype(BF16)
        b = jnp.where(valid, b, 0.0).astype(BF16)
        for j in range(d_expert // MXU_DIM):
            sl = slice(j * MXU_DIM, (j + 1) * MXU_DIM)
            g = _dot(a, wg_ref[0, :half, sl]) + _dot(b, wg_ref[0, half:, sl]) + bg_ref[0, :, sl]
            u = _dot(a, wu_ref[0, :half, sl]) + _dot(b, wu_ref[0, half:, sl]) + bu_ref[0, :, sl]
            g = jnp.minimum(g, SWIGLU_LIMIT)
            u = jnp.clip(u, -SWIGLU_LIMIT, SWIGLU_LIMIT)
            act = g * (1.0 / (1.0 + jnp.exp(-SWIGLU_ALPHA * g))) * (u + 1.0)
            act_ref[:, sl] = act.astype(BF16)
        out = _dot(act_ref[...], wd_ref[0]) + bd_ref[0]
        y_ref[...] = _pack_bf16_pairs(out[:, :half], out[:, half:])


def _experts(blk_e, blk_valid, blk_src, xs, w_gate, b_gate, w_up, b_up, w_down, b_down, *, bm):
    rows_sorted, half = xs.shape
    E, D, De = w_gate.shape
    n_blocks = rows_sorted // bm
    wspec = lambda r, c: pl.BlockSpec((1, r, c), lambda i, be, nv, src: (be[i], 0, 0))
    xspec = pl.BlockSpec((bm, half), lambda i, be, nv, src: (src[i], 0))
    return pl.pallas_call(
        functools.partial(_experts_kernel, bm=bm, d_model=D, d_expert=De),
        grid_spec=pltpu.PrefetchScalarGridSpec(
            num_scalar_prefetch=3,
            grid=(n_blocks,),
            in_specs=[xspec, wspec(D, De), wspec(1, De), wspec(D, De), wspec(1, De),
                      wspec(De, D), wspec(1, D)],
            out_specs=pl.BlockSpec((bm, half), lambda i, be, nv, src: (i, 0)),
            scratch_shapes=[pltpu.VMEM((bm, De), BF16)]),
        out_shape=jax.ShapeDtypeStruct((rows_sorted, half), jnp.uint32),
        compiler_params=_cparams("arbitrary"),
        name="experts",
    )(blk_e, blk_valid, blk_src, xs, w_gate.astype(BF16), b_gate.reshape(E, 1, De),
      w_up.astype(BF16), b_up.reshape(E, 1, De), w_down.astype(BF16), b_down.reshape(E, 1, D))


SC_CORES = 2
SC_SUBCORES = 16
SC_CHUNK = 128


def _sc_mesh():
    return plsc.VectorSubcoreMesh(core_axis_name="c", subcore_axis_name="s",
                                  num_cores=SC_CORES, num_subcores=SC_SUBCORES)


def _sc_scatter_rows(rows, dest, *, rows_out):
    T, width = rows.shape
    n_chunks = T // SC_CHUNK
    per_worker = n_chunks // (SC_CORES * SC_SUBCORES)

    @functools.partial(
        pl.kernel, mesh=_sc_mesh(),
        out_type=jax.ShapeDtypeStruct((rows_out, width), rows.dtype),
        scratch_types=[pltpu.VMEM((TOP_K, SC_CHUNK), jnp.int32),
                       pltpu.VMEM((SC_CHUNK, width), rows.dtype)],
        name="sc_dispatch")
    def scatter(rows_hbm, dest_hbm, out_hbm, idx_v, rows_v):
        worker = lax.axis_index("s") * SC_CORES + lax.axis_index("c")

        @pl.loop(0, per_worker)
        def _(n):
            c = worker * per_worker + n
            pltpu.sync_copy(rows_hbm.at[pl.ds(c * SC_CHUNK, SC_CHUNK)], rows_v)
            pltpu.sync_copy(dest_hbm.at[c], idx_v)
            for kk in range(TOP_K):
                pltpu.sync_copy(rows_v, out_hbm.at[idx_v.at[kk]])

    return scatter(rows, dest)


def _sc_gather_rows(table, src, *, n_tokens):
    _, width = table.shape
    n_chunks = n_tokens // SC_CHUNK
    per_worker = n_chunks // (SC_CORES * SC_SUBCORES)

    @functools.partial(
        pl.kernel, mesh=_sc_mesh(),
        out_type=jax.ShapeDtypeStruct((TOP_K, n_tokens, width), table.dtype),
        scratch_types=[pltpu.VMEM((TOP_K, SC_CHUNK), jnp.int32),
                       pltpu.VMEM((SC_CHUNK, width), table.dtype)],
        name="sc_combine")
    def gather(table_hbm, src_hbm, out_hbm, idx_v, rows_v):
        worker = lax.axis_index("s") * SC_CORES + lax.axis_index("c")

        @pl.loop(0, per_worker)
        def _(n):
            c = worker * per_worker + n
            pltpu.sync_copy(src_hbm.at[c], idx_v)
            for kk in range(TOP_K):
                pltpu.sync_copy(table_hbm.at[idx_v.at[kk]], rows_v)
                pltpu.sync_copy(rows_v, out_hbm.at[kk, pl.ds(c * SC_CHUNK, SC_CHUNK)])

    return gather(table, src)


def _weighted_sum_kernel(gate_ref, h_ref, y_ref, *rest, d_model):
    o_ref = rest[-1]
    half = d_model // 2
    h = h_ref[...]
    lo, hi = h[:, :half], h[:, half:]
    gates = gate_ref[...]
    for kk in range(TOP_K):
        a, b = _unpack_bf16_pairs(y_ref[kk])
        w = gates[:, kk:kk + 1]
        lo = lo + w * a
        hi = hi + w * b
    o_ref[:, :half] = lo
    o_ref[:, half:] = hi


def _weighted_sum(gate, h, y4, out_prev, *, first_tile, tm):
    T, D = h.shape
    n_tiles = y4.shape[1] // tm
    row = lambda width: pl.BlockSpec((tm, width), lambda i: (first_tile + i, 0))
    in_specs = [row(LANES), row(D), pl.BlockSpec((TOP_K, tm, D // 2), lambda i: (0, i, 0))]
    args = [gate, h, y4]
    aliases = {}
    if out_prev is not None:
        in_specs.append(pl.BlockSpec(memory_space=pl.ANY))
        args.append(out_prev)
        aliases = {3: 0}
    return pl.pallas_call(
        functools.partial(_weighted_sum_kernel, d_model=D),
        grid=(n_tiles,),
        in_specs=in_specs,
        out_specs=row(D),
        out_shape=jax.ShapeDtypeStruct((T, D), F32),
        input_output_aliases=aliases,
        compiler_params=_cparams("arbitrary"),
        name="weighted_sum",
    )(*args)


def _block_tables(counts, *, bm, n_blocks):
    padded = ((counts + bm - 1) // bm) * bm
    pend = jnp.cumsum(padded)
    poff = pend - padded
    n_used = pend[-1] // bm
    src = jnp.minimum(jnp.arange(n_blocks, dtype=jnp.int32), n_used - 1)
    blk_e = jnp.sum(pend[None, :] <= (src * bm)[:, None], axis=1)
    blk_e = jnp.minimum(blk_e, N_EXPERTS - 1).astype(jnp.int32)
    blk_valid = jnp.clip(poff[blk_e] + counts[blk_e] - src * bm, 0, bm)
    blk_valid = jnp.where(jnp.arange(n_blocks) < n_used, blk_valid, 0).astype(jnp.int32)
    return poff.astype(jnp.int32), blk_e, blk_valid, src.astype(jnp.int32)


def _rope_tables(seq):
    inv_freq = ROPE_THETA ** (-jnp.arange(0, DA_HEAD_DIM, 2, dtype=F32) / DA_HEAD_DIM)
    ang = jnp.arange(seq, dtype=F32)[:, None] * inv_freq[None, :]
    reps = LANES // (DA_HEAD_DIM // 2)
    cos = jnp.tile(jnp.cos(ang), (1, reps))
    sin = jnp.tile(jnp.sin(ang), (1, reps))
    first_half = (jnp.arange(LANES) % DA_HEAD_DIM) < DA_HEAD_DIM // 2
    return cos, jnp.where(first_half[None, :], -sin, sin)


ROW_TILE = 512
ATTN_TILE = 256
EXPERT_BLOCK = 512
COMBINE_PARTS = 4


def _forward(x, mem, g_mix, w_in, b_gates, q_norm, k_norm, lambda_q1, lambda_k1, lambda_q2, lambda_k2, g_subln, pool_w, pool_scale, g_mem, w_mem_kv, mq_norm, mk_norm, w_b_attn, w_b_pool, w_b_mem, w_out, g_ffn, w_router, b_router, w_gate, b_gate, w_up, b_up, w_down, b_down):
    B, S, D = x.shape
    T = B * S
    x2 = x.reshape(T, D)
    cos_tab, sin_tab = _rope_tables(S)
    km, vm = _memkv(mem, g_mem, w_mem_kv, mk_norm)
    q, k, v, pool, memo = _inproj(x2, g_mix, w_in, q_norm, k_norm, cos_tab, sin_tab, pool_w,
                                  pool_scale, km, vm, mq_norm, seq=S, tm=ROW_TILE)
    lam_params = jnp.stack([lambda_q1, lambda_k1, lambda_q2, lambda_k2])
    score_bound = (ATTN_BOUND_MARGIN * DA_HEAD_DIM * QUERY_SCALE
                   * jnp.max(jnp.abs(q_norm)) * jnp.max(jnp.abs(k_norm))).astype(F32)
    o = lax.cond(score_bound <= ATTN_BOUND_MAX,
                 lambda: _attn_bounded(score_bound, q, k, v, lam_params, g_subln, batch=B, seq=S, tq=ATTN_TILE),
                 lambda: _attn(q, k, v, lam_params, g_subln, batch=B, seq=S, tq=ATTN_TILE))
    d_in = q.shape[1] + k.shape[1] + v.shape[1] + pool.shape[1] + memo.shape[1]
    h1, hn, idx, rank, gate, cnt = _merge(x2, o, pool, memo, g_mix, w_in[:, d_in:], b_gates, w_b_attn,
                                          w_b_pool, w_b_mem, w_out, g_ffn, w_router, b_router,
                                          tm=ROW_TILE)
    n_blocks = (T * TOP_K) // EXPERT_BLOCK + N_EXPERTS
    poff, blk_e, blk_valid, blk_src = _block_tables(cnt[0, :N_EXPERTS], bm=EXPERT_BLOCK, n_blocks=n_blocks)
    idx_d = idx[:, :TOP_K].reshape(T * TOP_K // LANES, LANES)
    rank_d = rank[:, :TOP_K].reshape(T * TOP_K // LANES, LANES)
    dest = (poff[idx_d] + rank_d).astype(jnp.int32)
    dest = dest.reshape(T // SC_CHUNK, SC_CHUNK, TOP_K).transpose(0, 2, 1)
    xs = _sc_scatter_rows(hn, dest, rows_out=n_blocks * EXPERT_BLOCK)
    ys = _experts(blk_e, blk_valid, blk_src, xs, w_gate, b_gate, w_up, b_up, w_down, b_down,
                  bm=EXPERT_BLOCK)
    out = None
    chunks_per_part = dest.shape[0] // COMBINE_PARTS
    tokens_per_part = T // COMBINE_PARTS
    for part in range(COMBINE_PARTS):
        y4 = _sc_gather_rows(ys, dest[part * chunks_per_part:(part + 1) * chunks_per_part],
                             n_tokens=tokens_per_part)
        out = _weighted_sum(gate, h1, y4, out, first_tile=part * (tokens_per_part // ROW_TILE), tm=ROW_TILE)
    return dict(q=q, k=k, v=v, pool=pool, memo=memo, o=o, h1=h1, hn=hn, idx=idx, rank=rank,
                gate=gate, cnt=cnt, out=out.reshape(B, S, D))


def kernel(x, mem, g_mix, w_in, b_gates, q_norm, k_norm, lambda_q1, lambda_k1, lambda_q2, lambda_k2, g_subln, pool_w, pool_scale, g_mem, w_mem_kv, mq_norm, mk_norm, w_b_attn, w_b_pool, w_b_mem, w_out, g_ffn, w_router, b_router, w_gate, b_gate, w_up, b_up, w_down, b_down):
    return _forward(x, mem, g_mix, w_in, b_gates, q_norm, k_norm, lambda_q1, lambda_k1, lambda_q2,
                    lambda_k2, g_subln, pool_w, pool_scale, g_mem, w_mem_kv, mq_norm, mk_norm,
                    w_b_attn, w_b_pool, w_b_mem, w_out, g_ffn, w_router, b_router, w_gate, b_gate,
                    w_up, b_up, w_down, b_down)["out"]
```
